```python
import jax, jax.numpy as jnp
from jax import lax
import numpy as np

D_MODEL = 2048
BATCH = 1
SEQ = 8192
DEPTH = 1
DEC_BATCH = 8
DEC_SEQ = 64
PAST_LEN = 4096

CHUNK = 64
HEAD_DIM = 64
RWKV_WIDTH = D_MODEL // 2
FOX_WIDTH = D_MODEL - RWKV_WIDTH
RWKV_HEADS = RWKV_WIDTH // HEAD_DIM
FOX_HEADS = FOX_WIDTH // HEAD_DIM
DECAY_LORA = 64
ICLR_LORA = 64
GATE_LORA = 128
RWKV_PROJ = 3 * RWKV_WIDTH + DECAY_LORA + ICLR_LORA + GATE_LORA
FOX_PROJ = 3 * FOX_WIDTH + FOX_HEADS
IN_PROJ = RWKV_PROJ + FOX_PROJ
N_EXPERTS = 32
TOP_K = 4
D_EXPERT = D_MODEL
SWIGLU_LIMIT = 7.0
SWIGLU_ALPHA = 1.702
MOE_BLOCK = 128
PLE_DIM = 256
Q_BLOCK = 128
RMS_EPS = 1e-6
GN_EPS = 64e-5
L2_EPS = 1e-12

kernel_name = 'rwkv7_fox_moe_streaming_step'


def rms_norm(x, g):
    xf = x.astype(jnp.float32)
    y = xf * lax.rsqrt(jnp.mean(xf * xf, axis=-1, keepdims=True) + RMS_EPS)
    return (y * g.astype(jnp.float32)).astype(x.dtype)


def token_shift(z, prev, mu):
    shifted = jnp.concatenate([prev.astype(z.dtype), z[:, :-1]], axis=1)
    return z + mu * (shifted - z)


def rwkv_scan(s0, r, w, k, v, kk, a):
    def step(s, inp):
        r_t, w_t, k_t, v_t, kk_t, a_t = inp
        sa = jnp.einsum('bhvk,bhk->bhv', s, kk_t)
        s = (s * w_t[:, :, None, :]
             - sa[..., None] * (kk_t * a_t)[:, :, None, :]
             + v_t[..., None] * k_t[:, :, None, :])
        return s, jnp.einsum('bhvk,bhk->bhv', s, r_t)
    xs = tuple(jnp.moveaxis(t, 1, 0) for t in (r, w, k, v, kk, a))
    s, ys = lax.scan(step, s0, xs)
    return s, jnp.moveaxis(ys, 0, 1)


def rwkv7_mix(zs, s_past, lw):
    b, t, _ = zs.shape
    f32 = jnp.float32
    r, k, v, wd, ad, gd = jnp.split(
        zs, [RWKV_WIDTH, 2 * RWKV_WIDTH, 3 * RWKV_WIDTH, 3 * RWKV_WIDTH + DECAY_LORA,
             3 * RWKV_WIDTH + DECAY_LORA + ICLR_LORA], axis=-1)
    w_pre = (lw['rwkv_w0'] + jnp.tanh(wd) @ lw['rwkv_w2']).astype(f32)
    decay = jnp.exp(-jnp.exp(-jax.nn.softplus(-w_pre) - 0.5))
    a = jax.nn.sigmoid((lw['rwkv_a0'] + ad @ lw['rwkv_a2']).astype(f32))
    g = (jax.nn.sigmoid(gd) @ lw['rwkv_g2']).astype(f32)
    heads = lambda u: u.reshape(b, t, RWKV_HEADS, HEAD_DIM)
    kk = heads((k * lw['rwkv_kk']).astype(f32))
    kk = kk * lax.rsqrt(jnp.sum(kk * kk, axis=-1, keepdims=True) + L2_EPS)
    k_mod = k.astype(f32) * (1.0 + (a - 1.0) * lw['rwkv_ka'].astype(f32))
    r_h, k_h, v_h = heads(r.astype(f32)), heads(k_mod), heads(v.astype(f32))
    s_new, y = rwkv_scan(s_past.astype(f32), r_h, heads(decay), k_h, v_h, kk, heads(a))
    mu = jnp.mean(y, axis=-1, keepdims=True)
    var = jnp.mean(jnp.square(y - mu), axis=-1, keepdims=True)
    y = ((y - mu) * lax.rsqrt(var + GN_EPS)).reshape(b, t, RWKV_WIDTH)
    y = y * lw['rwkv_ln_g'].astype(f32) + lw['rwkv_ln_b'].astype(f32)
    bonus = jnp.sum(r_h * k_h * lw['rwkv_rk'].astype(f32), axis=-1, keepdims=True) * v_h
    y = (y + bonus.reshape(b, t, RWKV_WIDTH)) * g
    return y.astype(zs.dtype), s_new.astype(s_past.dtype)


def fox_block(q, c_q, qpos, k, v, c_k, kpos):
    s = jnp.einsum('bqhd,bkhd->bhqk', q, k, preferred_element_type=jnp.float32) * (HEAD_DIM ** -0.5)
    s = s + jnp.swapaxes(c_q, 1, 2)[..., :, None] - jnp.swapaxes(c_k, 1, 2)[..., None, :]
    s = jnp.where(kpos[None, :] <= qpos[:, None], s, -jnp.inf)
    p = jax.nn.softmax(s, axis=-1)
    return jnp.einsum('bhqk,bkhd->bqhd', p.astype(v.dtype), v)


def fox_mix(z_fox, k_past, v_past, lf_past, lw):
    b, t, _ = z_fox.shape
    q, k, v, fl = jnp.split(z_fox, [FOX_WIDTH, 2 * FOX_WIDTH, 3 * FOX_WIDTH], axis=-1)
    heads = lambda u: u.reshape(b, t, FOX_HEADS, HEAD_DIM)
    q = rms_norm(heads(q), lw['fox_q_g'])
    k = rms_norm(heads(k), lw['fox_k_g'])
    v = heads(v)
    logf = jax.nn.log_sigmoid((fl + lw['fox_b_f']).astype(jnp.float32))
    n_past = k_past.shape[1]
    k_all = jnp.concatenate([k_past.astype(k.dtype), k], axis=1)
    v_all = jnp.concatenate([v_past.astype(v.dtype), v], axis=1)
    c = jnp.cumsum(jnp.concatenate([lf_past.astype(jnp.float32), logf], axis=1), axis=1)
    kpos = jnp.arange(n_past + t)
    qpos = n_past + jnp.arange(t)
    c_q = c[:, n_past:]
    if t <= Q_BLOCK:
        o = fox_block(q, c_q, qpos, k_all, v_all, c, kpos)
    else:
        nb = t // Q_BLOCK
        qb = jnp.swapaxes(q.reshape(b, nb, Q_BLOCK, FOX_HEADS, HEAD_DIM), 0, 1)
        cb = jnp.swapaxes(c_q.reshape(b, nb, Q_BLOCK, FOX_HEADS), 0, 1)
        pb = qpos.reshape(nb, Q_BLOCK)
        o = lax.map(lambda blk: fox_block(blk[0], blk[1], blk[2], k_all, v_all, c, kpos), (qb, cb, pb))
        o = jnp.swapaxes(o, 0, 1).reshape(b, t, FOX_HEADS, HEAD_DIM)
    return o.reshape(b, t, FOX_WIDTH), k, v, logf


def moe_ffn(x, router_w, router_b, w_gu, b_gu, w_dn, b_dn):
    n_tok, d = x.shape
    n_rows = n_tok * TOP_K
    logits = x.astype(jnp.float32) @ router_w.astype(jnp.float32) + router_b.astype(jnp.float32)
    top_logit, top_idx = lax.top_k(logits, TOP_K)
    gates = jax.nn.softmax(top_logit, axis=-1).astype(x.dtype)
    flat_e = top_idx.reshape(n_rows)
    order = jnp.argsort(flat_e)
    sorted_e = flat_e[order]
    counts = jnp.bincount(flat_e, length=N_EXPERTS)
    padded = (counts + MOE_BLOCK - 1) // MOE_BLOCK * MOE_BLOCK
    start = jnp.cumsum(counts) - counts
    pad_end = jnp.cumsum(padded)
    pad_start = pad_end - padded
    dest = pad_start[sorted_e] + jnp.arange(n_rows) - start[sorted_e]
    n_blocks = -(-n_rows // MOE_BLOCK) + N_EXPERTS
    total = n_blocks * MOE_BLOCK
    row_tok = jnp.full((total,), n_tok, jnp.int32).at[dest].set((order // TOP_K).astype(jnp.int32))
    row_slot = jnp.full((total,), n_rows, jnp.int32).at[dest].set(order.astype(jnp.int32))
    block_expert = jnp.minimum(
        jnp.searchsorted(pad_end, jnp.arange(n_blocks) * MOE_BLOCK, side='right'), N_EXPERTS - 1)
    x_pad = jnp.concatenate([x, jnp.zeros((1, d), x.dtype)], axis=0)
    xb = x_pad[row_tok].reshape(n_blocks, MOE_BLOCK, d)

    def expert_block(blk):
        xe, e = blk
        gu = xe @ w_gu[e] + b_gu[e]
        g, u = gu[:, :D_EXPERT], gu[:, D_EXPERT:]
        g = jnp.minimum(g, SWIGLU_LIMIT)
        u = jnp.clip(u, -SWIGLU_LIMIT, SWIGLU_LIMIT)
        hid = (u + 1.0) * (g * jax.nn.sigmoid(SWIGLU_ALPHA * g))
        return hid @ w_dn[e] + b_dn[e]

    yb = lax.map(expert_block, (xb, block_expert)).reshape(total, d)
    y_rows = jnp.zeros((n_rows + 1, d), yb.dtype).at[row_slot].set(yb)[:n_rows]
    return jnp.einsum('tkd,tk->td', y_rows.reshape(n_tok, TOP_K, d), gates)


def hybrid_layer(x, p, k_past, v_past, lf_past, s_past, shift_past, lw):
    b, t, _ = x.shape
    h = rms_norm(x, lw['norm_mix_g'])
    z = h @ lw['w_in']
    z_rwkv, z_fox = z[..., :RWKV_PROJ], z[..., RWKV_PROJ:]
    zs = token_shift(z_rwkv, shift_past, lw['rwkv_mu'])
    y_rwkv, s_new = rwkv7_mix(zs, s_past, lw)
    y_fox, k_new, v_new, lf_new = fox_mix(z_fox, k_past, v_past, lf_past, lw)
    x = x + jnp.concatenate([y_rwkv, y_fox], axis=-1) @ lw['w_out']
    hf = rms_norm(x, lw['norm_ffn_g']).reshape(b * t, D_MODEL)
    x = x + moe_ffn(hf, lw['router_w'], lw['router_b'], lw['expert_w_gu'], lw['expert_b_gu'],
                    lw['expert_w_down'], lw['expert_b_down']).reshape(b, t, D_MODEL)
    gate = jax.nn.sigmoid(rms_norm(x, lw['ple_norm_g']) @ lw['ple_w_gate'])
    x = x + gate * (p.astype(x.dtype) @ lw['ple_w_proj'])
    return x, (k_new, v_new, lf_new, s_new, z_rwkv[:, -1:])


def setup_inputs(seed: int = 0) -> dict:
    key = jax.random.key(seed)
    ks = iter(jax.random.split(key, 48))
    f32 = jnp.float32
    nrm = lambda shape, scale: jax.random.normal(next(ks), shape, f32) * scale
    uni = lambda shape, lo, hi: jax.random.uniform(next(ks), shape, f32, lo, hi)
    L, D = DEPTH, D_MODEL
    return {
        'x_prompt': nrm((BATCH, SEQ, D), 1.0),
        'x_sample': nrm((DEC_BATCH, DEC_SEQ, D), 1.0),
        'cache_fox_k': nrm((L, DEC_BATCH, PAST_LEN, FOX_HEADS, HEAD_DIM), 1.0),
        'cache_fox_v': nrm((L, DEC_BATCH, PAST_LEN, FOX_HEADS, HEAD_DIM), 1.0),
        'cache_fox_logf': jax.nn.log_sigmoid(uni((L, DEC_BATCH, PAST_LEN, FOX_HEADS), 1.0, 5.0)),
        'state_rwkv': nrm((L, DEC_BATCH, RWKV_HEADS, HEAD_DIM, HEAD_DIM), 0.5),
        'state_rwkv_shift': nrm((L, DEC_BATCH, 1, RWKV_PROJ), 1.0),
        'p_prompt': nrm((L, BATCH, SEQ, PLE_DIM), 1.0),
        'p_sample': nrm((L, DEC_BATCH, DEC_SEQ, PLE_DIM), 1.0),
        'norm_mix_g': 1.0 + nrm((L, D), 0.02),
        'w_in': nrm((L, D, IN_PROJ), D ** -0.5),
        'rwkv_mu': uni((L, RWKV_PROJ), 0.0, 1.0),
        'rwkv_w0': uni((L, RWKV_WIDTH), -6.0, -1.0),
        'rwkv_w2': nrm((L, DECAY_LORA, RWKV_WIDTH), 0.1),
        'rwkv_a0': nrm((L, RWKV_WIDTH), 0.5),
        'rwkv_a2': nrm((L, ICLR_LORA, RWKV_WIDTH), ICLR_LORA ** -0.5),
        'rwkv_g2': nrm((L, GATE_LORA, RWKV_WIDTH), GATE_LORA ** -0.5),
        'rwkv_kk': 1.0 + nrm((L, RWKV_WIDTH), 0.1),
        'rwkv_ka': 1.0 + nrm((L, RWKV_WIDTH), 0.1),
        'rwkv_rk': nrm((L, RWKV_HEADS, HEAD_DIM), 0.1),
        'rwkv_ln_g': 1.0 + nrm((L, RWKV_WIDTH), 0.02),
        'rwkv_ln_b': nrm((L, RWKV_WIDTH), 0.02),
        'fox_q_g': 1.0 + nrm((L, HEAD_DIM), 0.02),
        'fox_k_g': 1.0 + nrm((L, HEAD_DIM), 0.02),
        'fox_b_f': uni((L, FOX_HEADS), 1.0, 5.0),
        'w_out': nrm((L, D, D), D ** -0.5),
        'norm_ffn_g': 1.0 + nrm((L, D), 0.02),
        'router_w': nrm((L, D, N_EXPERTS), D ** -0.5),
        'router_b': nrm((L, N_EXPERTS), 0.01),
        'expert_w_gu': nrm((L, N_EXPERTS, D, 2 * D_EXPERT), D ** -0.5),
        'expert_b_gu': nrm((L, N_EXPERTS, 2 * D_EXPERT), 0.01),
        'expert_w_down': nrm((L, N_EXPERTS, D_EXPERT, D), D_EXPERT ** -0.5),
        'expert_b_down': nrm((L, N_EXPERTS, D), 0.01),
        'ple_norm_g': 1.0 + nrm((L, D), 0.02),
        'ple_w_gate': nrm((L, D, D), D ** -0.5),
        'ple_w_proj': nrm((L, PLE_DIM, D), PLE_DIM ** -0.5),
    }


def reference(x_prompt, x_sample, cache_fox_k, cache_fox_v, cache_fox_logf, state_rwkv, state_rwkv_shift,
              p_prompt, p_sample, norm_mix_g, w_in, rwkv_mu, rwkv_w0, rwkv_w2, rwkv_a0, rwkv_a2, rwkv_g2,
              rwkv_kk, rwkv_ka, rwkv_rk, rwkv_ln_g, rwkv_ln_b, fox_q_g, fox_k_g, fox_b_f, w_out, norm_ffn_g,
              router_w, router_b, expert_w_gu, expert_b_gu, expert_w_down, expert_b_down, ple_norm_g,
              ple_w_gate, ple_w_proj):
    assert x_sample.shape[1] <= CHUNK
    bp = x_prompt.shape[0]
    dt = x_prompt.dtype
    y_p, y_s = x_prompt, x_sample
    st_p, st_s = [], []
    for i in range(DEPTH):
        lw = dict(norm_mix_g=norm_mix_g[i], w_in=w_in[i], rwkv_mu=rwkv_mu[i], rwkv_w0=rwkv_w0[i],
                  rwkv_w2=rwkv_w2[i], rwkv_a0=rwkv_a0[i], rwkv_a2=rwkv_a2[i], rwkv_g2=rwkv_g2[i],
                  rwkv_kk=rwkv_kk[i], rwkv_ka=rwkv_ka[i], rwkv_rk=rwkv_rk[i], rwkv_ln_g=rwkv_ln_g[i],
                  rwkv_ln_b=rwkv_ln_b[i], fox_q_g=fox_q_g[i], fox_k_g=fox_k_g[i], fox_b_f=fox_b_f[i],
                  w_out=w_out[i], norm_ffn_g=norm_ffn_g[i], router_w=router_w[i], router_b=router_b[i],
                  expert_w_gu=expert_w_gu[i], expert_b_gu=expert_b_gu[i], expert_w_down=expert_w_down[i],
                  expert_b_down=expert_b_down[i], ple_norm_g=ple_norm_g[i], ple_w_gate=ple_w_gate[i],
                  ple_w_proj=ple_w_proj[i])
        y_p, sp = hybrid_layer(y_p, p_prompt[i],
                               jnp.zeros((bp, 0, FOX_HEADS, HEAD_DIM), dt),
                               jnp.zeros((bp, 0, FOX_HEADS, HEAD_DIM), dt),
                               jnp.zeros((bp, 0, FOX_HEADS), jnp.float32),
                               jnp.zeros((bp, RWKV_HEADS, HEAD_DIM, HEAD_DIM), dt),
                               jnp.zeros((bp, 1, RWKV_PROJ), dt), lw)
        y_s, ss = hybrid_layer(y_s, p_sample[i], cache_fox_k[i], cache_fox_v[i], cache_fox_logf[i],
                               state_rwkv[i], state_rwkv_shift[i], lw)
        st_p.append(sp)
        st_s.append(ss)

    def stacked(states, j):
        return jnp.stack([s[j] for s in states])

    return (y_p, y_s,
            stacked(st_p, 0), stacked(st_p, 1), stacked(st_p, 2), stacked(st_p, 3), stacked(st_p, 4),
            stacked(st_s, 0), stacked(st_s, 1), stacked(st_s, 2), stacked(st_s, 3), stacked(st_s, 4))
```

```python
import functools

import numpy as np
import jax
import jax.numpy as jnp
from jax import lax
from jax.experimental import pallas as pl
from jax.experimental.pallas import tpu as pltpu

F32 = jnp.float32
BF16 = jnp.bfloat16
HI = lax.Precision.HIGHEST

D_MODEL = 2048
HEAD_DIM = 64
N_HEADS = 16
N_PAIRS = N_HEADS // 2
MIX_W = N_HEADS * HEAD_DIM
CHUNK = 64
RWKV_PROJ = 3 * MIX_W + 64 + 64 + 128
FOX_PROJ = 3 * MIX_W + N_HEADS
Z_HALF = RWKV_PROJ
Z_PAD = 2 * Z_HALF
N_EXPERTS = 32
TOP_K = 4
D_EXPERT = 2048
SWIGLU_LIMIT = 7.0
SWIGLU_ALPHA = 1.702
PLE_DIM = 256
RMS_EPS = 1e-6
GN_EPS = 64e-5
L2_EPS = 1e-12
NEG_BIG = -1e30

LANES = 128
MOE_ROWS = 256
MOE_TN = 512
VMEM_LIMIT = 52 * 1024 * 1024


def _cparams(n_axes, vmem=VMEM_LIMIT):
    return pltpu.CompilerParams(dimension_semantics=("arbitrary",) * n_axes, vmem_limit_bytes=vmem)


def _head_sum(x):
    r = lax.broadcasted_iota(jnp.int32, (LANES, LANES), 0) // HEAD_DIM
    c = lax.broadcasted_iota(jnp.int32, (LANES, LANES), 1) // HEAD_DIM
    bd = (r == c).astype(F32)
    parts = [jnp.dot(x[:, i * LANES:(i + 1) * LANES], bd, precision=HI, preferred_element_type=F32)
             for i in range(x.shape[1] // LANES)]
    return parts[0] if len(parts) == 1 else jnp.concatenate(parts, axis=1)


def _log_sigmoid(x):
    return jnp.minimum(x, 0.0) - jnp.log1p(jnp.exp(-jnp.abs(x)))


def _rms(x, g):
    ms = jnp.mean(x * x, axis=-1, keepdims=True)
    return x * lax.rsqrt(ms + RMS_EPS) * g


def _inproj_kernel(x_ref, g_ref, w_ref, o_ref, h_scr):
    @pl.when(pl.program_id(1) == 0)
    def _():
        h_scr[...] = _rms(x_ref[...], g_ref[...]).astype(BF16)

    o_ref[...] = jnp.dot(h_scr[...], w_ref[...], preferred_element_type=F32)


def _in_proj(x, g, w_bf):
    n = x.shape[0]
    tm = min(512, n)
    tn = 512
    return pl.pallas_call(
        _inproj_kernel,
        grid=(n // tm, Z_PAD // tn),
        in_specs=[pl.BlockSpec((tm, D_MODEL), lambda i, j: (i, 0)),
                  pl.BlockSpec((1, D_MODEL), lambda i, j: (0, 0)),
                  pl.BlockSpec((D_MODEL, tn), lambda i, j: (0, j))],
        out_specs=pl.BlockSpec((tm, tn), lambda i, j: (i, j)),
        out_shape=jax.ShapeDtypeStruct((n, Z_PAD), F32),
        scratch_shapes=[pltpu.VMEM((tm, D_MODEL), BF16)],
        compiler_params=_cparams(2),
        name="in_proj",
    )(x, g.reshape(1, D_MODEL), w_bf)


def _rwkv_pre_kernel(z_ref, prev_ref, mu_ref, vec_ref, wl_ref,
                     r_o, w_o, k_o, kk_o, b_o, v_o, g_o, bn_o, carry, *, n_prompt_tiles):
    i = pl.program_id(0)
    z = z_ref[...]
    tm = z.shape[0]

    @pl.when(jnp.logical_or(i == 0, i >= n_prompt_tiles))
    def _():
        carry[...] = prev_ref[0]

    prev = carry[...]
    rolled = pltpu.roll(z, 1, axis=0)
    row = lax.broadcasted_iota(jnp.int32, z.shape, 0)
    shifted = jnp.where(row == 0, prev, rolled)
    carry[...] = z[tm - 1:tm, :]
    zs = z + mu_ref[...] * (shifted - z)

    r = zs[:, 0:MIX_W]
    k = zs[:, MIX_W:2 * MIX_W]
    v = zs[:, 2 * MIX_W:3 * MIX_W]
    lo = zs[:, 3 * MIX_W:RWKV_PROJ]
    lane = lax.broadcasted_iota(jnp.int32, lo.shape, 1)
    f = jnp.where(lane < 64, jnp.tanh(lo), jnp.where(lane < 128, lo, jax.nn.sigmoid(lo)))
    lora = jnp.dot(f, wl_ref[...], precision=HI, preferred_element_type=F32)
    w_pre = vec_ref[0:1, :] + lora[:, 0:MIX_W]
    decay = jnp.exp(-jnp.exp(_log_sigmoid(w_pre) - 0.5))
    a = jax.nn.sigmoid(vec_ref[1:2, :] + lora[:, MIX_W:2 * MIX_W])
    g = lora[:, 2 * MIX_W:3 * MIX_W]
    kk = k * vec_ref[2:3, :]
    kk = kk * lax.rsqrt(_head_sum(kk * kk) + L2_EPS)
    k_mod = k * (1.0 + (a - 1.0) * vec_ref[3:4, :])
    bonus = _head_sum(r * k_mod * vec_ref[4:5, :]) * v

    r_o[...] = r
    w_o[...] = decay
    k_o[...] = k_mod
    kk_o[...] = kk
    b_o[...] = kk * a
    g_o[...] = g
    bn_o[...] = bonus
    for p in range(N_PAIRS):
        v_o[p] = v[:, p * LANES:(p + 1) * LANES]


def _rwkv_pre(z, prev, mu, vecs, wl, n_prompt):
    n = z.shape[0]
    tm = CHUNK
    nt = n // tm
    n_prompt_tiles = n_prompt // tm
    tok = jax.ShapeDtypeStruct((n, MIX_W), F32)
    tspec = pl.BlockSpec((tm, MIX_W), lambda i: (i, 0))
    return pl.pallas_call(
        functools.partial(_rwkv_pre_kernel, n_prompt_tiles=n_prompt_tiles),
        grid=(nt,),
        in_specs=[pl.BlockSpec((tm, Z_HALF), lambda i: (i, 0)),
                  pl.BlockSpec((1, 1, RWKV_PROJ), lambda i: (jnp.maximum(i - (n_prompt_tiles - 1), 0), 0, 0)),
                  pl.BlockSpec((1, RWKV_PROJ), lambda i: (0, 0)),
                  pl.BlockSpec((8, MIX_W), lambda i: (0, 0)),
                  pl.BlockSpec((256, 3 * MIX_W), lambda i: (0, 0))],
        out_specs=[tspec, tspec, tspec, tspec, tspec,
                   pl.BlockSpec((N_PAIRS, tm, LANES), lambda i: (0, i, 0)),
                   tspec, tspec],
        out_shape=[tok, tok, tok, tok, tok,
                   jax.ShapeDtypeStruct((N_PAIRS, n, LANES), F32), tok, tok],
        scratch_shapes=[pltpu.VMEM((1, RWKV_PROJ), F32)],
        compiler_params=_cparams(1),
        name="rwkv_pre",
    )(z, prev, mu, vecs, wl)


def _scan_kernel(rT, wT, kT, kkT, bT, v_ref, s0_ref, y_ref, sout_ref, s_scr, *, n_prompt_chunks):
    c = pl.program_id(0)

    @pl.when(jnp.logical_or(c == 0, c >= n_prompt_chunks))
    def _():
        s_scr[...] = s0_ref[0]

    lo_half = lax.broadcasted_iota(jnp.int32, (HEAD_DIM, LANES), 1) < HEAD_DIM

    def colb(xt, t):
        a = jnp.broadcast_to(xt[0:HEAD_DIM, t:t + 1], (HEAD_DIM, LANES))
        b = jnp.broadcast_to(xt[HEAD_DIM:2 * HEAD_DIM, t:t + 1], (HEAD_DIM, LANES))
        return jnp.where(lo_half, a, b)

    def pair_body(p, carry):
        rows = pl.ds(pl.multiple_of(p * LANES, LANES), LANES)
        r_c = rT[0, rows, :]
        w_c = wT[0, rows, :]
        k_c = kT[0, rows, :]
        kk_c = kkT[0, rows, :]
        b_c = bT[0, rows, :]
        s = s_scr[p]
        for t in range(CHUNK):
            v_row = v_ref[p, pl.ds(t, 1), :]
            sa = jnp.sum(s * colb(kk_c, t), axis=0, keepdims=True)
            s = s * colb(w_c, t) - colb(b_c, t) * sa + colb(k_c, t) * v_row
            y_ref[p, pl.ds(t, 1), :] = jnp.sum(s * colb(r_c, t), axis=0, keepdims=True)
        s_scr[p] = s
        return carry

    lax.fori_loop(0, N_PAIRS, pair_body, 0)
    sout_ref[0] = s_scr[...]


def _rwkv_scan(r_t, w_t, k_t, kk_t, b_t, v_pm, s0, n_prompt):
    nch = r_t.shape[0]
    n = nch * CHUNK
    npc = n_prompt // CHUNK
    n_seq = s0.shape[0]
    tspec = pl.BlockSpec((1, MIX_W, CHUNK), lambda c: (c, 0, 0))
    sidx = lambda c: (jnp.maximum(c - (npc - 1), 0), 0, 0, 0)
    return pl.pallas_call(
        functools.partial(_scan_kernel, n_prompt_chunks=npc),
        grid=(nch,),
        in_specs=[tspec, tspec, tspec, tspec, tspec,
                  pl.BlockSpec((N_PAIRS, CHUNK, LANES), lambda c: (0, c, 0)),
                  pl.BlockSpec((1, N_PAIRS, HEAD_DIM, LANES), sidx)],
        out_specs=[pl.BlockSpec((N_PAIRS, CHUNK, LANES), lambda c: (0, c, 0)),
                   pl.BlockSpec((1, N_PAIRS, HEAD_DIM, LANES), sidx)],
        out_shape=[jax.ShapeDtypeStruct((N_PAIRS, n, LANES), F32),
                   jax.ShapeDtypeStruct((n_seq, N_PAIRS, HEAD_DIM, LANES), F32)],
        scratch_shapes=[pltpu.VMEM((N_PAIRS, HEAD_DIM, LANES), F32)],
        compiler_params=_cparams(1),
        name="rwkv_scan",
    )(r_t, w_t, k_t, kk_t, b_t, v_pm, s0)


def _fox_pre_kernel(z_ref, vec_ref, bf_ref, q_o, k_o, v_o, kn_o, vn_o, lf_o):
    z = z_ref[...]
    q = z[:, 0:MIX_W]
    k = z[:, MIX_W:2 * MIX_W]
    v = z[:, 2 * MIX_W:3 * MIX_W]
    fl = z[:, 3 * MIX_W:3 * MIX_W + LANES]
    inv = 1.0 / HEAD_DIM
    qn = q * lax.rsqrt(_head_sum(q * q) * inv + RMS_EPS) * vec_ref[0:1, :]
    kn = k * lax.rsqrt(_head_sum(k * k) * inv + RMS_EPS) * vec_ref[1:2, :]
    qs = (qn * (HEAD_DIM ** -0.5)).astype(BF16)
    kb = kn.astype(BF16)
    vb = v.astype(BF16)
    for p in range(N_PAIRS):
        sl = slice(p * LANES, (p + 1) * LANES)
        q_o[p] = qs[:, sl]
        k_o[p] = kb[:, sl]
        v_o[p] = vb[:, sl]
    kn_o[...] = kn
    vn_o[...] = v
    lf_o[...] = _log_sigmoid(fl + bf_ref[...])


def _fox_pre(z, vecs, bf):
    n = z.shape[0]
    tm = min(256, n)
    pm = jax.ShapeDtypeStruct((N_PAIRS, n, LANES), BF16)
    pspec = pl.BlockSpec((N_PAIRS, tm, LANES), lambda i: (0, i, 0))
    tok = jax.ShapeDtypeStruct((n, MIX_W), F32)
    tspec = pl.BlockSpec((tm, MIX_W), lambda i: (i, 0))
    return pl.pallas_call(
        _fox_pre_kernel,
        grid=(n // tm,),
        in_specs=[pl.BlockSpec((tm, Z_HALF), lambda i: (i, 1)),
                  pl.BlockSpec((8, MIX_W), lambda i: (0, 0)),
                  pl.BlockSpec((1, LANES), lambda i: (0, 0))],
        out_specs=[pspec, pspec, pspec, tspec, tspec, pl.BlockSpec((tm, LANES), lambda i: (i, 0))],
        out_shape=[pm, pm, pm, tok, tok, jax.ShapeDtypeStruct((n, LANES), F32)],
        compiler_params=_cparams(1),
        name="fox_pre",
    )(z, vecs, bf)


def _cumsum_kernel(x_ref, o_ref):
    r = lax.broadcasted_iota(jnp.int32, (LANES, LANES), 0)
    c = lax.broadcasted_iota(jnp.int32, (LANES, LANES), 1)
    tri = (r <= c).astype(F32)
    carry = jnp.zeros((N_HEADS, 1), F32)
    for i in range(x_ref.shape[2] // LANES):
        sl = slice(i * LANES, (i + 1) * LANES)
        cs = jnp.dot(x_ref[0, :, sl], tri, precision=HI, preferred_element_type=F32) + carry
        o_ref[0, :, sl] = cs
        carry = cs[:, LANES - 1:LANES]


def _cumsum_lanes(x):
    b, h, t = x.shape
    return pl.pallas_call(
        _cumsum_kernel,
        grid=(b,),
        in_specs=[pl.BlockSpec((1, h, t), lambda i: (i, 0, 0))],
        out_specs=pl.BlockSpec((1, h, t), lambda i: (i, 0, 0)),
        out_shape=jax.ShapeDtypeStruct((b, h, t), F32),
        compiler_params=_cparams(1),
        name="cumsum_logf",
    )(x)


def _rep_lanes(m, tk):
    if tk % LANES == 0:
        return m if tk == LANES else jnp.concatenate([m] * (tk // LANES), axis=1)
    return m[:, :tk]


def _attn_update(qa, qb, kb, vb, ck, mask, m_ref, l_ref, acc_ref):
    tk = kb.shape[0]
    lo_half = lax.broadcasted_iota(jnp.int32, acc_ref.shape, 1) < HEAD_DIM
    pv = []
    alphas = []
    for h, qh in enumerate((qa, qb)):
        s = lax.dot_general(qh, kb, (((1,), (1,)), ((), ())), preferred_element_type=F32)
        s = s - ck[h:h + 1, :]
        if mask is not None:
            s = jnp.where(mask, s, NEG_BIG)
        m_prev = m_ref[h]
        m_next = jnp.maximum(m_prev, jnp.max(s, axis=1, keepdims=True))
        p = jnp.exp(s - _rep_lanes(m_next, tk))
        alpha = jnp.exp(m_prev - m_next)
        l_ref[h] = alpha * l_ref[h] + jnp.sum(p, axis=1, keepdims=True)
        m_ref[h] = m_next
        pv.append(jnp.dot(p.astype(BF16), vb, preferred_element_type=F32))
        alphas.append(alpha)
    acc_ref[...] = acc_ref[...] * jnp.where(lo_half, alphas[0], alphas[1]) + jnp.where(lo_half, pv[0], pv[1])


def _split_q(q):
    qf = q.astype(F32)
    lo_half = lax.broadcasted_iota(jnp.int32, qf.shape, 1) < HEAD_DIM
    return jnp.where(lo_half, qf, 0.0).astype(BF16), jnp.where(lo_half, 0.0, qf).astype(BF16)


def _attn_prompt_kernel(qi_ref, ki_ref, q_ref, k_ref, v_ref, c_ref, o_ref,
                        qa_scr, qb_scr, m_scr, l_scr, acc_scr, *, tq, tk):
    s_id = pl.program_id(1)
    qi = qi_ref[s_id]
    ki = ki_ref[s_id]

    @pl.when(ki == 0)
    def _():
        qa, qb = _split_q(q_ref[0])
        qa_scr[...] = qa
        qb_scr[...] = qb
        m_scr[...] = jnp.full(m_scr.shape, NEG_BIG, F32)
        l_scr[...] = jnp.zeros(l_scr.shape, F32)
        acc_scr[...] = jnp.zeros(acc_scr.shape, F32)

    rows = qi * tq + lax.broadcasted_iota(jnp.int32, (tq, tk), 0)
    cols = ki * tk + lax.broadcasted_iota(jnp.int32, (tq, tk), 1)
    _attn_update(qa_scr[...], qb_scr[...], k_ref[0], v_ref[0], c_ref[0], cols <= rows, m_scr, l_scr, acc_scr)

    @pl.when(ki == ((qi + 1) * tq - 1) // tk)
    def _():
        lo_half = lax.broadcasted_iota(jnp.int32, (tq, LANES), 1) < HEAD_DIM
        o_ref[0] = (acc_scr[...] / jnp.where(lo_half, l_scr[0], l_scr[1])).astype(o_ref.dtype)


def _attn_prompt(q, k, v, c, t):
    tq = min(1024, t)
    tk = min(512, t)
    steps = [(qi, ki) for qi in range(t // tq) for ki in range(((qi + 1) * tq - 1) // tk + 1)]
    qi_arr = jnp.asarray(np.array([s[0] for s in steps], np.int32))
    ki_arr = jnp.asarray(np.array([s[1] for s in steps], np.int32))
    grid_spec = pltpu.PrefetchScalarGridSpec(
        num_scalar_prefetch=2,
        grid=(N_PAIRS, len(steps)),
        in_specs=[pl.BlockSpec((1, tq, LANES), lambda p, s, qi, ki: (p, qi[s], 0)),
                  pl.BlockSpec((1, tk, LANES), lambda p, s, qi, ki: (p, ki[s], 0)),
                  pl.BlockSpec((1, tk, LANES), lambda p, s, qi, ki: (p, ki[s], 0)),
                  pl.BlockSpec((1, 2, tk), lambda p, s, qi, ki: (p, 0, ki[s]))],
        out_specs=pl.BlockSpec((1, tq, LANES), lambda p, s, qi, ki: (p, qi[s], 0)),
        scratch_shapes=[pltpu.VMEM((tq, LANES), BF16), pltpu.VMEM((tq, LANES), BF16),
                        pltpu.VMEM((2, tq, LANES), F32), pltpu.VMEM((2, tq, LANES), F32),
                        pltpu.VMEM((tq, LANES), F32)],
    )
    return pl.pallas_call(
        functools.partial(_attn_prompt_kernel, tq=tq, tk=tk),
        grid_spec=grid_spec,
        out_shape=jax.ShapeDtypeStruct((N_PAIRS, t, LANES), BF16),
        compiler_params=_cparams(2),
        name="fox_attn_prompt",
    )(qi_arr, ki_arr, q, k, v, c)


def _attn_sample_kernel(q_ref, ck_ref, cv_ref, kn_ref, vn_ref, cc_ref, cn_ref, o_ref,
                        m_scr, l_scr, acc_scr, *, nkb):
    j = pl.program_id(1)
    ts = q_ref.shape[1]

    @pl.when(j == 0)
    def _():
        m_scr[...] = jnp.full(m_scr.shape, NEG_BIG, F32)
        l_scr[...] = jnp.zeros(l_scr.shape, F32)
        acc_scr[...] = jnp.zeros(acc_scr.shape, F32)

    @pl.when(j < nkb)
    def _():
        for p in range(N_PAIRS):
            sl = slice(p * LANES, (p + 1) * LANES)
            qa, qb = _split_q(q_ref[p])
            _attn_update(qa, qb, ck_ref[0, :, sl].astype(BF16), cv_ref[0, :, sl].astype(BF16),
                         cc_ref[0, 2 * p:2 * p + 2, :], None, m_scr.at[p], l_scr.at[p], acc_scr.at[p])

    @pl.when(j == nkb)
    def _():
        rows = lax.broadcasted_iota(jnp.int32, (ts, ts), 0)
        cols = lax.broadcasted_iota(jnp.int32, (ts, ts), 1)
        lo_half = lax.broadcasted_iota(jnp.int32, (ts, LANES), 1) < HEAD_DIM
        for p in range(N_PAIRS):
            qa, qb = _split_q(q_ref[p])
            _attn_update(qa, qb, kn_ref[p], vn_ref[p], cn_ref[0, 2 * p:2 * p + 2, :], cols <= rows,
                         m_scr.at[p], l_scr.at[p], acc_scr.at[p])
            o_ref[p] = (acc_scr[p] / jnp.where(lo_half, l_scr[p, 0], l_scr[p, 1])).astype(o_ref.dtype)


def _attn_sample(q, k, v, cache_k, cache_v, c_cache, c_new, row0):
    b, past, _ = cache_k.shape
    ts = c_new.shape[2]
    tk = min(1024, past)
    nkb = past // tk
    blk0 = row0 // ts
    qspec = pl.BlockSpec((N_PAIRS, ts, LANES), lambda i, j: (0, blk0 + i, 0))
    cspec = pl.BlockSpec((1, tk, MIX_W), lambda i, j: (i, jnp.minimum(j, nkb - 1), 0))
    return pl.pallas_call(
        functools.partial(_attn_sample_kernel, nkb=nkb),
        grid=(b, nkb + 1),
        in_specs=[qspec, cspec, cspec, qspec, qspec,
                  pl.BlockSpec((1, N_HEADS, tk), lambda i, j: (i, 0, jnp.minimum(j, nkb - 1))),
                  pl.BlockSpec((1, N_HEADS, ts), lambda i, j: (i, 0, 0))],
        out_specs=pl.BlockSpec((N_PAIRS, ts, LANES), lambda i, j: (0, i, 0)),
        out_shape=jax.ShapeDtypeStruct((N_PAIRS, b * ts, LANES), BF16),
        scratch_shapes=[pltpu.VMEM((N_PAIRS, 2, ts, LANES), F32), pltpu.VMEM((N_PAIRS, 2, ts, LANES), F32),
                        pltpu.VMEM((N_PAIRS, ts, LANES), F32)],
        compiler_params=_cparams(2),
        name="fox_attn_sample",
    )(q, cache_k, cache_v, k, v, c_cache, c_new)


def _mix_out_kernel(x_ref, ys_ref, bn_ref, g_ref, yf_ref, ln_ref, wo_ref, gf_ref, rw_ref, rb_ref,
                    x1_o, hf_o, ti_o, tg_o):
    y = jnp.concatenate([ys_ref[p] for p in range(N_PAIRS)], axis=1)
    inv = 1.0 / HEAD_DIM
    mu = _head_sum(y) * inv
    d = y - mu
    var = _head_sum(d * d) * inv
    yn = d * lax.rsqrt(var + GN_EPS) * ln_ref[0:1, :] + ln_ref[1:2, :]
    yr = ((yn + bn_ref[...]) * g_ref[...]).astype(BF16)
    yf = jnp.concatenate([yf_ref[p] for p in range(N_PAIRS)], axis=1)
    mix = jnp.concatenate([yr, yf], axis=1)
    x1 = x_ref[...] + jnp.dot(mix, wo_ref[...], preferred_element_type=F32)
    x1_o[...] = x1
    hf = _rms(x1, gf_ref[...])
    hf_o[...] = hf

    logits = jnp.dot(hf, rw_ref[...], precision=HI, preferred_element_type=F32) + rb_ref[...]
    lane_e = lax.broadcasted_iota(jnp.int32, logits.shape, 1).astype(F32)
    vals = []
    idxs = []
    cur = logits
    for _ in range(TOP_K):
        m = jnp.max(cur, axis=1, keepdims=True)
        am = jnp.min(jnp.where(cur == m, lane_e, float(N_EXPERTS)), axis=1, keepdims=True)
        vals.append(m)
        idxs.append(am)
        cur = jnp.where(lane_e == am, -jnp.inf, cur)
    es = [jnp.exp(vv - vals[0]) for vv in vals]
    tot = es[0] + es[1] + es[2] + es[3]
    lane = lax.broadcasted_iota(jnp.int32, ti_o.shape, 1)
    ti = jnp.zeros(ti_o.shape, F32)
    tg = jnp.zeros(tg_o.shape, F32)
    for kk in range(TOP_K):
        ti = jnp.where(lane == kk, idxs[kk], ti)
        tg = jnp.where(lane == kk, es[kk] / tot, tg)
    ti_o[...] = ti.astype(jnp.int32)
    tg_o[...] = tg


def _mix_out(x, ys, bonus, g, yf, ln, wo_bf, gf, rw, rb):
    n = x.shape[0]
    tm = min(256, n)
    row = lambda w: pl.BlockSpec((tm, w), lambda i: (i, 0))
    pspec = pl.BlockSpec((N_PAIRS, tm, LANES), lambda i: (0, i, 0))
    full = lambda a: pl.BlockSpec(a.shape, lambda i: (0,) * a.ndim)
    return pl.pallas_call(
        _mix_out_kernel,
        grid=(n // tm,),
        in_specs=[row(D_MODEL), pspec, row(MIX_W), row(MIX_W), pspec, full(ln), full(wo_bf), full(gf),
                  full(rw), full(rb)],
        out_specs=[row(D_MODEL), row(D_MODEL), row(LANES), row(LANES)],
        out_shape=[jax.ShapeDtypeStruct((n, D_MODEL), F32), jax.ShapeDtypeStruct((n, D_MODEL), F32),
                   jax.ShapeDtypeStruct((n, LANES), jnp.int32), jax.ShapeDtypeStruct((n, LANES), F32)],
        compiler_params=_cparams(1),
        name="mix_out_router",
    )(x, ys, bonus, g, yf, ln, wo_bf, gf, rw, rb)


def _moe_gather_kernel(tok_ref, hf_hbm, o_ref, buf, sem):
    i = pl.program_id(0)
    nb = pl.num_programs(0)
    rows = o_ref.shape[0]

    def row_copy(blk, slot, r):
        t = tok_ref[blk * rows + r]
        return pltpu.make_async_copy(hf_hbm.at[pl.ds(t, 1)], buf.at[slot, pl.ds(r, 1)], sem.at[slot])

    def issue(blk, slot):
        def body(r, carry):
            row_copy(blk, slot, r).start()
            return carry
        lax.fori_loop(0, rows, body, 0)

    @pl.when(i == 0)
    def _():
        issue(0, 0)

    @pl.when(i + 1 < nb)
    def _():
        issue(i + 1, (i + 1) % 2)

    slot = i % 2

    def wait_body(r, carry):
        row_copy(i, slot, r).wait()
        return carry
    lax.fori_loop(0, rows, wait_body, 0)
    o_ref[...] = buf[slot].astype(BF16)


def _moe_gather(row_tok, hf, n_blocks):
    grid_spec = pltpu.PrefetchScalarGridSpec(
        num_scalar_prefetch=1,
        grid=(n_blocks,),
        in_specs=[pl.BlockSpec(memory_space=pl.ANY)],
        out_specs=pl.BlockSpec((MOE_ROWS, D_MODEL), lambda i, tok: (i, 0)),
        scratch_shapes=[pltpu.VMEM((2, MOE_ROWS, D_MODEL), F32), pltpu.SemaphoreType.DMA((2,))],
    )
    return pl.pallas_call(
        _moe_gather_kernel,
        grid_spec=grid_spec,
        out_shape=jax.ShapeDtypeStruct((n_blocks * MOE_ROWS, D_MODEL), BF16),
        compiler_params=_cparams(1),
        name="moe_gather",
    )(row_tok, hf)


STEP_RUN, STEP_NEW_WEIGHTS, STEP_ZERO = 0, 1, 2


def _moe_gu_kernel(se, sw, sb, sj, sf, x_ref, wg_ref, wu_ref, bg_ref, bu_ref, o_ref, wg_bf, wu_bf):
    s = pl.program_id(0)

    @pl.when(sf[s] == STEP_NEW_WEIGHTS)
    def _():
        wg_bf[...] = wg_ref[0].astype(BF16)
        wu_bf[...] = wu_ref[0].astype(BF16)

    @pl.when(sf[s] != STEP_ZERO)
    def _():
        x = x_ref[...]
        g = jnp.dot(x, wg_bf[...], preferred_element_type=F32) + bg_ref[0]
        u = jnp.dot(x, wu_bf[...], preferred_element_type=F32) + bu_ref[0]
        g = jnp.minimum(g, SWIGLU_LIMIT)
        u = jnp.clip(u, -SWIGLU_LIMIT, SWIGLU_LIMIT)
        o_ref[...] = ((u + 1.0) * (g * jax.nn.sigmoid(SWIGLU_ALPHA * g))).astype(BF16)

    @pl.when(sf[s] == STEP_ZERO)
    def _():
        o_ref[...] = jnp.zeros(o_ref.shape, o_ref.dtype)


def _moe_gate_up(sched, xs, w_gu, b_gu, n_blocks):
    nt = D_EXPERT // MOE_TN
    n_steps = nt * n_blocks
    wspec = lambda off: pl.BlockSpec((1, D_MODEL, MOE_TN), lambda s, se, sw, sb, sj, sf: (se[s], 0, off + sw[s]))
    bspec = lambda off: pl.BlockSpec((1, 1, MOE_TN), lambda s, se, sw, sb, sj, sf: (se[s], 0, off + sw[s]))
    grid_spec = pltpu.PrefetchScalarGridSpec(
        num_scalar_prefetch=5,
        grid=(n_steps,),
        in_specs=[pl.BlockSpec((MOE_ROWS, D_MODEL), lambda s, se, sw, sb, sj, sf: (sb[s], 0)),
                  wspec(0), wspec(nt), bspec(0), bspec(nt)],
        out_specs=pl.BlockSpec((MOE_ROWS, MOE_TN), lambda s, se, sw, sb, sj, sf: (sb[s], sj[s])),
        scratch_shapes=[pltpu.VMEM((D_MODEL, MOE_TN), BF16), pltpu.VMEM((D_MODEL, MOE_TN), BF16)],
    )
    return pl.pallas_call(
        _moe_gu_kernel,
        grid_spec=grid_spec,
        out_shape=jax.ShapeDtypeStruct((n_blocks * MOE_ROWS, D_EXPERT), BF16),
        compiler_params=_cparams(1),
        name="moe_gate_up",
    )(*sched, xs, w_gu, w_gu, b_gu, b_gu)


def _moe_dn_kernel(se, sw, sb, sj, sf, h_ref, wd_ref, bd_ref, o_ref, wd_bf):
    s = pl.program_id(0)

    @pl.when(sf[s] == STEP_NEW_WEIGHTS)
    def _():
        wd_bf[...] = wd_ref[0].astype(BF16)

    @pl.when(sf[s] != STEP_ZERO)
    def _():
        o_ref[...] = jnp.dot(h_ref[...], wd_bf[...], preferred_element_type=F32) + bd_ref[0]

    @pl.when(sf[s] == STEP_ZERO)
    def _():
        o_ref[...] = jnp.zeros(o_ref.shape, o_ref.dtype)


def _moe_down(sched, hid, w_dn, b_dn, n_blocks):
    nt = D_MODEL // MOE_TN
    n_steps = nt * n_blocks
    grid_spec = pltpu.PrefetchScalarGridSpec(
        num_scalar_prefetch=5,
        grid=(n_steps,),
        in_specs=[pl.BlockSpec((MOE_ROWS, D_EXPERT), lambda s, se, sw, sb, sj, sf: (sb[s], 0)),
                  pl.BlockSpec((1, D_EXPERT, MOE_TN), lambda s, se, sw, sb, sj, sf: (se[s], 0, sw[s])),
                  pl.BlockSpec((1, 1, MOE_TN), lambda s, se, sw, sb, sj, sf: (se[s], 0, sw[s]))],
        out_specs=pl.BlockSpec((MOE_ROWS, MOE_TN), lambda s, se, sw, sb, sj, sf: (sb[s], sj[s])),
        scratch_shapes=[pltpu.VMEM((D_EXPERT, MOE_TN), BF16)],
    )
    return pl.pallas_call(
        _moe_dn_kernel,
        grid_spec=grid_spec,
        out_shape=jax.ShapeDtypeStruct((n_blocks * MOE_ROWS, D_MODEL), F32),
        compiler_params=_cparams(1),
        name="moe_down",
    )(*sched, hid, w_dn, b_dn)


def _moe_combine_kernel(pos_ref, ys_hbm, x1_ref, tg_ref, o_ref, buf, sem):
    i = pl.program_id(0)
    nb = pl.num_programs(0)
    tm = o_ref.shape[0]

    def row_copy(blk, slot, r, kk):
        src = pos_ref[(blk * tm + r) * TOP_K + kk]
        return pltpu.make_async_copy(ys_hbm.at[pl.ds(src, 1)], buf.at[slot, kk, pl.ds(r, 1)], sem.at[slot])

    def issue(blk, slot):
        def body(r, carry):
            for kk in range(TOP_K):
                row_copy(blk, slot, r, kk).start()
            return carry
        lax.fori_loop(0, tm, body, 0)

    @pl.when(i == 0)
    def _():
        issue(0, 0)

    @pl.when(i + 1 < nb)
    def _():
        issue(i + 1, (i + 1) % 2)

    slot = i % 2

    def wait_body(r, carry):
        for kk in range(TOP_K):
            row_copy(i, slot, r, kk).wait()
        return carry
    lax.fori_loop(0, tm, wait_body, 0)

    tg = tg_ref[...]
    acc = x1_ref[...]
    for kk in range(TOP_K):
        acc = acc + tg[:, kk:kk + 1] * buf[slot, kk]
    o_ref[...] = acc


def _moe_combine(pos, ys, x1, tg):
    n = x1.shape[0]
    tm = min(128, n)
    grid_spec = pltpu.PrefetchScalarGridSpec(
        num_scalar_prefetch=1,
        grid=(n // tm,),
        in_specs=[pl.BlockSpec(memory_space=pl.ANY),
                  pl.BlockSpec((tm, D_MODEL), lambda i, pos: (i, 0)),
                  pl.BlockSpec((tm, LANES), lambda i, pos: (i, 0))],
        out_specs=pl.BlockSpec((tm, D_MODEL), lambda i, pos: (i, 0)),
        scratch_shapes=[pltpu.VMEM((2, TOP_K, tm, D_MODEL), F32), pltpu.SemaphoreType.DMA((2,))],
    )
    return pl.pallas_call(
        _moe_combine_kernel,
        grid_spec=grid_spec,
        out_shape=jax.ShapeDtypeStruct((n, D_MODEL), F32),
        compiler_params=_cparams(1),
        name="moe_combine",
    )(pos, ys, x1, tg)


def _moe_schedule(top_idx, n_blocks):
    n = top_idx.shape[0]
    n_rows = n * TOP_K
    nt = D_EXPERT // MOE_TN
    flat_e = top_idx.reshape(n_rows)
    onehot = (flat_e[:, None] == jnp.arange(N_EXPERTS, dtype=jnp.int32)[None, :]).astype(jnp.int32)
    cum = jnp.cumsum(onehot, axis=0)
    rank = jnp.take_along_axis(cum, flat_e[:, None], axis=1)[:, 0] - 1
    counts = cum[-1]
    nb_e = (counts + MOE_ROWS - 1) // MOE_ROWS
    blk_end = jnp.cumsum(nb_e)
    blk_start = blk_end - nb_e
    pos = (blk_start[flat_e] * MOE_ROWS + rank).astype(jnp.int32)
    row_tok = jnp.zeros((n_blocks * MOE_ROWS,), jnp.int32).at[pos].set(
        jnp.arange(n_rows, dtype=jnp.int32) // TOP_K)
    used = blk_end[-1]
    s = jnp.arange(nt * n_blocks, dtype=jnp.int32)
    live = s < nt * used
    s_eff = jnp.minimum(s, nt * used - 1)
    e = jnp.minimum(jnp.searchsorted(nt * blk_end, s_eff, side="right"), N_EXPERTS - 1).astype(jnp.int32)
    local = s_eff - nt * blk_start[e]
    nbe = jnp.maximum(nb_e[e], 1)
    sw = (local // nbe).astype(jnp.int32)
    tail = s - nt * used
    sb = jnp.where(live, blk_start[e] + local % nbe, used + tail // nt).astype(jnp.int32)
    sj = jnp.where(live, sw, tail % nt).astype(jnp.int32)
    sf = jnp.where(live, jnp.where(local % nbe == 0, STEP_NEW_WEIGHTS, STEP_RUN), STEP_ZERO).astype(jnp.int32)
    return pos, row_tok, (e, sw, sb, sj, sf)


def _ple_kernel(x_ref, p_ref, g_ref, wg_ref, wp_ref, o_ref):
    x = x_ref[...]
    h = _rms(x, g_ref[...]).astype(BF16)
    gate = jax.nn.sigmoid(jnp.dot(h, wg_ref[...], preferred_element_type=F32))
    pp = jnp.dot(p_ref[...].astype(BF16), wp_ref[...], preferred_element_type=F32)
    o_ref[...] = x + gate * pp


def _ple(x, p, g, wg_bf, wp_bf):
    n = x.shape[0]
    tm = min(256, n)
    full = lambda a: pl.BlockSpec(a.shape, lambda i: (0,) * a.ndim)
    return pl.pallas_call(
        _ple_kernel,
        grid=(n // tm,),
        in_specs=[pl.BlockSpec((tm, D_MODEL), lambda i: (i, 0)), pl.BlockSpec((tm, PLE_DIM), lambda i: (i, 0)),
                  full(g), full(wg_bf), full(wp_bf)],
        out_specs=pl.BlockSpec((tm, D_MODEL), lambda i: (i, 0)),
        out_shape=jax.ShapeDtypeStruct((n, D_MODEL), F32),
        compiler_params=_cparams(1),
        name="ple_gate",
    )(x, p, g, wg_bf, wp_bf)


def _pairs_from_state(s):
    b = s.shape[0]
    st = jnp.swapaxes(s, 2, 3).reshape(b, N_PAIRS, 2, HEAD_DIM, HEAD_DIM)
    return jnp.transpose(st, (0, 1, 3, 2, 4)).reshape(b, N_PAIRS, HEAD_DIM, LANES)


def _state_from_pairs(sp):
    b = sp.shape[0]
    st = sp.reshape(b, N_PAIRS, HEAD_DIM, 2, HEAD_DIM)
    st = jnp.transpose(st, (0, 1, 3, 2, 4)).reshape(b, N_HEADS, HEAD_DIM, HEAD_DIM)
    return jnp.swapaxes(st, 2, 3)


def _chunk_t(x):
    return jnp.swapaxes(x.reshape(x.shape[0] // CHUNK, CHUNK, MIX_W), 1, 2)


def _layer(x_p, x_s, cache_k, cache_v, cache_lf, state, shift, p_all, lw):
    t_p = x_p.shape[1]
    b_s, t_s, _ = x_s.shape
    n_p = x_p.shape[0] * t_p
    n_s = b_s * t_s
    n = n_p + n_s
    past = cache_k.shape[1]
    x = jnp.concatenate([x_p.reshape(n_p, D_MODEL), x_s.reshape(n_s, D_MODEL)], axis=0)

    w_in_bf = jnp.pad(lw["w_in"], ((0, 0), (0, Z_PAD - lw["w_in"].shape[1]))).astype(BF16)
    z = _in_proj(x, lw["norm_mix_g"], w_in_bf)

    zero_row = jnp.zeros((1, 1, RWKV_PROJ), F32)
    prev = jnp.concatenate([zero_row, shift.astype(F32)], axis=0)
    vecs = jnp.zeros((8, MIX_W), F32)
    vecs = vecs.at[0].set(lw["rwkv_w0"]).at[1].set(lw["rwkv_a0"]).at[2].set(lw["rwkv_kk"])
    vecs = vecs.at[3].set(lw["rwkv_ka"]).at[4].set(lw["rwkv_rk"].reshape(MIX_W))
    wl = jnp.zeros((256, 3 * MIX_W), F32)
    wl = wl.at[0:64, 0:MIX_W].set(lw["rwkv_w2"]).at[64:128, MIX_W:2 * MIX_W].set(lw["rwkv_a2"])
    wl = wl.at[128:256, 2 * MIX_W:].set(lw["rwkv_g2"])
    r, w, k, kk, bb, v_pm, g, bonus = _rwkv_pre(z, prev, lw["rwkv_mu"].reshape(1, RWKV_PROJ), vecs, wl, n_p)
    s0 = jnp.concatenate([jnp.zeros((1, N_PAIRS, HEAD_DIM, LANES), F32), _pairs_from_state(state.astype(F32))], axis=0)
    y_scan, s_out = _rwkv_scan(_chunk_t(r), _chunk_t(w), _chunk_t(k), _chunk_t(kk), _chunk_t(bb), v_pm, s0, n_p)
    s_new = _state_from_pairs(s_out)
    shift_new_p = z[n_p - 1:n_p, :RWKV_PROJ].reshape(1, 1, RWKV_PROJ)
    shift_new_s = z[n_p:, :RWKV_PROJ].reshape(b_s, t_s, RWKV_PROJ)[:, -1:, :]

    fvecs = jnp.zeros((8, MIX_W), F32)
    fvecs = fvecs.at[0].set(jnp.tile(lw["fox_q_g"], N_HEADS)).at[1].set(jnp.tile(lw["fox_k_g"], N_HEADS))
    bf = jnp.zeros((1, LANES), F32).at[0, :N_HEADS].set(lw["fox_b_f"])
    q_pm, k_pm, vv_pm, k_new, v_new, lf = _fox_pre(z, fvecs, bf)
    lf = lf[:, :N_HEADS]
    lf_p = lf[:n_p].T.reshape(1, N_HEADS, n_p)
    c_p = _cumsum_lanes(lf_p).reshape(N_PAIRS, 2, n_p)
    yf_p = _attn_prompt(q_pm, k_pm, vv_pm, c_p, n_p)
    lf_s = jnp.swapaxes(lf[n_p:].reshape(b_s, t_s, N_HEADS), 1, 2)
    lf_all = jnp.concatenate([jnp.swapaxes(cache_lf.astype(F32), 1, 2), lf_s], axis=2)
    pad = (-lf_all.shape[2]) % LANES
    c_all = _cumsum_lanes(jnp.pad(lf_all, ((0, 0), (0, 0), (0, pad))))
    yf_s = _attn_sample(q_pm, k_pm, vv_pm, cache_k.reshape(b_s, past, MIX_W), cache_v.reshape(b_s, past, MIX_W),
                        c_all[:, :, :past], c_all[:, :, past:past + t_s], n_p)
    yf = jnp.concatenate([yf_p, yf_s], axis=1)

    ln = jnp.stack([lw["rwkv_ln_g"], lw["rwkv_ln_b"]])
    x1, hf, ti, tg = _mix_out(x, y_scan, bonus, g, yf, ln, lw["w_out"].astype(BF16),
                              lw["norm_ffn_g"].reshape(1, D_MODEL), lw["router_w"], lw["router_b"].reshape(1, N_EXPERTS))

    n_blocks = n * TOP_K // MOE_ROWS + N_EXPERTS
    pos, row_tok, sched = _moe_schedule(ti[:, :TOP_K], n_blocks)
    xs = _moe_gather(row_tok, hf, n_blocks)
    hid = _moe_gate_up(sched, xs, lw["expert_w_gu"], lw["expert_b_gu"].reshape(N_EXPERTS, 1, 2 * D_EXPERT), n_blocks)
    ys = _moe_down(sched, hid, lw["expert_w_down"], lw["expert_b_down"].reshape(N_EXPERTS, 1, D_MODEL), n_blocks)
    x2 = _moe_combine(pos, ys, x1, tg)

    y = _ple(x2, p_all, lw["ple_norm_g"].reshape(1, D_MODEL), lw["ple_w_gate"].astype(BF16),
             lw["ple_w_proj"].astype(BF16))

    heads = lambda a, bsz, t: a.reshape(bsz, t, N_HEADS, HEAD_DIM)
    out_p = (y[:n_p].reshape(x_p.shape), heads(k_new[:n_p], 1, n_p), heads(v_new[:n_p], 1, n_p),
             lf[:n_p].reshape(1, n_p, N_HEADS), s_new[:1], shift_new_p)
    out_s = (y[n_p:].reshape(x_s.shape), heads(k_new[n_p:], b_s, t_s), heads(v_new[n_p:], b_s, t_s),
             lf[n_p:].reshape(b_s, t_s, N_HEADS), s_new[1:], shift_new_s)
    return out_p, out_s


def kernel(x_prompt, x_sample, cache_fox_k, cache_fox_v, cache_fox_logf, state_rwkv, state_rwkv_shift, p_prompt, p_sample, norm_mix_g, w_in, rwkv_mu, rwkv_w0, rwkv_w2, rwkv_a0, rwkv_a2, rwkv_g2, rwkv_kk, rwkv_ka, rwkv_rk, rwkv_ln_g, rwkv_ln_b, fox_q_g, fox_k_g, fox_b_f, w_out, norm_ffn_g, router_w, router_b, expert_w_gu, expert_b_gu, expert_w_down, expert_b_down, ple_norm_g, ple_w_gate, ple_w_proj):
    assert x_prompt.shape[0] == 1 and w_in.shape[0] == 1, "one prompt stream, one layer"
    lw = dict(norm_mix_g=norm_mix_g[0], w_in=w_in[0], rwkv_mu=rwkv_mu[0], rwkv_w0=rwkv_w0[0], rwkv_w2=rwkv_w2[0],
              rwkv_a0=rwkv_a0[0], rwkv_a2=rwkv_a2[0], rwkv_g2=rwkv_g2[0], rwkv_kk=rwkv_kk[0], rwkv_ka=rwkv_ka[0],
              rwkv_rk=rwkv_rk[0], rwkv_ln_g=rwkv_ln_g[0], rwkv_ln_b=rwkv_ln_b[0], fox_q_g=fox_q_g[0],
              fox_k_g=fox_k_g[0], fox_b_f=fox_b_f[0], w_out=w_out[0], norm_ffn_g=norm_ffn_g[0],
              router_w=router_w[0], router_b=router_b[0], expert_w_gu=expert_w_gu[0], expert_b_gu=expert_b_gu[0],
              expert_w_down=expert_w_down[0], expert_b_down=expert_b_down[0], ple_norm_g=ple_norm_g[0],
              ple_w_gate=ple_w_gate[0], ple_w_proj=ple_w_proj[0])
    n_p = x_prompt.shape[1]
    p_all = jnp.concatenate([p_prompt[0].reshape(n_p, PLE_DIM), p_sample[0].reshape(-1, PLE_DIM)], axis=0)
    (y_p, k_p, v_p, lf_p, s_p, sh_p), (y_s, k_s, v_s, lf_s, s_s, sh_s) = _layer(
        x_prompt, x_sample, cache_fox_k[0], cache_fox_v[0], cache_fox_logf[0], state_rwkv[0],
        state_rwkv_shift[0], p_all, lw)
    add = lambda a: a[None]
    return (y_p, y_s, add(k_p), add(v_p), add(lf_p), add(s_p), add(sh_p),
            add(k_s), add(v_s), add(lf_s), add(s_s), add(sh_s))
```

```python
import functools

import numpy as np
import jax
import jax.numpy as jnp
from jax import lax
from jax.experimental import pallas as pl
from jax.experimental.pallas import tpu as pltpu

F32 = jnp.float32
BF16 = jnp.bfloat16
HI = lax.Precision.HIGHEST

D_MODEL = 2048
HEAD_DIM = 64
N_HEADS = 16
N_PAIRS = N_HEADS // 2
MIX_W = N_HEADS * HEAD_DIM
CHUNK = 64
RWKV_PROJ = 3 * MIX_W + 64 + 64 + 128
FOX_PROJ = 3 * MIX_W + N_HEADS
Z_HALF = RWKV_PROJ
Z_PAD = 2 * Z_HALF
N_EXPERTS = 32
TOP_K = 4
D_EXPERT = 2048
SWIGLU_LIMIT = 7.0
SWIGLU_ALPHA = 1.702
PLE_DIM = 256
RMS_EPS = 1e-6
GN_EPS = 64e-5
L2_EPS = 1e-12
NEG_BIG = -1e30
LOG2E = 1.4426950408889634

LANES = 128
MOE_ROWS = 256
MOE_TN = 512
VMEM_LIMIT = 52 * 1024 * 1024


def _cparams(n_axes, vmem=VMEM_LIMIT):
    return pltpu.CompilerParams(dimension_semantics=("arbitrary",) * n_axes, vmem_limit_bytes=vmem)


def _head_sum(x):
    r = lax.broadcasted_iota(jnp.int32, (LANES, LANES), 0) // HEAD_DIM
    c = lax.broadcasted_iota(jnp.int32, (LANES, LANES), 1) // HEAD_DIM
    bd = (r == c).astype(F32)
    parts = [jnp.dot(x[:, i * LANES:(i + 1) * LANES], bd, precision=HI, preferred_element_type=F32)
             for i in range(x.shape[1] // LANES)]
    return parts[0] if len(parts) == 1 else jnp.concatenate(parts, axis=1)


def _log_sigmoid(x):
    return jnp.minimum(x, 0.0) - jnp.log1p(jnp.exp(-jnp.abs(x)))


def _rms(x, g):
    ms = jnp.mean(x * x, axis=-1, keepdims=True)
    return x * lax.rsqrt(ms + RMS_EPS) * g


def _inproj_kernel(x_ref, g_ref, w_ref, o_ref, h_scr):
    @pl.when(pl.program_id(1) == 0)
    def _():
        h_scr[...] = _rms(x_ref[...], g_ref[...]).astype(BF16)

    o_ref[...] = jnp.dot(h_scr[...], w_ref[...], preferred_element_type=F32)


def _in_proj(x, g, w_bf):
    n = x.shape[0]
    tm = min(512, n)
    tn = 512
    return pl.pallas_call(
        _inproj_kernel,
        grid=(n // tm, Z_PAD // tn),
        in_specs=[pl.BlockSpec((tm, D_MODEL), lambda i, j: (i, 0)),
                  pl.BlockSpec((1, D_MODEL), lambda i, j: (0, 0)),
                  pl.BlockSpec((D_MODEL, tn), lambda i, j: (0, j))],
        out_specs=pl.BlockSpec((tm, tn), lambda i, j: (i, j)),
        out_shape=jax.ShapeDtypeStruct((n, Z_PAD), F32),
        scratch_shapes=[pltpu.VMEM((tm, D_MODEL), BF16)],
        compiler_params=_cparams(2),
        name="in_proj",
    )(x, g.reshape(1, D_MODEL), w_bf)


def _rwkv_pre_kernel(z_ref, prev_ref, mu_ref, vec_ref, wl_ref,
                     r_o, w_o, k_o, kk_o, b_o, v_o, g_o, bn_o, carry, *, n_prompt_tiles):
    i = pl.program_id(0)
    z = z_ref[...]
    tm = z.shape[0]

    @pl.when(jnp.logical_or(i == 0, i >= n_prompt_tiles))
    def _():
        carry[...] = prev_ref[0]

    prev = carry[...]
    rolled = pltpu.roll(z, 1, axis=0)
    row = lax.broadcasted_iota(jnp.int32, z.shape, 0)
    shifted = jnp.where(row == 0, prev, rolled)
    carry[...] = z[tm - 1:tm, :]
    zs = z + mu_ref[...] * (shifted - z)

    r = zs[:, 0:MIX_W]
    k = zs[:, MIX_W:2 * MIX_W]
    v = zs[:, 2 * MIX_W:3 * MIX_W]
    lo = zs[:, 3 * MIX_W:RWKV_PROJ]
    lane = lax.broadcasted_iota(jnp.int32, lo.shape, 1)
    f = jnp.where(lane < 64, jnp.tanh(lo), jnp.where(lane < 128, lo, jax.nn.sigmoid(lo)))
    lora = jnp.dot(f, wl_ref[...], precision=HI, preferred_element_type=F32)
    w_pre = vec_ref[0:1, :] + lora[:, 0:MIX_W]
    log_decay = -jnp.exp(_log_sigmoid(w_pre) - 0.5)
    a = jax.nn.sigmoid(vec_ref[1:2, :] + lora[:, MIX_W:2 * MIX_W])
    g = lora[:, 2 * MIX_W:3 * MIX_W]
    kk = k * vec_ref[2:3, :]
    kk = kk * lax.rsqrt(_head_sum(kk * kk) + L2_EPS)
    k_mod = k * (1.0 + (a - 1.0) * vec_ref[3:4, :])
    bonus = _head_sum(r * k_mod * vec_ref[4:5, :]) * v
    kka = kk * a

    g_o[...] = g
    bn_o[...] = bonus
    for p in range(N_PAIRS):
        sl = slice(p * LANES, (p + 1) * LANES)
        r_o[p] = r[:, sl]
        w_o[p] = log_decay[:, sl]
        k_o[p] = k_mod[:, sl]
        kk_o[p] = kk[:, sl]
        b_o[p] = kka[:, sl]
        v_o[p] = v[:, sl]


def _rwkv_pre(z, prev, mu, vecs, wl, n_prompt):
    n = z.shape[0]
    tm = CHUNK
    nt = n // tm
    n_prompt_tiles = n_prompt // tm
    tok = jax.ShapeDtypeStruct((n, MIX_W), F32)
    tspec = pl.BlockSpec((tm, MIX_W), lambda i: (i, 0))
    pm = jax.ShapeDtypeStruct((N_PAIRS, n, LANES), F32)
    pspec = pl.BlockSpec((N_PAIRS, tm, LANES), lambda i: (0, i, 0))
    return pl.pallas_call(
        functools.partial(_rwkv_pre_kernel, n_prompt_tiles=n_prompt_tiles),
        grid=(nt,),
        in_specs=[pl.BlockSpec((tm, Z_HALF), lambda i: (i, 0)),
                  pl.BlockSpec((1, 1, RWKV_PROJ), lambda i: (jnp.maximum(i - (n_prompt_tiles - 1), 0), 0, 0)),
                  pl.BlockSpec((1, RWKV_PROJ), lambda i: (0, 0)),
                  pl.BlockSpec((8, MIX_W), lambda i: (0, 0)),
                  pl.BlockSpec((256, 3 * MIX_W), lambda i: (0, 0))],
        out_specs=[pspec, pspec, pspec, pspec, pspec, pspec, tspec, tspec],
        out_shape=[pm, pm, pm, pm, pm, pm, tok, tok],
        scratch_shapes=[pltpu.VMEM((1, RWKV_PROJ), F32)],
        compiler_params=_cparams(1),
        name="rwkv_pre",
    )(z, prev, mu, vecs, wl)


def _dot(a, b):
    return jnp.dot(a, b, precision=HI, preferred_element_type=F32)


def _bdot(a, b):
    return jnp.dot(a, b, preferred_element_type=F32)


def _split_bf16(x):
    hi = x.astype(BF16)
    return hi, (x - hi.astype(F32)).astype(BF16)


def _dot3(a, b):
    a_hi, a_lo = _split_bf16(a)
    b_hi, b_lo = _split_bf16(b)
    return _bdot(a_hi, b_hi) + (_bdot(a_hi, b_lo) + _bdot(a_lo, b_hi))


def _pair_rows(x):
    lo_half = lax.broadcasted_iota(jnp.int32, x.shape, 1) < HEAD_DIM
    return jnp.concatenate([jnp.where(lo_half, x, 0.0), jnp.where(lo_half, 0.0, x)], axis=0)


def _scan_chunk(P, r, lw, k, kk, b, v):
    c2 = 2 * CHUNK
    i = lax.broadcasted_iota(jnp.int32, (c2, c2), 0)
    j = lax.broadcasted_iota(jnp.int32, (c2, c2), 1)
    ti = lax.broadcasted_iota(jnp.int32, (CHUNK, CHUNK), 0)
    tj = lax.broadcasted_iota(jnp.int32, (CHUNK, CHUNK), 1)
    cl = _dot((tj <= ti).astype(F32), lw)
    yield
    g_end = cl[CHUNK - 1:CHUNK, :]
    e_neg = jnp.exp(-cl)
    e_end = jnp.exp(g_end - cl)
    kap = _pair_rows(kk * jnp.exp(cl - lw)).astype(BF16)
    rt = _pair_rows(r * jnp.exp(cl)).astype(BF16)
    bt = _pair_rows(b * e_neg).astype(BF16)
    kt = _pair_rows(k * e_neg).astype(BF16)
    kh = _pair_rows(k * e_end)
    bh = _pair_rows(b * e_end)
    vv = _pair_rows(v)
    vv_b = vv.astype(BF16)
    p_b = P.astype(BF16)

    g = lax.dot_general(jnp.concatenate([kap, rt], axis=0), jnp.concatenate([bt, kt], axis=0),
                        (((1,), (1,)), ((), ())), preferred_element_type=F32)
    yield
    strict = j < i
    incl = j <= i
    a_b = jnp.where(strict, g[:c2, :c2], 0.0)
    a_bb = a_b.astype(BF16)
    a_k = jnp.where(strict, g[:c2, c2:], 0.0).astype(BF16)
    r_b = jnp.where(incl, g[c2:, :c2], 0.0).astype(BF16)
    r_k = jnp.where(incl, g[c2:, c2:], 0.0).astype(BF16)

    t_inv = (i == j).astype(F32) - jnp.where(jnp.logical_and((i & 1) == 1, j == i - 1), a_b, 0.0)
    n = 2
    while n < CHUNK:
        m = jnp.logical_and((i >> n.bit_length()) == (j >> n.bit_length()),
                            jnp.logical_and((i & (2 * n - 1)) >= n, (j & (2 * n - 1)) < n))
        t_b = t_inv.astype(BF16)
        ta = _bdot(t_b, a_bb).astype(BF16)
        yield
        t_inv = t_inv - jnp.where(m, _bdot(ta, t_b), 0.0)
        yield
        n *= 2

    w = _bdot(jnp.concatenate([kap, a_k], axis=1), jnp.concatenate([p_b, vv_b], axis=0))
    yield
    u = _bdot(t_inv.astype(BF16), w.astype(BF16))
    yield
    vu_b = jnp.concatenate([vv_b, u.astype(BF16)], axis=0)
    y2 = _bdot(rt, p_b) + _bdot(jnp.concatenate([r_k, -r_b], axis=1), vu_b)
    y = y2[:CHUNK] + y2[CHUNK:]
    yield
    g_col = jnp.broadcast_to(jnp.exp(g_end), (c2, c2)).T
    p_new = g_col * P + _dot3(jnp.concatenate([kh.T, -bh.T], axis=1), jnp.concatenate([vv, u], axis=0))
    return y, p_new


def _run_interleaved(gens):
    results = [None] * len(gens)
    live = list(range(len(gens)))
    while live:
        for idx in list(live):
            try:
                next(gens[idx])
            except StopIteration as stop:
                results[idx] = stop.value
                live.remove(idx)
    return results


def _scan_kernel(r_ref, w_ref, k_ref, kk_ref, b_ref, v_ref, s0_ref, y_ref, sout_ref, s_scr, *, n_prompt_chunks):
    c = pl.program_id(0)

    @pl.when(jnp.logical_or(c == 0, c >= n_prompt_chunks))
    def _():
        s_scr[...] = s0_ref[0]

    outs = _run_interleaved([
        _scan_chunk(s_scr[p], r_ref[p], w_ref[p], k_ref[p], kk_ref[p], b_ref[p], v_ref[p])
        for p in range(N_PAIRS)])
    for p, (y, p_new) in enumerate(outs):
        y_ref[p] = y
        s_scr[p] = p_new
        sout_ref[0, p] = p_new


def _rwkv_scan(r, w, k, kk, b, v, s0, n_prompt):
    n = r.shape[1]
    npc = n_prompt // CHUNK
    n_seq = s0.shape[0]
    pspec = pl.BlockSpec((N_PAIRS, CHUNK, LANES), lambda c: (0, c, 0))
    sspec = pl.BlockSpec((1, N_PAIRS, LANES, LANES), lambda c: (jnp.maximum(c - (npc - 1), 0), 0, 0, 0))
    return pl.pallas_call(
        functools.partial(_scan_kernel, n_prompt_chunks=npc),
        grid=(n // CHUNK,),
        in_specs=[pspec, pspec, pspec, pspec, pspec, pspec, sspec],
        out_specs=[pspec, sspec],
        out_shape=[jax.ShapeDtypeStruct((N_PAIRS, n, LANES), F32),
                   jax.ShapeDtypeStruct((n_seq, N_PAIRS, LANES, LANES), F32)],
        scratch_shapes=[pltpu.VMEM((N_PAIRS, LANES, LANES), F32)],
        compiler_params=_cparams(1),
        name="rwkv_scan",
    )(r, w, k, kk, b, v, s0)


def _fox_pre_kernel(z_ref, vec_ref, bf_ref, q_o, k_o, v_o, kn_o, vn_o, lf_o):
    z = z_ref[...]
    q = z[:, 0:MIX_W]
    k = z[:, MIX_W:2 * MIX_W]
    v = z[:, 2 * MIX_W:3 * MIX_W]
    fl = z[:, 3 * MIX_W:3 * MIX_W + LANES]
    inv = 1.0 / HEAD_DIM
    qn = q * lax.rsqrt(_head_sum(q * q) * inv + RMS_EPS) * vec_ref[0:1, :]
    kn = k * lax.rsqrt(_head_sum(k * k) * inv + RMS_EPS) * vec_ref[1:2, :]
    qs = (qn * (HEAD_DIM ** -0.5 * LOG2E)).astype(BF16)
    kb = kn.astype(BF16)
    vb = v.astype(BF16)
    for p in range(N_PAIRS):
        sl = slice(p * LANES, (p + 1) * LANES)
        q_o[p] = qs[:, sl]
        k_o[p] = kb[:, sl]
        v_o[p] = vb[:, sl]
    kn_o[...] = kn
    vn_o[...] = v
    lf_o[...] = _log_sigmoid(fl + bf_ref[...])


def _fox_pre(z, vecs, bf):
    n = z.shape[0]
    tm = min(256, n)
    pm = jax.ShapeDtypeStruct((N_PAIRS, n, LANES), BF16)
    pspec = pl.BlockSpec((N_PAIRS, tm, LANES), lambda i: (0, i, 0))
    tok = jax.ShapeDtypeStruct((n, MIX_W), F32)
    tspec = pl.BlockSpec((tm, MIX_W), lambda i: (i, 0))
    return pl.pallas_call(
        _fox_pre_kernel,
        grid=(n // tm,),
        in_specs=[pl.BlockSpec((tm, Z_HALF), lambda i: (i, 1)),
                  pl.BlockSpec((8, MIX_W), lambda i: (0, 0)),
                  pl.BlockSpec((1, LANES), lambda i: (0, 0))],
        out_specs=[pspec, pspec, pspec, tspec, tspec, pl.BlockSpec((tm, LANES), lambda i: (i, 0))],
        out_shape=[pm, pm, pm, tok, tok, jax.ShapeDtypeStruct((n, LANES), F32)],
        compiler_params=_cparams(1),
        name="fox_pre",
    )(z, vecs, bf)


def _cumsum_kernel(x_ref, o_ref):
    r = lax.broadcasted_iota(jnp.int32, (LANES, LANES), 0)
    c = lax.broadcasted_iota(jnp.int32, (LANES, LANES), 1)
    tri = (r <= c).astype(F32)
    carry = jnp.zeros((N_HEADS, 1), F32)
    for i in range(x_ref.shape[2] // LANES):
        sl = slice(i * LANES, (i + 1) * LANES)
        cs = jnp.dot(x_ref[0, :, sl], tri, precision=HI, preferred_element_type=F32) + carry
        o_ref[0, :, sl] = cs
        carry = cs[:, LANES - 1:LANES]


def _cumsum_lanes(x):
    b, h, t = x.shape
    return pl.pallas_call(
        _cumsum_kernel,
        grid=(b,),
        in_specs=[pl.BlockSpec((1, h, t), lambda i: (i, 0, 0))],
        out_specs=pl.BlockSpec((1, h, t), lambda i: (i, 0, 0)),
        out_shape=jax.ShapeDtypeStruct((b, h, t), F32),
        compiler_params=_cparams(1),
        name="cumsum_logf",
    )(x)


def _rep_lanes(m, tk):
    if tk % LANES == 0:
        return m if tk == LANES else jnp.concatenate([m] * (tk // LANES), axis=1)
    return m[:, :tk]


def _attn_update(qa, qb, kb, vb, ck, mask, m_ref, l_ref, acc_ref):
    tk = kb.shape[0]
    lo_half = lax.broadcasted_iota(jnp.int32, acc_ref.shape, 1) < HEAD_DIM
    ck2 = ck * LOG2E
    pv = []
    alphas = []
    for h, qh in enumerate((qa, qb)):
        s = lax.dot_general(qh, kb, (((1,), (1,)), ((), ())), preferred_element_type=F32)
        s = s - ck2[h:h + 1, :]
        if mask is not None:
            s = jnp.where(mask, s, NEG_BIG)
        m_prev = m_ref[h]
        m_next = jnp.maximum(m_prev, jnp.max(s, axis=1, keepdims=True))
        p = jnp.exp2(s - _rep_lanes(m_next, tk))
        alpha = jnp.exp2(m_prev - m_next)
        l_ref[h] = alpha * l_ref[h] + jnp.sum(p, axis=1, keepdims=True)
        m_ref[h] = m_next
        pv.append(jnp.dot(p.astype(BF16), vb, preferred_element_type=F32))
        alphas.append(alpha)
    acc_ref[...] = acc_ref[...] * jnp.where(lo_half, alphas[0], alphas[1]) + jnp.where(lo_half, pv[0], pv[1])


def _split_q(q):
    qf = q.astype(F32)
    lo_half = lax.broadcasted_iota(jnp.int32, qf.shape, 1) < HEAD_DIM
    return jnp.where(lo_half, qf, 0.0).astype(BF16), jnp.where(lo_half, 0.0, qf).astype(BF16)


def _attn_prompt_kernel(qi_ref, ki_ref, q_ref, k_ref, v_ref, c_ref, o_ref,
                        qa_scr, qb_scr, m_scr, l_scr, acc_scr, *, tq, tk):
    s_id = pl.program_id(1)
    qi = qi_ref[s_id]
    ki = ki_ref[s_id]

    @pl.when(ki == 0)
    def _():
        qa, qb = _split_q(q_ref[0])
        qa_scr[...] = qa
        qb_scr[...] = qb
        m_scr[...] = jnp.full(m_scr.shape, NEG_BIG, F32)
        l_scr[...] = jnp.zeros(l_scr.shape, F32)
        acc_scr[...] = jnp.zeros(acc_scr.shape, F32)

    crosses_diagonal = ki * tk + (tk - 1) > qi * tq

    @pl.when(crosses_diagonal)
    def _():
        rows = qi * tq + lax.broadcasted_iota(jnp.int32, (tq, tk), 0)
        cols = ki * tk + lax.broadcasted_iota(jnp.int32, (tq, tk), 1)
        _attn_update(qa_scr[...], qb_scr[...], k_ref[0], v_ref[0], c_ref[0], cols <= rows, m_scr, l_scr, acc_scr)

    @pl.when(jnp.logical_not(crosses_diagonal))
    def _():
        _attn_update(qa_scr[...], qb_scr[...], k_ref[0], v_ref[0], c_ref[0], None, m_scr, l_scr, acc_scr)

    @pl.when(ki == ((qi + 1) * tq - 1) // tk)
    def _():
        lo_half = lax.broadcasted_iota(jnp.int32, (tq, LANES), 1) < HEAD_DIM
        o_ref[0] = (acc_scr[...] / jnp.where(lo_half, l_scr[0], l_scr[1])).astype(o_ref.dtype)


def _attn_prompt(q, k, v, c, t):
    tq = min(1024, t)
    tk = min(512, t)
    steps = [(qi, ki) for qi in range(t // tq) for ki in range(((qi + 1) * tq - 1) // tk + 1)]
    qi_arr = jnp.asarray(np.array([s[0] for s in steps], np.int32))
    ki_arr = jnp.asarray(np.array([s[1] for s in steps], np.int32))
    grid_spec = pltpu.PrefetchScalarGridSpec(
        num_scalar_prefetch=2,
        grid=(N_PAIRS, len(steps)),
        in_specs=[pl.BlockSpec((1, tq, LANES), lambda p, s, qi, ki: (p, qi[s], 0)),
                  pl.BlockSpec((1, tk, LANES), lambda p, s, qi, ki: (p, ki[s], 0)),
                  pl.BlockSpec((1, tk, LANES), lambda p, s, qi, ki: (p, ki[s], 0)),
                  pl.BlockSpec((1, 2, tk), lambda p, s, qi, ki: (p, 0, ki[s]))],
        out_specs=pl.BlockSpec((1, tq, LANES), lambda p, s, qi, ki: (p, qi[s], 0)),
        scratch_shapes=[pltpu.VMEM((tq, LANES), BF16), pltpu.VMEM((tq, LANES), BF16),
                        pltpu.VMEM((2, tq, LANES), F32), pltpu.VMEM((2, tq, LANES), F32),
                        pltpu.VMEM((tq, LANES), F32)],
    )
    return pl.pallas_call(
        functools.partial(_attn_prompt_kernel, tq=tq, tk=tk),
        grid_spec=grid_spec,
        out_shape=jax.ShapeDtypeStruct((N_PAIRS, t, LANES), BF16),
        compiler_params=_cparams(2),
        name="fox_attn_prompt",
    )(qi_arr, ki_arr, q, k, v, c)


def _attn_sample_kernel(q_ref, ck_ref, cv_ref, kn_ref, vn_ref, cc_ref, cn_ref, o_ref,
                        m_scr, l_scr, acc_scr, *, nkb):
    j = pl.program_id(1)
    ts = q_ref.shape[1]

    @pl.when(j == 0)
    def _():
        m_scr[...] = jnp.full(m_scr.shape, NEG_BIG, F32)
        l_scr[...] = jnp.zeros(l_scr.shape, F32)
        acc_scr[...] = jnp.zeros(acc_scr.shape, F32)

    @pl.when(j < nkb)
    def _():
        for p in range(N_PAIRS):
            sl = slice(p * LANES, (p + 1) * LANES)
            qa, qb = _split_q(q_ref[p])
            _attn_update(qa, qb, ck_ref[0, :, sl].astype(BF16), cv_ref[0, :, sl].astype(BF16),
                         cc_ref[0, 2 * p:2 * p + 2, :], None, m_scr.at[p], l_scr.at[p], acc_scr.at[p])

    @pl.when(j == nkb)
    def _():
        rows = lax.broadcasted_iota(jnp.int32, (ts, ts), 0)
        cols = lax.broadcasted_iota(jnp.int32, (ts, ts), 1)
        lo_half = lax.broadcasted_iota(jnp.int32, (ts, LANES), 1) < HEAD_DIM
        for p in range(N_PAIRS):
            qa, qb = _split_q(q_ref[p])
            _attn_update(qa, qb, kn_ref[p], vn_ref[p], cn_ref[0, 2 * p:2 * p + 2, :], cols <= rows,
                         m_scr.at[p], l_scr.at[p], acc_scr.at[p])
            o_ref[p] = (acc_scr[p] / jnp.where(lo_half, l_scr[p, 0], l_scr[p, 1])).astype(o_ref.dtype)


def _attn_sample(q, k, v, cache_k, cache_v, c_cache, c_new, row0):
    b, past, _ = cache_k.shape
    ts = c_new.shape[2]
    tk = min(1024, past)
    nkb = past // tk
    blk0 = row0 // ts
    qspec = pl.BlockSpec((N_PAIRS, ts, LANES), lambda i, j: (0, blk0 + i, 0))
    cspec = pl.BlockSpec((1, tk, MIX_W), lambda i, j: (i, jnp.minimum(j, nkb - 1), 0))
    return pl.pallas_call(
        functools.partial(_attn_sample_kernel, nkb=nkb),
        grid=(b, nkb + 1),
        in_specs=[qspec, cspec, cspec, qspec, qspec,
                  pl.BlockSpec((1, N_HEADS, tk), lambda i, j: (i, 0, jnp.minimum(j, nkb - 1))),
                  pl.BlockSpec((1, N_HEADS, ts), lambda i, j: (i, 0, 0))],
        out_specs=pl.BlockSpec((N_PAIRS, ts, LANES), lambda i, j: (0, i, 0)),
        out_shape=jax.ShapeDtypeStruct((N_PAIRS, b * ts, LANES), BF16),
        scratch_shapes=[pltpu.VMEM((N_PAIRS, 2, ts, LANES), F32), pltpu.VMEM((N_PAIRS, 2, ts, LANES), F32),
                        pltpu.VMEM((N_PAIRS, ts, LANES), F32)],
        compiler_params=_cparams(2),
        name="fox_attn_sample",
    )(q, cache_k, cache_v, k, v, c_cache, c_new)


def _mix_out_kernel(x_ref, ys_ref, bn_ref, g_ref, yf_ref, ln_ref, wo_ref, gf_ref, rw_ref, rb_ref,
                    x1_o, hf_o, ti_o, tg_o, cnt_o, cnt_scr):
    y = jnp.concatenate([ys_ref[p] for p in range(N_PAIRS)], axis=1)
    inv = 1.0 / HEAD_DIM
    mu = _head_sum(y) * inv
    d = y - mu
    var = _head_sum(d * d) * inv
    yn = d * lax.rsqrt(var + GN_EPS) * ln_ref[0:1, :] + ln_ref[1:2, :]
    yr = ((yn + bn_ref[...]) * g_ref[...]).astype(BF16)
    yf = jnp.concatenate([yf_ref[p] for p in range(N_PAIRS)], axis=1)
    mix = jnp.concatenate([yr, yf], axis=1)
    x1 = x_ref[...] + jnp.dot(mix, wo_ref[...], preferred_element_type=F32)
    x1_o[...] = x1
    hf = _rms(x1, gf_ref[...])
    hf_o[...] = hf

    logits = jnp.dot(hf, rw_ref[...], precision=HI, preferred_element_type=F32) + rb_ref[...]
    lane_e = lax.broadcasted_iota(jnp.int32, logits.shape, 1).astype(F32)
    vals = []
    idxs = []
    cur = logits
    for _ in range(TOP_K):
        m = jnp.max(cur, axis=1, keepdims=True)
        am = jnp.min(jnp.where(cur == m, lane_e, float(N_EXPERTS)), axis=1, keepdims=True)
        vals.append(m)
        idxs.append(am)
        cur = jnp.where(lane_e == am, -jnp.inf, cur)
    es = [jnp.exp(vv - vals[0]) for vv in vals]
    tot = es[0] + es[1] + es[2] + es[3]

    @pl.when(pl.program_id(0) == 0)
    def _():
        cnt_scr[...] = jnp.zeros(cnt_scr.shape, F32)

    tm = logits.shape[0]
    sel = [lane_e == idxs[kk] for kk in range(TOP_K)]
    onehot = jnp.where(jnp.logical_or(jnp.logical_or(sel[0], sel[1]), jnp.logical_or(sel[2], sel[3])), 1.0, 0.0)
    rr = lax.broadcasted_iota(jnp.int32, (tm, tm), 0)
    cc = lax.broadcasted_iota(jnp.int32, (tm, tm), 1)
    before = jnp.dot((cc < rr).astype(BF16), onehot.astype(BF16), preferred_element_type=F32) + cnt_scr[...]
    ranks = [jnp.sum(jnp.where(sel[kk], before, 0.0), axis=1, keepdims=True) for kk in range(TOP_K)]
    cnt_scr[...] = cnt_scr[...] + jnp.sum(onehot, axis=0, keepdims=True)
    cnt_o[...] = jnp.broadcast_to(cnt_scr[...], cnt_o.shape).astype(jnp.int32)

    lane = lax.broadcasted_iota(jnp.int32, ti_o.shape, 1)
    ti = jnp.zeros(ti_o.shape, F32)
    tg = jnp.zeros(tg_o.shape, F32)
    for kk in range(TOP_K):
        ti = jnp.where(lane == kk, idxs[kk], ti)
        ti = jnp.where(lane == TOP_K + kk, ranks[kk], ti)
        tg = jnp.where(lane == kk, es[kk] / tot, tg)
    ti_o[...] = ti.astype(jnp.int32)
    tg_o[...] = tg


def _mix_out(x, ys, bonus, g, yf, ln, wo_bf, gf, rw, rb):
    n = x.shape[0]
    tm = min(256, n)
    row = lambda w: pl.BlockSpec((tm, w), lambda i: (i, 0))
    pspec = pl.BlockSpec((N_PAIRS, tm, LANES), lambda i: (0, i, 0))
    full = lambda a: pl.BlockSpec(a.shape, lambda i: (0,) * a.ndim)
    return pl.pallas_call(
        _mix_out_kernel,
        grid=(n // tm,),
        in_specs=[row(D_MODEL), pspec, row(MIX_W), row(MIX_W), pspec, full(ln), full(wo_bf), full(gf),
                  full(rw), full(rb)],
        out_specs=[row(D_MODEL), row(D_MODEL), row(LANES), row(LANES),
                   pl.BlockSpec((8, N_EXPERTS), lambda i: (0, 0))],
        out_shape=[jax.ShapeDtypeStruct((n, D_MODEL), F32), jax.ShapeDtypeStruct((n, D_MODEL), F32),
                   jax.ShapeDtypeStruct((n, LANES), jnp.int32), jax.ShapeDtypeStruct((n, LANES), F32),
                   jax.ShapeDtypeStruct((8, N_EXPERTS), jnp.int32)],
        scratch_shapes=[pltpu.VMEM((1, N_EXPERTS), F32)],
        compiler_params=_cparams(1),
        name="mix_out_router",
    )(x, ys, bonus, g, yf, ln, wo_bf, gf, rw, rb)


def _moe_gather_kernel(tok_ref, hf_hbm, o_ref, buf, sem):
    i = pl.program_id(0)
    nb = pl.num_programs(0)
    rows = o_ref.shape[0]

    def row_copy(blk, slot, r):
        t = tok_ref[blk * rows + r]
        return pltpu.make_async_copy(hf_hbm.at[pl.ds(t, 1)], buf.at[slot, pl.ds(r, 1)], sem.at[slot])

    def issue(blk, slot):
        def body(r, carry):
            row_copy(blk, slot, r).start()
            return carry
        lax.fori_loop(0, rows, body, 0)

    @pl.when(i == 0)
    def _():
        issue(0, 0)

    @pl.when(i + 1 < nb)
    def _():
        issue(i + 1, (i + 1) % 2)

    slot = i % 2

    def wait_body(r, carry):
        row_copy(i, slot, r).wait()
        return carry
    lax.fori_loop(0, rows, wait_body, 0)
    o_ref[...] = buf[slot].astype(BF16)


def _moe_gather(row_tok, hf, n_blocks):
    grid_spec = pltpu.PrefetchScalarGridSpec(
        num_scalar_prefetch=1,
        grid=(n_blocks,),
        in_specs=[pl.BlockSpec(memory_space=pl.ANY)],
        out_specs=pl.BlockSpec((MOE_ROWS, D_MODEL), lambda i, tok: (i, 0)),
        scratch_shapes=[pltpu.VMEM((2, MOE_ROWS, D_MODEL), F32), pltpu.SemaphoreType.DMA((2,))],
    )
    return pl.pallas_call(
        _moe_gather_kernel,
        grid_spec=grid_spec,
        out_shape=jax.ShapeDtypeStruct((n_blocks * MOE_ROWS, D_MODEL), BF16),
        compiler_params=_cparams(1),
        name="moe_gather",
    )(row_tok, hf)


STEP_RUN, STEP_NEW_WEIGHTS, STEP_ZERO = 0, 1, 2


def _moe_gu_kernel(se, sw, sb, sj, sf, x_ref, wg_ref, wu_ref, bg_ref, bu_ref, o_ref, wg_bf, wu_bf):
    s = pl.program_id(0)

    @pl.when(sf[s] == STEP_NEW_WEIGHTS)
    def _():
        wg_bf[...] = wg_ref[0].astype(BF16)
        wu_bf[...] = wu_ref[0].astype(BF16)

    @pl.when(sf[s] != STEP_ZERO)
    def _():
        x = x_ref[...]
        g = jnp.dot(x, wg_bf[...], preferred_element_type=F32) + bg_ref[0]
        u = jnp.dot(x, wu_bf[...], preferred_element_type=F32) + bu_ref[0]
        g = jnp.minimum(g, SWIGLU_LIMIT)
        u = jnp.clip(u, -SWIGLU_LIMIT, SWIGLU_LIMIT)
        o_ref[...] = ((u + 1.0) * (g * jax.nn.sigmoid(SWIGLU_ALPHA * g))).astype(BF16)

    @pl.when(sf[s] == STEP_ZERO)
    def _():
        o_ref[...] = jnp.zeros(o_ref.shape, o_ref.dtype)


def _moe_gate_up(sched, xs, w_gu, b_gu, n_blocks):
    nt = D_EXPERT // MOE_TN
    n_steps = nt * n_blocks
    wspec = lambda off: pl.BlockSpec((1, D_MODEL, MOE_TN), lambda s, se, sw, sb, sj, sf: (se[s], 0, off + sw[s]))
    bspec = lambda off: pl.BlockSpec((1, 1, MOE_TN), lambda s, se, sw, sb, sj, sf: (se[s], 0, off + sw[s]))
    grid_spec = pltpu.PrefetchScalarGridSpec(
        num_scalar_prefetch=5,
        grid=(n_steps,),
        in_specs=[pl.BlockSpec((MOE_ROWS, D_MODEL), lambda s, se, sw, sb, sj, sf: (sb[s], 0)),
                  wspec(0), wspec(nt), bspec(0), bspec(nt)],
        out_specs=pl.BlockSpec((MOE_ROWS, MOE_TN), lambda s, se, sw, sb, sj, sf: (sb[s], sj[s])),
        scratch_shapes=[pltpu.VMEM((D_MODEL, MOE_TN), BF16), pltpu.VMEM((D_MODEL, MOE_TN), BF16)],
    )
    return pl.pallas_call(
        _moe_gu_kernel,
        grid_spec=grid_spec,
        out_shape=jax.ShapeDtypeStruct((n_blocks * MOE_ROWS, D_EXPERT), BF16),
        compiler_params=_cparams(1),
        name="moe_gate_up",
    )(*sched, xs, w_gu, w_gu, b_gu, b_gu)


def _moe_dn_kernel(se, sw, sb, sj, sf, h_ref, wd_ref, bd_ref, o_ref, wd_bf):
    s = pl.program_id(0)

    @pl.when(sf[s] == STEP_NEW_WEIGHTS)
    def _():
        wd_bf[...] = wd_ref[0].astype(BF16)

    @pl.when(sf[s] != STEP_ZERO)
    def _():
        o_ref[...] = jnp.dot(h_ref[...], wd_bf[...], preferred_element_type=F32) + bd_ref[0]

    @pl.when(sf[s] == STEP_ZERO)
    def _():
        o_ref[...] = jnp.zeros(o_ref.shape, o_ref.dtype)


def _moe_down(sched, hid, w_dn, b_dn, n_blocks):
    nt = D_MODEL // MOE_TN
    n_steps = nt * n_blocks
    grid_spec = pltpu.PrefetchScalarGridSpec(
        num_scalar_prefetch=5,
        grid=(n_steps,),
        in_specs=[pl.BlockSpec((MOE_ROWS, D_EXPERT), lambda s, se, sw, sb, sj, sf: (sb[s], 0)),
                  pl.BlockSpec((1, D_EXPERT, MOE_TN), lambda s, se, sw, sb, sj, sf: (se[s], 0, sw[s])),
                  pl.BlockSpec((1, 1, MOE_TN), lambda s, se, sw, sb, sj, sf: (se[s], 0, sw[s]))],
        out_specs=pl.BlockSpec((MOE_ROWS, MOE_TN), lambda s, se, sw, sb, sj, sf: (sb[s], sj[s])),
        scratch_shapes=[pltpu.VMEM((D_EXPERT, MOE_TN), BF16)],
    )
    return pl.pallas_call(
        _moe_dn_kernel,
        grid_spec=grid_spec,
        out_shape=jax.ShapeDtypeStruct((n_blocks * MOE_ROWS, D_MODEL), F32),
        compiler_params=_cparams(1),
        name="moe_down",
    )(*sched, hid, w_dn, b_dn)


def _moe_combine_kernel(pos_ref, ys_hbm, x1_ref, tg_ref, o_ref, buf, sem):
    i = pl.program_id(0)
    nb = pl.num_programs(0)
    tm = o_ref.shape[0]

    def row_copy(blk, slot, r, kk):
        src = pos_ref[(blk * tm + r) * TOP_K + kk]
        return pltpu.make_async_copy(ys_hbm.at[pl.ds(src, 1)], buf.at[slot, kk, pl.ds(r, 1)], sem.at[slot])

    def issue(blk, slot):
        def body(r, carry):
            for kk in range(TOP_K):
                row_copy(blk, slot, r, kk).start()
            return carry
        lax.fori_loop(0, tm, body, 0)

    @pl.when(i == 0)
    def _():
        issue(0, 0)

    @pl.when(i + 1 < nb)
    def _():
        issue(i + 1, (i + 1) % 2)

    slot = i % 2

    def wait_body(r, carry):
        for kk in range(TOP_K):
            row_copy(i, slot, r, kk).wait()
        return carry
    lax.fori_loop(0, tm, wait_body, 0)

    tg = tg_ref[...]
    acc = x1_ref[...]
    for kk in range(TOP_K):
        acc = acc + tg[:, kk:kk + 1] * buf[slot, kk]
    o_ref[...] = acc


def _moe_combine(pos, ys, x1, tg):
    n = x1.shape[0]
    tm = min(128, n)
    grid_spec = pltpu.PrefetchScalarGridSpec(
        num_scalar_prefetch=1,
        grid=(n // tm,),
        in_specs=[pl.BlockSpec(memory_space=pl.ANY),
                  pl.BlockSpec((tm, D_MODEL), lambda i, pos: (i, 0)),
                  pl.BlockSpec((tm, LANES), lambda i, pos: (i, 0))],
        out_specs=pl.BlockSpec((tm, D_MODEL), lambda i, pos: (i, 0)),
        scratch_shapes=[pltpu.VMEM((2, TOP_K, tm, D_MODEL), F32), pltpu.SemaphoreType.DMA((2,))],
    )
    return pl.pallas_call(
        _moe_combine_kernel,
        grid_spec=grid_spec,
        out_shape=jax.ShapeDtypeStruct((n, D_MODEL), F32),
        compiler_params=_cparams(1),
        name="moe_combine",
    )(pos, ys, x1, tg)


def _moe_schedule(top_idx, rank, counts, n_blocks):
    n = top_idx.shape[0]
    n_rows = n * TOP_K
    nt = D_EXPERT // MOE_TN
    flat_e = top_idx.reshape(n_rows)
    nb_e = (counts + MOE_ROWS - 1) // MOE_ROWS
    blk_end = jnp.cumsum(nb_e)
    blk_start = blk_end - nb_e
    pos = (blk_start[flat_e] * MOE_ROWS + rank.reshape(n_rows)).astype(jnp.int32)
    total = n_blocks * MOE_ROWS
    row_tok = (jnp.arange(total, dtype=jnp.int32) % n).at[pos].set(jnp.arange(n_rows, dtype=jnp.int32) // TOP_K)
    used = blk_end[-1]
    s = jnp.arange(nt * n_blocks, dtype=jnp.int32)
    live = s < nt * used
    s_eff = jnp.minimum(s, nt * used - 1)
    e = jnp.minimum(jnp.searchsorted(nt * blk_end, s_eff, side="right"), N_EXPERTS - 1).astype(jnp.int32)
    local = s_eff - nt * blk_start[e]
    nbe = jnp.maximum(nb_e[e], 1)
    sw = (local // nbe).astype(jnp.int32)
    tail = s - nt * used
    sb = jnp.where(live, blk_start[e] + local % nbe, used + tail // nt).astype(jnp.int32)
    sj = jnp.where(live, sw, tail % nt).astype(jnp.int32)
    sf = jnp.where(live, jnp.where(local % nbe == 0, STEP_NEW_WEIGHTS, STEP_RUN), STEP_ZERO).astype(jnp.int32)
    return pos, row_tok, (e, sw, sb, sj, sf)


def _ple_kernel(x_ref, p_ref, g_ref, wg_ref, wp_ref, o_ref):
    x = x_ref[...]
    h = _rms(x, g_ref[...]).astype(BF16)
    gate = jax.nn.sigmoid(jnp.dot(h, wg_ref[...], preferred_element_type=F32))
    pp = jnp.dot(p_ref[...].astype(BF16), wp_ref[...], preferred_element_type=F32)
    o_ref[...] = x + gate * pp


def _ple(x, p, g, wg_bf, wp_bf):
    n = x.shape[0]
    tm = min(256, n)
    full = lambda a: pl.BlockSpec(a.shape, lambda i: (0,) * a.ndim)
    return pl.pallas_call(
        _ple_kernel,
        grid=(n // tm,),
        in_specs=[pl.BlockSpec((tm, D_MODEL), lambda i: (i, 0)), pl.BlockSpec((tm, PLE_DIM), lambda i: (i, 0)),
                  full(g), full(wg_bf), full(wp_bf)],
        out_specs=pl.BlockSpec((tm, D_MODEL), lambda i: (i, 0)),
        out_shape=jax.ShapeDtypeStruct((n, D_MODEL), F32),
        compiler_params=_cparams(1),
        name="ple_gate",
    )(x, p, g, wg_bf, wp_bf)


def _pairs_from_state(s):
    b = s.shape[0]
    st = jnp.swapaxes(s, 2, 3).reshape(b, N_PAIRS, 2, HEAD_DIM, HEAD_DIM)
    z = jnp.zeros((b, N_PAIRS, HEAD_DIM, HEAD_DIM), s.dtype)
    top = jnp.concatenate([st[:, :, 0], z], axis=3)
    bot = jnp.concatenate([z, st[:, :, 1]], axis=3)
    return jnp.concatenate([top, bot], axis=2)


def _state_from_pairs(sp):
    b = sp.shape[0]
    st = jnp.stack([sp[:, :, :HEAD_DIM, :HEAD_DIM], sp[:, :, HEAD_DIM:, HEAD_DIM:]], axis=2)
    return jnp.swapaxes(st.reshape(b, N_HEADS, HEAD_DIM, HEAD_DIM), 2, 3)


def _layer(x_p, x_s, cache_k, cache_v, cache_lf, state, shift, p_all, lw):
    t_p = x_p.shape[1]
    b_s, t_s, _ = x_s.shape
    n_p = x_p.shape[0] * t_p
    n_s = b_s * t_s
    n = n_p + n_s
    past = cache_k.shape[1]
    x = jnp.concatenate([x_p.reshape(n_p, D_MODEL), x_s.reshape(n_s, D_MODEL)], axis=0)

    w_in_bf = jnp.pad(lw["w_in"], ((0, 0), (0, Z_PAD - lw["w_in"].shape[1]))).astype(BF16)
    z = _in_proj(x, lw["norm_mix_g"], w_in_bf)

    zero_row = jnp.zeros((1, 1, RWKV_PROJ), F32)
    prev = jnp.concatenate([zero_row, shift.astype(F32)], axis=0)
    vecs = jnp.zeros((8, MIX_W), F32)
    vecs = vecs.at[0].set(lw["rwkv_w0"]).at[1].set(lw["rwkv_a0"]).at[2].set(lw["rwkv_kk"])
    vecs = vecs.at[3].set(lw["rwkv_ka"]).at[4].set(lw["rwkv_rk"].reshape(MIX_W))
    wl = jnp.zeros((256, 3 * MIX_W), F32)
    wl = wl.at[0:64, 0:MIX_W].set(lw["rwkv_w2"]).at[64:128, MIX_W:2 * MIX_W].set(lw["rwkv_a2"])
    wl = wl.at[128:256, 2 * MIX_W:].set(lw["rwkv_g2"])
    r, w, k, kk, bb, v_pm, g, bonus = _rwkv_pre(z, prev, lw["rwkv_mu"].reshape(1, RWKV_PROJ), vecs, wl, n_p)
    s0 = jnp.concatenate([jnp.zeros((1, N_PAIRS, LANES, LANES), F32), _pairs_from_state(state.astype(F32))], axis=0)
    y_scan, s_out = _rwkv_scan(r, w, k, kk, bb, v_pm, s0, n_p)
    s_new = _state_from_pairs(s_out)
    shift_new_p = z[n_p - 1:n_p, :RWKV_PROJ].reshape(1, 1, RWKV_PROJ)
    shift_new_s = z[n_p:, :RWKV_PROJ].reshape(b_s, t_s, RWKV_PROJ)[:, -1:, :]

    fvecs = jnp.zeros((8, MIX_W), F32)
    fvecs = fvecs.at[0].set(jnp.tile(lw["fox_q_g"], N_HEADS)).at[1].set(jnp.tile(lw["fox_k_g"], N_HEADS))
    bf = jnp.zeros((1, LANES), F32).at[0, :N_HEADS].set(lw["fox_b_f"])
    q_pm, k_pm, vv_pm, k_new, v_new, lf = _fox_pre(z, fvecs, bf)
    lf = lf[:, :N_HEADS]
    lf_p = lf[:n_p].T.reshape(1, N_HEADS, n_p)
    c_p = _cumsum_lanes(lf_p).reshape(N_PAIRS, 2, n_p)
    yf_p = _attn_prompt(q_pm, k_pm, vv_pm, c_p, n_p)
    lf_s = jnp.swapaxes(lf[n_p:].reshape(b_s, t_s, N_HEADS), 1, 2)
    lf_all = jnp.concatenate([jnp.swapaxes(cache_lf.astype(F32), 1, 2), lf_s], axis=2)
    pad = (-lf_all.shape[2]) % LANES
    c_all = _cumsum_lanes(jnp.pad(lf_all, ((0, 0), (0, 0), (0, pad))))
    yf_s = _attn_sample(q_pm, k_pm, vv_pm, cache_k.reshape(b_s, past, MIX_W), cache_v.reshape(b_s, past, MIX_W),
                        c_all[:, :, :past], c_all[:, :, past:past + t_s], n_p)
    yf = jnp.concatenate([yf_p, yf_s], axis=1)

    ln = jnp.stack([lw["rwkv_ln_g"], lw["rwkv_ln_b"]])
    x1, hf, ti, tg, cnt = _mix_out(x, y_scan, bonus, g, yf, ln, lw["w_out"].astype(BF16),
                                   lw["norm_ffn_g"].reshape(1, D_MODEL), lw["router_w"],
                                   lw["router_b"].reshape(1, N_EXPERTS))

    n_blocks = n * TOP_K // MOE_ROWS + N_EXPERTS
    pos, row_tok, sched = _moe_schedule(ti[:, :TOP_K], ti[:, TOP_K:2 * TOP_K], cnt[0], n_blocks)
    xs = _moe_gather(row_tok, hf, n_blocks)
    hid = _moe_gate_up(sched, xs, lw["expert_w_gu"], lw["expert_b_gu"].reshape(N_EXPERTS, 1, 2 * D_EXPERT), n_blocks)
    ys = _moe_down(sched, hid, lw["expert_w_down"], lw["expert_b_down"].reshape(N_EXPERTS, 1, D_MODEL), n_blocks)
    x2 = _moe_combine(pos, ys, x1, tg)

    y = _ple(x2, p_all, lw["ple_norm_g"].reshape(1, D_MODEL), lw["ple_w_gate"].astype(BF16),
             lw["ple_w_proj"].astype(BF16))

    heads = lambda a, bsz, t: a.reshape(bsz, t, N_HEADS, HEAD_DIM)
    out_p = (y[:n_p].reshape(x_p.shape), heads(k_new[:n_p], 1, n_p), heads(v_new[:n_p], 1, n_p),
             lf[:n_p].reshape(1, n_p, N_HEADS), s_new[:1], shift_new_p)
    out_s = (y[n_p:].reshape(x_s.shape), heads(k_new[n_p:], b_s, t_s), heads(v_new[n_p:], b_s, t_s),
             lf[n_p:].reshape(b_s, t_s, N_HEADS), s_new[1:], shift_new_s)
    return out_p, out_s


def kernel(x_prompt, x_sample, cache_fox_k, cache_fox_v, cache_fox_logf, state_rwkv, state_rwkv_shift, p_prompt, p_sample, norm_mix_g, w_in, rwkv_mu, rwkv_w0, rwkv_w2, rwkv_a0, rwkv_a2, rwkv_g2, rwkv_kk, rwkv_ka, rwkv_rk, rwkv_ln_g, rwkv_ln_b, fox_q_g, fox_k_g, fox_b_f, w_out, norm_ffn_g, router_w, router_b, expert_w_gu, expert_b_gu, expert_w_down, expert_b_down, ple_norm_g, ple_w_gate, ple_w_proj):
    assert x_prompt.shape[0] == 1 and w_in.shape[0] == 1, "one prompt stream, one layer"
    lw = dict(norm_mix_g=norm_mix_g[0], w_in=w_in[0], rwkv_mu=rwkv_mu[0], rwkv_w0=rwkv_w0[0], rwkv_w2=rwkv_w2[0],
              rwkv_a0=rwkv_a0[0], rwkv_a2=rwkv_a2[0], rwkv_g2=rwkv_g2[0], rwkv_kk=rwkv_kk[0], rwkv_ka=rwkv_ka[0],
              rwkv_rk=rwkv_rk[0], rwkv_ln_g=rwkv_ln_g[0], rwkv_ln_b=rwkv_ln_b[0], fox_q_g=fox_q_g[0],
              fox_k_g=fox_k_g[0], fox_b_f=fox_b_f[0], w_out=w_out[0], norm_ffn_g=norm_ffn_g[0],
              router_w=router_w[0], router_b=router_b[0], expert_w_gu=expert_w_gu[0], expert_b_gu=expert_b_gu[0],
              expert_w_down=expert_w_down[0], expert_b_down=expert_b_down[0], ple_norm_g=ple_norm_g[0],
              ple_w_gate=ple_w_gate[0], ple_w_proj=ple_w_proj[0])
    n_p = x_prompt.shape[1]
    p_all = jnp.concatenate([p_prompt[0].reshape(n_p, PLE_DIM), p_sample[0].reshape(-1, PLE_DIM)], axis=0)
    (y_p, k_p, v_p, lf_p, s_p, sh_p), (y_s, k_s, v_s, lf_s, s_s, sh_s) = _layer(
        x_prompt, x_sample, cache_fox_k[0], cache_fox_v[0], cache_fox_logf[0], state_rwkv[0],
        state_rwkv_shift[0], p_all, lw)
    add = lambda a: a[None]
    return (y_p, y_s, add(k_p), add(v_p), add(lf_p), add(s_p), add(sh_p),
            add(k_s), add(v_s), add(lf_s), add(s_s), add(sh_s))
```

```python
import functools

import numpy as np
import jax
import jax.numpy as jnp
from jax import lax
from jax.experimental import pallas as pl
from jax.experimental.pallas import tpu as pltpu

F32 = jnp.float32
BF16 = jnp.bfloat16
HI = lax.Precision.HIGHEST

D_MODEL = 2048
HEAD_DIM = 64
N_HEADS = 16
N_PAIRS = N_HEADS // 2
MIX_W = N_HEADS * HEAD_DIM
CHUNK = 64
RWKV_PROJ = 3 * MIX_W + 64 + 64 + 128
FOX_PROJ = 3 * MIX_W + N_HEADS
Z_HALF = RWKV_PROJ
Z_PAD = 2 * Z_HALF
N_EXPERTS = 32
TOP_K = 4
D_EXPERT = 2048
SWIGLU_LIMIT = 7.0
SWIGLU_ALPHA = 1.702
PLE_DIM = 256
RMS_EPS = 1e-6
GN_EPS = 64e-5
L2_EPS = 1e-12
NEG_BIG = -1e30
LOG2E = 1.4426950408889634

LANES = 128
MOE_ROWS = 256
MOE_TN = 1024
ROW_CHUNKS = D_MODEL // LANES
VMEM_LIMIT = 52 * 1024 * 1024


def _cparams(n_axes, vmem=VMEM_LIMIT):
    return pltpu.CompilerParams(dimension_semantics=("arbitrary",) * n_axes, vmem_limit_bytes=vmem)


def _head_sum(x):
    r = lax.broadcasted_iota(jnp.int32, (LANES, LANES), 0) // HEAD_DIM
    c = lax.broadcasted_iota(jnp.int32, (LANES, LANES), 1) // HEAD_DIM
    bd = (r == c).astype(F32)
    parts = [jnp.dot(x[:, i * LANES:(i + 1) * LANES], bd, precision=HI, preferred_element_type=F32)
             for i in range(x.shape[1] // LANES)]
    return parts[0] if len(parts) == 1 else jnp.concatenate(parts, axis=1)


def _log_sigmoid(x):
    return jnp.minimum(x, 0.0) - jnp.log1p(jnp.exp(-jnp.abs(x)))


def _rms(x, g):
    ms = jnp.mean(x * x, axis=-1, keepdims=True)
    return x * lax.rsqrt(ms + RMS_EPS) * g


def _row_tile(n_p, n_s, pref):
    return pref if (n_p % pref == 0 and n_s % pref == 0) else 128


def _two_group_specs(tm, width, np_tiles, n_grid_axes=1):
    if n_grid_axes == 1:
        return (pl.BlockSpec((tm, width), lambda i: (jnp.minimum(i, np_tiles - 1), 0)),
                pl.BlockSpec((tm, width), lambda i: (jnp.maximum(i - np_tiles, 0), 0)))
    return (pl.BlockSpec((tm, width), lambda i, j: (jnp.minimum(i, np_tiles - 1), 0)),
            pl.BlockSpec((tm, width), lambda i, j: (jnp.maximum(i - np_tiles, 0), 0)))


def _inproj_kernel(xp_ref, xs_ref, g_ref, w_ref, o_ref, h_scr, *, np_tiles):
    i = pl.program_id(0)
    first = pl.program_id(1) == 0

    @pl.when(jnp.logical_and(first, i < np_tiles))
    def _():
        h_scr[...] = _rms(xp_ref[...], g_ref[...]).astype(BF16)

    @pl.when(jnp.logical_and(first, i >= np_tiles))
    def _():
        h_scr[...] = _rms(xs_ref[...], g_ref[...]).astype(BF16)

    o_ref[...] = jnp.dot(h_scr[...], w_ref[...], preferred_element_type=F32)


def _in_proj(x_p, x_s, g, w_bf):
    n_p, n_s = x_p.shape[0], x_s.shape[0]
    n = n_p + n_s
    tm = _row_tile(n_p, n_s, 512)
    tn = 512
    xp_spec, xs_spec = _two_group_specs(tm, D_MODEL, n_p // tm, 2)
    return pl.pallas_call(
        functools.partial(_inproj_kernel, np_tiles=n_p // tm),
        grid=(n // tm, Z_PAD // tn),
        in_specs=[xp_spec, xs_spec,
                  pl.BlockSpec((1, D_MODEL), lambda i, j: (0, 0)),
                  pl.BlockSpec((D_MODEL, tn), lambda i, j: (0, j))],
        out_specs=pl.BlockSpec((tm, tn), lambda i, j: (i, j)),
        out_shape=jax.ShapeDtypeStruct((n, Z_PAD), F32),
        scratch_shapes=[pltpu.VMEM((tm, D_MODEL), BF16)],
        compiler_params=_cparams(2),
        name="in_proj",
    )(x_p, x_s, g.reshape(1, D_MODEL), w_bf)


def _rwkv_pre_kernel(z_ref, prev_ref, mu_ref, vec_ref, wl_ref,
                     r_o, w_o, k_o, kk_o, b_o, v_o, g_o, bn_o, carry, *, n_prompt_tiles):
    i = pl.program_id(0)
    z = z_ref[...]
    tm = z.shape[0]

    @pl.when(jnp.logical_or(i == 0, i >= n_prompt_tiles))
    def _():
        carry[...] = prev_ref[0]

    prev = carry[...]
    rolled = pltpu.roll(z, 1, axis=0)
    row = lax.broadcasted_iota(jnp.int32, z.shape, 0)
    shifted = jnp.where(row == 0, prev, rolled)
    carry[...] = z[tm - 1:tm, :]
    zs = z + mu_ref[...] * (shifted - z)

    r = zs[:, 0:MIX_W]
    k = zs[:, MIX_W:2 * MIX_W]
    v = zs[:, 2 * MIX_W:3 * MIX_W]
    lo = zs[:, 3 * MIX_W:RWKV_PROJ]
    lane = lax.broadcasted_iota(jnp.int32, lo.shape, 1)
    f = jnp.where(lane < 64, jnp.tanh(lo), jnp.where(lane < 128, lo, jax.nn.sigmoid(lo)))
    lora = jnp.dot(f, wl_ref[...], precision=HI, preferred_element_type=F32)
    w_pre = vec_ref[0:1, :] + lora[:, 0:MIX_W]
    log_decay = -jnp.exp(_log_sigmoid(w_pre) - 0.5)
    a = jax.nn.sigmoid(vec_ref[1:2, :] + lora[:, MIX_W:2 * MIX_W])
    g = lora[:, 2 * MIX_W:3 * MIX_W]
    kk = k * vec_ref[2:3, :]
    kk = kk * lax.rsqrt(_head_sum(kk * kk) + L2_EPS)
    k_mod = k * (1.0 + (a - 1.0) * vec_ref[3:4, :])
    bonus = _head_sum(r * k_mod * vec_ref[4:5, :]) * v
    kka = kk * a

    g_o[...] = g
    bn_o[...] = bonus
    for p in range(N_PAIRS):
        sl = slice(p * LANES, (p + 1) * LANES)
        r_o[p] = r[:, sl]
        w_o[p] = log_decay[:, sl]
        k_o[p] = k_mod[:, sl]
        kk_o[p] = kk[:, sl]
        b_o[p] = kka[:, sl]
        v_o[p] = v[:, sl]


def _rwkv_pre(z, prev, mu, vecs, wl, n_prompt):
    n = z.shape[0]
    tm = CHUNK
    nt = n // tm
    n_prompt_tiles = n_prompt // tm
    tok = jax.ShapeDtypeStruct((n, MIX_W), F32)
    tspec = pl.BlockSpec((tm, MIX_W), lambda i: (i, 0))
    pm = jax.ShapeDtypeStruct((N_PAIRS, n, LANES), F32)
    pspec = pl.BlockSpec((N_PAIRS, tm, LANES), lambda i: (0, i, 0))
    return pl.pallas_call(
        functools.partial(_rwkv_pre_kernel, n_prompt_tiles=n_prompt_tiles),
        grid=(nt,),
        in_specs=[pl.BlockSpec((tm, Z_HALF), lambda i: (i, 0)),
                  pl.BlockSpec((1, 1, RWKV_PROJ), lambda i: (jnp.maximum(i - (n_prompt_tiles - 1), 0), 0, 0)),
                  pl.BlockSpec((1, RWKV_PROJ), lambda i: (0, 0)),
                  pl.BlockSpec((8, MIX_W), lambda i: (0, 0)),
                  pl.BlockSpec((256, 3 * MIX_W), lambda i: (0, 0))],
        out_specs=[pspec, pspec, pspec, pspec, pspec, pspec, tspec, tspec],
        out_shape=[pm, pm, pm, pm, pm, pm, tok, tok],
        scratch_shapes=[pltpu.VMEM((1, RWKV_PROJ), F32)],
        compiler_params=_cparams(1),
        name="rwkv_pre",
    )(z, prev, mu, vecs, wl)


def _dot(a, b):
    return jnp.dot(a, b, precision=HI, preferred_element_type=F32)


def _bdot(a, b):
    return jnp.dot(a, b, preferred_element_type=F32)


def _split_bf16(x):
    hi = x.astype(BF16)
    return hi, (x - hi.astype(F32)).astype(BF16)


def _dot3(a, b):
    a_hi, a_lo = _split_bf16(a)
    b_hi, b_lo = _split_bf16(b)
    return _bdot(a_hi, b_hi) + (_bdot(a_hi, b_lo) + _bdot(a_lo, b_hi))


def _pair_rows(x):
    lo_half = lax.broadcasted_iota(jnp.int32, x.shape, 1) < HEAD_DIM
    return jnp.concatenate([jnp.where(lo_half, x, 0.0), jnp.where(lo_half, 0.0, x)], axis=0)


def _scan_chunk(P, r, lw, k, kk, b, v):
    c2 = 2 * CHUNK
    i = lax.broadcasted_iota(jnp.int32, (c2, c2), 0)
    j = lax.broadcasted_iota(jnp.int32, (c2, c2), 1)
    ti = lax.broadcasted_iota(jnp.int32, (CHUNK, CHUNK), 0)
    tj = lax.broadcasted_iota(jnp.int32, (CHUNK, CHUNK), 1)
    cl = _dot((tj <= ti).astype(F32), lw)
    yield
    g_end = cl[CHUNK - 1:CHUNK, :]
    e_neg = jnp.exp(-cl)
    e_end = jnp.exp(g_end - cl)
    kap = _pair_rows(kk * jnp.exp(cl - lw)).astype(BF16)
    rt = _pair_rows(r * jnp.exp(cl)).astype(BF16)
    bt = _pair_rows(b * e_neg).astype(BF16)
    kt = _pair_rows(k * e_neg).astype(BF16)
    kh = _pair_rows(k * e_end)
    bh = _pair_rows(b * e_end)
    vv = _pair_rows(v)
    vv_b = vv.astype(BF16)
    p_b = P.astype(BF16)

    g = lax.dot_general(jnp.concatenate([kap, rt], axis=0), jnp.concatenate([bt, kt], axis=0),
                        (((1,), (1,)), ((), ())), preferred_element_type=F32)
    yield
    strict = j < i
    incl = j <= i
    a_b = jnp.where(strict, g[:c2, :c2], 0.0)
    a_bb = a_b.astype(BF16)
    a_k = jnp.where(strict, g[:c2, c2:], 0.0).astype(BF16)
    r_b = jnp.where(incl, g[c2:, :c2], 0.0).astype(BF16)
    r_k = jnp.where(incl, g[c2:, c2:], 0.0).astype(BF16)

    t_inv = (i == j).astype(F32) - jnp.where(jnp.logical_and((i & 1) == 1, j == i - 1), a_b, 0.0)
    n = 2
    while n < CHUNK:
        m = jnp.logical_and((i >> n.bit_length()) == (j >> n.bit_length()),
                            jnp.logical_and((i & (2 * n - 1)) >= n, (j & (2 * n - 1)) < n))
        t_b = t_inv.astype(BF16)
        ta = _bdot(t_b, a_bb).astype(BF16)
        yield
        t_inv = t_inv - jnp.where(m, _bdot(ta, t_b), 0.0)
        yield
        n *= 2

    w = _bdot(jnp.concatenate([kap, a_k], axis=1), jnp.concatenate([p_b, vv_b], axis=0))
    yield
    u = _bdot(t_inv.astype(BF16), w.astype(BF16))
    yield
    vu_b = jnp.concatenate([vv_b, u.astype(BF16)], axis=0)
    y2 = _bdot(rt, p_b) + _bdot(jnp.concatenate([r_k, -r_b], axis=1), vu_b)
    y = y2[:CHUNK] + y2[CHUNK:]
    yield
    g_col = jnp.broadcast_to(jnp.exp(g_end), (c2, c2)).T
    p_new = g_col * P + _dot3(jnp.concatenate([kh.T, -bh.T], axis=1), jnp.concatenate([vv, u], axis=0))
    return y, p_new


def _run_interleaved(gens):
    results = [None] * len(gens)
    live = list(range(len(gens)))
    while live:
        for idx in list(live):
            try:
                next(gens[idx])
            except StopIteration as stop:
                results[idx] = stop.value
                live.remove(idx)
    return results


def _scan_kernel(r_ref, w_ref, k_ref, kk_ref, b_ref, v_ref, s0_ref, y_ref, sout_ref, s_scr, *, n_prompt_chunks):
    c = pl.program_id(0)

    @pl.when(jnp.logical_or(c == 0, c >= n_prompt_chunks))
    def _():
        s_scr[...] = s0_ref[0]

    outs = _run_interleaved([
        _scan_chunk(s_scr[p], r_ref[p], w_ref[p], k_ref[p], kk_ref[p], b_ref[p], v_ref[p])
        for p in range(N_PAIRS)])
    for p, (y, p_new) in enumerate(outs):
        y_ref[p] = y
        s_scr[p] = p_new
        sout_ref[0, p] = p_new


def _rwkv_scan(r, w, k, kk, b, v, s0, n_prompt):
    n = r.shape[1]
    npc = n_prompt // CHUNK
    n_seq = s0.shape[0]
    pspec = pl.BlockSpec((N_PAIRS, CHUNK, LANES), lambda c: (0, c, 0))
    sspec = pl.BlockSpec((1, N_PAIRS, LANES, LANES), lambda c: (jnp.maximum(c - (npc - 1), 0), 0, 0, 0))
    return pl.pallas_call(
        functools.partial(_scan_kernel, n_prompt_chunks=npc),
        grid=(n // CHUNK,),
        in_specs=[pspec, pspec, pspec, pspec, pspec, pspec, sspec],
        out_specs=[pspec, sspec],
        out_shape=[jax.ShapeDtypeStruct((N_PAIRS, n, LANES), F32),
                   jax.ShapeDtypeStruct((n_seq, N_PAIRS, LANES, LANES), F32)],
        scratch_shapes=[pltpu.VMEM((N_PAIRS, LANES, LANES), F32)],
        compiler_params=_cparams(1),
        name="rwkv_scan",
    )(r, w, k, kk, b, v, s0)


def _fox_pre_kernel(z_ref, vec_ref, bf_ref, q_o, k_o, v_o, knp_o, vnp_o, kns_o, vns_o, lf_o, *, np_tiles):
    z = z_ref[...]
    q = z[:, 0:MIX_W]
    k = z[:, MIX_W:2 * MIX_W]
    v = z[:, 2 * MIX_W:3 * MIX_W]
    fl = z[:, 3 * MIX_W:3 * MIX_W + LANES]
    inv = 1.0 / HEAD_DIM
    qn = q * lax.rsqrt(_head_sum(q * q) * inv + RMS_EPS) * vec_ref[0:1, :]
    kn = k * lax.rsqrt(_head_sum(k * k) * inv + RMS_EPS) * vec_ref[1:2, :]
    qs = (qn * (HEAD_DIM ** -0.5 * LOG2E)).astype(BF16)
    kb = kn.astype(BF16)
    vb = v.astype(BF16)
    for p in range(N_PAIRS):
        sl = slice(p * LANES, (p + 1) * LANES)
        q_o[p] = qs[:, sl]
        k_o[p] = kb[:, sl]
        v_o[p] = vb[:, sl]
    lf_o[...] = _log_sigmoid(fl + bf_ref[...])

    @pl.when(pl.program_id(0) < np_tiles)
    def _():
        knp_o[...] = kn
        vnp_o[...] = v

    @pl.when(pl.program_id(0) >= np_tiles)
    def _():
        kns_o[...] = kn
        vns_o[...] = v


def _fox_pre(z, vecs, bf, n_p):
    n = z.shape[0]
    n_s = n - n_p
    tm = _row_tile(n_p, n_s, 256)
    np_tiles = n_p // tm
    pm = jax.ShapeDtypeStruct((N_PAIRS, n, LANES), BF16)
    pspec = pl.BlockSpec((N_PAIRS, tm, LANES), lambda i: (0, i, 0))
    p_spec, s_spec = _two_group_specs(tm, MIX_W, np_tiles)
    tok_p = jax.ShapeDtypeStruct((n_p, MIX_W), F32)
    tok_s = jax.ShapeDtypeStruct((n_s, MIX_W), F32)
    return pl.pallas_call(
        functools.partial(_fox_pre_kernel, np_tiles=np_tiles),
        grid=(n // tm,),
        in_specs=[pl.BlockSpec((tm, Z_HALF), lambda i: (i, 1)),
                  pl.BlockSpec((8, MIX_W), lambda i: (0, 0)),
                  pl.BlockSpec((1, LANES), lambda i: (0, 0))],
        out_specs=[pspec, pspec, pspec, p_spec, p_spec, s_spec, s_spec,
                   pl.BlockSpec((tm, LANES), lambda i: (i, 0))],
        out_shape=[pm, pm, pm, tok_p, tok_p, tok_s, tok_s, jax.ShapeDtypeStruct((n, LANES), F32)],
        compiler_params=_cparams(1),
        name="fox_pre",
    )(z, vecs, bf)


def _cumsum_kernel(x_ref, o_ref):
    r = lax.broadcasted_iota(jnp.int32, (LANES, LANES), 0)
    c = lax.broadcasted_iota(jnp.int32, (LANES, LANES), 1)
    tri = (r <= c).astype(F32)
    carry = jnp.zeros((N_HEADS, 1), F32)
    for i in range(x_ref.shape[2] // LANES):
        sl = slice(i * LANES, (i + 1) * LANES)
        cs = jnp.dot(x_ref[0, :, sl], tri, precision=HI, preferred_element_type=F32) + carry
        o_ref[0, :, sl] = cs
        carry = cs[:, LANES - 1:LANES]


def _cumsum_lanes(x):
    b, h, t = x.shape
    return pl.pallas_call(
        _cumsum_kernel,
        grid=(b,),
        in_specs=[pl.BlockSpec((1, h, t), lambda i: (i, 0, 0))],
        out_specs=pl.BlockSpec((1, h, t), lambda i: (i, 0, 0)),
        out_shape=jax.ShapeDtypeStruct((b, h, t), F32),
        compiler_params=_cparams(1),
        name="cumsum_logf",
    )(x)


def _rep_lanes(m, tk):
    if tk % LANES == 0:
        return m if tk == LANES else jnp.concatenate([m] * (tk // LANES), axis=1)
    return m[:, :tk]


def _attn_update(qa, qb, kb, vb, ck, mask, m_ref, l_ref, acc_ref):
    tk = kb.shape[0]
    lo_half = lax.broadcasted_iota(jnp.int32, acc_ref.shape, 1) < HEAD_DIM
    ck2 = ck * LOG2E
    pv = []
    alphas = []
    for h, qh in enumerate((qa, qb)):
        s = lax.dot_general(qh, kb, (((1,), (1,)), ((), ())), preferred_element_type=F32)
        s = s - ck2[h:h + 1, :]
        if mask is not None:
            s = jnp.where(mask, s, NEG_BIG)
        m_prev = m_ref[h]
        m_next = jnp.maximum(m_prev, jnp.max(s, axis=1, keepdims=True))
        p = jnp.exp2(s - _rep_lanes(m_next, tk))
        alpha = jnp.exp2(m_prev - m_next)
        l_ref[h] = alpha * l_ref[h] + jnp.sum(p, axis=1, keepdims=True)
        m_ref[h] = m_next
        pv.append(jnp.dot(p.astype(BF16), vb, preferred_element_type=F32))
        alphas.append(alpha)
    acc_ref[...] = acc_ref[...] * jnp.where(lo_half, alphas[0], alphas[1]) + jnp.where(lo_half, pv[0], pv[1])


def _split_q(q):
    qf = q.astype(F32)
    lo_half = lax.broadcasted_iota(jnp.int32, qf.shape, 1) < HEAD_DIM
    return jnp.where(lo_half, qf, 0.0).astype(BF16), jnp.where(lo_half, 0.0, qf).astype(BF16)


def _attn_prompt_kernel(qi_ref, ki_ref, q_ref, k_ref, v_ref, c_ref, o_ref,
                        qa_scr, qb_scr, m_scr, l_scr, acc_scr, *, tq, tk):
    s_id = pl.program_id(1)
    qi = qi_ref[s_id]
    ki = ki_ref[s_id]

    @pl.when(ki == 0)
    def _():
        qa, qb = _split_q(q_ref[0])
        qa_scr[...] = qa
        qb_scr[...] = qb
        m_scr[...] = jnp.full(m_scr.shape, NEG_BIG, F32)
        l_scr[...] = jnp.zeros(l_scr.shape, F32)
        acc_scr[...] = jnp.zeros(acc_scr.shape, F32)

    crosses_diagonal = ki * tk + (tk - 1) > qi * tq

    @pl.when(crosses_diagonal)
    def _():
        rows = qi * tq + lax.broadcasted_iota(jnp.int32, (tq, tk), 0)
        cols = ki * tk + lax.broadcasted_iota(jnp.int32, (tq, tk), 1)
        _attn_update(qa_scr[...], qb_scr[...], k_ref[0], v_ref[0], c_ref[0], cols <= rows, m_scr, l_scr, acc_scr)

    @pl.when(jnp.logical_not(crosses_diagonal))
    def _():
        _attn_update(qa_scr[...], qb_scr[...], k_ref[0], v_ref[0], c_ref[0], None, m_scr, l_scr, acc_scr)

    @pl.when(ki == ((qi + 1) * tq - 1) // tk)
    def _():
        lo_half = lax.broadcasted_iota(jnp.int32, (tq, LANES), 1) < HEAD_DIM
        o_ref[0] = (acc_scr[...] / jnp.where(lo_half, l_scr[0], l_scr[1])).astype(o_ref.dtype)


def _attn_prompt(q, k, v, c, t):
    tq = min(1024, t)
    tk = min(512, t)
    steps = [(qi, ki) for qi in range(t // tq) for ki in range(((qi + 1) * tq - 1) // tk + 1)]
    qi_arr = jnp.asarray(np.array([s[0] for s in steps], np.int32))
    ki_arr = jnp.asarray(np.array([s[1] for s in steps], np.int32))
    grid_spec = pltpu.PrefetchScalarGridSpec(
        num_scalar_prefetch=2,
        grid=(N_PAIRS, len(steps)),
        in_specs=[pl.BlockSpec((1, tq, LANES), lambda p, s, qi, ki: (p, qi[s], 0)),
                  pl.BlockSpec((1, tk, LANES), lambda p, s, qi, ki: (p, ki[s], 0)),
                  pl.BlockSpec((1, tk, LANES), lambda p, s, qi, ki: (p, ki[s], 0)),
                  pl.BlockSpec((1, 2, tk), lambda p, s, qi, ki: (p, 0, ki[s]))],
        out_specs=pl.BlockSpec((1, tq, LANES), lambda p, s, qi, ki: (p, qi[s], 0)),
        scratch_shapes=[pltpu.VMEM((tq, LANES), BF16), pltpu.VMEM((tq, LANES), BF16),
                        pltpu.VMEM((2, tq, LANES), F32), pltpu.VMEM((2, tq, LANES), F32),
                        pltpu.VMEM((tq, LANES), F32)],
    )
    return pl.pallas_call(
        functools.partial(_attn_prompt_kernel, tq=tq, tk=tk),
        grid_spec=grid_spec,
        out_shape=jax.ShapeDtypeStruct((N_PAIRS, t, LANES), BF16),
        compiler_params=_cparams(2),
        name="fox_attn_prompt",
    )(qi_arr, ki_arr, q, k, v, c)


def _attn_sample_kernel(q_ref, ck_ref, cv_ref, kn_ref, vn_ref, cc_ref, cn_ref, o_ref,
                        m_scr, l_scr, acc_scr, *, nkb):
    j = pl.program_id(1)
    ts = q_ref.shape[1]

    @pl.when(j == 0)
    def _():
        m_scr[...] = jnp.full(m_scr.shape, NEG_BIG, F32)
        l_scr[...] = jnp.zeros(l_scr.shape, F32)
        acc_scr[...] = jnp.zeros(acc_scr.shape, F32)

    @pl.when(j < nkb)
    def _():
        for p in range(N_PAIRS):
            sl = slice(p * LANES, (p + 1) * LANES)
            qa, qb = _split_q(q_ref[p])
            _attn_update(qa, qb, ck_ref[0, :, sl].astype(BF16), cv_ref[0, :, sl].astype(BF16),
                         cc_ref[0, 2 * p:2 * p + 2, :], None, m_scr.at[p], l_scr.at[p], acc_scr.at[p])

    @pl.when(j == nkb)
    def _():
        rows = lax.broadcasted_iota(jnp.int32, (ts, ts), 0)
        cols = lax.broadcasted_iota(jnp.int32, (ts, ts), 1)
        lo_half = lax.broadcasted_iota(jnp.int32, (ts, LANES), 1) < HEAD_DIM
        for p in range(N_PAIRS):
            qa, qb = _split_q(q_ref[p])
            _attn_update(qa, qb, kn_ref[p], vn_ref[p], cn_ref[0, 2 * p:2 * p + 2, :], cols <= rows,
                         m_scr.at[p], l_scr.at[p], acc_scr.at[p])
            o_ref[p] = (acc_scr[p] / jnp.where(lo_half, l_scr[p, 0], l_scr[p, 1])).astype(o_ref.dtype)


def _attn_sample(q, k, v, cache_k, cache_v, c_cache, c_new, row0):
    b, past, _ = cache_k.shape
    ts = c_new.shape[2]
    tk = min(1024, past)
    nkb = past // tk
    blk0 = row0 // ts
    qspec = pl.BlockSpec((N_PAIRS, ts, LANES), lambda i, j: (0, blk0 + i, 0))
    cspec = pl.BlockSpec((1, tk, MIX_W), lambda i, j: (i, jnp.minimum(j, nkb - 1), 0))
    return pl.pallas_call(
        functools.partial(_attn_sample_kernel, nkb=nkb),
        grid=(b, nkb + 1),
        in_specs=[qspec, cspec, cspec, qspec, qspec,
                  pl.BlockSpec((1, N_HEADS, tk), lambda i, j: (i, 0, jnp.minimum(j, nkb - 1))),
                  pl.BlockSpec((1, N_HEADS, ts), lambda i, j: (i, 0, 0))],
        out_specs=pl.BlockSpec((N_PAIRS, ts, LANES), lambda i, j: (0, i, 0)),
        out_shape=jax.ShapeDtypeStruct((N_PAIRS, b * ts, LANES), BF16),
        scratch_shapes=[pltpu.VMEM((N_PAIRS, 2, ts, LANES), F32), pltpu.VMEM((N_PAIRS, 2, ts, LANES), F32),
                        pltpu.VMEM((N_PAIRS, ts, LANES), F32)],
        compiler_params=_cparams(2),
        name="fox_attn_sample",
    )(q, cache_k, cache_v, k, v, c_cache, c_new)


def _mix_out_kernel(xp_ref, xs_ref, ys_ref, bn_ref, g_ref, yf_ref, ln_ref, wo_ref, gf_ref, rw_ref, rb_ref,
                    x1_o, hf_o, ti_o, tg_o, cnt_o, cnt_scr, *, np_tiles):
    y = jnp.concatenate([ys_ref[p] for p in range(N_PAIRS)], axis=1)
    inv = 1.0 / HEAD_DIM
    mu = _head_sum(y) * inv
    d = y - mu
    var = _head_sum(d * d) * inv
    yn = d * lax.rsqrt(var + GN_EPS) * ln_ref[0:1, :] + ln_ref[1:2, :]
    yr = ((yn + bn_ref[...]) * g_ref[...]).astype(BF16)
    yf = jnp.concatenate([yf_ref[p] for p in range(N_PAIRS)], axis=1)
    mix = jnp.concatenate([yr, yf], axis=1)
    x_res = jnp.where(pl.program_id(0) < np_tiles, xp_ref[...], xs_ref[...])
    x1 = x_res + jnp.dot(mix, wo_ref[...], preferred_element_type=F32)
    x1_o[...] = x1
    hf = _rms(x1, gf_ref[...])
    tm_rows = hf.shape[0]
    for jc in range(ROW_CHUNKS):
        hf_o[pl.ds(jc, tm_rows, stride=ROW_CHUNKS), :] = hf[:, jc * LANES:(jc + 1) * LANES]

    logits = jnp.dot(hf, rw_ref[...], precision=HI, preferred_element_type=F32) + rb_ref[...]
    lane_e = lax.broadcasted_iota(jnp.int32, logits.shape, 1).astype(F32)
    vals = []
    idxs = []
    cur = logits
    for _ in range(TOP_K):
        m = jnp.max(cur, axis=1, keepdims=True)
        am = jnp.min(jnp.where(cur == m, lane_e, float(N_EXPERTS)), axis=1, keepdims=True)
        vals.append(m)
        idxs.append(am)
        cur = jnp.where(lane_e == am, -jnp.inf, cur)
    es = [jnp.exp(vv - vals[0]) for vv in vals]
    tot = es[0] + es[1] + es[2] + es[3]

    @pl.when(pl.program_id(0) == 0)
    def _():
        cnt_scr[...] = jnp.zeros(cnt_scr.shape, F32)

    tm = logits.shape[0]
    sel = [lane_e == idxs[kk] for kk in range(TOP_K)]
    onehot = jnp.where(jnp.logical_or(jnp.logical_or(sel[0], sel[1]), jnp.logical_or(sel[2], sel[3])), 1.0, 0.0)
    rr = lax.broadcasted_iota(jnp.int32, (tm, tm), 0)
    cc = lax.broadcasted_iota(jnp.int32, (tm, tm), 1)
    before = jnp.dot((cc < rr).astype(BF16), onehot.astype(BF16), preferred_element_type=F32) + cnt_scr[...]
    ranks = [jnp.sum(jnp.where(sel[kk], before, 0.0), axis=1, keepdims=True) for kk in range(TOP_K)]
    cnt_scr[...] = cnt_scr[...] + jnp.sum(onehot, axis=0, keepdims=True)
    cnt_o[...] = jnp.broadcast_to(cnt_scr[...], cnt_o.shape).astype(jnp.int32)

    lane = lax.broadcasted_iota(jnp.int32, ti_o.shape, 1)
    ti = jnp.zeros(ti_o.shape, F32)
    tg = jnp.zeros(tg_o.shape, F32)
    for kk in range(TOP_K):
        ti = jnp.where(lane == kk, idxs[kk], ti)
        ti = jnp.where(lane == TOP_K + kk, ranks[kk], ti)
        tg = jnp.where(lane == kk, es[kk] / tot, tg)
    ti_o[...] = ti.astype(jnp.int32)
    tg_o[...] = tg


def _mix_out(x_p, x_s, ys, bonus, g, yf, ln, wo_bf, gf, rw, rb):
    n_p, n_s = x_p.shape[0], x_s.shape[0]
    n = n_p + n_s
    tm = _row_tile(n_p, n_s, 256)
    xp_spec, xs_spec = _two_group_specs(tm, D_MODEL, n_p // tm)
    row = lambda w: pl.BlockSpec((tm, w), lambda i: (i, 0))
    pspec = pl.BlockSpec((N_PAIRS, tm, LANES), lambda i: (0, i, 0))
    full = lambda a: pl.BlockSpec(a.shape, lambda i: (0,) * a.ndim)
    return pl.pallas_call(
        functools.partial(_mix_out_kernel, np_tiles=n_p // tm),
        grid=(n // tm,),
        in_specs=[xp_spec, xs_spec, pspec, row(MIX_W), row(MIX_W), pspec, full(ln), full(wo_bf), full(gf),
                  full(rw), full(rb)],
        out_specs=[row(D_MODEL), pl.BlockSpec((tm * ROW_CHUNKS, LANES), lambda i: (i, 0)), row(LANES), row(LANES),
                   pl.BlockSpec((8, N_EXPERTS), lambda i: (0, 0))],
        out_shape=[jax.ShapeDtypeStruct((n, D_MODEL), F32), jax.ShapeDtypeStruct((n * ROW_CHUNKS, LANES), F32),
                   jax.ShapeDtypeStruct((n, LANES), jnp.int32), jax.ShapeDtypeStruct((n, LANES), F32),
                   jax.ShapeDtypeStruct((8, N_EXPERTS), jnp.int32)],
        scratch_shapes=[pltpu.VMEM((1, N_EXPERTS), F32)],
        compiler_params=_cparams(1),
        name="mix_out_router",
    )(x_p, x_s, ys, bonus, g, yf, ln, wo_bf, gf, rw, rb)


def _moe_gather_kernel(tok_ref, hf_hbm, o_ref, buf, sem):
    i = pl.program_id(0)
    nb = pl.num_programs(0)
    rows = o_ref.shape[0]

    def row_copy(blk, slot, r):
        t = tok_ref[blk * rows + r]
        return pltpu.make_async_copy(hf_hbm.at[pl.ds(pl.multiple_of(t * ROW_CHUNKS, ROW_CHUNKS), ROW_CHUNKS)],
                                     buf.at[slot, pl.ds(pl.multiple_of(r * ROW_CHUNKS, ROW_CHUNKS), ROW_CHUNKS)],
                                     sem.at[slot])

    def issue(blk, slot):
        def body(r, carry):
            row_copy(blk, slot, r).start()
            return carry
        lax.fori_loop(0, rows, body, 0)

    @pl.when(i == 0)
    def _():
        issue(0, 0)

    @pl.when(i + 1 < nb)
    def _():
        issue(i + 1, (i + 1) % 2)

    slot = i % 2

    def wait_body(r, carry):
        row_copy(i, slot, r).wait()
        return carry
    lax.fori_loop(0, rows, wait_body, 0)
    for jc in range(ROW_CHUNKS):
        o_ref[:, jc * LANES:(jc + 1) * LANES] = buf[slot, pl.ds(jc, rows, stride=ROW_CHUNKS), :].astype(BF16)


def _moe_gather(row_tok, hf, n_blocks):
    grid_spec = pltpu.PrefetchScalarGridSpec(
        num_scalar_prefetch=1,
        grid=(n_blocks,),
        in_specs=[pl.BlockSpec(memory_space=pl.ANY)],
        out_specs=pl.BlockSpec((MOE_ROWS, D_MODEL), lambda i, tok: (i, 0)),
        scratch_shapes=[pltpu.VMEM((2, MOE_ROWS * ROW_CHUNKS, LANES), F32), pltpu.SemaphoreType.DMA((2,))],
    )
    return pl.pallas_call(
        _moe_gather_kernel,
        grid_spec=grid_spec,
        out_shape=jax.ShapeDtypeStruct((n_blocks * MOE_ROWS, D_MODEL), BF16),
        compiler_params=_cparams(1),
        name="moe_gather",
    )(row_tok, hf)


STEP_RUN, STEP_NEW_WEIGHTS, STEP_ZERO = 0, 1, 2


def _moe_gu_kernel(se, sw, sb, sj, sf, x_ref, wg_ref, wu_ref, bg_ref, bu_ref, o_ref, wg_bf, wu_bf):
    s = pl.program_id(0)

    @pl.when(sf[s] == STEP_NEW_WEIGHTS)
    def _():
        wg_bf[...] = wg_ref[0].astype(BF16)
        wu_bf[...] = wu_ref[0].astype(BF16)

    @pl.when(sf[s] != STEP_ZERO)
    def _():
        x = x_ref[...]
        g = jnp.dot(x, wg_bf[...], preferred_element_type=F32) + bg_ref[0]
        u = jnp.dot(x, wu_bf[...], preferred_element_type=F32) + bu_ref[0]
        g = jnp.minimum(g, SWIGLU_LIMIT)
        u = jnp.clip(u, -SWIGLU_LIMIT, SWIGLU_LIMIT)
        o_ref[...] = ((u + 1.0) * (g * jax.nn.sigmoid(SWIGLU_ALPHA * g))).astype(BF16)

    @pl.when(sf[s] == STEP_ZERO)
    def _():
        o_ref[...] = jnp.zeros(o_ref.shape, o_ref.dtype)


def _moe_gate_up(sched, xs, w_gu, b_gu, n_blocks):
    nt = D_EXPERT // MOE_TN
    n_steps = nt * n_blocks
    wspec = lambda off: pl.BlockSpec((1, D_MODEL, MOE_TN), lambda s, se, sw, sb, sj, sf: (se[s], 0, off + sw[s]))
    bspec = lambda off: pl.BlockSpec((1, 1, MOE_TN), lambda s, se, sw, sb, sj, sf: (se[s], 0, off + sw[s]))
    grid_spec = pltpu.PrefetchScalarGridSpec(
        num_scalar_prefetch=5,
        grid=(n_steps,),
        in_specs=[pl.BlockSpec((MOE_ROWS, D_MODEL), lambda s, se, sw, sb, sj, sf: (sb[s], 0)),
                  wspec(0), wspec(nt), bspec(0), bspec(nt)],
        out_specs=pl.BlockSpec((MOE_ROWS, MOE_TN), lambda s, se, sw, sb, sj, sf: (sb[s], sj[s])),
        scratch_shapes=[pltpu.VMEM((D_MODEL, MOE_TN), BF16), pltpu.VMEM((D_MODEL, MOE_TN), BF16)],
    )
    return pl.pallas_call(
        _moe_gu_kernel,
        grid_spec=grid_spec,
        out_shape=jax.ShapeDtypeStruct((n_blocks * MOE_ROWS, D_EXPERT), BF16),
        compiler_params=_cparams(1),
        name="moe_gate_up",
    )(*sched, xs, w_gu, w_gu, b_gu, b_gu)


def _moe_dn_kernel(se, sw, sb, sj, sf, h_ref, wd_ref, bd_ref, o_ref, wd_bf):
    s = pl.program_id(0)

    @pl.when(sf[s] == STEP_NEW_WEIGHTS)
    def _():
        wd_bf[...] = wd_ref[0].astype(BF16)

    @pl.when(sf[s] != STEP_ZERO)
    def _():
        o_ref[...] = jnp.dot(h_ref[...], wd_bf[...], preferred_element_type=F32) + bd_ref[0]

    @pl.when(sf[s] == STEP_ZERO)
    def _():
        o_ref[...] = jnp.zeros(o_ref.shape, o_ref.dtype)


def _moe_down(sched, hid, w_dn, b_dn, n_blocks):
    nt = D_MODEL // MOE_TN
    n_steps = nt * n_blocks
    grid_spec = pltpu.PrefetchScalarGridSpec(
        num_scalar_prefetch=5,
        grid=(n_steps,),
        in_specs=[pl.BlockSpec((MOE_ROWS, D_EXPERT), lambda s, se, sw, sb, sj, sf: (sb[s], 0)),
                  pl.BlockSpec((1, D_EXPERT, MOE_TN), lambda s, se, sw, sb, sj, sf: (se[s], 0, sw[s])),
                  pl.BlockSpec((1, 1, MOE_TN), lambda s, se, sw, sb, sj, sf: (se[s], 0, sw[s]))],
        out_specs=pl.BlockSpec((MOE_ROWS, MOE_TN), lambda s, se, sw, sb, sj, sf: (sb[s], sj[s])),
        scratch_shapes=[pltpu.VMEM((D_EXPERT, MOE_TN), BF16)],
    )
    return pl.pallas_call(
        _moe_dn_kernel,
        grid_spec=grid_spec,
        out_shape=jax.ShapeDtypeStruct((n_blocks * MOE_ROWS, D_MODEL), F32),
        compiler_params=_cparams(1),
        name="moe_down",
    )(*sched, hid, w_dn, b_dn)


def _moe_combine_kernel(pos_ref, ys_hbm, x1_ref, tg_ref, o_ref, buf, sem):
    i = pl.program_id(0)
    nb = pl.num_programs(0)
    tm = o_ref.shape[0]

    def row_copy(blk, slot, r, kk):
        src = pos_ref[(blk * tm + r) * TOP_K + kk]
        return pltpu.make_async_copy(ys_hbm.at[pl.ds(src, 1)], buf.at[slot, kk, pl.ds(r, 1)], sem.at[slot])

    def issue(blk, slot):
        def body(r, carry):
            for kk in range(TOP_K):
                row_copy(blk, slot, r, kk).start()
            return carry
        lax.fori_loop(0, tm, body, 0)

    @pl.when(i == 0)
    def _():
        issue(0, 0)

    @pl.when(i + 1 < nb)
    def _():
        issue(i + 1, (i + 1) % 2)

    slot = i % 2

    def wait_body(r, carry):
        for kk in range(TOP_K):
            row_copy(i, slot, r, kk).wait()
        return carry
    lax.fori_loop(0, tm, wait_body, 0)

    tg = tg_ref[...]
    acc = x1_ref[...]
    for kk in range(TOP_K):
        acc = acc + tg[:, kk:kk + 1] * buf[slot, kk]
    o_ref[...] = acc


def _moe_combine(pos, ys, x1, tg):
    n = x1.shape[0]
    tm = min(128, n)
    grid_spec = pltpu.PrefetchScalarGridSpec(
        num_scalar_prefetch=1,
        grid=(n // tm,),
        in_specs=[pl.BlockSpec(memory_space=pl.ANY),
                  pl.BlockSpec((tm, D_MODEL), lambda i, pos: (i, 0)),
                  pl.BlockSpec((tm, LANES), lambda i, pos: (i, 0))],
        out_specs=pl.BlockSpec((tm, D_MODEL), lambda i, pos: (i, 0)),
        scratch_shapes=[pltpu.VMEM((2, TOP_K, tm, D_MODEL), F32), pltpu.SemaphoreType.DMA((2,))],
    )
    return pl.pallas_call(
        _moe_combine_kernel,
        grid_spec=grid_spec,
        out_shape=jax.ShapeDtypeStruct((n, D_MODEL), F32),
        compiler_params=_cparams(1),
        name="moe_combine",
    )(pos, ys, x1, tg)


def _moe_schedule(top_idx, rank, counts, n_blocks):
    n = top_idx.shape[0]
    n_rows = n * TOP_K
    nt = D_EXPERT // MOE_TN
    flat_e = top_idx.reshape(n_rows)
    nb_e = (counts + MOE_ROWS - 1) // MOE_ROWS
    blk_end = jnp.cumsum(nb_e)
    blk_start = blk_end - nb_e
    pos = (blk_start[flat_e] * MOE_ROWS + rank.reshape(n_rows)).astype(jnp.int32)
    total = n_blocks * MOE_ROWS
    row_tok = (jnp.arange(total, dtype=jnp.int32) % n).at[pos].set(jnp.arange(n_rows, dtype=jnp.int32) // TOP_K)
    used = blk_end[-1]
    s = jnp.arange(nt * n_blocks, dtype=jnp.int32)
    live = s < nt * used
    s_eff = jnp.maximum(jnp.minimum(s, nt * used - 1), 0)
    e = jnp.minimum(jnp.sum((s_eff[:, None] >= nt * blk_end[None, :]).astype(jnp.int32), axis=1), N_EXPERTS - 1)
    local = s_eff - nt * blk_start[e]
    nbe = jnp.maximum(nb_e[e], 1)
    sw = (local // nbe).astype(jnp.int32)
    tail = s - nt * used
    sb = jnp.where(live, blk_start[e] + local % nbe, used + tail // nt).astype(jnp.int32)
    sj = jnp.where(live, sw, tail % nt).astype(jnp.int32)
    sf = jnp.where(live, jnp.where(local % nbe == 0, STEP_NEW_WEIGHTS, STEP_RUN), STEP_ZERO).astype(jnp.int32)
    return pos, row_tok, (e, sw, sb, sj, sf)


def _ple_kernel(x_ref, pp_ref, ps_ref, g_ref, wg_ref, wp_ref, op_ref, os_ref, *, np_tiles):
    i = pl.program_id(0)
    x = x_ref[...]
    h = _rms(x, g_ref[...]).astype(BF16)
    gate = jax.nn.sigmoid(jnp.dot(h, wg_ref[...], preferred_element_type=F32))
    p = jnp.where(i < np_tiles, pp_ref[...], ps_ref[...])
    y = x + gate * jnp.dot(p.astype(BF16), wp_ref[...], preferred_element_type=F32)

    @pl.when(i < np_tiles)
    def _():
        op_ref[...] = y

    @pl.when(i >= np_tiles)
    def _():
        os_ref[...] = y


def _ple(x, p_p, p_s, g, wg_bf, wp_bf):
    n_p, n_s = p_p.shape[0], p_s.shape[0]
    n = n_p + n_s
    tm = _row_tile(n_p, n_s, 256)
    np_tiles = n_p // tm
    full = lambda a: pl.BlockSpec(a.shape, lambda i: (0,) * a.ndim)
    pp_spec, ps_spec = _two_group_specs(tm, PLE_DIM, np_tiles)
    op_spec, os_spec = _two_group_specs(tm, D_MODEL, np_tiles)
    return pl.pallas_call(
        functools.partial(_ple_kernel, np_tiles=np_tiles),
        grid=(n // tm,),
        in_specs=[pl.BlockSpec((tm, D_MODEL), lambda i: (i, 0)), pp_spec, ps_spec,
                  full(g), full(wg_bf), full(wp_bf)],
        out_specs=[op_spec, os_spec],
        out_shape=[jax.ShapeDtypeStruct((n_p, D_MODEL), F32), jax.ShapeDtypeStruct((n_s, D_MODEL), F32)],
        compiler_params=_cparams(1),
        name="ple_gate",
    )(x, p_p, p_s, g, wg_bf, wp_bf)


def _pairs_from_state(s):
    b = s.shape[0]
    st = jnp.swapaxes(s, 2, 3).reshape(b, N_PAIRS, 2, HEAD_DIM, HEAD_DIM)
    z = jnp.zeros((b, N_PAIRS, HEAD_DIM, HEAD_DIM), s.dtype)
    top = jnp.concatenate([st[:, :, 0], z], axis=3)
    bot = jnp.concatenate([z, st[:, :, 1]], axis=3)
    return jnp.concatenate([top, bot], axis=2)


def _state_from_pairs(sp):
    b = sp.shape[0]
    st = jnp.stack([sp[:, :, :HEAD_DIM, :HEAD_DIM], sp[:, :, HEAD_DIM:, HEAD_DIM:]], axis=2)
    return jnp.swapaxes(st.reshape(b, N_HEADS, HEAD_DIM, HEAD_DIM), 2, 3)


def _layer(x_p, x_s, cache_k, cache_v, cache_lf, state, shift, p_p, p_s, lw):
    t_p = x_p.shape[1]
    b_s, t_s, _ = x_s.shape
    n_p = x_p.shape[0] * t_p
    n_s = b_s * t_s
    n = n_p + n_s
    past = cache_k.shape[1]
    xp2 = x_p.reshape(n_p, D_MODEL)
    xs2 = x_s.reshape(n_s, D_MODEL)

    w_in_bf = jnp.pad(lw["w_in"], ((0, 0), (0, Z_PAD - lw["w_in"].shape[1]))).astype(BF16)
    z = _in_proj(xp2, xs2, lw["norm_mix_g"], w_in_bf)

    zero_row = jnp.zeros((1, 1, RWKV_PROJ), F32)
    prev = jnp.concatenate([zero_row, shift.astype(F32)], axis=0)
    vecs = jnp.zeros((8, MIX_W), F32)
    vecs = vecs.at[0].set(lw["rwkv_w0"]).at[1].set(lw["rwkv_a0"]).at[2].set(lw["rwkv_kk"])
    vecs = vecs.at[3].set(lw["rwkv_ka"]).at[4].set(lw["rwkv_rk"].reshape(MIX_W))
    wl = jnp.zeros((256, 3 * MIX_W), F32)
    wl = wl.at[0:64, 0:MIX_W].set(lw["rwkv_w2"]).at[64:128, MIX_W:2 * MIX_W].set(lw["rwkv_a2"])
    wl = wl.at[128:256, 2 * MIX_W:].set(lw["rwkv_g2"])
    r, w, k, kk, bb, v_pm, g, bonus = _rwkv_pre(z, prev, lw["rwkv_mu"].reshape(1, RWKV_PROJ), vecs, wl, n_p)
    s0 = jnp.concatenate([jnp.zeros((1, N_PAIRS, LANES, LANES), F32), _pairs_from_state(state.astype(F32))], axis=0)
    y_scan, s_out = _rwkv_scan(r, w, k, kk, bb, v_pm, s0, n_p)
    s_new = _state_from_pairs(s_out)
    shift_new_p = z[n_p - 1:n_p, :RWKV_PROJ].reshape(1, 1, RWKV_PROJ)
    shift_new_s = z[n_p:, :RWKV_PROJ].reshape(b_s, t_s, RWKV_PROJ)[:, -1:, :]

    fvecs = jnp.zeros((8, MIX_W), F32)
    fvecs = fvecs.at[0].set(jnp.tile(lw["fox_q_g"], N_HEADS)).at[1].set(jnp.tile(lw["fox_k_g"], N_HEADS))
    bf = jnp.zeros((1, LANES), F32).at[0, :N_HEADS].set(lw["fox_b_f"])
    q_pm, k_pm, vv_pm, k_new_p, v_new_p, k_new_s, v_new_s, lf = _fox_pre(z, fvecs, bf, n_p)
    lf = lf[:, :N_HEADS]
    lf_p = lf[:n_p].T.reshape(1, N_HEADS, n_p)
    c_p = _cumsum_lanes(lf_p).reshape(N_PAIRS, 2, n_p)
    yf_p = _attn_prompt(q_pm, k_pm, vv_pm, c_p, n_p)
    lf_s = jnp.swapaxes(lf[n_p:].reshape(b_s, t_s, N_HEADS), 1, 2)
    lf_all = jnp.concatenate([jnp.swapaxes(cache_lf.astype(F32), 1, 2), lf_s], axis=2)
    pad = (-lf_all.shape[2]) % LANES
    c_all = _cumsum_lanes(jnp.pad(lf_all, ((0, 0), (0, 0), (0, pad))))
    yf_s = _attn_sample(q_pm, k_pm, vv_pm, cache_k.reshape(b_s, past, MIX_W), cache_v.reshape(b_s, past, MIX_W),
                        c_all[:, :, :past], c_all[:, :, past:past + t_s], n_p)
    yf = jnp.concatenate([yf_p, yf_s], axis=1)

    ln = jnp.stack([lw["rwkv_ln_g"], lw["rwkv_ln_b"]])
    x1, hf, ti, tg, cnt = _mix_out(xp2, xs2, y_scan, bonus, g, yf, ln, lw["w_out"].astype(BF16),
                                   lw["norm_ffn_g"].reshape(1, D_MODEL), lw["router_w"],
                                   lw["router_b"].reshape(1, N_EXPERTS))

    n_blocks = n * TOP_K // MOE_ROWS + N_EXPERTS
    pos, row_tok, sched = _moe_schedule(ti[:, :TOP_K], ti[:, TOP_K:2 * TOP_K], cnt[0], n_blocks)
    xs = _moe_gather(row_tok, hf, n_blocks)
    hid = _moe_gate_up(sched, xs, lw["expert_w_gu"], lw["expert_b_gu"].reshape(N_EXPERTS, 1, 2 * D_EXPERT), n_blocks)
    ys = _moe_down(sched, hid, lw["expert_w_down"], lw["expert_b_down"].reshape(N_EXPERTS, 1, D_MODEL), n_blocks)
    x2 = _moe_combine(pos, ys, x1, tg)

    y_p, y_s = _ple(x2, p_p, p_s, lw["ple_norm_g"].reshape(1, D_MODEL), lw["ple_w_gate"].astype(BF16),
                    lw["ple_w_proj"].astype(BF16))

    heads = lambda a, bsz, t: a.reshape(bsz, t, N_HEADS, HEAD_DIM)
    out_p = (y_p.reshape(x_p.shape), heads(k_new_p, 1, n_p), heads(v_new_p, 1, n_p),
             lf[:n_p].reshape(1, n_p, N_HEADS), s_new[:1], shift_new_p)
    out_s = (y_s.reshape(x_s.shape), heads(k_new_s, b_s, t_s), heads(v_new_s, b_s, t_s),
             lf[n_p:].reshape(b_s, t_s, N_HEADS), s_new[1:], shift_new_s)
    return out_p, out_s


def kernel(x_prompt, x_sample, cache_fox_k, cache_fox_v, cache_fox_logf, state_rwkv, state_rwkv_shift, p_prompt, p_sample, norm_mix_g, w_in, rwkv_mu, rwkv_w0, rwkv_w2, rwkv_a0, rwkv_a2, rwkv_g2, rwkv_kk, rwkv_ka, rwkv_rk, rwkv_ln_g, rwkv_ln_b, fox_q_g, fox_k_g, fox_b_f, w_out, norm_ffn_g, router_w, router_b, expert_w_gu, expert_b_gu, expert_w_down, expert_b_down, ple_norm_g, ple_w_gate, ple_w_proj):
    assert x_prompt.shape[0] == 1 and w_in.shape[0] == 1, "one prompt stream, one layer"
    lw = dict(norm_mix_g=norm_mix_g[0], w_in=w_in[0], rwkv_mu=rwkv_mu[0], rwkv_w0=rwkv_w0[0], rwkv_w2=rwkv_w2[0],
              rwkv_a0=rwkv_a0[0], rwkv_a2=rwkv_a2[0], rwkv_g2=rwkv_g2[0], rwkv_kk=rwkv_kk[0], rwkv_ka=rwkv_ka[0],
              rwkv_rk=rwkv_rk[0], rwkv_ln_g=rwkv_ln_g[0], rwkv_ln_b=rwkv_ln_b[0], fox_q_g=fox_q_g[0],
              fox_k_g=fox_k_g[0], fox_b_f=fox_b_f[0], w_out=w_out[0], norm_ffn_g=norm_ffn_g[0],
              router_w=router_w[0], router_b=router_b[0], expert_w_gu=expert_w_gu[0], expert_b_gu=expert_b_gu[0],
              expert_w_down=expert_w_down[0], expert_b_down=expert_b_down[0], ple_norm_g=ple_norm_g[0],
              ple_w_gate=ple_w_gate[0], ple_w_proj=ple_w_proj[0])
    n_p = x_prompt.shape[1]
    (y_p, k_p, v_p, lf_p, s_p, sh_p), (y_s, k_s, v_s, lf_s, s_s, sh_s) = _layer(
        x_prompt, x_sample, cache_fox_k[0], cache_fox_v[0], cache_fox_logf[0], state_rwkv[0],
        state_rwkv_shift[0], p_prompt[0].reshape(n_p, PLE_DIM), p_sample[0].reshape(-1, PLE_DIM), lw)
    add = lambda a: a[None]
    return (y_p, y_s, add(k_p), add(v_p), add(lf_p), add(s_p), add(sh_p),
            add(k_s), add(v_s), add(lf_s), add(s_s), add(sh_s))
```

```python
import functools

import numpy as np
import jax
import jax.numpy as jnp
from jax import lax
from jax.experimental import pallas as pl
from jax.experimental.pallas import tpu as pltpu

F32 = jnp.float32
BF16 = jnp.bfloat16
HI = lax.Precision.HIGHEST

D_MODEL = 2048
HEAD_DIM = 64
N_HEADS = 16
N_PAIRS = N_HEADS // 2
MIX_W = N_HEADS * HEAD_DIM
CHUNK = 64
RWKV_PROJ = 3 * MIX_W + 64 + 64 + 128
FOX_PROJ = 3 * MIX_W + N_HEADS
Z_HALF = RWKV_PROJ
Z_PAD = 2 * Z_HALF
N_EXPERTS = 32
TOP_K = 4
D_EXPERT = 2048
SWIGLU_LIMIT = 7.0
SWIGLU_ALPHA = 1.702
PLE_DIM = 256
RMS_EPS = 1e-6
GN_EPS = 64e-5
L2_EPS = 1e-12
NEG_BIG = -1e30
LOG2E = 1.4426950408889634

LANES = 128
MOE_ROWS = 256
MOE_TN = 1024
ROW_CHUNKS = D_MODEL // LANES
VMEM_LIMIT = 52 * 1024 * 1024


def _cparams(n_axes, vmem=VMEM_LIMIT):
    return pltpu.CompilerParams(dimension_semantics=("arbitrary",) * n_axes, vmem_limit_bytes=vmem)


def _head_sum(x):
    r = lax.broadcasted_iota(jnp.int32, (LANES, LANES), 0) // HEAD_DIM
    c = lax.broadcasted_iota(jnp.int32, (LANES, LANES), 1) // HEAD_DIM
    bd = jnp.where(r == c, 1.0, 0.0).astype(BF16)
    hi = x.astype(BF16)
    lo = (x - hi.astype(F32)).astype(BF16)
    parts = []
    for i in range(x.shape[1] // LANES):
        sl = slice(i * LANES, (i + 1) * LANES)
        parts.append(jnp.dot(hi[:, sl], bd, preferred_element_type=F32)
                     + jnp.dot(lo[:, sl], bd, preferred_element_type=F32))
    return parts[0] if len(parts) == 1 else jnp.concatenate(parts, axis=1)


def _log_sigmoid(x):
    return jnp.minimum(x, 0.0) - jnp.log1p(jnp.exp(-jnp.abs(x)))


def _rms(x, g):
    ms = jnp.mean(x * x, axis=-1, keepdims=True)
    return x * lax.rsqrt(ms + RMS_EPS) * g


def _row_tile(n_p, n_s, pref):
    return pref if (n_p % pref == 0 and n_s % pref == 0) else 128


def _two_group_specs(tm, width, np_tiles, n_grid_axes=1):
    if n_grid_axes == 1:
        return (pl.BlockSpec((tm, width), lambda i: (jnp.minimum(i, np_tiles - 1), 0)),
                pl.BlockSpec((tm, width), lambda i: (jnp.maximum(i - np_tiles, 0), 0)))
    return (pl.BlockSpec((tm, width), lambda i, j: (jnp.minimum(i, np_tiles - 1), 0)),
            pl.BlockSpec((tm, width), lambda i, j: (jnp.maximum(i - np_tiles, 0), 0)))


def _inproj_kernel(xp_ref, xs_ref, g_ref, w_ref, o_ref, h_scr, *, np_tiles):
    i = pl.program_id(0)
    first = pl.program_id(1) == 0

    @pl.when(jnp.logical_and(first, i < np_tiles))
    def _():
        h_scr[...] = _rms(xp_ref[...], g_ref[...]).astype(BF16)

    @pl.when(jnp.logical_and(first, i >= np_tiles))
    def _():
        h_scr[...] = _rms(xs_ref[...], g_ref[...]).astype(BF16)

    o_ref[...] = jnp.dot(h_scr[...], w_ref[...], preferred_element_type=F32)


def _in_proj(x_p, x_s, g, w_bf):
    n_p, n_s = x_p.shape[0], x_s.shape[0]
    n = n_p + n_s
    tm = _row_tile(n_p, n_s, 512)
    tn = Z_PAD // 4
    xp_spec, xs_spec = _two_group_specs(tm, D_MODEL, n_p // tm, 2)
    return pl.pallas_call(
        functools.partial(_inproj_kernel, np_tiles=n_p // tm),
        grid=(n // tm, Z_PAD // tn),
        in_specs=[xp_spec, xs_spec,
                  pl.BlockSpec((1, D_MODEL), lambda i, j: (0, 0)),
                  pl.BlockSpec((D_MODEL, tn), lambda i, j: (0, j))],
        out_specs=pl.BlockSpec((tm, tn), lambda i, j: (i, j)),
        out_shape=jax.ShapeDtypeStruct((n, Z_PAD), F32),
        scratch_shapes=[pltpu.VMEM((tm, D_MODEL), BF16)],
        compiler_params=_cparams(2),
        name="in_proj",
    )(x_p, x_s, g.reshape(1, D_MODEL), w_bf)


def _rwkv_pre_kernel(z_ref, prev_ref, mu_ref, vec_ref, wl_ref,
                     r_o, w_o, k_o, kk_o, b_o, v_o, g_o, bn_o, carry, *, n_prompt_tiles):
    i = pl.program_id(0)
    z = z_ref[...]
    tm = z.shape[0]

    @pl.when(i == 0)
    def _():
        carry[...] = prev_ref[0, 0:1, :]

    rolled = pltpu.roll(z, 1, axis=0)
    row = lax.broadcasted_iota(jnp.int32, z.shape, 0)
    shifted = jnp.where(row == 0, carry[...], rolled)
    is_sample = i >= n_prompt_tiles
    for c in range(tm // CHUNK):
        shifted = jnp.where(jnp.logical_and(is_sample, row == c * CHUNK), prev_ref[0, c:c + 1, :], shifted)
    carry[...] = z[tm - 1:tm, :]
    zs = z + mu_ref[...] * (shifted - z)

    r = zs[:, 0:MIX_W]
    k = zs[:, MIX_W:2 * MIX_W]
    v = zs[:, 2 * MIX_W:3 * MIX_W]
    lo = zs[:, 3 * MIX_W:RWKV_PROJ]
    lane = lax.broadcasted_iota(jnp.int32, lo.shape, 1)
    f = jnp.where(lane < 64, jnp.tanh(lo), jnp.where(lane < 128, lo, jax.nn.sigmoid(lo)))
    lora = _dot3(f, wl_ref[...])
    w_pre = vec_ref[0:1, :] + lora[:, 0:MIX_W]
    log_decay = -jnp.exp(_log_sigmoid(w_pre) - 0.5)
    a = jax.nn.sigmoid(vec_ref[1:2, :] + lora[:, MIX_W:2 * MIX_W])
    g = lora[:, 2 * MIX_W:3 * MIX_W]
    kk = k * vec_ref[2:3, :]
    kk = kk * lax.rsqrt(_head_sum(kk * kk) + L2_EPS)
    k_mod = k * (1.0 + (a - 1.0) * vec_ref[3:4, :])
    bonus = _head_sum(r * k_mod * vec_ref[4:5, :]) * v
    kka = kk * a

    g_o[...] = g
    bn_o[...] = bonus
    for p in range(N_PAIRS):
        sl = slice(p * LANES, (p + 1) * LANES)
        r_o[p] = r[:, sl]
        w_o[p] = log_decay[:, sl]
        k_o[p] = k_mod[:, sl]
        kk_o[p] = kk[:, sl]
        b_o[p] = kka[:, sl]
        v_o[p] = v[:, sl]


def _rwkv_pre(z, shift, mu, vecs, wl, n_prompt):
    n = z.shape[0]
    n_s = n - n_prompt
    tm = _row_tile(n_prompt, n_s, 256)
    if tm % CHUNK:
        tm = CHUNK
    cpt = tm // CHUNK
    nt = n // tm
    n_prompt_tiles = n_prompt // tm
    prev = jnp.concatenate([jnp.zeros((1, cpt, RWKV_PROJ), F32),
                            shift.astype(F32).reshape(n_s // tm, cpt, RWKV_PROJ)], axis=0)
    tok = jax.ShapeDtypeStruct((n, MIX_W), F32)
    tspec = pl.BlockSpec((tm, MIX_W), lambda i: (i, 0))
    pm = jax.ShapeDtypeStruct((N_PAIRS, n, LANES), F32)
    pspec = pl.BlockSpec((N_PAIRS, tm, LANES), lambda i: (0, i, 0))
    return pl.pallas_call(
        functools.partial(_rwkv_pre_kernel, n_prompt_tiles=n_prompt_tiles),
        grid=(nt,),
        in_specs=[pl.BlockSpec((tm, Z_HALF), lambda i: (i, 0)),
                  pl.BlockSpec((1, cpt, RWKV_PROJ), lambda i: (jnp.maximum(i - (n_prompt_tiles - 1), 0), 0, 0)),
                  pl.BlockSpec((1, RWKV_PROJ), lambda i: (0, 0)),
                  pl.BlockSpec((8, MIX_W), lambda i: (0, 0)),
                  pl.BlockSpec((256, 3 * MIX_W), lambda i: (0, 0))],
        out_specs=[pspec, pspec, pspec, pspec, pspec, pspec, tspec, tspec],
        out_shape=[pm, pm, pm, pm, pm, pm, tok, tok],
        scratch_shapes=[pltpu.VMEM((1, RWKV_PROJ), F32)],
        compiler_params=_cparams(1),
        name="rwkv_pre",
    )(z, prev, mu, vecs, wl)


def _dot(a, b):
    return jnp.dot(a, b, precision=HI, preferred_element_type=F32)


def _bdot(a, b):
    return jnp.dot(a, b, preferred_element_type=F32)


def _split_bf16(x):
    hi = x.astype(BF16)
    return hi, (x - hi.astype(F32)).astype(BF16)


def _dot3(a, b):
    a_hi, a_lo = _split_bf16(a)
    b_hi, b_lo = _split_bf16(b)
    return _bdot(a_hi, b_hi) + (_bdot(a_hi, b_lo) + _bdot(a_lo, b_hi))


def _pair_rows(x):
    lo_half = lax.broadcasted_iota(jnp.int32, x.shape, 1) < HEAD_DIM
    return jnp.concatenate([jnp.where(lo_half, x, 0.0), jnp.where(lo_half, 0.0, x)], axis=0)


def _scan_chunk(P, r, lw, k, kk, b, v):
    c2 = 2 * CHUNK
    i = lax.broadcasted_iota(jnp.int32, (c2, c2), 0)
    j = lax.broadcasted_iota(jnp.int32, (c2, c2), 1)
    ti = lax.broadcasted_iota(jnp.int32, (CHUNK, CHUNK), 0)
    tj = lax.broadcasted_iota(jnp.int32, (CHUNK, CHUNK), 1)
    cl = _dot((tj <= ti).astype(F32), lw)
    yield
    g_end = cl[CHUNK - 1:CHUNK, :]
    e_neg = jnp.exp(-cl)
    e_end = jnp.exp(g_end - cl)
    kap = _pair_rows(kk * jnp.exp(cl - lw)).astype(BF16)
    rt = _pair_rows(r * jnp.exp(cl)).astype(BF16)
    bt = _pair_rows(b * e_neg).astype(BF16)
    kt = _pair_rows(k * e_neg).astype(BF16)
    kh = _pair_rows(k * e_end)
    bh = _pair_rows(b * e_end)
    vv = _pair_rows(v)
    vv_b = vv.astype(BF16)
    p_b = P.astype(BF16)

    g = lax.dot_general(jnp.concatenate([kap, rt], axis=0), jnp.concatenate([bt, kt], axis=0),
                        (((1,), (1,)), ((), ())), preferred_element_type=F32)
    yield
    strict = j < i
    incl = j <= i
    a_b = jnp.where(strict, g[:c2, :c2], 0.0)
    a_bb = a_b.astype(BF16)
    a_k = jnp.where(strict, g[:c2, c2:], 0.0).astype(BF16)
    r_b = jnp.where(incl, g[c2:, :c2], 0.0).astype(BF16)
    r_k = jnp.where(incl, g[c2:, c2:], 0.0).astype(BF16)

    t_inv = (i == j).astype(F32) - jnp.where(jnp.logical_and((i & 1) == 1, j == i - 1), a_b, 0.0)
    n = 2
    while n < CHUNK:
        m = jnp.logical_and((i >> n.bit_length()) == (j >> n.bit_length()),
                            jnp.logical_and((i & (2 * n - 1)) >= n, (j & (2 * n - 1)) < n))
        t_b = t_inv.astype(BF16)
        ta = _bdot(t_b, a_bb).astype(BF16)
        yield
        t_inv = t_inv - jnp.where(m, _bdot(ta, t_b), 0.0)
        yield
        n *= 2

    w = _bdot(jnp.concatenate([kap, a_k], axis=1), jnp.concatenate([p_b, vv_b], axis=0))
    yield
    u = _bdot(t_inv.astype(BF16), w.astype(BF16))
    yield
    vu_b = jnp.concatenate([vv_b, u.astype(BF16)], axis=0)
    y2 = _bdot(rt, p_b) + _bdot(jnp.concatenate([r_k, -r_b], axis=1), vu_b)
    y = y2[:CHUNK] + y2[CHUNK:]
    yield
    g_col = jnp.broadcast_to(jnp.exp(g_end), (c2, c2)).T
    p_new = g_col * P + _dot3(jnp.concatenate([kh.T, -bh.T], axis=1), jnp.concatenate([vv, u], axis=0))
    return y, p_new


def _run_interleaved(gens):
    results = [None] * len(gens)
    live = list(range(len(gens)))
    while live:
        for idx in list(live):
            try:
                next(gens[idx])
            except StopIteration as stop:
                results[idx] = stop.value
                live.remove(idx)
    return results


def _scan_kernel(r_ref, w_ref, k_ref, kk_ref, b_ref, v_ref, s0_ref, y_ref, sout_ref, s_scr, *, n_prompt_chunks):
    c = pl.program_id(0)

    @pl.when(jnp.logical_or(c == 0, c >= n_prompt_chunks))
    def _():
        s_scr[...] = s0_ref[0]

    outs = _run_interleaved([
        _scan_chunk(s_scr[p], r_ref[p], w_ref[p], k_ref[p], kk_ref[p], b_ref[p], v_ref[p])
        for p in range(N_PAIRS)])
    for p, (y, p_new) in enumerate(outs):
        y_ref[p] = y
        s_scr[p] = p_new
        sout_ref[0, p] = p_new


def _rwkv_scan(r, w, k, kk, b, v, s0, n_prompt):
    n = r.shape[1]
    npc = n_prompt // CHUNK
    n_seq = s0.shape[0]
    pspec = pl.BlockSpec((N_PAIRS, CHUNK, LANES), lambda c: (0, c, 0))
    sspec = pl.BlockSpec((1, N_PAIRS, LANES, LANES), lambda c: (jnp.maximum(c - (npc - 1), 0), 0, 0, 0))
    return pl.pallas_call(
        functools.partial(_scan_kernel, n_prompt_chunks=npc),
        grid=(n // CHUNK,),
        in_specs=[pspec, pspec, pspec, pspec, pspec, pspec, sspec],
        out_specs=[pspec, sspec],
        out_shape=[jax.ShapeDtypeStruct((N_PAIRS, n, LANES), F32),
                   jax.ShapeDtypeStruct((n_seq, N_PAIRS, LANES, LANES), F32)],
        scratch_shapes=[pltpu.VMEM((N_PAIRS, LANES, LANES), F32)],
        compiler_params=_cparams(1),
        name="rwkv_scan",
    )(r, w, k, kk, b, v, s0)


def _fox_pre_kernel(z_ref, vec_ref, bf_ref, q_o, k_o, v_o, knp_o, vnp_o, kns_o, vns_o, lf_o, *, np_tiles):
    z = z_ref[...]
    q = z[:, 0:MIX_W]
    k = z[:, MIX_W:2 * MIX_W]
    v = z[:, 2 * MIX_W:3 * MIX_W]
    fl = z[:, 3 * MIX_W:3 * MIX_W + LANES]
    inv = 1.0 / HEAD_DIM
    qn = q * lax.rsqrt(_head_sum(q * q) * inv + RMS_EPS) * vec_ref[0:1, :]
    kn = k * lax.rsqrt(_head_sum(k * k) * inv + RMS_EPS) * vec_ref[1:2, :]
    qs = (qn * (HEAD_DIM ** -0.5 * LOG2E)).astype(BF16)
    kb = kn.astype(BF16)
    vb = v.astype(BF16)
    for p in range(N_PAIRS):
        sl = slice(p * LANES, (p + 1) * LANES)
        q_o[p] = qs[:, sl]
        k_o[p] = kb[:, sl]
        v_o[p] = vb[:, sl]
    lf_o[...] = _log_sigmoid(fl + bf_ref[...])

    @pl.when(pl.program_id(0) < np_tiles)
    def _():
        knp_o[...] = kn
        vnp_o[...] = v

    @pl.when(pl.program_id(0) >= np_tiles)
    def _():
        kns_o[...] = kn
        vns_o[...] = v


def _fox_pre(z, vecs, bf, n_p):
    n = z.shape[0]
    n_s = n - n_p
    tm = _row_tile(n_p, n_s, 256)
    np_tiles = n_p // tm
    pm = jax.ShapeDtypeStruct((N_PAIRS, n, LANES), BF16)
    pspec = pl.BlockSpec((N_PAIRS, tm, LANES), lambda i: (0, i, 0))
    p_spec, s_spec = _two_group_specs(tm, MIX_W, np_tiles)
    tok_p = jax.ShapeDtypeStruct((n_p, MIX_W), F32)
    tok_s = jax.ShapeDtypeStruct((n_s, MIX_W), F32)
    return pl.pallas_call(
        functools.partial(_fox_pre_kernel, np_tiles=np_tiles),
        grid=(n // tm,),
        in_specs=[pl.BlockSpec((tm, Z_HALF), lambda i: (i, 1)),
                  pl.BlockSpec((8, MIX_W), lambda i: (0, 0)),
                  pl.BlockSpec((1, LANES), lambda i: (0, 0))],
        out_specs=[pspec, pspec, pspec, p_spec, p_spec, s_spec, s_spec,
                   pl.BlockSpec((tm, LANES), lambda i: (i, 0))],
        out_shape=[pm, pm, pm, tok_p, tok_p, tok_s, tok_s, jax.ShapeDtypeStruct((n, LANES), F32)],
        compiler_params=_cparams(1),
        name="fox_pre",
    )(z, vecs, bf)


def _cumsum_kernel(x_ref, o_ref):
    r = lax.broadcasted_iota(jnp.int32, (LANES, LANES), 0)
    c = lax.broadcasted_iota(jnp.int32, (LANES, LANES), 1)
    tri = (r <= c).astype(F32)
    carry = jnp.zeros((N_HEADS, 1), F32)
    for i in range(x_ref.shape[2] // LANES):
        sl = slice(i * LANES, (i + 1) * LANES)
        cs = jnp.dot(x_ref[0, :, sl], tri, precision=HI, preferred_element_type=F32) + carry
        o_ref[0, :, sl] = cs
        carry = cs[:, LANES - 1:LANES]


def _cumsum_lanes(x):
    b, h, t = x.shape
    return pl.pallas_call(
        _cumsum_kernel,
        grid=(b,),
        in_specs=[pl.BlockSpec((1, h, t), lambda i: (i, 0, 0))],
        out_specs=pl.BlockSpec((1, h, t), lambda i: (i, 0, 0)),
        out_shape=jax.ShapeDtypeStruct((b, h, t), F32),
        compiler_params=_cparams(1),
        name="cumsum_logf",
    )(x)


def _rep_lanes(m, tk):
    if tk % LANES == 0:
        return m if tk == LANES else jnp.concatenate([m] * (tk // LANES), axis=1)
    return m[:, :tk]


def _attn_update(qa, qb, kb, vb, ck, mask, m_ref, l_ref, acc_ref):
    tk = kb.shape[0]
    lo_half = lax.broadcasted_iota(jnp.int32, acc_ref.shape, 1) < HEAD_DIM
    ck2 = ck * LOG2E
    pv = []
    alphas = []
    for h, qh in enumerate((qa, qb)):
        s = lax.dot_general(qh, kb, (((1,), (1,)), ((), ())), preferred_element_type=F32)
        s = s - ck2[h:h + 1, :]
        if mask is not None:
            s = jnp.where(mask, s, NEG_BIG)
        m_prev = m_ref[h]
        m_next = jnp.maximum(m_prev, jnp.max(s, axis=1, keepdims=True))
        p = jnp.exp2(s - _rep_lanes(m_next, tk))
        alpha = jnp.exp2(m_prev - m_next)
        l_ref[h] = alpha * l_ref[h] + jnp.sum(p, axis=1, keepdims=True)
        m_ref[h] = m_next
        pv.append(jnp.dot(p.astype(BF16), vb, preferred_element_type=F32))
        alphas.append(alpha)
    acc_ref[...] = acc_ref[...] * jnp.where(lo_half, alphas[0], alphas[1]) + jnp.where(lo_half, pv[0], pv[1])


def _split_q(q):
    qf = q.astype(F32)
    lo_half = lax.broadcasted_iota(jnp.int32, qf.shape, 1) < HEAD_DIM
    return jnp.where(lo_half, qf, 0.0).astype(BF16), jnp.where(lo_half, 0.0, qf).astype(BF16)


def _attn_prompt_kernel(qi_ref, ki_ref, q_ref, k_ref, v_ref, c_ref, o_ref,
                        qa_scr, qb_scr, m_scr, l_scr, acc_scr, *, tq, tk):
    s_id = pl.program_id(1)
    qi = qi_ref[s_id]
    ki = ki_ref[s_id]

    @pl.when(ki == 0)
    def _():
        qa, qb = _split_q(q_ref[0])
        qa_scr[...] = qa
        qb_scr[...] = qb
        m_scr[...] = jnp.full(m_scr.shape, NEG_BIG, F32)
        l_scr[...] = jnp.zeros(l_scr.shape, F32)
        acc_scr[...] = jnp.zeros(acc_scr.shape, F32)

    crosses_diagonal = ki * tk + (tk - 1) > qi * tq

    @pl.when(crosses_diagonal)
    def _():
        rows = qi * tq + lax.broadcasted_iota(jnp.int32, (tq, tk), 0)
        cols = ki * tk + lax.broadcasted_iota(jnp.int32, (tq, tk), 1)
        _attn_update(qa_scr[...], qb_scr[...], k_ref[0], v_ref[0], c_ref[0], cols <= rows, m_scr, l_scr, acc_scr)

    @pl.when(jnp.logical_not(crosses_diagonal))
    def _():
        _attn_update(qa_scr[...], qb_scr[...], k_ref[0], v_ref[0], c_ref[0], None, m_scr, l_scr, acc_scr)

    @pl.when(ki == ((qi + 1) * tq - 1) // tk)
    def _():
        lo_half = lax.broadcasted_iota(jnp.int32, (tq, LANES), 1) < HEAD_DIM
        o_ref[0] = (acc_scr[...] / jnp.where(lo_half, l_scr[0], l_scr[1])).astype(o_ref.dtype)


def _attn_prompt(q, k, v, c, t):
    tq = min(1024, t)
    tk = min(512, t)
    steps = [(qi, ki) for qi in range(t // tq) for ki in range(((qi + 1) * tq - 1) // tk + 1)]
    qi_arr = jnp.asarray(np.array([s[0] for s in steps], np.int32))
    ki_arr = jnp.asarray(np.array([s[1] for s in steps], np.int32))
    grid_spec = pltpu.PrefetchScalarGridSpec(
        num_scalar_prefetch=2,
        grid=(N_PAIRS, len(steps)),
        in_specs=[pl.BlockSpec((1, tq, LANES), lambda p, s, qi, ki: (p, qi[s], 0)),
                  pl.BlockSpec((1, tk, LANES), lambda p, s, qi, ki: (p, ki[s], 0)),
                  pl.BlockSpec((1, tk, LANES), lambda p, s, qi, ki: (p, ki[s], 0)),
                  pl.BlockSpec((1, 2, tk), lambda p, s, qi, ki: (p, 0, ki[s]))],
        out_specs=pl.BlockSpec((1, tq, LANES), lambda p, s, qi, ki: (p, qi[s], 0)),
        scratch_shapes=[pltpu.VMEM((tq, LANES), BF16), pltpu.VMEM((tq, LANES), BF16),
                        pltpu.VMEM((2, tq, LANES), F32), pltpu.VMEM((2, tq, LANES), F32),
                        pltpu.VMEM((tq, LANES), F32)],
    )
    return pl.pallas_call(
        functools.partial(_attn_prompt_kernel, tq=tq, tk=tk),
        grid_spec=grid_spec,
        out_shape=jax.ShapeDtypeStruct((N_PAIRS, t, LANES), BF16),
        compiler_params=_cparams(2),
        name="fox_attn_prompt",
    )(qi_arr, ki_arr, q, k, v, c)


def _attn_sample_kernel(q_ref, ck_ref, cv_ref, kn_ref, vn_ref, cc_ref, cn_ref, o_ref,
                        m_scr, l_scr, acc_scr, *, nkb):
    j = pl.program_id(1)
    ts = q_ref.shape[1]

    @pl.when(j == 0)
    def _():
        m_scr[...] = jnp.full(m_scr.shape, NEG_BIG, F32)
        l_scr[...] = jnp.zeros(l_scr.shape, F32)
        acc_scr[...] = jnp.zeros(acc_scr.shape, F32)

    @pl.when(j < nkb)
    def _():
        for p in range(N_PAIRS):
            sl = slice(p * LANES, (p + 1) * LANES)
            qa, qb = _split_q(q_ref[p])
            _attn_update(qa, qb, ck_ref[0, :, sl].astype(BF16), cv_ref[0, :, sl].astype(BF16),
                         cc_ref[0, 2 * p:2 * p + 2, :], None, m_scr.at[p], l_scr.at[p], acc_scr.at[p])

    @pl.when(j == nkb)
    def _():
        rows = lax.broadcasted_iota(jnp.int32, (ts, ts), 0)
        cols = lax.broadcasted_iota(jnp.int32, (ts, ts), 1)
        lo_half = lax.broadcasted_iota(jnp.int32, (ts, LANES), 1) < HEAD_DIM
        for p in range(N_PAIRS):
            qa, qb = _split_q(q_ref[p])
            _attn_update(qa, qb, kn_ref[p], vn_ref[p], cn_ref[0, 2 * p:2 * p + 2, :], cols <= rows,
                         m_scr.at[p], l_scr.at[p], acc_scr.at[p])
            o_ref[p] = (acc_scr[p] / jnp.where(lo_half, l_scr[p, 0], l_scr[p, 1])).astype(o_ref.dtype)


def _attn_sample(q, k, v, cache_k, cache_v, c_cache, c_new, row0):
    b, past, _ = cache_k.shape
    ts = c_new.shape[2]
    tk = min(1024, past)
    nkb = past // tk
    blk0 = row0 // ts
    qspec = pl.BlockSpec((N_PAIRS, ts, LANES), lambda i, j: (0, blk0 + i, 0))
    cspec = pl.BlockSpec((1, tk, MIX_W), lambda i, j: (i, jnp.minimum(j, nkb - 1), 0))
    return pl.pallas_call(
        functools.partial(_attn_sample_kernel, nkb=nkb),
        grid=(b, nkb + 1),
        in_specs=[qspec, cspec, cspec, qspec, qspec,
                  pl.BlockSpec((1, N_HEADS, tk), lambda i, j: (i, 0, jnp.minimum(j, nkb - 1))),
                  pl.BlockSpec((1, N_HEADS, ts), lambda i, j: (i, 0, 0))],
        out_specs=pl.BlockSpec((N_PAIRS, ts, LANES), lambda i, j: (0, i, 0)),
        out_shape=jax.ShapeDtypeStruct((N_PAIRS, b * ts, LANES), BF16),
        scratch_shapes=[pltpu.VMEM((N_PAIRS, 2, ts, LANES), F32), pltpu.VMEM((N_PAIRS, 2, ts, LANES), F32),
                        pltpu.VMEM((N_PAIRS, ts, LANES), F32)],
        compiler_params=_cparams(2),
        name="fox_attn_sample",
    )(q, cache_k, cache_v, k, v, c_cache, c_new)


def _mix_out_kernel(xp_ref, xs_ref, ys_ref, bn_ref, g_ref, yf_ref, ln_ref, wo_ref, gf_ref, rw_ref, rb_ref,
                    x1_o, hf_o, ti_o, tg_o, cnt_o, cnt_scr, *, np_tiles):
    y = jnp.concatenate([ys_ref[p] for p in range(N_PAIRS)], axis=1)
    inv = 1.0 / HEAD_DIM
    mu = _head_sum(y) * inv
    d = y - mu
    var = _head_sum(d * d) * inv
    yn = d * lax.rsqrt(var + GN_EPS) * ln_ref[0:1, :] + ln_ref[1:2, :]
    yr = ((yn + bn_ref[...]) * g_ref[...]).astype(BF16)
    yf = jnp.concatenate([yf_ref[p] for p in range(N_PAIRS)], axis=1)
    mix = jnp.concatenate([yr, yf], axis=1)
    x_res = jnp.where(pl.program_id(0) < np_tiles, xp_ref[...], xs_ref[...])
    x1 = x_res + jnp.dot(mix, wo_ref[...], preferred_element_type=F32)
    x1_o[...] = x1
    hf = _rms(x1, gf_ref[...])
    tm_rows = hf.shape[0]
    for jc in range(ROW_CHUNKS):
        hf_o[pl.ds(jc, tm_rows, stride=ROW_CHUNKS), :] = hf[:, jc * LANES:(jc + 1) * LANES]

    logits = _dot3(hf, rw_ref[...]) + rb_ref[...]
    lane_e = lax.broadcasted_iota(jnp.int32, logits.shape, 1).astype(F32)
    vals = []
    idxs = []
    cur = logits
    for _ in range(TOP_K):
        m = jnp.max(cur, axis=1, keepdims=True)
        am = jnp.min(jnp.where(cur == m, lane_e, float(N_EXPERTS)), axis=1, keepdims=True)
        vals.append(m)
        idxs.append(am)
        cur = jnp.where(lane_e == am, -jnp.inf, cur)
    es = [jnp.exp(vv - vals[0]) for vv in vals]
    tot = es[0] + es[1] + es[2] + es[3]

    @pl.when(pl.program_id(0) == 0)
    def _():
        cnt_scr[...] = jnp.zeros(cnt_scr.shape, F32)

    tm = logits.shape[0]
    sel = [lane_e == idxs[kk] for kk in range(TOP_K)]
    onehot = jnp.where(jnp.logical_or(jnp.logical_or(sel[0], sel[1]), jnp.logical_or(sel[2], sel[3])), 1.0, 0.0)
    rr = lax.broadcasted_iota(jnp.int32, (tm, tm), 0)
    cc = lax.broadcasted_iota(jnp.int32, (tm, tm), 1)
    before = jnp.dot((cc < rr).astype(BF16), onehot.astype(BF16), preferred_element_type=F32) + cnt_scr[...]
    ranks = [jnp.sum(jnp.where(sel[kk], before, 0.0), axis=1, keepdims=True) for kk in range(TOP_K)]
    cnt_scr[...] = cnt_scr[...] + jnp.sum(onehot, axis=0, keepdims=True)
    cnt_o[...] = jnp.broadcast_to(cnt_scr[...], cnt_o.shape).astype(jnp.int32)

    lane = lax.broadcasted_iota(jnp.int32, ti_o.shape, 1)
    ti = jnp.zeros(ti_o.shape, F32)
    tg = jnp.zeros(tg_o.shape, F32)
    for kk in range(TOP_K):
        ti = jnp.where(lane == kk, idxs[kk], ti)
        ti = jnp.where(lane == TOP_K + kk, ranks[kk], ti)
        tg = jnp.where(lane == kk, es[kk] / tot, tg)
    ti_o[...] = ti.astype(jnp.int32)
    tg_o[...] = tg


def _mix_out(x_p, x_s, ys, bonus, g, yf, ln, wo_bf, gf, rw, rb):
    n_p, n_s = x_p.shape[0], x_s.shape[0]
    n = n_p + n_s
    tm = _row_tile(n_p, n_s, 256)
    xp_spec, xs_spec = _two_group_specs(tm, D_MODEL, n_p // tm)
    row = lambda w: pl.BlockSpec((tm, w), lambda i: (i, 0))
    pspec = pl.BlockSpec((N_PAIRS, tm, LANES), lambda i: (0, i, 0))
    full = lambda a: pl.BlockSpec(a.shape, lambda i: (0,) * a.ndim)
    return pl.pallas_call(
        functools.partial(_mix_out_kernel, np_tiles=n_p // tm),
        grid=(n // tm,),
        in_specs=[xp_spec, xs_spec, pspec, row(MIX_W), row(MIX_W), pspec, full(ln), full(wo_bf), full(gf),
                  full(rw), full(rb)],
        out_specs=[row(D_MODEL), pl.BlockSpec((tm * ROW_CHUNKS, LANES), lambda i: (i, 0)), row(LANES), row(LANES),
                   pl.BlockSpec((8, N_EXPERTS), lambda i: (0, 0))],
        out_shape=[jax.ShapeDtypeStruct((n, D_MODEL), F32), jax.ShapeDtypeStruct((n * ROW_CHUNKS, LANES), F32),
                   jax.ShapeDtypeStruct((n, LANES), jnp.int32), jax.ShapeDtypeStruct((n, LANES), F32),
                   jax.ShapeDtypeStruct((8, N_EXPERTS), jnp.int32)],
        scratch_shapes=[pltpu.VMEM((1, N_EXPERTS), F32)],
        compiler_params=_cparams(1),
        name="mix_out_router",
    )(x_p, x_s, ys, bonus, g, yf, ln, wo_bf, gf, rw, rb)


def _moe_gather_kernel(tok_ref, hf_hbm, o_ref, buf, sem):
    i = pl.program_id(0)
    nb = pl.num_programs(0)
    rows = o_ref.shape[0]

    def row_copy(blk, slot, r):
        t = tok_ref[blk * rows + r]
        return pltpu.make_async_copy(hf_hbm.at[pl.ds(pl.multiple_of(t * ROW_CHUNKS, ROW_CHUNKS), ROW_CHUNKS)],
                                     buf.at[slot, pl.ds(pl.multiple_of(r * ROW_CHUNKS, ROW_CHUNKS), ROW_CHUNKS)],
                                     sem.at[slot])

    def issue(blk, slot):
        def body(r, carry):
            row_copy(blk, slot, r).start()
            return carry
        lax.fori_loop(0, rows, body, 0, unroll=8)

    @pl.when(i == 0)
    def _():
        issue(0, 0)

    @pl.when(i + 1 < nb)
    def _():
        issue(i + 1, (i + 1) % 2)

    slot = i % 2

    pltpu.make_async_copy(hf_hbm.at[pl.ds(0, rows * ROW_CHUNKS)], buf.at[slot], sem.at[slot]).wait()
    for jc in range(ROW_CHUNKS):
        o_ref[:, jc * LANES:(jc + 1) * LANES] = buf[slot, pl.ds(jc, rows, stride=ROW_CHUNKS), :].astype(BF16)


def _moe_gather(row_tok, hf, n_blocks):
    grid_spec = pltpu.PrefetchScalarGridSpec(
        num_scalar_prefetch=1,
        grid=(n_blocks,),
        in_specs=[pl.BlockSpec(memory_space=pl.ANY)],
        out_specs=pl.BlockSpec((MOE_ROWS, D_MODEL), lambda i, tok: (i, 0)),
        scratch_shapes=[pltpu.VMEM((2, MOE_ROWS * ROW_CHUNKS, LANES), F32), pltpu.SemaphoreType.DMA((2,))],
    )
    return pl.pallas_call(
        _moe_gather_kernel,
        grid_spec=grid_spec,
        out_shape=jax.ShapeDtypeStruct((n_blocks * MOE_ROWS, D_MODEL), BF16),
        compiler_params=_cparams(1),
        name="moe_gather",
    )(row_tok, hf)


STEP_RUN, STEP_NEW_WEIGHTS, STEP_ZERO = 0, 1, 2


def _moe_gu_kernel(se, sw, sb, sj, sf, x_ref, wg_ref, wu_ref, bg_ref, bu_ref, o_ref, wg_bf, wu_bf):
    s = pl.program_id(0)

    @pl.when(sf[s] == STEP_NEW_WEIGHTS)
    def _():
        wg_bf[...] = wg_ref[0].astype(BF16)
        wu_bf[...] = wu_ref[0].astype(BF16)

    @pl.when(sf[s] != STEP_ZERO)
    def _():
        x = x_ref[...]
        g = jnp.dot(x, wg_bf[...], preferred_element_type=F32) + bg_ref[0]
        u = jnp.dot(x, wu_bf[...], preferred_element_type=F32) + bu_ref[0]
        g = jnp.minimum(g, SWIGLU_LIMIT)
        u = jnp.clip(u, -SWIGLU_LIMIT, SWIGLU_LIMIT)
        o_ref[...] = ((u + 1.0) * (g * jax.nn.sigmoid(SWIGLU_ALPHA * g))).astype(BF16)

    @pl.when(sf[s] == STEP_ZERO)
    def _():
        o_ref[...] = jnp.zeros(o_ref.shape, o_ref.dtype)


def _moe_gate_up(sched, xs, w_gu, b_gu, n_blocks):
    nt = D_EXPERT // MOE_TN
    n_steps = nt * n_blocks
    wspec = lambda off: pl.BlockSpec((1, D_MODEL, MOE_TN), lambda s, se, sw, sb, sj, sf: (se[s], 0, off + sw[s]))
    bspec = lambda off: pl.BlockSpec((1, 1, MOE_TN), lambda s, se, sw, sb, sj, sf: (se[s], 0, off + sw[s]))
    grid_spec = pltpu.PrefetchScalarGridSpec(
        num_scalar_prefetch=5,
        grid=(n_steps,),
        in_specs=[pl.BlockSpec((MOE_ROWS, D_MODEL), lambda s, se, sw, sb, sj, sf: (sb[s], 0)),
                  wspec(0), wspec(nt), bspec(0), bspec(nt)],
        out_specs=pl.BlockSpec((MOE_ROWS, MOE_TN), lambda s, se, sw, sb, sj, sf: (sb[s], sj[s])),
        scratch_shapes=[pltpu.VMEM((D_MODEL, MOE_TN), BF16), pltpu.VMEM((D_MODEL, MOE_TN), BF16)],
    )
    return pl.pallas_call(
        _moe_gu_kernel,
        grid_spec=grid_spec,
        out_shape=jax.ShapeDtypeStruct((n_blocks * MOE_ROWS, D_EXPERT), BF16),
        compiler_params=_cparams(1),
        name="moe_gate_up",
    )(*sched, xs, w_gu, w_gu, b_gu, b_gu)


def _moe_dn_kernel(se, sw, sb, sj, sf, h_ref, wd_ref, bd_ref, o_ref, wd_bf):
    s = pl.program_id(0)

    @pl.when(sf[s] == STEP_NEW_WEIGHTS)
    def _():
        wd_bf[...] = wd_ref[0].astype(BF16)

    @pl.when(sf[s] != STEP_ZERO)
    def _():
        o_ref[...] = jnp.dot(h_ref[...], wd_bf[...], preferred_element_type=F32) + bd_ref[0]

    @pl.when(sf[s] == STEP_ZERO)
    def _():
        o_ref[...] = jnp.zeros(o_ref.shape, o_ref.dtype)


def _moe_down(sched, hid, w_dn, b_dn, n_blocks):
    nt = D_MODEL // MOE_TN
    n_steps = nt * n_blocks
    grid_spec = pltpu.PrefetchScalarGridSpec(
        num_scalar_prefetch=5,
        grid=(n_steps,),
        in_specs=[pl.BlockSpec((MOE_ROWS, D_EXPERT), lambda s, se, sw, sb, sj, sf: (sb[s], 0)),
                  pl.BlockSpec((1, D_EXPERT, MOE_TN), lambda s, se, sw, sb, sj, sf: (se[s], 0, sw[s])),
                  pl.BlockSpec((1, 1, MOE_TN), lambda s, se, sw, sb, sj, sf: (se[s], 0, sw[s]))],
        out_specs=pl.BlockSpec((MOE_ROWS, MOE_TN), lambda s, se, sw, sb, sj, sf: (sb[s], sj[s])),
        scratch_shapes=[pltpu.VMEM((D_EXPERT, MOE_TN), BF16)],
    )
    return pl.pallas_call(
        _moe_dn_kernel,
        grid_spec=grid_spec,
        out_shape=jax.ShapeDtypeStruct((n_blocks * MOE_ROWS, D_MODEL), F32),
        compiler_params=_cparams(1),
        name="moe_down",
    )(*sched, hid, w_dn, b_dn)


def _moe_combine_kernel(pos_ref, ys_hbm, x1_ref, tg_ref, o_ref, buf, sem):
    i = pl.program_id(0)
    nb = pl.num_programs(0)
    tm = o_ref.shape[0]

    def row_copy(blk, slot, r, kk):
        src = pos_ref[(blk * tm + r) * TOP_K + kk]
        return pltpu.make_async_copy(ys_hbm.at[pl.ds(src, 1)], buf.at[slot, kk, pl.ds(r, 1)], sem.at[slot])

    def issue(blk, slot):
        def body(r, carry):
            for kk in range(TOP_K):
                row_copy(blk, slot, r, kk).start()
            return carry
        lax.fori_loop(0, tm, body, 0, unroll=4)

    @pl.when(i == 0)
    def _():
        issue(0, 0)

    @pl.when(i + 1 < nb)
    def _():
        issue(i + 1, (i + 1) % 2)

    slot = i % 2

    for kk in range(TOP_K):
        pltpu.make_async_copy(ys_hbm.at[pl.ds(0, tm)], buf.at[slot, kk], sem.at[slot]).wait()

    tg = tg_ref[...]
    acc = x1_ref[...]
    for kk in range(TOP_K):
        acc = acc + tg[:, kk:kk + 1] * buf[slot, kk]
    o_ref[...] = acc


def _moe_combine(pos, ys, x1, tg):
    n = x1.shape[0]
    tm = min(128, n)
    grid_spec = pltpu.PrefetchScalarGridSpec(
        num_scalar_prefetch=1,
        grid=(n // tm,),
        in_specs=[pl.BlockSpec(memory_space=pl.ANY),
                  pl.BlockSpec((tm, D_MODEL), lambda i, pos: (i, 0)),
                  pl.BlockSpec((tm, LANES), lambda i, pos: (i, 0))],
        out_specs=pl.BlockSpec((tm, D_MODEL), lambda i, pos: (i, 0)),
        scratch_shapes=[pltpu.VMEM((2, TOP_K, tm, D_MODEL), F32), pltpu.SemaphoreType.DMA((2,))],
    )
    return pl.pallas_call(
        _moe_combine_kernel,
        grid_spec=grid_spec,
        out_shape=jax.ShapeDtypeStruct((n, D_MODEL), F32),
        compiler_params=_cparams(1),
        name="moe_combine",
    )(pos, ys, x1, tg)


def _moe_schedule(top_idx, rank, counts, n_blocks):
    n = top_idx.shape[0]
    n_rows = n * TOP_K
    nt = D_EXPERT // MOE_TN
    flat_e = top_idx.reshape(n_rows)
    nb_e = (counts + MOE_ROWS - 1) // MOE_ROWS
    blk_end = jnp.cumsum(nb_e)
    blk_start = blk_end - nb_e
    pos = (blk_start[flat_e] * MOE_ROWS + rank.reshape(n_rows)).astype(jnp.int32)
    total = n_blocks * MOE_ROWS
    row_tok = (jnp.arange(total, dtype=jnp.int32) % n).at[pos].set(jnp.arange(n_rows, dtype=jnp.int32) // TOP_K)
    used = blk_end[-1]
    s = jnp.arange(nt * n_blocks, dtype=jnp.int32)
    live = s < nt * used
    s_eff = jnp.maximum(jnp.minimum(s, nt * used - 1), 0)
    e = jnp.minimum(jnp.sum((s_eff[:, None] >= nt * blk_end[None, :]).astype(jnp.int32), axis=1), N_EXPERTS - 1)
    local = s_eff - nt * blk_start[e]
    nbe = jnp.maximum(nb_e[e], 1)
    sw = (local // nbe).astype(jnp.int32)
    tail = s - nt * used
    sb = jnp.where(live, blk_start[e] + local % nbe, used + tail // nt).astype(jnp.int32)
    sj = jnp.where(live, sw, tail % nt).astype(jnp.int32)
    sf = jnp.where(live, jnp.where(local % nbe == 0, STEP_NEW_WEIGHTS, STEP_RUN), STEP_ZERO).astype(jnp.int32)
    return pos, row_tok, (e, sw, sb, sj, sf)


def _ple_kernel(x_ref, pp_ref, ps_ref, g_ref, wg_ref, wp_ref, op_ref, os_ref, *, np_tiles):
    i = pl.program_id(0)
    x = x_ref[...]
    h = _rms(x, g_ref[...]).astype(BF16)
    gate = jax.nn.sigmoid(jnp.dot(h, wg_ref[...], preferred_element_type=F32))
    p = jnp.where(i < np_tiles, pp_ref[...], ps_ref[...])
    y = x + gate * jnp.dot(p.astype(BF16), wp_ref[...], preferred_element_type=F32)

    @pl.when(i < np_tiles)
    def _():
        op_ref[...] = y

    @pl.when(i >= np_tiles)
    def _():
        os_ref[...] = y


def _ple(x, p_p, p_s, g, wg_bf, wp_bf):
    n_p, n_s = p_p.shape[0], p_s.shape[0]
    n = n_p + n_s
    tm = _row_tile(n_p, n_s, 256)
    np_tiles = n_p // tm
    full = lambda a: pl.BlockSpec(a.shape, lambda i: (0,) * a.ndim)
    pp_spec, ps_spec = _two_group_specs(tm, PLE_DIM, np_tiles)
    op_spec, os_spec = _two_group_specs(tm, D_MODEL, np_tiles)
    return pl.pallas_call(
        functools.partial(_ple_kernel, np_tiles=np_tiles),
        grid=(n // tm,),
        in_specs=[pl.BlockSpec((tm, D_MODEL), lambda i: (i, 0)), pp_spec, ps_spec,
                  full(g), full(wg_bf), full(wp_bf)],
        out_specs=[op_spec, os_spec],
        out_shape=[jax.ShapeDtypeStruct((n_p, D_MODEL), F32), jax.ShapeDtypeStruct((n_s, D_MODEL), F32)],
        compiler_params=_cparams(1),
        name="ple_gate",
    )(x, p_p, p_s, g, wg_bf, wp_bf)


def _pairs_from_state(s):
    b = s.shape[0]
    st = jnp.swapaxes(s, 2, 3).reshape(b, N_PAIRS, 2, HEAD_DIM, HEAD_DIM)
    z = jnp.zeros((b, N_PAIRS, HEAD_DIM, HEAD_DIM), s.dtype)
    top = jnp.concatenate([st[:, :, 0], z], axis=3)
    bot = jnp.concatenate([z, st[:, :, 1]], axis=3)
    return jnp.concatenate([top, bot], axis=2)


def _state_from_pairs(sp):
    b = sp.shape[0]
    st = jnp.stack([sp[:, :, :HEAD_DIM, :HEAD_DIM], sp[:, :, HEAD_DIM:, HEAD_DIM:]], axis=2)
    return jnp.swapaxes(st.reshape(b, N_HEADS, HEAD_DIM, HEAD_DIM), 2, 3)


def _layer(x_p, x_s, cache_k, cache_v, cache_lf, state, shift, p_p, p_s, lw):
    t_p = x_p.shape[1]
    b_s, t_s, _ = x_s.shape
    n_p = x_p.shape[0] * t_p
    n_s = b_s * t_s
    n = n_p + n_s
    past = cache_k.shape[1]
    xp2 = x_p.reshape(n_p, D_MODEL)
    xs2 = x_s.reshape(n_s, D_MODEL)

    w_in_bf = jnp.pad(lw["w_in"], ((0, 0), (0, Z_PAD - lw["w_in"].shape[1]))).astype(BF16)
    z = _in_proj(xp2, xs2, lw["norm_mix_g"], w_in_bf)

    assert t_s == CHUNK, "each sample stream contributes exactly one scan chunk"
    vecs = jnp.zeros((8, MIX_W), F32)
    vecs = vecs.at[0].set(lw["rwkv_w0"]).at[1].set(lw["rwkv_a0"]).at[2].set(lw["rwkv_kk"])
    vecs = vecs.at[3].set(lw["rwkv_ka"]).at[4].set(lw["rwkv_rk"].reshape(MIX_W))
    wl = jnp.zeros((256, 3 * MIX_W), F32)
    wl = wl.at[0:64, 0:MIX_W].set(lw["rwkv_w2"]).at[64:128, MIX_W:2 * MIX_W].set(lw["rwkv_a2"])
    wl = wl.at[128:256, 2 * MIX_W:].set(lw["rwkv_g2"])
    r, w, k, kk, bb, v_pm, g, bonus = _rwkv_pre(z, shift, lw["rwkv_mu"].reshape(1, RWKV_PROJ), vecs, wl, n_p)
    s0 = jnp.concatenate([jnp.zeros((1, N_PAIRS, LANES, LANES), F32), _pairs_from_state(state.astype(F32))], axis=0)
    y_scan, s_out = _rwkv_scan(r, w, k, kk, bb, v_pm, s0, n_p)
    s_new = _state_from_pairs(s_out)
    shift_new_p = z[n_p - 1:n_p, :RWKV_PROJ].reshape(1, 1, RWKV_PROJ)
    shift_new_s = z[n_p:, :RWKV_PROJ].reshape(b_s, t_s, RWKV_PROJ)[:, -1:, :]

    fvecs = jnp.zeros((8, MIX_W), F32)
    fvecs = fvecs.at[0].set(jnp.tile(lw["fox_q_g"], N_HEADS)).at[1].set(jnp.tile(lw["fox_k_g"], N_HEADS))
    bf = jnp.zeros((1, LANES), F32).at[0, :N_HEADS].set(lw["fox_b_f"])
    q_pm, k_pm, vv_pm, k_new_p, v_new_p, k_new_s, v_new_s, lf = _fox_pre(z, fvecs, bf, n_p)
    lf = lf[:, :N_HEADS]
    lf_p = lf[:n_p].T.reshape(1, N_HEADS, n_p)
    c_p = _cumsum_lanes(lf_p).reshape(N_PAIRS, 2, n_p)
    yf_p = _attn_prompt(q_pm, k_pm, vv_pm, c_p, n_p)
    lf_s = jnp.swapaxes(lf[n_p:].reshape(b_s, t_s, N_HEADS), 1, 2)
    lf_all = jnp.concatenate([jnp.swapaxes(cache_lf.astype(F32), 1, 2), lf_s], axis=2)
    pad = (-lf_all.shape[2]) % LANES
    c_all = _cumsum_lanes(jnp.pad(lf_all, ((0, 0), (0, 0), (0, pad))))
    yf_s = _attn_sample(q_pm, k_pm, vv_pm, cache_k.reshape(b_s, past, MIX_W), cache_v.reshape(b_s, past, MIX_W),
                        c_all[:, :, :past], c_all[:, :, past:past + t_s], n_p)
    yf = jnp.concatenate([yf_p, yf_s], axis=1)

    ln = jnp.stack([lw["rwkv_ln_g"], lw["rwkv_ln_b"]])
    x1, hf, ti, tg, cnt = _mix_out(xp2, xs2, y_scan, bonus, g, yf, ln, lw["w_out"].astype(BF16),
                                   lw["norm_ffn_g"].reshape(1, D_MODEL), lw["router_w"],
                                   lw["router_b"].reshape(1, N_EXPERTS))

    n_blocks = n * TOP_K // MOE_ROWS + N_EXPERTS
    pos, row_tok, sched = _moe_schedule(ti[:, :TOP_K], ti[:, TOP_K:2 * TOP_K], cnt[0], n_blocks)
    xs = _moe_gather(row_tok, hf, n_blocks)
    hid = _moe_gate_up(sched, xs, lw["expert_w_gu"], lw["expert_b_gu"].reshape(N_EXPERTS, 1, 2 * D_EXPERT), n_blocks)
    ys = _moe_down(sched, hid, lw["expert_w_down"], lw["expert_b_down"].reshape(N_EXPERTS, 1, D_MODEL), n_blocks)
    x2 = _moe_combine(pos, ys, x1, tg)

    y_p, y_s = _ple(x2, p_p, p_s, lw["ple_norm_g"].reshape(1, D_MODEL), lw["ple_w_gate"].astype(BF16),
                    lw["ple_w_proj"].astype(BF16))

    heads = lambda a, bsz, t: a.reshape(bsz, t, N_HEADS, HEAD_DIM)
    out_p = (y_p.reshape(x_p.shape), heads(k_new_p, 1, n_p), heads(v_new_p, 1, n_p),
             lf[:n_p].reshape(1, n_p, N_HEADS), s_new[:1], shift_new_p)
    out_s = (y_s.reshape(x_s.shape), heads(k_new_s, b_s, t_s), heads(v_new_s, b_s, t_s),
             lf[n_p:].reshape(b_s, t_s, N_HEADS), s_new[1:], shift_new_s)
    return out_p, out_s


def kernel(x_prompt, x_sample, cache_fox_k, cache_fox_v, cache_fox_logf, state_rwkv, state_rwkv_shift, p_prompt, p_sample, norm_mix_g, w_in, rwkv_mu, rwkv_w0, rwkv_w2, rwkv_a0, rwkv_a2, rwkv_g2, rwkv_kk, rwkv_ka, rwkv_rk, rwkv_ln_g, rwkv_ln_b, fox_q_g, fox_k_g, fox_b_f, w_out, norm_ffn_g, router_w, router_b, expert_w_gu, expert_b_gu, expert_w_down, expert_b_down, ple_norm_g, ple_w_gate, ple_w_proj):
    assert x_prompt.shape[0] == 1 and w_in.shape[0] == 1, "one prompt stream, one layer"
    lw = dict(norm_mix_g=norm_mix_g[0], w_in=w_in[0], rwkv_mu=rwkv_mu[0], rwkv_w0=rwkv_w0[0], rwkv_w2=rwkv_w2[0],
              rwkv_a0=rwkv_a0[0], rwkv_a2=rwkv_a2[0], rwkv_g2=rwkv_g2[0], rwkv_kk=rwkv_kk[0], rwkv_ka=rwkv_ka[0],
              rwkv_rk=rwkv_rk[0], rwkv_ln_g=rwkv_ln_g[0], rwkv_ln_b=rwkv_ln_b[0], fox_q_g=fox_q_g[0],
              fox_k_g=fox_k_g[0], fox_b_f=fox_b_f[0], w_out=w_out[0], norm_ffn_g=norm_ffn_g[0],
              router_w=router_w[0], router_b=router_b[0], expert_w_gu=expert_w_gu[0], expert_b_gu=expert_b_gu[0],
              expert_w_down=expert_w_down[0], expert_b_down=expert_b_down[0], ple_norm_g=ple_norm_g[0],
              ple_w_gate=ple_w_gate[0], ple_w_proj=ple_w_proj[0])
    n_p = x_prompt.shape[1]
    (y_p, k_p, v_p, lf_p, s_p, sh_p), (y_s, k_s, v_s, lf_s, s_s, sh_s) = _layer(
        x_prompt, x_sample, cache_fox_k[0], cache_fox_v[0], cache_fox_logf[0], state_rwkv[0],
        state_rwkv_shift[0], p_prompt[0].reshape(n_p, PLE_DIM), p_sample[0].reshape(-1, PLE_DIM), lw)
    add = lambda a: a[None]
    return (y_p, y_s, add(k_p), add(v_p), add(lf_p), add(s_p), add(sh_p),
            add(k_s), add(v_s), add(lf_s), add(s_s), add(sh_s))
```

```python
import functools

import numpy as np
import jax
import jax.numpy as jnp
from jax import lax
from jax.experimental import pallas as pl
from jax.experimental.pallas import tpu as pltpu

F32 = jnp.float32
BF16 = jnp.bfloat16
HI = lax.Precision.HIGHEST

D_MODEL = 2048
HEAD_DIM = 64
N_HEADS = 16
N_PAIRS = N_HEADS // 2
MIX_W = N_HEADS * HEAD_DIM
CHUNK = 64
RWKV_PROJ = 3 * MIX_W + 64 + 64 + 128
FOX_PROJ = 3 * MIX_W + N_HEADS
Z_HALF = RWKV_PROJ
Z_PAD = 2 * Z_HALF
N_EXPERTS = 32
TOP_K = 4
D_EXPERT = 2048
SWIGLU_LIMIT = 7.0
SWIGLU_ALPHA = 1.702
PLE_DIM = 256
RMS_EPS = 1e-6
GN_EPS = 64e-5
L2_EPS = 1e-12
NEG_BIG = -1e30
LOG2E = 1.4426950408889634

LANES = 128
MOE_ROWS = 512
MOE_TN = 1024
ROW_CHUNKS = D_MODEL // LANES
VMEM_LIMIT = 52 * 1024 * 1024


def _cparams(n_axes, vmem=VMEM_LIMIT):
    return pltpu.CompilerParams(dimension_semantics=("arbitrary",) * n_axes, vmem_limit_bytes=vmem)


def _head_sum(x):
    r = lax.broadcasted_iota(jnp.int32, (LANES, LANES), 0) // HEAD_DIM
    c = lax.broadcasted_iota(jnp.int32, (LANES, LANES), 1) // HEAD_DIM
    bd = jnp.where(r == c, 1.0, 0.0).astype(BF16)
    hi = x.astype(BF16)
    lo = (x - hi.astype(F32)).astype(BF16)
    parts = []
    for i in range(x.shape[1] // LANES):
        sl = slice(i * LANES, (i + 1) * LANES)
        parts.append(jnp.dot(hi[:, sl], bd, preferred_element_type=F32)
                     + jnp.dot(lo[:, sl], bd, preferred_element_type=F32))
    return parts[0] if len(parts) == 1 else jnp.concatenate(parts, axis=1)


def _log_sigmoid(x):
    return jnp.minimum(x, 0.0) - jnp.log1p(jnp.exp(-jnp.abs(x)))


def _rms(x, g):
    ms = jnp.mean(x * x, axis=-1, keepdims=True)
    return x * lax.rsqrt(ms + RMS_EPS) * g


def _row_tile(n_p, n_s, pref):
    return pref if (n_p % pref == 0 and n_s % pref == 0) else 128


def _two_group_specs(tm, width, np_tiles, n_grid_axes=1):
    if n_grid_axes == 1:
        return (pl.BlockSpec((tm, width), lambda i: (jnp.minimum(i, np_tiles - 1), 0)),
                pl.BlockSpec((tm, width), lambda i: (jnp.maximum(i - np_tiles, 0), 0)))
    return (pl.BlockSpec((tm, width), lambda i, j: (jnp.minimum(i, np_tiles - 1), 0)),
            pl.BlockSpec((tm, width), lambda i, j: (jnp.maximum(i - np_tiles, 0), 0)))


def _inproj_kernel(xp_ref, xs_ref, g_ref, w_ref, o_ref, h_scr, *, np_tiles):
    i = pl.program_id(0)
    first = pl.program_id(1) == 0

    @pl.when(jnp.logical_and(first, i < np_tiles))
    def _():
        h_scr[...] = _rms(xp_ref[...], g_ref[...]).astype(BF16)

    @pl.when(jnp.logical_and(first, i >= np_tiles))
    def _():
        h_scr[...] = _rms(xs_ref[...], g_ref[...]).astype(BF16)

    o_ref[...] = jnp.dot(h_scr[...], w_ref[...], preferred_element_type=F32)


def _in_proj(x_p, x_s, g, w_bf):
    n_p, n_s = x_p.shape[0], x_s.shape[0]
    n = n_p + n_s
    tm = _row_tile(n_p, n_s, 512)
    tn = Z_PAD // 4
    xp_spec, xs_spec = _two_group_specs(tm, D_MODEL, n_p // tm, 2)
    return pl.pallas_call(
        functools.partial(_inproj_kernel, np_tiles=n_p // tm),
        grid=(n // tm, Z_PAD // tn),
        in_specs=[xp_spec, xs_spec,
                  pl.BlockSpec((1, D_MODEL), lambda i, j: (0, 0)),
                  pl.BlockSpec((D_MODEL, tn), lambda i, j: (0, j))],
        out_specs=pl.BlockSpec((tm, tn), lambda i, j: (i, j)),
        out_shape=jax.ShapeDtypeStruct((n, Z_PAD), F32),
        scratch_shapes=[pltpu.VMEM((tm, D_MODEL), BF16)],
        compiler_params=_cparams(2),
        name="in_proj",
    )(x_p, x_s, g.reshape(1, D_MODEL), w_bf)


def _rwkv_pre_kernel(z_ref, prev_ref, mu_ref, vec_ref, wl_ref,
                     r_o, w_o, k_o, kk_o, b_o, v_o, g_o, bn_o, carry, *, n_prompt_tiles):
    i = pl.program_id(0)
    z = z_ref[...]
    tm = z.shape[0]

    @pl.when(i == 0)
    def _():
        carry[...] = prev_ref[0, 0:1, :]

    rolled = pltpu.roll(z, 1, axis=0)
    row = lax.broadcasted_iota(jnp.int32, z.shape, 0)
    shifted = jnp.where(row == 0, carry[...], rolled)
    is_sample = i >= n_prompt_tiles
    for c in range(tm // CHUNK):
        shifted = jnp.where(jnp.logical_and(is_sample, row == c * CHUNK), prev_ref[0, c:c + 1, :], shifted)
    carry[...] = z[tm - 1:tm, :]
    zs = z + mu_ref[...] * (shifted - z)

    r = zs[:, 0:MIX_W]
    k = zs[:, MIX_W:2 * MIX_W]
    v = zs[:, 2 * MIX_W:3 * MIX_W]
    lo = zs[:, 3 * MIX_W:RWKV_PROJ]
    lane = lax.broadcasted_iota(jnp.int32, lo.shape, 1)
    f = jnp.where(lane < 64, jnp.tanh(lo), jnp.where(lane < 128, lo, jax.nn.sigmoid(lo)))
    lora = _dot3(f, wl_ref[...])
    w_pre = vec_ref[0:1, :] + lora[:, 0:MIX_W]
    log_decay = -jnp.exp(_log_sigmoid(w_pre) - 0.5)
    a = jax.nn.sigmoid(vec_ref[1:2, :] + lora[:, MIX_W:2 * MIX_W])
    g = lora[:, 2 * MIX_W:3 * MIX_W]
    kk = k * vec_ref[2:3, :]
    kk = kk * lax.rsqrt(_head_sum(kk * kk) + L2_EPS)
    k_mod = k * (1.0 + (a - 1.0) * vec_ref[3:4, :])
    bonus = _head_sum(r * k_mod * vec_ref[4:5, :]) * v
    kka = kk * a

    g_o[...] = g
    bn_o[...] = bonus
    for p in range(N_PAIRS):
        sl = slice(p * LANES, (p + 1) * LANES)
        r_o[p] = r[:, sl]
        w_o[p] = log_decay[:, sl]
        k_o[p] = k_mod[:, sl]
        kk_o[p] = kk[:, sl]
        b_o[p] = kka[:, sl]
        v_o[p] = v[:, sl]


def _rwkv_pre(z, shift, mu, vecs, wl, n_prompt):
    n = z.shape[0]
    n_s = n - n_prompt
    tm = _row_tile(n_prompt, n_s, 256)
    if tm % CHUNK:
        tm = CHUNK
    cpt = tm // CHUNK
    nt = n // tm
    n_prompt_tiles = n_prompt // tm
    prev = jnp.concatenate([jnp.zeros((1, cpt, RWKV_PROJ), F32),
                            shift.astype(F32).reshape(n_s // tm, cpt, RWKV_PROJ)], axis=0)
    tok = jax.ShapeDtypeStruct((n, MIX_W), F32)
    tspec = pl.BlockSpec((tm, MIX_W), lambda i: (i, 0))
    pm = jax.ShapeDtypeStruct((N_PAIRS, n, LANES), F32)
    pspec = pl.BlockSpec((N_PAIRS, tm, LANES), lambda i: (0, i, 0))
    return pl.pallas_call(
        functools.partial(_rwkv_pre_kernel, n_prompt_tiles=n_prompt_tiles),
        grid=(nt,),
        in_specs=[pl.BlockSpec((tm, Z_HALF), lambda i: (i, 0)),
                  pl.BlockSpec((1, cpt, RWKV_PROJ), lambda i: (jnp.maximum(i - (n_prompt_tiles - 1), 0), 0, 0)),
                  pl.BlockSpec((1, RWKV_PROJ), lambda i: (0, 0)),
                  pl.BlockSpec((8, MIX_W), lambda i: (0, 0)),
                  pl.BlockSpec((256, 3 * MIX_W), lambda i: (0, 0))],
        out_specs=[pspec, pspec, pspec, pspec, pspec, pspec, tspec, tspec],
        out_shape=[pm, pm, pm, pm, pm, pm, tok, tok],
        scratch_shapes=[pltpu.VMEM((1, RWKV_PROJ), F32)],
        compiler_params=_cparams(1),
        name="rwkv_pre",
    )(z, prev, mu, vecs, wl)


def _dot(a, b):
    return jnp.dot(a, b, precision=HI, preferred_element_type=F32)


def _bdot(a, b):
    return jnp.dot(a, b, preferred_element_type=F32)


def _split_bf16(x):
    hi = x.astype(BF16)
    return hi, (x - hi.astype(F32)).astype(BF16)


def _dot3(a, b):
    a_hi, a_lo = _split_bf16(a)
    b_hi, b_lo = _split_bf16(b)
    return _bdot(a_hi, b_hi) + (_bdot(a_hi, b_lo) + _bdot(a_lo, b_hi))


def _pair_rows(x):
    lo_half = lax.broadcasted_iota(jnp.int32, x.shape, 1) < HEAD_DIM
    return jnp.concatenate([jnp.where(lo_half, x, 0.0), jnp.where(lo_half, 0.0, x)], axis=0)


def _scan_chunk(P, r, lw, k, kk, b, v):
    c2 = 2 * CHUNK
    i = lax.broadcasted_iota(jnp.int32, (c2, c2), 0)
    j = lax.broadcasted_iota(jnp.int32, (c2, c2), 1)
    ti = lax.broadcasted_iota(jnp.int32, (CHUNK, CHUNK), 0)
    tj = lax.broadcasted_iota(jnp.int32, (CHUNK, CHUNK), 1)
    cl = _dot((tj <= ti).astype(F32), lw)
    yield
    g_end = cl[CHUNK - 1:CHUNK, :]
    e_neg = jnp.exp(-cl)
    e_end = jnp.exp(g_end - cl)
    kap = _pair_rows(kk * jnp.exp(cl - lw)).astype(BF16)
    rt = _pair_rows(r * jnp.exp(cl)).astype(BF16)
    bt = _pair_rows(b * e_neg).astype(BF16)
    kt = _pair_rows(k * e_neg).astype(BF16)
    kh = _pair_rows(k * e_end)
    bh = _pair_rows(b * e_end)
    vv = _pair_rows(v)
    vv_b = vv.astype(BF16)
    p_b = P.astype(BF16)

    g = lax.dot_general(jnp.concatenate([kap, rt], axis=0), jnp.concatenate([bt, kt], axis=0),
                        (((1,), (1,)), ((), ())), preferred_element_type=F32)
    yield
    strict = j < i
    incl = j <= i
    a_b = jnp.where(strict, g[:c2, :c2], 0.0)
    a_bb = a_b.astype(BF16)
    a_k = jnp.where(strict, g[:c2, c2:], 0.0).astype(BF16)
    r_b = jnp.where(incl, g[c2:, :c2], 0.0).astype(BF16)
    r_k = jnp.where(incl, g[c2:, c2:], 0.0).astype(BF16)

    t_inv = (i == j).astype(F32) - jnp.where(jnp.logical_and((i & 1) == 1, j == i - 1), a_b, 0.0)
    n = 2
    while n < CHUNK:
        m = jnp.logical_and((i >> n.bit_length()) == (j >> n.bit_length()),
                            jnp.logical_and((i & (2 * n - 1)) >= n, (j & (2 * n - 1)) < n))
        t_b = t_inv.astype(BF16)
        ta = _bdot(t_b, a_bb).astype(BF16)
        yield
        t_inv = t_inv - jnp.where(m, _bdot(ta, t_b), 0.0)
        yield
        n *= 2

    w = _bdot(jnp.concatenate([kap, a_k], axis=1), jnp.concatenate([p_b, vv_b], axis=0))
    yield
    u = _bdot(t_inv.astype(BF16), w.astype(BF16))
    yield
    vu_b = jnp.concatenate([vv_b, u.astype(BF16)], axis=0)
    y2 = _bdot(rt, p_b) + _bdot(jnp.concatenate([r_k, -r_b], axis=1), vu_b)
    y = y2[:CHUNK] + y2[CHUNK:]
    yield
    g_col = jnp.broadcast_to(jnp.exp(g_end), (c2, c2)).T
    p_new = g_col * P + _dot3(jnp.concatenate([kh.T, -bh.T], axis=1), jnp.concatenate([vv, u], axis=0))
    return y, p_new


def _run_interleaved(gens):
    results = [None] * len(gens)
    live = list(range(len(gens)))
    while live:
        for idx in list(live):
            try:
                next(gens[idx])
            except StopIteration as stop:
                results[idx] = stop.value
                live.remove(idx)
    return results


def _scan_kernel(r_ref, w_ref, k_ref, kk_ref, b_ref, v_ref, s0_ref, y_ref, sout_ref, s_scr, *, n_prompt_chunks):
    c = pl.program_id(0)

    @pl.when(jnp.logical_or(c == 0, c >= n_prompt_chunks))
    def _():
        s_scr[...] = s0_ref[0]

    outs = _run_interleaved([
        _scan_chunk(s_scr[p], r_ref[p], w_ref[p], k_ref[p], kk_ref[p], b_ref[p], v_ref[p])
        for p in range(N_PAIRS)])
    for p, (y, p_new) in enumerate(outs):
        y_ref[p] = y
        s_scr[p] = p_new
        sout_ref[0, p] = p_new


def _rwkv_scan(r, w, k, kk, b, v, s0, n_prompt):
    n = r.shape[1]
    npc = n_prompt // CHUNK
    n_seq = s0.shape[0]
    pspec = pl.BlockSpec((N_PAIRS, CHUNK, LANES), lambda c: (0, c, 0))
    sspec = pl.BlockSpec((1, N_PAIRS, LANES, LANES), lambda c: (jnp.maximum(c - (npc - 1), 0), 0, 0, 0))
    return pl.pallas_call(
        functools.partial(_scan_kernel, n_prompt_chunks=npc),
        grid=(n // CHUNK,),
        in_specs=[pspec, pspec, pspec, pspec, pspec, pspec, sspec],
        out_specs=[pspec, sspec],
        out_shape=[jax.ShapeDtypeStruct((N_PAIRS, n, LANES), F32),
                   jax.ShapeDtypeStruct((n_seq, N_PAIRS, LANES, LANES), F32)],
        scratch_shapes=[pltpu.VMEM((N_PAIRS, LANES, LANES), F32)],
        compiler_params=_cparams(1),
        name="rwkv_scan",
    )(r, w, k, kk, b, v, s0)


def _fox_pre_kernel(z_ref, vec_ref, bf_ref, q_o, k_o, v_o, knp_o, vnp_o, kns_o, vns_o, lf_o, *, np_tiles):
    z = z_ref[...]
    q = z[:, 0:MIX_W]
    k = z[:, MIX_W:2 * MIX_W]
    v = z[:, 2 * MIX_W:3 * MIX_W]
    fl = z[:, 3 * MIX_W:3 * MIX_W + LANES]
    inv = 1.0 / HEAD_DIM
    qn = q * lax.rsqrt(_head_sum(q * q) * inv + RMS_EPS) * vec_ref[0:1, :]
    kn = k * lax.rsqrt(_head_sum(k * k) * inv + RMS_EPS) * vec_ref[1:2, :]
    qs = (qn * (HEAD_DIM ** -0.5 * LOG2E)).astype(BF16)
    kb = kn.astype(BF16)
    vb = v.astype(BF16)
    for p in range(N_PAIRS):
        sl = slice(p * LANES, (p + 1) * LANES)
        q_o[p] = qs[:, sl]
        k_o[p] = kb[:, sl]
        v_o[p] = vb[:, sl]
    lf_o[...] = _log_sigmoid(fl + bf_ref[...])

    @pl.when(pl.program_id(0) < np_tiles)
    def _():
        knp_o[...] = kn
        vnp_o[...] = v

    @pl.when(pl.program_id(0) >= np_tiles)
    def _():
        kns_o[...] = kn
        vns_o[...] = v


def _fox_pre(z, vecs, bf, n_p):
    n = z.shape[0]
    n_s = n - n_p
    tm = _row_tile(n_p, n_s, 256)
    np_tiles = n_p // tm
    pm = jax.ShapeDtypeStruct((N_PAIRS, n, LANES), BF16)
    pspec = pl.BlockSpec((N_PAIRS, tm, LANES), lambda i: (0, i, 0))
    p_spec, s_spec = _two_group_specs(tm, MIX_W, np_tiles)
    tok_p = jax.ShapeDtypeStruct((n_p, MIX_W), F32)
    tok_s = jax.ShapeDtypeStruct((n_s, MIX_W), F32)
    return pl.pallas_call(
        functools.partial(_fox_pre_kernel, np_tiles=np_tiles),
        grid=(n // tm,),
        in_specs=[pl.BlockSpec((tm, Z_HALF), lambda i: (i, 1)),
                  pl.BlockSpec((8, MIX_W), lambda i: (0, 0)),
                  pl.BlockSpec((1, LANES), lambda i: (0, 0))],
        out_specs=[pspec, pspec, pspec, p_spec, p_spec, s_spec, s_spec,
                   pl.BlockSpec((tm, LANES), lambda i: (i, 0))],
        out_shape=[pm, pm, pm, tok_p, tok_p, tok_s, tok_s, jax.ShapeDtypeStruct((n, LANES), F32)],
        compiler_params=_cparams(1),
        name="fox_pre",
    )(z, vecs, bf)


def _cumsum_kernel(x_ref, o_ref):
    r = lax.broadcasted_iota(jnp.int32, (LANES, LANES), 0)
    c = lax.broadcasted_iota(jnp.int32, (LANES, LANES), 1)
    tri = (r <= c).astype(F32)
    carry = jnp.zeros((N_HEADS, 1), F32)
    for i in range(x_ref.shape[2] // LANES):
        sl = slice(i * LANES, (i + 1) * LANES)
        cs = jnp.dot(x_ref[0, :, sl], tri, precision=HI, preferred_element_type=F32) + carry
        o_ref[0, :, sl] = cs
        carry = cs[:, LANES - 1:LANES]


def _cumsum_lanes(x):
    b, h, t = x.shape
    return pl.pallas_call(
        _cumsum_kernel,
        grid=(b,),
        in_specs=[pl.BlockSpec((1, h, t), lambda i: (i, 0, 0))],
        out_specs=pl.BlockSpec((1, h, t), lambda i: (i, 0, 0)),
        out_shape=jax.ShapeDtypeStruct((b, h, t), F32),
        compiler_params=_cparams(1),
        name="cumsum_logf",
    )(x)


def _rep_lanes(m, tk):
    if tk % LANES == 0:
        return m if tk == LANES else jnp.concatenate([m] * (tk // LANES), axis=1)
    return m[:, :tk]


DEN_LANE = (HEAD_DIM, 0)


def _attn_update(qa, qb, kb, vb, ck, mask, m_ref, acc_ref):
    tk = kb.shape[0]
    lane = lax.broadcasted_iota(jnp.int32, vb.shape, 1)
    own = (lane < HEAD_DIM, lane >= HEAD_DIM)
    ck2 = ck * LOG2E
    for h, qh in enumerate((qa, qb)):
        s = lax.dot_general(qh, kb, (((1,), (1,)), ((), ())), preferred_element_type=F32)
        s = s - ck2[h:h + 1, :]
        if mask is not None:
            s = jnp.where(mask, s, NEG_BIG)
        m_prev = m_ref[h]
        m_next = jnp.maximum(m_prev, jnp.max(s, axis=1, keepdims=True))
        p = jnp.exp2(s - _rep_lanes(m_next, tk))
        alpha = jnp.exp2(m_prev - m_next)
        m_ref[h] = m_next
        v_aug = jnp.where(own[h], vb.astype(F32), jnp.where(lane == DEN_LANE[h], 1.0, 0.0)).astype(BF16)
        acc_ref[h] = acc_ref[h] * alpha + jnp.dot(p.astype(BF16), v_aug, preferred_element_type=F32)


def _attn_finish(acc_ref):
    lo_half = lax.broadcasted_iota(jnp.int32, acc_ref.shape[1:], 1) < HEAD_DIM
    acc_a = acc_ref[0]
    acc_b = acc_ref[1]
    return jnp.where(lo_half, acc_a / acc_a[:, DEN_LANE[0]:DEN_LANE[0] + 1], acc_b / acc_b[:, DEN_LANE[1]:DEN_LANE[1] + 1])


def _split_q(q):
    qf = q.astype(F32)
    lo_half = lax.broadcasted_iota(jnp.int32, qf.shape, 1) < HEAD_DIM
    return jnp.where(lo_half, qf, 0.0).astype(BF16), jnp.where(lo_half, 0.0, qf).astype(BF16)


def _attn_prompt_kernel(qi_ref, ki_ref, q_ref, k_ref, v_ref, c_ref, o_ref,
                        qa_scr, qb_scr, m_scr, acc_scr, *, tq, tk):
    s_id = pl.program_id(1)
    qi = qi_ref[s_id]
    ki = ki_ref[s_id]

    @pl.when(ki == 0)
    def _():
        qa, qb = _split_q(q_ref[0])
        qa_scr[...] = qa
        qb_scr[...] = qb
        m_scr[...] = jnp.full(m_scr.shape, NEG_BIG, F32)
        acc_scr[...] = jnp.zeros(acc_scr.shape, F32)

    crosses_diagonal = ki * tk + (tk - 1) > qi * tq

    @pl.when(crosses_diagonal)
    def _():
        rows = qi * tq + lax.broadcasted_iota(jnp.int32, (tq, tk), 0)
        cols = ki * tk + lax.broadcasted_iota(jnp.int32, (tq, tk), 1)
        _attn_update(qa_scr[...], qb_scr[...], k_ref[0], v_ref[0], c_ref[0], cols <= rows, m_scr, acc_scr)

    @pl.when(jnp.logical_not(crosses_diagonal))
    def _():
        _attn_update(qa_scr[...], qb_scr[...], k_ref[0], v_ref[0], c_ref[0], None, m_scr, acc_scr)

    @pl.when(ki == ((qi + 1) * tq - 1) // tk)
    def _():
        o_ref[0] = _attn_finish(acc_scr).astype(o_ref.dtype)


def _attn_prompt(q, k, v, c, t):
    tq = min(1024, t)
    tk = min(512, t)
    steps = [(qi, ki) for qi in range(t // tq) for ki in range(((qi + 1) * tq - 1) // tk + 1)]
    qi_arr = jnp.asarray(np.array([s[0] for s in steps], np.int32))
    ki_arr = jnp.asarray(np.array([s[1] for s in steps], np.int32))
    grid_spec = pltpu.PrefetchScalarGridSpec(
        num_scalar_prefetch=2,
        grid=(N_PAIRS, len(steps)),
        in_specs=[pl.BlockSpec((1, tq, LANES), lambda p, s, qi, ki: (p, qi[s], 0)),
                  pl.BlockSpec((1, tk, LANES), lambda p, s, qi, ki: (p, ki[s], 0)),
                  pl.BlockSpec((1, tk, LANES), lambda p, s, qi, ki: (p, ki[s], 0)),
                  pl.BlockSpec((1, 2, tk), lambda p, s, qi, ki: (p, 0, ki[s]))],
        out_specs=pl.BlockSpec((1, tq, LANES), lambda p, s, qi, ki: (p, qi[s], 0)),
        scratch_shapes=[pltpu.VMEM((tq, LANES), BF16), pltpu.VMEM((tq, LANES), BF16),
                        pltpu.VMEM((2, tq, LANES), F32), pltpu.VMEM((2, tq, LANES), F32)],
    )
    return pl.pallas_call(
        functools.partial(_attn_prompt_kernel, tq=tq, tk=tk),
        grid_spec=grid_spec,
        out_shape=jax.ShapeDtypeStruct((N_PAIRS, t, LANES), BF16),
        compiler_params=_cparams(2),
        name="fox_attn_prompt",
    )(qi_arr, ki_arr, q, k, v, c)


def _attn_sample_kernel(q_ref, ck_ref, cv_ref, kn_ref, vn_ref, cc_ref, cn_ref, o_ref,
                        m_scr, acc_scr, *, nkb):
    j = pl.program_id(1)
    ts = q_ref.shape[1]

    @pl.when(j == 0)
    def _():
        m_scr[...] = jnp.full(m_scr.shape, NEG_BIG, F32)
        acc_scr[...] = jnp.zeros(acc_scr.shape, F32)

    @pl.when(j < nkb)
    def _():
        for p in range(N_PAIRS):
            sl = slice(p * LANES, (p + 1) * LANES)
            qa, qb = _split_q(q_ref[p])
            _attn_update(qa, qb, ck_ref[0, :, sl].astype(BF16), cv_ref[0, :, sl].astype(BF16),
                         cc_ref[0, 2 * p:2 * p + 2, :], None, m_scr.at[p], acc_scr.at[p])

    @pl.when(j == nkb)
    def _():
        rows = lax.broadcasted_iota(jnp.int32, (ts, ts), 0)
        cols = lax.broadcasted_iota(jnp.int32, (ts, ts), 1)
        for p in range(N_PAIRS):
            qa, qb = _split_q(q_ref[p])
            _attn_update(qa, qb, kn_ref[p], vn_ref[p], cn_ref[0, 2 * p:2 * p + 2, :], cols <= rows,
                         m_scr.at[p], acc_scr.at[p])
            o_ref[p] = _attn_finish(acc_scr.at[p]).astype(o_ref.dtype)


def _attn_sample(q, k, v, cache_k, cache_v, c_cache, c_new, row0):
    b, past, _ = cache_k.shape
    ts = c_new.shape[2]
    tk = min(1024, past)
    nkb = past // tk
    blk0 = row0 // ts
    qspec = pl.BlockSpec((N_PAIRS, ts, LANES), lambda i, j: (0, blk0 + i, 0))
    cspec = pl.BlockSpec((1, tk, MIX_W), lambda i, j: (i, jnp.minimum(j, nkb - 1), 0))
    return pl.pallas_call(
        functools.partial(_attn_sample_kernel, nkb=nkb),
        grid=(b, nkb + 1),
        in_specs=[qspec, cspec, cspec, qspec, qspec,
                  pl.BlockSpec((1, N_HEADS, tk), lambda i, j: (i, 0, jnp.minimum(j, nkb - 1))),
                  pl.BlockSpec((1, N_HEADS, ts), lambda i, j: (i, 0, 0))],
        out_specs=pl.BlockSpec((N_PAIRS, ts, LANES), lambda i, j: (0, i, 0)),
        out_shape=jax.ShapeDtypeStruct((N_PAIRS, b * ts, LANES), BF16),
        scratch_shapes=[pltpu.VMEM((N_PAIRS, 2, ts, LANES), F32), pltpu.VMEM((N_PAIRS, 2, ts, LANES), F32)],
        compiler_params=_cparams(2),
        name="fox_attn_sample",
    )(q, cache_k, cache_v, k, v, c_cache, c_new)


def _mix_out_kernel(xp_ref, xs_ref, ys_ref, bn_ref, g_ref, yf_ref, ln_ref, wo_ref, gf_ref, rw_ref, rb_ref,
                    x1_o, hf_o, ti_o, tg_o, cnt_o, cnt_scr, *, np_tiles):
    y = jnp.concatenate([ys_ref[p] for p in range(N_PAIRS)], axis=1)
    inv = 1.0 / HEAD_DIM
    mu = _head_sum(y) * inv
    d = y - mu
    var = _head_sum(d * d) * inv
    yn = d * lax.rsqrt(var + GN_EPS) * ln_ref[0:1, :] + ln_ref[1:2, :]
    yr = ((yn + bn_ref[...]) * g_ref[...]).astype(BF16)
    yf = jnp.concatenate([yf_ref[p] for p in range(N_PAIRS)], axis=1)
    mix = jnp.concatenate([yr, yf], axis=1)
    x_res = jnp.where(pl.program_id(0) < np_tiles, xp_ref[...], xs_ref[...])
    x1 = x_res + jnp.dot(mix, wo_ref[...], preferred_element_type=F32)
    x1_o[...] = x1
    hf = _rms(x1, gf_ref[...])
    tm_rows = hf.shape[0]
    for jc in range(ROW_CHUNKS):
        hf_o[pl.ds(jc, tm_rows, stride=ROW_CHUNKS), :] = hf[:, jc * LANES:(jc + 1) * LANES]

    logits = _dot3(hf, rw_ref[...]) + rb_ref[...]
    lane_e = lax.broadcasted_iota(jnp.int32, logits.shape, 1).astype(F32)
    vals = []
    idxs = []
    cur = logits
    for _ in range(TOP_K):
        m = jnp.max(cur, axis=1, keepdims=True)
        am = jnp.min(jnp.where(cur == m, lane_e, float(N_EXPERTS)), axis=1, keepdims=True)
        vals.append(m)
        idxs.append(am)
        cur = jnp.where(lane_e == am, -jnp.inf, cur)
    es = [jnp.exp(vv - vals[0]) for vv in vals]
    tot = es[0] + es[1] + es[2] + es[3]

    @pl.when(pl.program_id(0) == 0)
    def _():
        cnt_scr[...] = jnp.zeros(cnt_scr.shape, F32)

    tm = logits.shape[0]
    sel = [lane_e == idxs[kk] for kk in range(TOP_K)]
    onehot = jnp.where(jnp.logical_or(jnp.logical_or(sel[0], sel[1]), jnp.logical_or(sel[2], sel[3])), 1.0, 0.0)
    rr = lax.broadcasted_iota(jnp.int32, (tm, tm), 0)
    cc = lax.broadcasted_iota(jnp.int32, (tm, tm), 1)
    before = jnp.dot((cc < rr).astype(BF16), onehot.astype(BF16), preferred_element_type=F32) + cnt_scr[...]
    ranks = [jnp.sum(jnp.where(sel[kk], before, 0.0), axis=1, keepdims=True) for kk in range(TOP_K)]
    cnt_scr[...] = cnt_scr[...] + jnp.sum(onehot, axis=0, keepdims=True)
    cnt_o[...] = jnp.broadcast_to(cnt_scr[...], cnt_o.shape).astype(jnp.int32)

    lane = lax.broadcasted_iota(jnp.int32, ti_o.shape, 1)
    ti = jnp.zeros(ti_o.shape, F32)
    tg = jnp.zeros(tg_o.shape, F32)
    for kk in range(TOP_K):
        ti = jnp.where(lane == kk, idxs[kk], ti)
        ti = jnp.where(lane == TOP_K + kk, ranks[kk], ti)
        tg = jnp.where(lane == kk, es[kk] / tot, tg)
    ti_o[...] = ti.astype(jnp.int32)
    tg_o[...] = tg


def _mix_out(x_p, x_s, ys, bonus, g, yf, ln, wo_bf, gf, rw, rb):
    n_p, n_s = x_p.shape[0], x_s.shape[0]
    n = n_p + n_s
    tm = _row_tile(n_p, n_s, 256)
    xp_spec, xs_spec = _two_group_specs(tm, D_MODEL, n_p // tm)
    row = lambda w: pl.BlockSpec((tm, w), lambda i: (i, 0))
    pspec = pl.BlockSpec((N_PAIRS, tm, LANES), lambda i: (0, i, 0))
    full = lambda a: pl.BlockSpec(a.shape, lambda i: (0,) * a.ndim)
    return pl.pallas_call(
        functools.partial(_mix_out_kernel, np_tiles=n_p // tm),
        grid=(n // tm,),
        in_specs=[xp_spec, xs_spec, pspec, row(MIX_W), row(MIX_W), pspec, full(ln), full(wo_bf), full(gf),
                  full(rw), full(rb)],
        out_specs=[row(D_MODEL), pl.BlockSpec((tm * ROW_CHUNKS, LANES), lambda i: (i, 0)), row(LANES), row(LANES),
                   pl.BlockSpec((8, N_EXPERTS), lambda i: (0, 0))],
        out_shape=[jax.ShapeDtypeStruct((n, D_MODEL), F32), jax.ShapeDtypeStruct((n * ROW_CHUNKS, LANES), F32),
                   jax.ShapeDtypeStruct((n, LANES), jnp.int32), jax.ShapeDtypeStruct((n, LANES), F32),
                   jax.ShapeDtypeStruct((8, N_EXPERTS), jnp.int32)],
        scratch_shapes=[pltpu.VMEM((1, N_EXPERTS), F32)],
        compiler_params=_cparams(1),
        name="mix_out_router",
    )(x_p, x_s, ys, bonus, g, yf, ln, wo_bf, gf, rw, rb)


def _moe_gather_kernel(tok_ref, hf_hbm, o_ref, buf, sem):
    i = pl.program_id(0)
    nb = pl.num_programs(0)
    rows = o_ref.shape[0]

    def row_copy(blk, slot, r):
        t = tok_ref[blk * rows + r]
        return pltpu.make_async_copy(hf_hbm.at[pl.ds(pl.multiple_of(t * ROW_CHUNKS, ROW_CHUNKS), ROW_CHUNKS)],
                                     buf.at[slot, pl.ds(pl.multiple_of(r * ROW_CHUNKS, ROW_CHUNKS), ROW_CHUNKS)],
                                     sem.at[slot])

    def issue(blk, slot):
        def body(r, carry):
            row_copy(blk, slot, r).start()
            return carry
        lax.fori_loop(0, rows, body, 0, unroll=8)

    @pl.when(i == 0)
    def _():
        issue(0, 0)

    @pl.when(i + 1 < nb)
    def _():
        issue(i + 1, (i + 1) % 2)

    slot = i % 2

    pltpu.make_async_copy(hf_hbm.at[pl.ds(0, rows * ROW_CHUNKS)], buf.at[slot], sem.at[slot]).wait()
    for jc in range(ROW_CHUNKS):
        o_ref[:, jc * LANES:(jc + 1) * LANES] = buf[slot, pl.ds(jc, rows, stride=ROW_CHUNKS), :].astype(BF16)


def _moe_gather(row_tok, hf, n_blocks):
    grid_spec = pltpu.PrefetchScalarGridSpec(
        num_scalar_prefetch=1,
        grid=(n_blocks,),
        in_specs=[pl.BlockSpec(memory_space=pl.ANY)],
        out_specs=pl.BlockSpec((MOE_ROWS, D_MODEL), lambda i, tok: (i, 0)),
        scratch_shapes=[pltpu.VMEM((2, MOE_ROWS * ROW_CHUNKS, LANES), F32), pltpu.SemaphoreType.DMA((2,))],
    )
    return pl.pallas_call(
        _moe_gather_kernel,
        grid_spec=grid_spec,
        out_shape=jax.ShapeDtypeStruct((n_blocks * MOE_ROWS, D_MODEL), BF16),
        compiler_params=_cparams(1),
        name="moe_gather",
    )(row_tok, hf)


STEP_RUN, STEP_NEW_WEIGHTS, STEP_ZERO = 0, 1, 2


def _moe_gu_kernel(se, sw, sb, sj, sf, x_ref, wg_ref, wu_ref, bg_ref, bu_ref, o_ref, wg_bf, wu_bf):
    s = pl.program_id(0)

    @pl.when(sf[s] == STEP_NEW_WEIGHTS)
    def _():
        wg_bf[...] = wg_ref[0].astype(BF16)
        wu_bf[...] = wu_ref[0].astype(BF16)

    @pl.when(sf[s] != STEP_ZERO)
    def _():
        x = x_ref[...]
        g = jnp.dot(x, wg_bf[...], preferred_element_type=F32) + bg_ref[0]
        u = jnp.dot(x, wu_bf[...], preferred_element_type=F32) + bu_ref[0]
        g = jnp.minimum(g, SWIGLU_LIMIT)
        u = jnp.clip(u, -SWIGLU_LIMIT, SWIGLU_LIMIT)
        o_ref[...] = ((u + 1.0) * (g * jax.nn.sigmoid(SWIGLU_ALPHA * g))).astype(BF16)

    @pl.when(sf[s] == STEP_ZERO)
    def _():
        o_ref[...] = jnp.zeros(o_ref.shape, o_ref.dtype)


def _moe_gate_up(sched, xs, w_gu, b_gu, n_blocks):
    nt = D_EXPERT // MOE_TN
    n_steps = nt * n_blocks
    wspec = lambda off: pl.BlockSpec((1, D_MODEL, MOE_TN), lambda s, se, sw, sb, sj, sf: (se[s], 0, off + sw[s]))
    bspec = lambda off: pl.BlockSpec((1, 1, MOE_TN), lambda s, se, sw, sb, sj, sf: (se[s], 0, off + sw[s]))
    grid_spec = pltpu.PrefetchScalarGridSpec(
        num_scalar_prefetch=5,
        grid=(n_steps,),
        in_specs=[pl.BlockSpec((MOE_ROWS, D_MODEL), lambda s, se, sw, sb, sj, sf: (sb[s], 0)),
                  wspec(0), wspec(nt), bspec(0), bspec(nt)],
        out_specs=pl.BlockSpec((MOE_ROWS, MOE_TN), lambda s, se, sw, sb, sj, sf: (sb[s], sj[s])),
        scratch_shapes=[pltpu.VMEM((D_MODEL, MOE_TN), BF16), pltpu.VMEM((D_MODEL, MOE_TN), BF16)],
    )
    return pl.pallas_call(
        _moe_gu_kernel,
        grid_spec=grid_spec,
        out_shape=jax.ShapeDtypeStruct((n_blocks * MOE_ROWS, D_EXPERT), BF16),
        compiler_params=_cparams(1),
        name="moe_gate_up",
    )(*sched, xs, w_gu, w_gu, b_gu, b_gu)


def _moe_dn_kernel(se, sw, sb, sj, sf, h_ref, wd_ref, bd_ref, o_ref, wd_bf):
    s = pl.program_id(0)

    @pl.when(sf[s] == STEP_NEW_WEIGHTS)
    def _():
        wd_bf[...] = wd_ref[0].astype(BF16)

    @pl.when(sf[s] != STEP_ZERO)
    def _():
        o_ref[...] = jnp.dot(h_ref[...], wd_bf[...], preferred_element_type=F32) + bd_ref[0]

    @pl.when(sf[s] == STEP_ZERO)
    def _():
        o_ref[...] = jnp.zeros(o_ref.shape, o_ref.dtype)


def _moe_down(sched, hid, w_dn, b_dn, n_blocks):
    nt = D_MODEL // MOE_TN
    n_steps = nt * n_blocks
    grid_spec = pltpu.PrefetchScalarGridSpec(
        num_scalar_prefetch=5,
        grid=(n_steps,),
        in_specs=[pl.BlockSpec((MOE_ROWS, D_EXPERT), lambda s, se, sw, sb, sj, sf: (sb[s], 0)),
                  pl.BlockSpec((1, D_EXPERT, MOE_TN), lambda s, se, sw, sb, sj, sf: (se[s], 0, sw[s])),
                  pl.BlockSpec((1, 1, MOE_TN), lambda s, se, sw, sb, sj, sf: (se[s], 0, sw[s]))],
        out_specs=pl.BlockSpec((MOE_ROWS, MOE_TN), lambda s, se, sw, sb, sj, sf: (sb[s], sj[s])),
        scratch_shapes=[pltpu.VMEM((D_EXPERT, MOE_TN), BF16)],
    )
    return pl.pallas_call(
        _moe_dn_kernel,
        grid_spec=grid_spec,
        out_shape=jax.ShapeDtypeStruct((n_blocks * MOE_ROWS, D_MODEL), F32),
        compiler_params=_cparams(1),
        name="moe_down",
    )(*sched, hid, w_dn, b_dn)


def _moe_combine_kernel(pos_ref, ys_hbm, x1_ref, tg_ref, o_ref, buf, sem):
    i = pl.program_id(0)
    nb = pl.num_programs(0)
    tm = o_ref.shape[0]

    def row_copy(blk, slot, r, kk):
        src = pos_ref[(blk * tm + r) * TOP_K + kk]
        return pltpu.make_async_copy(ys_hbm.at[pl.ds(src, 1)], buf.at[slot, kk, pl.ds(r, 1)], sem.at[slot])

    def issue(blk, slot):
        def body(r, carry):
            for kk in range(TOP_K):
                row_copy(blk, slot, r, kk).start()
            return carry
        lax.fori_loop(0, tm, body, 0, unroll=4)

    @pl.when(i == 0)
    def _():
        issue(0, 0)

    @pl.when(i + 1 < nb)
    def _():
        issue(i + 1, (i + 1) % 2)

    slot = i % 2

    for kk in range(TOP_K):
        pltpu.make_async_copy(ys_hbm.at[pl.ds(0, tm)], buf.at[slot, kk], sem.at[slot]).wait()

    tg = tg_ref[...]
    acc = x1_ref[...]
    for kk in range(TOP_K):
        acc = acc + tg[:, kk:kk + 1] * buf[slot, kk]
    o_ref[...] = acc


def _moe_combine(pos, ys, x1, tg):
    n = x1.shape[0]
    tm = min(128, n)
    grid_spec = pltpu.PrefetchScalarGridSpec(
        num_scalar_prefetch=1,
        grid=(n // tm,),
        in_specs=[pl.BlockSpec(memory_space=pl.ANY),
                  pl.BlockSpec((tm, D_MODEL), lambda i, pos: (i, 0)),
                  pl.BlockSpec((tm, LANES), lambda i, pos: (i, 0))],
        out_specs=pl.BlockSpec((tm, D_MODEL), lambda i, pos: (i, 0)),
        scratch_shapes=[pltpu.VMEM((2, TOP_K, tm, D_MODEL), F32), pltpu.SemaphoreType.DMA((2,))],
    )
    return pl.pallas_call(
        _moe_combine_kernel,
        grid_spec=grid_spec,
        out_shape=jax.ShapeDtypeStruct((n, D_MODEL), F32),
        compiler_params=_cparams(1),
        name="moe_combine",
    )(pos, ys, x1, tg)


def _moe_schedule(top_idx, rank, counts, n_blocks):
    n = top_idx.shape[0]
    n_rows = n * TOP_K
    nt = D_EXPERT // MOE_TN
    flat_e = top_idx.reshape(n_rows)
    nb_e = (counts + MOE_ROWS - 1) // MOE_ROWS
    blk_end = jnp.cumsum(nb_e)
    blk_start = blk_end - nb_e
    pos = (blk_start[flat_e] * MOE_ROWS + rank.reshape(n_rows)).astype(jnp.int32)
    total = n_blocks * MOE_ROWS
    row_tok = (jnp.arange(total, dtype=jnp.int32) % n).at[pos].set(jnp.arange(n_rows, dtype=jnp.int32) // TOP_K)
    used = blk_end[-1]
    s = jnp.arange(nt * n_blocks, dtype=jnp.int32)
    live = s < nt * used
    s_eff = jnp.maximum(jnp.minimum(s, nt * used - 1), 0)
    e = jnp.minimum(jnp.sum((s_eff[:, None] >= nt * blk_end[None, :]).astype(jnp.int32), axis=1), N_EXPERTS - 1)
    local = s_eff - nt * blk_start[e]
    nbe = jnp.maximum(nb_e[e], 1)
    sw = (local // nbe).astype(jnp.int32)
    tail = s - nt * used
    sb = jnp.where(live, blk_start[e] + local % nbe, used + tail // nt).astype(jnp.int32)
    sj = jnp.where(live, sw, tail % nt).astype(jnp.int32)
    sf = jnp.where(live, jnp.where(local % nbe == 0, STEP_NEW_WEIGHTS, STEP_RUN), STEP_ZERO).astype(jnp.int32)
    return pos, row_tok, (e, sw, sb, sj, sf)


def _ple_kernel(x_ref, pp_ref, ps_ref, g_ref, wg_ref, wp_ref, op_ref, os_ref, *, np_tiles):
    i = pl.program_id(0)
    x = x_ref[...]
    h = _rms(x, g_ref[...]).astype(BF16)
    gate = jax.nn.sigmoid(jnp.dot(h, wg_ref[...], preferred_element_type=F32))
    p = jnp.where(i < np_tiles, pp_ref[...], ps_ref[...])
    y = x + gate * jnp.dot(p.astype(BF16), wp_ref[...], preferred_element_type=F32)

    @pl.when(i < np_tiles)
    def _():
        op_ref[...] = y

    @pl.when(i >= np_tiles)
    def _():
        os_ref[...] = y


def _ple(x, p_p, p_s, g, wg_bf, wp_bf):
    n_p, n_s = p_p.shape[0], p_s.shape[0]
    n = n_p + n_s
    tm = _row_tile(n_p, n_s, 256)
    np_tiles = n_p // tm
    full = lambda a: pl.BlockSpec(a.shape, lambda i: (0,) * a.ndim)
    pp_spec, ps_spec = _two_group_specs(tm, PLE_DIM, np_tiles)
    op_spec, os_spec = _two_group_specs(tm, D_MODEL, np_tiles)
    return pl.pallas_call(
        functools.partial(_ple_kernel, np_tiles=np_tiles),
        grid=(n // tm,),
        in_specs=[pl.BlockSpec((tm, D_MODEL), lambda i: (i, 0)), pp_spec, ps_spec,
                  full(g), full(wg_bf), full(wp_bf)],
        out_specs=[op_spec, os_spec],
        out_shape=[jax.ShapeDtypeStruct((n_p, D_MODEL), F32), jax.ShapeDtypeStruct((n_s, D_MODEL), F32)],
        compiler_params=_cparams(1),
        name="ple_gate",
    )(x, p_p, p_s, g, wg_bf, wp_bf)


def _pairs_from_state(s):
    b = s.shape[0]
    st = jnp.swapaxes(s, 2, 3).reshape(b, N_PAIRS, 2, HEAD_DIM, HEAD_DIM)
    z = jnp.zeros((b, N_PAIRS, HEAD_DIM, HEAD_DIM), s.dtype)
    top = jnp.concatenate([st[:, :, 0], z], axis=3)
    bot = jnp.concatenate([z, st[:, :, 1]], axis=3)
    return jnp.concatenate([top, bot], axis=2)


def _state_from_pairs(sp):
    b = sp.shape[0]
    st = jnp.stack([sp[:, :, :HEAD_DIM, :HEAD_DIM], sp[:, :, HEAD_DIM:, HEAD_DIM:]], axis=2)
    return jnp.swapaxes(st.reshape(b, N_HEADS, HEAD_DIM, HEAD_DIM), 2, 3)


def _layer(x_p, x_s, cache_k, cache_v, cache_lf, state, shift, p_p, p_s, lw):
    t_p = x_p.shape[1]
    b_s, t_s, _ = x_s.shape
    n_p = x_p.shape[0] * t_p
    n_s = b_s * t_s
    n = n_p + n_s
    past = cache_k.shape[1]
    xp2 = x_p.reshape(n_p, D_MODEL)
    xs2 = x_s.reshape(n_s, D_MODEL)

    w_in_bf = jnp.pad(lw["w_in"], ((0, 0), (0, Z_PAD - lw["w_in"].shape[1]))).astype(BF16)
    z = _in_proj(xp2, xs2, lw["norm_mix_g"], w_in_bf)

    assert t_s == CHUNK, "each sample stream contributes exactly one scan chunk"
    vecs = jnp.zeros((8, MIX_W), F32)
    vecs = vecs.at[0].set(lw["rwkv_w0"]).at[1].set(lw["rwkv_a0"]).at[2].set(lw["rwkv_kk"])
    vecs = vecs.at[3].set(lw["rwkv_ka"]).at[4].set(lw["rwkv_rk"].reshape(MIX_W))
    wl = jnp.zeros((256, 3 * MIX_W), F32)
    wl = wl.at[0:64, 0:MIX_W].set(lw["rwkv_w2"]).at[64:128, MIX_W:2 * MIX_W].set(lw["rwkv_a2"])
    wl = wl.at[128:256, 2 * MIX_W:].set(lw["rwkv_g2"])
    r, w, k, kk, bb, v_pm, g, bonus = _rwkv_pre(z, shift, lw["rwkv_mu"].reshape(1, RWKV_PROJ), vecs, wl, n_p)
    s0 = jnp.concatenate([jnp.zeros((1, N_PAIRS, LANES, LANES), F32), _pairs_from_state(state.astype(F32))], axis=0)
    y_scan, s_out = _rwkv_scan(r, w, k, kk, bb, v_pm, s0, n_p)
    s_new = _state_from_pairs(s_out)
    shift_new_p = z[n_p - 1:n_p, :RWKV_PROJ].reshape(1, 1, RWKV_PROJ)
    shift_new_s = z[n_p:, :RWKV_PROJ].reshape(b_s, t_s, RWKV_PROJ)[:, -1:, :]

    fvecs = jnp.zeros((8, MIX_W), F32)
    fvecs = fvecs.at[0].set(jnp.tile(lw["fox_q_g"], N_HEADS)).at[1].set(jnp.tile(lw["fox_k_g"], N_HEADS))
    bf = jnp.zeros((1, LANES), F32).at[0, :N_HEADS].set(lw["fox_b_f"])
    q_pm, k_pm, vv_pm, k_new_p, v_new_p, k_new_s, v_new_s, lf = _fox_pre(z, fvecs, bf, n_p)
    lf = lf[:, :N_HEADS]
    lf_p = lf[:n_p].T.reshape(1, N_HEADS, n_p)
    c_p = _cumsum_lanes(lf_p).reshape(N_PAIRS, 2, n_p)
    yf_p = _attn_prompt(q_pm, k_pm, vv_pm, c_p, n_p)
    lf_s = jnp.swapaxes(lf[n_p:].reshape(b_s, t_s, N_HEADS), 1, 2)
    lf_all = jnp.concatenate([jnp.swapaxes(cache_lf.astype(F32), 1, 2), lf_s], axis=2)
    pad = (-lf_all.shape[2]) % LANES
    c_all = _cumsum_lanes(jnp.pad(lf_all, ((0, 0), (0, 0), (0, pad))))
    yf_s = _attn_sample(q_pm, k_pm, vv_pm, cache_k.reshape(b_s, past, MIX_W), cache_v.reshape(b_s, past, MIX_W),
                        c_all[:, :, :past], c_all[:, :, past:past + t_s], n_p)
    yf = jnp.concatenate([yf_p, yf_s], axis=1)

    ln = jnp.stack([lw["rwkv_ln_g"], lw["rwkv_ln_b"]])
    x1, hf, ti, tg, cnt = _mix_out(xp2, xs2, y_scan, bonus, g, yf, ln, lw["w_out"].astype(BF16),
                                   lw["norm_ffn_g"].reshape(1, D_MODEL), lw["router_w"],
                                   lw["router_b"].reshape(1, N_EXPERTS))

    n_blocks = n * TOP_K // MOE_ROWS + N_EXPERTS
    pos, row_tok, sched = _moe_schedule(ti[:, :TOP_K], ti[:, TOP_K:2 * TOP_K], cnt[0], n_blocks)
    xs = _moe_gather(row_tok, hf, n_blocks)
    hid = _moe_gate_up(sched, xs, lw["expert_w_gu"], lw["expert_b_gu"].reshape(N_EXPERTS, 1, 2 * D_EXPERT), n_blocks)
    ys = _moe_down(sched, hid, lw["expert_w_down"], lw["expert_b_down"].reshape(N_EXPERTS, 1, D_MODEL), n_blocks)
    x2 = _moe_combine(pos, ys, x1, tg)

    y_p, y_s = _ple(x2, p_p, p_s, lw["ple_norm_g"].reshape(1, D_MODEL), lw["ple_w_gate"].astype(BF16),
                    lw["ple_w_proj"].astype(BF16))

    heads = lambda a, bsz, t: a.reshape(bsz, t, N_HEADS, HEAD_DIM)
    out_p = (y_p.reshape(x_p.shape), heads(k_new_p, 1, n_p), heads(v_new_p, 1, n_p),
             lf[:n_p].reshape(1, n_p, N_HEADS), s_new[:1], shift_new_p)
    out_s = (y_s.reshape(x_s.shape), heads(k_new_s, b_s, t_s), heads(v_new_s, b_s, t_s),
             lf[n_p:].reshape(b_s, t_s, N_HEADS), s_new[1:], shift_new_s)
    return out_p, out_s


def kernel(x_prompt, x_sample, cache_fox_k, cache_fox_v, cache_fox_logf, state_rwkv, state_rwkv_shift, p_prompt, p_sample, norm_mix_g, w_in, rwkv_mu, rwkv_w0, rwkv_w2, rwkv_a0, rwkv_a2, rwkv_g2, rwkv_kk, rwkv_ka, rwkv_rk, rwkv_ln_g, rwkv_ln_b, fox_q_g, fox_k_g, fox_b_f, w_out, norm_ffn_g, router_w, router_b, expert_w_gu, expert_b_gu, expert_w_down, expert_b_down, ple_norm_g, ple_w_gate, ple_w_proj):
    assert x_prompt.shape[0] == 1 and w_in.shape[0] == 1, "one prompt stream, one layer"
    lw = dict(norm_mix_g=norm_mix_g[0], w_in=w_in[0], rwkv_mu=rwkv_mu[0], rwkv_w0=rwkv_w0[0], rwkv_w2=rwkv_w2[0],
              rwkv_a0=rwkv_a0[0], rwkv_a2=rwkv_a2[0], rwkv_g2=rwkv_g2[0], rwkv_kk=rwkv_kk[0], rwkv_ka=rwkv_ka[0],
              rwkv_rk=rwkv_rk[0], rwkv_ln_g=rwkv_ln_g[0], rwkv_ln_b=rwkv_ln_b[0], fox_q_g=fox_q_g[0],
              fox_k_g=fox_k_g[0], fox_b_f=fox_b_f[0], w_out=w_out[0], norm_ffn_g=norm_ffn_g[0],
              router_w=router_w[0], router_b=router_b[0], expert_w_gu=expert_w_gu[0], expert_b_gu=expert_b_gu[0],
              expert_w_down=expert_w_down[0], expert_b_down=expert_b_down[0], ple_norm_g=ple_norm_g[0],
              ple_w_gate=ple_w_gate[0], ple_w_proj=ple_w_proj[0])
    n_p = x_prompt.shape[1]
    (y_p, k_p, v_p, lf_p, s_p, sh_p), (y_s, k_s, v_s, lf_s, s_s, sh_s) = _layer(
        x_prompt, x_sample, cache_fox_k[0], cache_fox_v[0], cache_fox_logf[0], state_rwkv[0],
        state_rwkv_shift[0], p_prompt[0].reshape(n_p, PLE_DIM), p_sample[0].reshape(-1, PLE_DIM), lw)
    add = lambda a: a[None]
    return (y_p, y_s, add(k_p), add(v_p), add(lf_p), add(s_p), add(sh_p),
            add(k_s), add(v_s), add(lf_s), add(s_s), add(sh_s))
```

```python
import functools

import numpy as np
import jax
import jax.numpy as jnp
from jax import lax
from jax.experimental import pallas as pl
from jax.experimental.pallas import tpu as pltpu

F32 = jnp.float32
BF16 = jnp.bfloat16
HI = lax.Precision.HIGHEST

D_MODEL = 2048
HEAD_DIM = 64
N_HEADS = 16
N_PAIRS = N_HEADS // 2
MIX_W = N_HEADS * HEAD_DIM
CHUNK = 64
RWKV_PROJ = 3 * MIX_W + 64 + 64 + 128
FOX_PROJ = 3 * MIX_W + N_HEADS
Z_HALF = RWKV_PROJ
Z_PAD = 2 * Z_HALF
N_EXPERTS = 32
TOP_K = 4
D_EXPERT = 2048
SWIGLU_LIMIT = 7.0
SWIGLU_ALPHA = 1.702
PLE_DIM = 256
RMS_EPS = 1e-6
GN_EPS = 64e-5
L2_EPS = 1e-12
NEG_BIG = -1e30
LOG2E = 1.4426950408889634

LANES = 128
MOE_ROWS = 512
MOE_TN = 1024
ROW_CHUNKS = D_MODEL // 2 // LANES
VMEM_LIMIT = 52 * 1024 * 1024


def _cparams(n_axes, vmem=VMEM_LIMIT):
    return pltpu.CompilerParams(dimension_semantics=("arbitrary",) * n_axes, vmem_limit_bytes=vmem)


def _head_sum(x):
    r = lax.broadcasted_iota(jnp.int32, (LANES, LANES), 0) // HEAD_DIM
    c = lax.broadcasted_iota(jnp.int32, (LANES, LANES), 1) // HEAD_DIM
    bd = jnp.where(r == c, 1.0, 0.0).astype(BF16)
    hi = x.astype(BF16)
    lo = (x - hi.astype(F32)).astype(BF16)
    parts = []
    for i in range(x.shape[1] // LANES):
        sl = slice(i * LANES, (i + 1) * LANES)
        parts.append(jnp.dot(hi[:, sl], bd, preferred_element_type=F32)
                     + jnp.dot(lo[:, sl], bd, preferred_element_type=F32))
    return parts[0] if len(parts) == 1 else jnp.concatenate(parts, axis=1)


def _log_sigmoid(x):
    return jnp.minimum(x, 0.0) - jnp.log1p(jnp.exp(-jnp.abs(x)))


def _rms(x, g):
    ms = jnp.mean(x * x, axis=-1, keepdims=True)
    return x * lax.rsqrt(ms + RMS_EPS) * g


def _row_tile(n_p, n_s, pref):
    return pref if (n_p % pref == 0 and n_s % pref == 0) else 128


def _two_group_specs(tm, width, np_tiles, n_grid_axes=1):
    if n_grid_axes == 1:
        return (pl.BlockSpec((tm, width), lambda i: (jnp.minimum(i, np_tiles - 1), 0)),
                pl.BlockSpec((tm, width), lambda i: (jnp.maximum(i - np_tiles, 0), 0)))
    return (pl.BlockSpec((tm, width), lambda i, j: (jnp.minimum(i, np_tiles - 1), 0)),
            pl.BlockSpec((tm, width), lambda i, j: (jnp.maximum(i - np_tiles, 0), 0)))


def _inproj_kernel(xp_ref, xs_ref, g_ref, w_ref, o_ref, h_scr, *, np_tiles):
    i = pl.program_id(0)
    first = pl.program_id(1) == 0

    @pl.when(jnp.logical_and(first, i < np_tiles))
    def _():
        h_scr[...] = _rms(xp_ref[...], g_ref[...]).astype(BF16)

    @pl.when(jnp.logical_and(first, i >= np_tiles))
    def _():
        h_scr[...] = _rms(xs_ref[...], g_ref[...]).astype(BF16)

    o_ref[...] = jnp.dot(h_scr[...], w_ref[...], preferred_element_type=F32)


def _in_proj(x_p, x_s, g, w_bf):
    n_p, n_s = x_p.shape[0], x_s.shape[0]
    n = n_p + n_s
    tm = _row_tile(n_p, n_s, 512)
    tn = Z_PAD // 4
    xp_spec, xs_spec = _two_group_specs(tm, D_MODEL, n_p // tm, 2)
    return pl.pallas_call(
        functools.partial(_inproj_kernel, np_tiles=n_p // tm),
        grid=(n // tm, Z_PAD // tn),
        in_specs=[xp_spec, xs_spec,
                  pl.BlockSpec((1, D_MODEL), lambda i, j: (0, 0)),
                  pl.BlockSpec((D_MODEL, tn), lambda i, j: (0, j))],
        out_specs=pl.BlockSpec((tm, tn), lambda i, j: (i, j)),
        out_shape=jax.ShapeDtypeStruct((n, Z_PAD), F32),
        scratch_shapes=[pltpu.VMEM((tm, D_MODEL), BF16)],
        compiler_params=_cparams(2),
        name="in_proj",
    )(x_p, x_s, g.reshape(1, D_MODEL), w_bf)


def _rwkv_pre_kernel(z_ref, prev_ref, mu_ref, vec_ref, wl_ref,
                     r_o, w_o, k_o, kk_o, b_o, v_o, g_o, bn_o, carry, *, n_prompt_tiles):
    i = pl.program_id(0)
    z = z_ref[...]
    tm = z.shape[0]

    @pl.when(i == 0)
    def _():
        carry[...] = prev_ref[0, 0:1, :]

    rolled = pltpu.roll(z, 1, axis=0)
    row = lax.broadcasted_iota(jnp.int32, z.shape, 0)
    shifted = jnp.where(row == 0, carry[...], rolled)
    is_sample = i >= n_prompt_tiles
    for c in range(tm // CHUNK):
        shifted = jnp.where(jnp.logical_and(is_sample, row == c * CHUNK), prev_ref[0, c:c + 1, :], shifted)
    carry[...] = z[tm - 1:tm, :]
    zs = z + mu_ref[...] * (shifted - z)

    r = zs[:, 0:MIX_W]
    k = zs[:, MIX_W:2 * MIX_W]
    v = zs[:, 2 * MIX_W:3 * MIX_W]
    lo = zs[:, 3 * MIX_W:RWKV_PROJ]
    lane = lax.broadcasted_iota(jnp.int32, lo.shape, 1)
    f = jnp.where(lane < 64, jnp.tanh(lo), jnp.where(lane < 128, lo, jax.nn.sigmoid(lo)))
    lora = _dot3(f, wl_ref[...])
    w_pre = vec_ref[0:1, :] + lora[:, 0:MIX_W]
    log_decay = -jnp.exp(_log_sigmoid(w_pre) - 0.5)
    a = jax.nn.sigmoid(vec_ref[1:2, :] + lora[:, MIX_W:2 * MIX_W])
    g = lora[:, 2 * MIX_W:3 * MIX_W]
    kk = k * vec_ref[2:3, :]
    kk = kk * lax.rsqrt(_head_sum(kk * kk) + L2_EPS)
    k_mod = k * (1.0 + (a - 1.0) * vec_ref[3:4, :])
    bonus = _head_sum(r * k_mod * vec_ref[4:5, :]) * v
    kka = kk * a

    g_o[...] = g
    bn_o[...] = bonus
    for p in range(N_PAIRS):
        sl = slice(p * LANES, (p + 1) * LANES)
        r_o[p] = r[:, sl]
        w_o[p] = log_decay[:, sl]
        k_o[p] = k_mod[:, sl]
        kk_o[p] = kk[:, sl]
        b_o[p] = kka[:, sl]
        v_o[p] = v[:, sl]


def _rwkv_pre(z, shift, mu, vecs, wl, n_prompt):
    n = z.shape[0]
    n_s = n - n_prompt
    tm = _row_tile(n_prompt, n_s, 256)
    if tm % CHUNK:
        tm = CHUNK
    cpt = tm // CHUNK
    nt = n // tm
    n_prompt_tiles = n_prompt // tm
    prev = jnp.concatenate([jnp.zeros((1, cpt, RWKV_PROJ), F32),
                            shift.astype(F32).reshape(n_s // tm, cpt, RWKV_PROJ)], axis=0)
    tok = jax.ShapeDtypeStruct((n, MIX_W), F32)
    tspec = pl.BlockSpec((tm, MIX_W), lambda i: (i, 0))
    pm = jax.ShapeDtypeStruct((N_PAIRS, n, LANES), F32)
    pspec = pl.BlockSpec((N_PAIRS, tm, LANES), lambda i: (0, i, 0))
    return pl.pallas_call(
        functools.partial(_rwkv_pre_kernel, n_prompt_tiles=n_prompt_tiles),
        grid=(nt,),
        in_specs=[pl.BlockSpec((tm, Z_HALF), lambda i: (i, 0)),
                  pl.BlockSpec((1, cpt, RWKV_PROJ), lambda i: (jnp.maximum(i - (n_prompt_tiles - 1), 0), 0, 0)),
                  pl.BlockSpec((1, RWKV_PROJ), lambda i: (0, 0)),
                  pl.BlockSpec((8, MIX_W), lambda i: (0, 0)),
                  pl.BlockSpec((256, 3 * MIX_W), lambda i: (0, 0))],
        out_specs=[pspec, pspec, pspec, pspec, pspec, pspec, tspec, tspec],
        out_shape=[pm, pm, pm, pm, pm, pm, tok, tok],
        scratch_shapes=[pltpu.VMEM((1, RWKV_PROJ), F32)],
        compiler_params=_cparams(1),
        name="rwkv_pre",
    )(z, prev, mu, vecs, wl)


def _dot(a, b):
    return jnp.dot(a, b, precision=HI, preferred_element_type=F32)


def _bdot(a, b):
    return jnp.dot(a, b, preferred_element_type=F32)


def _split_bf16(x):
    hi = x.astype(BF16)
    return hi, (x - hi.astype(F32)).astype(BF16)


def _dot3(a, b):
    a_hi, a_lo = _split_bf16(a)
    b_hi, b_lo = _split_bf16(b)
    return _bdot(a_hi, b_hi) + (_bdot(a_hi, b_lo) + _bdot(a_lo, b_hi))


def _pair_rows(x):
    lo_half = lax.broadcasted_iota(jnp.int32, x.shape, 1) < HEAD_DIM
    return jnp.concatenate([jnp.where(lo_half, x, 0.0), jnp.where(lo_half, 0.0, x)], axis=0)


def _scan_chunk(P, r, lw, k, kk, b, v):
    c2 = 2 * CHUNK
    i = lax.broadcasted_iota(jnp.int32, (c2, c2), 0)
    j = lax.broadcasted_iota(jnp.int32, (c2, c2), 1)
    ti = lax.broadcasted_iota(jnp.int32, (CHUNK, CHUNK), 0)
    tj = lax.broadcasted_iota(jnp.int32, (CHUNK, CHUNK), 1)
    cl = _dot((tj <= ti).astype(F32), lw)
    yield
    g_end = cl[CHUNK - 1:CHUNK, :]
    e_neg = jnp.exp(-cl)
    e_end = jnp.exp(g_end - cl)
    kap = _pair_rows(kk * jnp.exp(cl - lw)).astype(BF16)
    rt = _pair_rows(r * jnp.exp(cl)).astype(BF16)
    bt = _pair_rows(b * e_neg).astype(BF16)
    kt = _pair_rows(k * e_neg).astype(BF16)
    kh = _pair_rows(k * e_end)
    bh = _pair_rows(b * e_end)
    vv = _pair_rows(v)
    vv_b = vv.astype(BF16)
    p_b = P.astype(BF16)

    g = lax.dot_general(jnp.concatenate([kap, rt], axis=0), jnp.concatenate([bt, kt], axis=0),
                        (((1,), (1,)), ((), ())), preferred_element_type=F32)
    yield
    strict = j < i
    incl = j <= i
    a_b = jnp.where(strict, g[:c2, :c2], 0.0)
    a_bb = a_b.astype(BF16)
    a_k = jnp.where(strict, g[:c2, c2:], 0.0).astype(BF16)
    r_b = jnp.where(incl, g[c2:, :c2], 0.0).astype(BF16)
    r_k = jnp.where(incl, g[c2:, c2:], 0.0).astype(BF16)

    t_inv = (i == j).astype(F32) - jnp.where(jnp.logical_and((i & 1) == 1, j == i - 1), a_b, 0.0)
    n = 2
    while n < CHUNK:
        m = jnp.logical_and((i >> n.bit_length()) == (j >> n.bit_length()),
                            jnp.logical_and((i & (2 * n - 1)) >= n, (j & (2 * n - 1)) < n))
        t_b = t_inv.astype(BF16)
        ta = _bdot(t_b, a_bb).astype(BF16)
        yield
        t_inv = t_inv - jnp.where(m, _bdot(ta, t_b), 0.0)
        yield
        n *= 2

    w = _bdot(jnp.concatenate([kap, a_k], axis=1), jnp.concatenate([p_b, vv_b], axis=0))
    yield
    u = _bdot(t_inv.astype(BF16), w.astype(BF16))
    yield
    vu_b = jnp.concatenate([vv_b, u.astype(BF16)], axis=0)
    y2 = _bdot(rt, p_b) + _bdot(jnp.concatenate([r_k, -r_b], axis=1), vu_b)
    y = y2[:CHUNK] + y2[CHUNK:]
    yield
    g_col = jnp.broadcast_to(jnp.exp(g_end), (c2, c2)).T
    p_new = g_col * P + _dot3(jnp.concatenate([kh.T, -bh.T], axis=1), jnp.concatenate([vv, u], axis=0))
    return y, p_new


def _run_interleaved(gens):
    results = [None] * len(gens)
    live = list(range(len(gens)))
    while live:
        for idx in list(live):
            try:
                next(gens[idx])
            except StopIteration as stop:
                results[idx] = stop.value
                live.remove(idx)
    return results


def _scan_kernel(r_ref, w_ref, k_ref, kk_ref, b_ref, v_ref, s0_ref, y_ref, sout_ref, s_scr, *, n_prompt_chunks):
    c = pl.program_id(0)

    @pl.when(jnp.logical_or(c == 0, c >= n_prompt_chunks))
    def _():
        s_scr[...] = s0_ref[0]

    outs = _run_interleaved([
        _scan_chunk(s_scr[p], r_ref[p], w_ref[p], k_ref[p], kk_ref[p], b_ref[p], v_ref[p])
        for p in range(N_PAIRS)])
    for p, (y, p_new) in enumerate(outs):
        y_ref[p] = y
        s_scr[p] = p_new
        sout_ref[0, p] = p_new


def _rwkv_scan(r, w, k, kk, b, v, s0, n_prompt):
    n = r.shape[1]
    npc = n_prompt // CHUNK
    n_seq = s0.shape[0]
    pspec = pl.BlockSpec((N_PAIRS, CHUNK, LANES), lambda c: (0, c, 0))
    sspec = pl.BlockSpec((1, N_PAIRS, LANES, LANES), lambda c: (jnp.maximum(c - (npc - 1), 0), 0, 0, 0))
    return pl.pallas_call(
        functools.partial(_scan_kernel, n_prompt_chunks=npc),
        grid=(n // CHUNK,),
        in_specs=[pspec, pspec, pspec, pspec, pspec, pspec, sspec],
        out_specs=[pspec, sspec],
        out_shape=[jax.ShapeDtypeStruct((N_PAIRS, n, LANES), F32),
                   jax.ShapeDtypeStruct((n_seq, N_PAIRS, LANES, LANES), F32)],
        scratch_shapes=[pltpu.VMEM((N_PAIRS, LANES, LANES), F32)],
        compiler_params=_cparams(1),
        name="rwkv_scan",
    )(r, w, k, kk, b, v, s0)


def _fox_pre_kernel(z_ref, vec_ref, bf_ref, q_o, k_o, v_o, knp_o, vnp_o, kns_o, vns_o, lf_o, *, np_tiles):
    z = z_ref[...]
    q = z[:, 0:MIX_W]
    k = z[:, MIX_W:2 * MIX_W]
    v = z[:, 2 * MIX_W:3 * MIX_W]
    fl = z[:, 3 * MIX_W:3 * MIX_W + LANES]
    inv = 1.0 / HEAD_DIM
    qn = q * lax.rsqrt(_head_sum(q * q) * inv + RMS_EPS) * vec_ref[0:1, :]
    kn = k * lax.rsqrt(_head_sum(k * k) * inv + RMS_EPS) * vec_ref[1:2, :]
    qs = (qn * (HEAD_DIM ** -0.5 * LOG2E)).astype(BF16)
    kb = kn.astype(BF16)
    vb = v.astype(BF16)
    for p in range(N_PAIRS):
        sl = slice(p * LANES, (p + 1) * LANES)
        q_o[p] = qs[:, sl]
        k_o[p] = kb[:, sl]
        v_o[p] = vb[:, sl]
    lf_o[...] = _log_sigmoid(fl + bf_ref[...])

    @pl.when(pl.program_id(0) < np_tiles)
    def _():
        knp_o[...] = kn
        vnp_o[...] = v

    @pl.when(pl.program_id(0) >= np_tiles)
    def _():
        kns_o[...] = kn
        vns_o[...] = v


def _fox_pre(z, vecs, bf, n_p):
    n = z.shape[0]
    n_s = n - n_p
    tm = _row_tile(n_p, n_s, 256)
    np_tiles = n_p // tm
    pm = jax.ShapeDtypeStruct((N_PAIRS, n, LANES), BF16)
    pspec = pl.BlockSpec((N_PAIRS, tm, LANES), lambda i: (0, i, 0))
    p_spec, s_spec = _two_group_specs(tm, MIX_W, np_tiles)
    tok_p = jax.ShapeDtypeStruct((n_p, MIX_W), F32)
    tok_s = jax.ShapeDtypeStruct((n_s, MIX_W), F32)
    return pl.pallas_call(
        functools.partial(_fox_pre_kernel, np_tiles=np_tiles),
        grid=(n // tm,),
        in_specs=[pl.BlockSpec((tm, Z_HALF), lambda i: (i, 1)),
                  pl.BlockSpec((8, MIX_W), lambda i: (0, 0)),
                  pl.BlockSpec((1, LANES), lambda i: (0, 0))],
        out_specs=[pspec, pspec, pspec, p_spec, p_spec, s_spec, s_spec,
                   pl.BlockSpec((tm, LANES), lambda i: (i, 0))],
        out_shape=[pm, pm, pm, tok_p, tok_p, tok_s, tok_s, jax.ShapeDtypeStruct((n, LANES), F32)],
        compiler_params=_cparams(1),
        name="fox_pre",
    )(z, vecs, bf)


def _cumsum_kernel(x_ref, o_ref):
    r = lax.broadcasted_iota(jnp.int32, (LANES, LANES), 0)
    c = lax.broadcasted_iota(jnp.int32, (LANES, LANES), 1)
    tri = (r <= c).astype(F32)
    carry = jnp.zeros((N_HEADS, 1), F32)
    for i in range(x_ref.shape[2] // LANES):
        sl = slice(i * LANES, (i + 1) * LANES)
        cs = jnp.dot(x_ref[0, :, sl], tri, precision=HI, preferred_element_type=F32) + carry
        o_ref[0, :, sl] = cs
        carry = cs[:, LANES - 1:LANES]


def _cumsum_lanes(x):
    b, h, t = x.shape
    return pl.pallas_call(
        _cumsum_kernel,
        grid=(b,),
        in_specs=[pl.BlockSpec((1, h, t), lambda i: (i, 0, 0))],
        out_specs=pl.BlockSpec((1, h, t), lambda i: (i, 0, 0)),
        out_shape=jax.ShapeDtypeStruct((b, h, t), F32),
        compiler_params=_cparams(1),
        name="cumsum_logf",
    )(x)


def _rep_lanes(m, tk):
    if tk % LANES == 0:
        return m if tk == LANES else jnp.concatenate([m] * (tk // LANES), axis=1)
    return m[:, :tk]


DEN_LANE = (HEAD_DIM, 0)


def _attn_update(qa, qb, kb, vb, ck, mask, m_ref, acc_ref):
    tk = kb.shape[0]
    lane = lax.broadcasted_iota(jnp.int32, vb.shape, 1)
    own = (lane < HEAD_DIM, lane >= HEAD_DIM)
    ck2 = ck * LOG2E
    for h, qh in enumerate((qa, qb)):
        s = lax.dot_general(qh, kb, (((1,), (1,)), ((), ())), preferred_element_type=F32)
        s = s - ck2[h:h + 1, :]
        if mask is not None:
            s = jnp.where(mask, s, NEG_BIG)
        m_prev = m_ref[h]
        m_next = jnp.maximum(m_prev, jnp.max(s, axis=1, keepdims=True))
        p = jnp.exp2(s - _rep_lanes(m_next, tk))
        alpha = jnp.exp2(m_prev - m_next)
        m_ref[h] = m_next
        v_aug = jnp.where(own[h], vb.astype(F32), jnp.where(lane == DEN_LANE[h], 1.0, 0.0)).astype(BF16)
        acc_ref[h] = acc_ref[h] * alpha + jnp.dot(p.astype(BF16), v_aug, preferred_element_type=F32)


def _attn_finish(acc_ref):
    lo_half = lax.broadcasted_iota(jnp.int32, acc_ref.shape[1:], 1) < HEAD_DIM
    acc_a = acc_ref[0]
    acc_b = acc_ref[1]
    return jnp.where(lo_half, acc_a / acc_a[:, DEN_LANE[0]:DEN_LANE[0] + 1], acc_b / acc_b[:, DEN_LANE[1]:DEN_LANE[1] + 1])


def _split_q(q):
    qf = q.astype(F32)
    lo_half = lax.broadcasted_iota(jnp.int32, qf.shape, 1) < HEAD_DIM
    return jnp.where(lo_half, qf, 0.0).astype(BF16), jnp.where(lo_half, 0.0, qf).astype(BF16)


def _attn_prompt_kernel(qi_ref, ki_ref, q_ref, k_ref, v_ref, c_ref, o_ref,
                        qa_scr, qb_scr, m_scr, acc_scr, *, tq, tk):
    s_id = pl.program_id(1)
    qi = qi_ref[s_id]
    ki = ki_ref[s_id]

    @pl.when(ki == 0)
    def _():
        qa, qb = _split_q(q_ref[0])
        qa_scr[...] = qa
        qb_scr[...] = qb
        m_scr[...] = jnp.full(m_scr.shape, NEG_BIG, F32)
        acc_scr[...] = jnp.zeros(acc_scr.shape, F32)

    crosses_diagonal = ki * tk + (tk - 1) > qi * tq

    @pl.when(crosses_diagonal)
    def _():
        rows = qi * tq + lax.broadcasted_iota(jnp.int32, (tq, tk), 0)
        cols = ki * tk + lax.broadcasted_iota(jnp.int32, (tq, tk), 1)
        _attn_update(qa_scr[...], qb_scr[...], k_ref[0], v_ref[0], c_ref[0], cols <= rows, m_scr, acc_scr)

    @pl.when(jnp.logical_not(crosses_diagonal))
    def _():
        _attn_update(qa_scr[...], qb_scr[...], k_ref[0], v_ref[0], c_ref[0], None, m_scr, acc_scr)

    @pl.when(ki == ((qi + 1) * tq - 1) // tk)
    def _():
        o_ref[0] = _attn_finish(acc_scr).astype(o_ref.dtype)


def _attn_prompt(q, k, v, c, t):
    tq = min(1024, t)
    tk = min(512, t)
    steps = [(qi, ki) for qi in range(t // tq) for ki in range(((qi + 1) * tq - 1) // tk + 1)]
    qi_arr = jnp.asarray(np.array([s[0] for s in steps], np.int32))
    ki_arr = jnp.asarray(np.array([s[1] for s in steps], np.int32))
    grid_spec = pltpu.PrefetchScalarGridSpec(
        num_scalar_prefetch=2,
        grid=(N_PAIRS, len(steps)),
        in_specs=[pl.BlockSpec((1, tq, LANES), lambda p, s, qi, ki: (p, qi[s], 0)),
                  pl.BlockSpec((1, tk, LANES), lambda p, s, qi, ki: (p, ki[s], 0)),
                  pl.BlockSpec((1, tk, LANES), lambda p, s, qi, ki: (p, ki[s], 0)),
                  pl.BlockSpec((1, 2, tk), lambda p, s, qi, ki: (p, 0, ki[s]))],
        out_specs=pl.BlockSpec((1, tq, LANES), lambda p, s, qi, ki: (p, qi[s], 0)),
        scratch_shapes=[pltpu.VMEM((tq, LANES), BF16), pltpu.VMEM((tq, LANES), BF16),
                        pltpu.VMEM((2, tq, LANES), F32), pltpu.VMEM((2, tq, LANES), F32)],
    )
    return pl.pallas_call(
        functools.partial(_attn_prompt_kernel, tq=tq, tk=tk),
        grid_spec=grid_spec,
        out_shape=jax.ShapeDtypeStruct((N_PAIRS, t, LANES), BF16),
        compiler_params=_cparams(2),
        name="fox_attn_prompt",
    )(qi_arr, ki_arr, q, k, v, c)


def _attn_sample_kernel(q_ref, ck_ref, cv_ref, kn_ref, vn_ref, cc_ref, cn_ref, o_ref,
                        m_scr, acc_scr, *, nkb):
    j = pl.program_id(1)
    ts = q_ref.shape[1]

    @pl.when(j == 0)
    def _():
        m_scr[...] = jnp.full(m_scr.shape, NEG_BIG, F32)
        acc_scr[...] = jnp.zeros(acc_scr.shape, F32)

    @pl.when(j < nkb)
    def _():
        for p in range(N_PAIRS):
            sl = slice(p * LANES, (p + 1) * LANES)
            qa, qb = _split_q(q_ref[p])
            _attn_update(qa, qb, ck_ref[0, :, sl].astype(BF16), cv_ref[0, :, sl].astype(BF16),
                         cc_ref[0, 2 * p:2 * p + 2, :], None, m_scr.at[p], acc_scr.at[p])

    @pl.when(j == nkb)
    def _():
        rows = lax.broadcasted_iota(jnp.int32, (ts, ts), 0)
        cols = lax.broadcasted_iota(jnp.int32, (ts, ts), 1)
        for p in range(N_PAIRS):
            qa, qb = _split_q(q_ref[p])
            _attn_update(qa, qb, kn_ref[p], vn_ref[p], cn_ref[0, 2 * p:2 * p + 2, :], cols <= rows,
                         m_scr.at[p], acc_scr.at[p])
            o_ref[p] = _attn_finish(acc_scr.at[p]).astype(o_ref.dtype)


def _attn_sample(q, k, v, cache_k, cache_v, c_cache, c_new, row0):
    b, past, _ = cache_k.shape
    ts = c_new.shape[2]
    tk = min(1024, past)
    nkb = past // tk
    blk0 = row0 // ts
    qspec = pl.BlockSpec((N_PAIRS, ts, LANES), lambda i, j: (0, blk0 + i, 0))
    cspec = pl.BlockSpec((1, tk, MIX_W), lambda i, j: (i, jnp.minimum(j, nkb - 1), 0))
    return pl.pallas_call(
        functools.partial(_attn_sample_kernel, nkb=nkb),
        grid=(b, nkb + 1),
        in_specs=[qspec, cspec, cspec, qspec, qspec,
                  pl.BlockSpec((1, N_HEADS, tk), lambda i, j: (i, 0, jnp.minimum(j, nkb - 1))),
                  pl.BlockSpec((1, N_HEADS, ts), lambda i, j: (i, 0, 0))],
        out_specs=pl.BlockSpec((N_PAIRS, ts, LANES), lambda i, j: (0, i, 0)),
        out_shape=jax.ShapeDtypeStruct((N_PAIRS, b * ts, LANES), BF16),
        scratch_shapes=[pltpu.VMEM((N_PAIRS, 2, ts, LANES), F32), pltpu.VMEM((N_PAIRS, 2, ts, LANES), F32)],
        compiler_params=_cparams(2),
        name="fox_attn_sample",
    )(q, cache_k, cache_v, k, v, c_cache, c_new)


def _mix_out_kernel(xp_ref, xs_ref, ys_ref, bn_ref, g_ref, yfp_ref, yfs_ref, ln_ref, wo_ref, gf_ref, rw_ref, rb_ref,
                    x1_o, hf_o, ti_o, tg_o, cnt_o, cnt_scr, *, np_tiles):
    y = jnp.concatenate([ys_ref[p] for p in range(N_PAIRS)], axis=1)
    inv = 1.0 / HEAD_DIM
    mu = _head_sum(y) * inv
    d = y - mu
    var = _head_sum(d * d) * inv
    yn = d * lax.rsqrt(var + GN_EPS) * ln_ref[0:1, :] + ln_ref[1:2, :]
    yr = ((yn + bn_ref[...]) * g_ref[...]).astype(BF16)
    in_prompt = pl.program_id(0) < np_tiles
    yf = jnp.concatenate([jnp.where(in_prompt, yfp_ref[p].astype(F32), yfs_ref[p].astype(F32)).astype(BF16)
                          for p in range(N_PAIRS)], axis=1)
    mix = jnp.concatenate([yr, yf], axis=1)
    x_res = jnp.where(pl.program_id(0) < np_tiles, xp_ref[...], xs_ref[...])
    x1 = x_res + jnp.dot(mix, wo_ref[...], preferred_element_type=F32)
    x1_o[...] = x1
    hf = _rms(x1, gf_ref[...])
    bits = pltpu.bitcast(hf, jnp.uint32)
    rne = (bits + jnp.uint32(0x7FFF) + ((bits >> 16) & jnp.uint32(1))) >> 16
    words = rne[:, :D_MODEL // 2] | (rne[:, D_MODEL // 2:] << 16)
    tm_rows = hf.shape[0]
    for jc in range(ROW_CHUNKS):
        hf_o[pl.ds(jc, tm_rows, stride=ROW_CHUNKS), :] = words[:, jc * LANES:(jc + 1) * LANES]

    logits = _dot3(hf, rw_ref[...]) + rb_ref[...]
    lane_e = lax.broadcasted_iota(jnp.int32, logits.shape, 1).astype(F32)
    vals = []
    idxs = []
    cur = logits
    for _ in range(TOP_K):
        m = jnp.max(cur, axis=1, keepdims=True)
        am = jnp.min(jnp.where(cur == m, lane_e, float(N_EXPERTS)), axis=1, keepdims=True)
        vals.append(m)
        idxs.append(am)
        cur = jnp.where(lane_e == am, -jnp.inf, cur)
    es = [jnp.exp(vv - vals[0]) for vv in vals]
    tot = es[0] + es[1] + es[2] + es[3]

    @pl.when(pl.program_id(0) == 0)
    def _():
        cnt_scr[...] = jnp.zeros(cnt_scr.shape, F32)

    tm = logits.shape[0]
    sel = [lane_e == idxs[kk] for kk in range(TOP_K)]
    onehot = jnp.where(jnp.logical_or(jnp.logical_or(sel[0], sel[1]), jnp.logical_or(sel[2], sel[3])), 1.0, 0.0)
    rr = lax.broadcasted_iota(jnp.int32, (tm, tm), 0)
    cc = lax.broadcasted_iota(jnp.int32, (tm, tm), 1)
    before = jnp.dot((cc < rr).astype(BF16), onehot.astype(BF16), preferred_element_type=F32) + cnt_scr[...]
    ranks = [jnp.sum(jnp.where(sel[kk], before, 0.0), axis=1, keepdims=True) for kk in range(TOP_K)]
    cnt_scr[...] = cnt_scr[...] + jnp.sum(onehot, axis=0, keepdims=True)
    cnt_o[...] = jnp.broadcast_to(cnt_scr[...], cnt_o.shape).astype(jnp.int32)

    lane = lax.broadcasted_iota(jnp.int32, ti_o.shape, 1)
    ti = jnp.zeros(ti_o.shape, F32)
    tg = jnp.zeros(tg_o.shape, F32)
    for kk in range(TOP_K):
        ti = jnp.where(lane == kk, idxs[kk], ti)
        ti = jnp.where(lane == TOP_K + kk, ranks[kk], ti)
        tg = jnp.where(lane == kk, es[kk] / tot, tg)
    ti_o[...] = ti.astype(jnp.int32)
    tg_o[...] = tg


def _mix_out(x_p, x_s, ys, bonus, g, yf_p, yf_s, ln, wo_bf, gf, rw, rb):
    n_p, n_s = x_p.shape[0], x_s.shape[0]
    n = n_p + n_s
    tm = _row_tile(n_p, n_s, 256)
    np_tiles = n_p // tm
    xp_spec, xs_spec = _two_group_specs(tm, D_MODEL, np_tiles)
    yfp_spec = pl.BlockSpec((N_PAIRS, tm, LANES), lambda i: (0, jnp.minimum(i, np_tiles - 1), 0))
    yfs_spec = pl.BlockSpec((N_PAIRS, tm, LANES), lambda i: (0, jnp.maximum(i - np_tiles, 0), 0))
    row = lambda w: pl.BlockSpec((tm, w), lambda i: (i, 0))
    pspec = pl.BlockSpec((N_PAIRS, tm, LANES), lambda i: (0, i, 0))
    full = lambda a: pl.BlockSpec(a.shape, lambda i: (0,) * a.ndim)
    return pl.pallas_call(
        functools.partial(_mix_out_kernel, np_tiles=n_p // tm),
        grid=(n // tm,),
        in_specs=[xp_spec, xs_spec, pspec, row(MIX_W), row(MIX_W), yfp_spec, yfs_spec, full(ln), full(wo_bf), full(gf),
                  full(rw), full(rb)],
        out_specs=[row(D_MODEL), pl.BlockSpec((tm * ROW_CHUNKS, LANES), lambda i: (i, 0)), row(LANES), row(LANES),
                   pl.BlockSpec((8, N_EXPERTS), lambda i: (0, 0))],
        out_shape=[jax.ShapeDtypeStruct((n, D_MODEL), F32), jax.ShapeDtypeStruct((n * ROW_CHUNKS, LANES), jnp.uint32),
                   jax.ShapeDtypeStruct((n, LANES), jnp.int32), jax.ShapeDtypeStruct((n, LANES), F32),
                   jax.ShapeDtypeStruct((8, N_EXPERTS), jnp.int32)],
        scratch_shapes=[pltpu.VMEM((1, N_EXPERTS), F32)],
        compiler_params=_cparams(1),
        name="mix_out_router",
    )(x_p, x_s, ys, bonus, g, yf_p, yf_s, ln, wo_bf, gf, rw, rb)


def _moe_gather_kernel(tok_ref, hf_hbm, o_ref, buf, sem):
    i = pl.program_id(0)
    nb = pl.num_programs(0)
    rows = o_ref.shape[0]

    def row_copy(blk, slot, r):
        t = tok_ref[blk * rows + r]
        return pltpu.make_async_copy(hf_hbm.at[pl.ds(pl.multiple_of(t * ROW_CHUNKS, ROW_CHUNKS), ROW_CHUNKS)],
                                     buf.at[slot, pl.ds(pl.multiple_of(r * ROW_CHUNKS, ROW_CHUNKS), ROW_CHUNKS)],
                                     sem.at[slot])

    def issue(blk, slot):
        def body(r, carry):
            row_copy(blk, slot, r).start()
            return carry
        lax.fori_loop(0, rows, body, 0, unroll=8)

    @pl.when(i == 0)
    def _():
        issue(0, 0)

    @pl.when(i + 1 < nb)
    def _():
        issue(i + 1, (i + 1) % 2)

    slot = i % 2

    pltpu.make_async_copy(hf_hbm.at[pl.ds(0, rows * ROW_CHUNKS)], buf.at[slot], sem.at[slot]).wait()
    half = D_MODEL // 2
    for jc in range(ROW_CHUNKS):
        words = buf[slot, pl.ds(jc, rows, stride=ROW_CHUNKS), :]
        lo = pltpu.bitcast(words << 16, F32)
        hi = pltpu.bitcast(words & jnp.uint32(0xFFFF0000), F32)
        o_ref[:, jc * LANES:(jc + 1) * LANES] = lo.astype(BF16)
        o_ref[:, half + jc * LANES:half + (jc + 1) * LANES] = hi.astype(BF16)


def _moe_gather(row_tok, hf, n_blocks):
    grid_spec = pltpu.PrefetchScalarGridSpec(
        num_scalar_prefetch=1,
        grid=(n_blocks,),
        in_specs=[pl.BlockSpec(memory_space=pl.ANY)],
        out_specs=pl.BlockSpec((MOE_ROWS, D_MODEL), lambda i, tok: (i, 0)),
        scratch_shapes=[pltpu.VMEM((2, MOE_ROWS * ROW_CHUNKS, LANES), jnp.uint32), pltpu.SemaphoreType.DMA((2,))],
    )
    return pl.pallas_call(
        _moe_gather_kernel,
        grid_spec=grid_spec,
        out_shape=jax.ShapeDtypeStruct((n_blocks * MOE_ROWS, D_MODEL), BF16),
        compiler_params=_cparams(1),
        name="moe_gather",
    )(row_tok, hf)


STEP_RUN, STEP_NEW_WEIGHTS, STEP_ZERO = 0, 1, 2


def _moe_gu_kernel(se, sw, sb, sj, sf, x_ref, wg_ref, wu_ref, bg_ref, bu_ref, o_ref, wg_bf, wu_bf):
    s = pl.program_id(0)

    @pl.when(sf[s] == STEP_NEW_WEIGHTS)
    def _():
        wg_bf[...] = wg_ref[0].astype(BF16)
        wu_bf[...] = wu_ref[0].astype(BF16)

    @pl.when(sf[s] != STEP_ZERO)
    def _():
        x = x_ref[...]
        g = jnp.dot(x, wg_bf[...], preferred_element_type=F32) + bg_ref[0]
        u = jnp.dot(x, wu_bf[...], preferred_element_type=F32) + bu_ref[0]
        g = jnp.minimum(g, SWIGLU_LIMIT)
        u = jnp.clip(u, -SWIGLU_LIMIT, SWIGLU_LIMIT)
        o_ref[...] = ((u + 1.0) * (g * jax.nn.sigmoid(SWIGLU_ALPHA * g))).astype(BF16)

    @pl.when(sf[s] == STEP_ZERO)
    def _():
        o_ref[...] = jnp.zeros(o_ref.shape, o_ref.dtype)


def _moe_gate_up(sched, xs, w_gu, b_gu, n_blocks):
    nt = D_EXPERT // MOE_TN
    n_steps = nt * n_blocks
    wspec = lambda off: pl.BlockSpec((1, D_MODEL, MOE_TN), lambda s, se, sw, sb, sj, sf: (se[s], 0, off + sw[s]))
    bspec = lambda off: pl.BlockSpec((1, 1, MOE_TN), lambda s, se, sw, sb, sj, sf: (se[s], 0, off + sw[s]))
    grid_spec = pltpu.PrefetchScalarGridSpec(
        num_scalar_prefetch=5,
        grid=(n_steps,),
        in_specs=[pl.BlockSpec((MOE_ROWS, D_MODEL), lambda s, se, sw, sb, sj, sf: (sb[s], 0)),
                  wspec(0), wspec(nt), bspec(0), bspec(nt)],
        out_specs=pl.BlockSpec((MOE_ROWS, MOE_TN), lambda s, se, sw, sb, sj, sf: (sb[s], sj[s])),
        scratch_shapes=[pltpu.VMEM((D_MODEL, MOE_TN), BF16), pltpu.VMEM((D_MODEL, MOE_TN), BF16)],
    )
    return pl.pallas_call(
        _moe_gu_kernel,
        grid_spec=grid_spec,
        out_shape=jax.ShapeDtypeStruct((n_blocks * MOE_ROWS, D_EXPERT), BF16),
        compiler_params=_cparams(1),
        name="moe_gate_up",
    )(*sched, xs, w_gu, w_gu, b_gu, b_gu)


def _moe_dn_kernel(se, sw, sb, sj, sf, h_ref, wd_ref, bd_ref, o_ref, wd_bf):
    s = pl.program_id(0)

    @pl.when(sf[s] == STEP_NEW_WEIGHTS)
    def _():
        wd_bf[...] = wd_ref[0].astype(BF16)

    @pl.when(sf[s] != STEP_ZERO)
    def _():
        o_ref[...] = jnp.dot(h_ref[...], wd_bf[...], preferred_element_type=F32) + bd_ref[0]

    @pl.when(sf[s] == STEP_ZERO)
    def _():
        o_ref[...] = jnp.zeros(o_ref.shape, o_ref.dtype)


def _moe_down(sched, hid, w_dn, b_dn, n_blocks):
    nt = D_MODEL // MOE_TN
    n_steps = nt * n_blocks
    grid_spec = pltpu.PrefetchScalarGridSpec(
        num_scalar_prefetch=5,
        grid=(n_steps,),
        in_specs=[pl.BlockSpec((MOE_ROWS, D_EXPERT), lambda s, se, sw, sb, sj, sf: (sb[s], 0)),
                  pl.BlockSpec((1, D_EXPERT, MOE_TN), lambda s, se, sw, sb, sj, sf: (se[s], 0, sw[s])),
                  pl.BlockSpec((1, 1, MOE_TN), lambda s, se, sw, sb, sj, sf: (se[s], 0, sw[s]))],
        out_specs=pl.BlockSpec((MOE_ROWS, MOE_TN), lambda s, se, sw, sb, sj, sf: (sb[s], sj[s])),
        scratch_shapes=[pltpu.VMEM((D_EXPERT, MOE_TN), BF16)],
    )
    return pl.pallas_call(
        _moe_dn_kernel,
        grid_spec=grid_spec,
        out_shape=jax.ShapeDtypeStruct((n_blocks * MOE_ROWS, D_MODEL), F32),
        compiler_params=_cparams(1),
        name="moe_down",
    )(*sched, hid, w_dn, b_dn)


def _moe_combine_kernel(pos_ref, ys_hbm, x1_ref, tg_ref, o_ref, buf, sem):
    i = pl.program_id(0)
    nb = pl.num_programs(0)
    tm = o_ref.shape[0]

    def row_copy(blk, slot, r, kk):
        src = pos_ref[(blk * tm + r) * TOP_K + kk]
        return pltpu.make_async_copy(ys_hbm.at[pl.ds(src, 1)], buf.at[slot, kk, pl.ds(r, 1)], sem.at[slot])

    def issue(blk, slot):
        def body(r, carry):
            for kk in range(TOP_K):
                row_copy(blk, slot, r, kk).start()
            return carry
        lax.fori_loop(0, tm, body, 0, unroll=4)

    @pl.when(i == 0)
    def _():
        issue(0, 0)

    @pl.when(i + 1 < nb)
    def _():
        issue(i + 1, (i + 1) % 2)

    slot = i % 2

    for kk in range(TOP_K):
        pltpu.make_async_copy(ys_hbm.at[pl.ds(0, tm)], buf.at[slot, kk], sem.at[slot]).wait()

    tg = tg_ref[...]
    acc = x1_ref[...]
    for kk in range(TOP_K):
        acc = acc + tg[:, kk:kk + 1] * buf[slot, kk]
    o_ref[...] = acc


def _moe_combine(pos, ys, x1, tg):
    n = x1.shape[0]
    tm = min(128, n)
    grid_spec = pltpu.PrefetchScalarGridSpec(
        num_scalar_prefetch=1,
        grid=(n // tm,),
        in_specs=[pl.BlockSpec(memory_space=pl.ANY),
                  pl.BlockSpec((tm, D_MODEL), lambda i, pos: (i, 0)),
                  pl.BlockSpec((tm, LANES), lambda i, pos: (i, 0))],
        out_specs=pl.BlockSpec((tm, D_MODEL), lambda i, pos: (i, 0)),
        scratch_shapes=[pltpu.VMEM((2, TOP_K, tm, D_MODEL), F32), pltpu.SemaphoreType.DMA((2,))],
    )
    return pl.pallas_call(
        _moe_combine_kernel,
        grid_spec=grid_spec,
        out_shape=jax.ShapeDtypeStruct((n, D_MODEL), F32),
        compiler_params=_cparams(1),
        name="moe_combine",
    )(pos, ys, x1, tg)


def _moe_schedule(top_idx, rank, counts, n_blocks):
    n = top_idx.shape[0]
    n_rows = n * TOP_K
    nt = D_EXPERT // MOE_TN
    flat_e = top_idx.reshape(n_rows)
    nb_e = (counts + MOE_ROWS - 1) // MOE_ROWS
    blk_end = jnp.cumsum(nb_e)
    blk_start = blk_end - nb_e
    pos = (blk_start[flat_e] * MOE_ROWS + rank.reshape(n_rows)).astype(jnp.int32)
    total = n_blocks * MOE_ROWS
    row_tok = (jnp.arange(total, dtype=jnp.int32) % n).at[pos].set(jnp.arange(n_rows, dtype=jnp.int32) // TOP_K)
    used = blk_end[-1]
    s = jnp.arange(nt * n_blocks, dtype=jnp.int32)
    live = s < nt * used
    s_eff = jnp.maximum(jnp.minimum(s, nt * used - 1), 0)
    e = jnp.minimum(jnp.sum((s_eff[:, None] >= nt * blk_end[None, :]).astype(jnp.int32), axis=1), N_EXPERTS - 1)
    local = s_eff - nt * blk_start[e]
    nbe = jnp.maximum(nb_e[e], 1)
    sw = (local // nbe).astype(jnp.int32)
    tail = s - nt * used
    sb = jnp.where(live, blk_start[e] + local % nbe, used + tail // nt).astype(jnp.int32)
    sj = jnp.where(live, sw, tail % nt).astype(jnp.int32)
    sf = jnp.where(live, jnp.where(local % nbe == 0, STEP_NEW_WEIGHTS, STEP_RUN), STEP_ZERO).astype(jnp.int32)
    return pos, row_tok, (e, sw, sb, sj, sf)


def _ple_kernel(x_ref, pp_ref, ps_ref, g_ref, wg_ref, wp_ref, op_ref, os_ref, *, np_tiles):
    i = pl.program_id(0)
    x = x_ref[...]
    h = _rms(x, g_ref[...]).astype(BF16)
    gate = jax.nn.sigmoid(jnp.dot(h, wg_ref[...], preferred_element_type=F32))
    p = jnp.where(i < np_tiles, pp_ref[...], ps_ref[...])
    y = x + gate * jnp.dot(p.astype(BF16), wp_ref[...], preferred_element_type=F32)

    @pl.when(i < np_tiles)
    def _():
        op_ref[...] = y

    @pl.when(i >= np_tiles)
    def _():
        os_ref[...] = y


def _ple(x, p_p, p_s, g, wg_bf, wp_bf):
    n_p, n_s = p_p.shape[0], p_s.shape[0]
    n = n_p + n_s
    tm = _row_tile(n_p, n_s, 256)
    np_tiles = n_p // tm
    full = lambda a: pl.BlockSpec(a.shape, lambda i: (0,) * a.ndim)
    pp_spec, ps_spec = _two_group_specs(tm, PLE_DIM, np_tiles)
    op_spec, os_spec = _two_group_specs(tm, D_MODEL, np_tiles)
    return pl.pallas_call(
        functools.partial(_ple_kernel, np_tiles=np_tiles),
        grid=(n // tm,),
        in_specs=[pl.BlockSpec((tm, D_MODEL), lambda i: (i, 0)), pp_spec, ps_spec,
                  full(g), full(wg_bf), full(wp_bf)],
        out_specs=[op_spec, os_spec],
        out_shape=[jax.ShapeDtypeStruct((n_p, D_MODEL), F32), jax.ShapeDtypeStruct((n_s, D_MODEL), F32)],
        compiler_params=_cparams(1),
        name="ple_gate",
    )(x, p_p, p_s, g, wg_bf, wp_bf)


def _pairs_from_state(s):
    b = s.shape[0]
    st = jnp.swapaxes(s, 2, 3).reshape(b, N_PAIRS, 2, HEAD_DIM, HEAD_DIM)
    z = jnp.zeros((b, N_PAIRS, HEAD_DIM, HEAD_DIM), s.dtype)
    top = jnp.concatenate([st[:, :, 0], z], axis=3)
    bot = jnp.concatenate([z, st[:, :, 1]], axis=3)
    return jnp.concatenate([top, bot], axis=2)


def _state_from_pairs(sp):
    b = sp.shape[0]
    st = jnp.stack([sp[:, :, :HEAD_DIM, :HEAD_DIM], sp[:, :, HEAD_DIM:, HEAD_DIM:]], axis=2)
    return jnp.swapaxes(st.reshape(b, N_HEADS, HEAD_DIM, HEAD_DIM), 2, 3)


def _layer(x_p, x_s, cache_k, cache_v, cache_lf, state, shift, p_p, p_s, lw):
    t_p = x_p.shape[1]
    b_s, t_s, _ = x_s.shape
    n_p = x_p.shape[0] * t_p
    n_s = b_s * t_s
    n = n_p + n_s
    past = cache_k.shape[1]
    xp2 = x_p.reshape(n_p, D_MODEL)
    xs2 = x_s.reshape(n_s, D_MODEL)

    w_in_bf = jnp.pad(lw["w_in"], ((0, 0), (0, Z_PAD - lw["w_in"].shape[1]))).astype(BF16)
    z = _in_proj(xp2, xs2, lw["norm_mix_g"], w_in_bf)

    assert t_s == CHUNK, "each sample stream contributes exactly one scan chunk"
    vecs = jnp.zeros((8, MIX_W), F32)
    vecs = vecs.at[0].set(lw["rwkv_w0"]).at[1].set(lw["rwkv_a0"]).at[2].set(lw["rwkv_kk"])
    vecs = vecs.at[3].set(lw["rwkv_ka"]).at[4].set(lw["rwkv_rk"].reshape(MIX_W))
    wl = jnp.zeros((256, 3 * MIX_W), F32)
    wl = wl.at[0:64, 0:MIX_W].set(lw["rwkv_w2"]).at[64:128, MIX_W:2 * MIX_W].set(lw["rwkv_a2"])
    wl = wl.at[128:256, 2 * MIX_W:].set(lw["rwkv_g2"])
    r, w, k, kk, bb, v_pm, g, bonus = _rwkv_pre(z, shift, lw["rwkv_mu"].reshape(1, RWKV_PROJ), vecs, wl, n_p)
    s0 = jnp.concatenate([jnp.zeros((1, N_PAIRS, LANES, LANES), F32), _pairs_from_state(state.astype(F32))], axis=0)
    y_scan, s_out = _rwkv_scan(r, w, k, kk, bb, v_pm, s0, n_p)
    s_new = _state_from_pairs(s_out)
    shift_new_p = z[n_p - 1:n_p, :RWKV_PROJ].reshape(1, 1, RWKV_PROJ)
    shift_new_s = z[n_p:, :RWKV_PROJ].reshape(b_s, t_s, RWKV_PROJ)[:, -1:, :]

    fvecs = jnp.zeros((8, MIX_W), F32)
    fvecs = fvecs.at[0].set(jnp.tile(lw["fox_q_g"], N_HEADS)).at[1].set(jnp.tile(lw["fox_k_g"], N_HEADS))
    bf = jnp.zeros((1, LANES), F32).at[0, :N_HEADS].set(lw["fox_b_f"])
    q_pm, k_pm, vv_pm, k_new_p, v_new_p, k_new_s, v_new_s, lf = _fox_pre(z, fvecs, bf, n_p)
    lf = lf[:, :N_HEADS]
    lf_p = lf[:n_p].T.reshape(1, N_HEADS, n_p)
    c_p = _cumsum_lanes(lf_p).reshape(N_PAIRS, 2, n_p)
    yf_p = _attn_prompt(q_pm, k_pm, vv_pm, c_p, n_p)
    lf_s = jnp.swapaxes(lf[n_p:].reshape(b_s, t_s, N_HEADS), 1, 2)
    lf_all = jnp.concatenate([jnp.swapaxes(cache_lf.astype(F32), 1, 2), lf_s], axis=2)
    pad = (-lf_all.shape[2]) % LANES
    c_all = _cumsum_lanes(jnp.pad(lf_all, ((0, 0), (0, 0), (0, pad))))
    yf_s = _attn_sample(q_pm, k_pm, vv_pm, cache_k.reshape(b_s, past, MIX_W), cache_v.reshape(b_s, past, MIX_W),
                        c_all[:, :, :past], c_all[:, :, past:past + t_s], n_p)

    ln = jnp.stack([lw["rwkv_ln_g"], lw["rwkv_ln_b"]])
    x1, hf, ti, tg, cnt = _mix_out(xp2, xs2, y_scan, bonus, g, yf_p, yf_s, ln, lw["w_out"].astype(BF16),
                                   lw["norm_ffn_g"].reshape(1, D_MODEL), lw["router_w"],
                                   lw["router_b"].reshape(1, N_EXPERTS))

    n_blocks = n * TOP_K // MOE_ROWS + N_EXPERTS
    pos, row_tok, sched = _moe_schedule(ti[:, :TOP_K], ti[:, TOP_K:2 * TOP_K], cnt[0], n_blocks)
    xs = _moe_gather(row_tok, hf, n_blocks)
    hid = _moe_gate_up(sched, xs, lw["expert_w_gu"], lw["expert_b_gu"].reshape(N_EXPERTS, 1, 2 * D_EXPERT), n_blocks)
    ys = _moe_down(sched, hid, lw["expert_w_down"], lw["expert_b_down"].reshape(N_EXPERTS, 1, D_MODEL), n_blocks)
    x2 = _moe_combine(pos, ys, x1, tg)

    y_p, y_s = _ple(x2, p_p, p_s, lw["ple_norm_g"].reshape(1, D_MODEL), lw["ple_w_gate"].astype(BF16),
                    lw["ple_w_proj"].astype(BF16))

    heads = lambda a, bsz, t: a.reshape(bsz, t, N_HEADS, HEAD_DIM)
    out_p = (y_p.reshape(x_p.shape), heads(k_new_p, 1, n_p), heads(v_new_p, 1, n_p),
             lf[:n_p].reshape(1, n_p, N_HEADS), s_new[:1], shift_new_p)
    out_s = (y_s.reshape(x_s.shape), heads(k_new_s, b_s, t_s), heads(v_new_s, b_s, t_s),
             lf[n_p:].reshape(b_s, t_s, N_HEADS), s_new[1:], shift_new_s)
    return out_p, out_s


def kernel(x_prompt, x_sample, cache_fox_k, cache_fox_v, cache_fox_logf, state_rwkv, state_rwkv_shift, p_prompt, p_sample, norm_mix_g, w_in, rwkv_mu, rwkv_w0, rwkv_w2, rwkv_a0, rwkv_a2, rwkv_g2, rwkv_kk, rwkv_ka, rwkv_rk, rwkv_ln_g, rwkv_ln_b, fox_q_g, fox_k_g, fox_b_f, w_out, norm_ffn_g, router_w, router_b, expert_w_gu, expert_b_gu, expert_w_down, expert_b_down, ple_norm_g, ple_w_gate, ple_w_proj):
    assert x_prompt.shape[0] == 1 and w_in.shape[0] == 1, "one prompt stream, one layer"
    lw = dict(norm_mix_g=norm_mix_g[0], w_in=w_in[0], rwkv_mu=rwkv_mu[0], rwkv_w0=rwkv_w0[0], rwkv_w2=rwkv_w2[0],
              rwkv_a0=rwkv_a0[0], rwkv_a2=rwkv_a2[0], rwkv_g2=rwkv_g2[0], rwkv_kk=rwkv_kk[0], rwkv_ka=rwkv_ka[0],
              rwkv_rk=rwkv_rk[0], rwkv_ln_g=rwkv_ln_g[0], rwkv_ln_b=rwkv_ln_b[0], fox_q_g=fox_q_g[0],
              fox_k_g=fox_k_g[0], fox_b_f=fox_b_f[0], w_out=w_out[0], norm_ffn_g=norm_ffn_g[0],
              router_w=router_w[0], router_b=router_b[0], expert_w_gu=expert_w_gu[0], expert_b_gu=expert_b_gu[0],
              expert_w_down=expert_w_down[0], expert_b_down=expert_b_down[0], ple_norm_g=ple_norm_g[0],
              ple_w_gate=ple_w_gate[0], ple_w_proj=ple_w_proj[0])
    n_p = x_prompt.shape[1]
    (y_p, k_p, v_p, lf_p, s_p, sh_p), (y_s, k_s, v_s, lf_s, s_s, sh_s) = _layer(
        x_prompt, x_sample, cache_fox_k[0], cache_fox_v[0], cache_fox_logf[0], state_rwkv[0],
        state_rwkv_shift[0], p_prompt[0].reshape(n_p, PLE_DIM), p_sample[0].reshape(-1, PLE_DIM), lw)
    add = lambda a: a[None]
    return (y_p, y_s, add(k_p), add(v_p), add(lf_p), add(s_p), add(sh_p),
            add(k_s), add(v_s), add(lf_s), add(s_s), add(sh_s))
```

```python
import functools

import numpy as np
import jax
import jax.numpy as jnp
from jax import lax
from jax.experimental import pallas as pl
from jax.experimental.pallas import tpu as pltpu

F32 = jnp.float32
BF16 = jnp.bfloat16
HI = lax.Precision.HIGHEST

D_MODEL = 2048
HEAD_DIM = 64
N_HEADS = 16
N_PAIRS = N_HEADS // 2
MIX_W = N_HEADS * HEAD_DIM
CHUNK = 64
RWKV_PROJ = 3 * MIX_W + 64 + 64 + 128
FOX_PROJ = 3 * MIX_W + N_HEADS
Z_HALF = RWKV_PROJ
Z_PAD = 2 * Z_HALF
N_EXPERTS = 32
TOP_K = 4
D_EXPERT = 2048
SWIGLU_LIMIT = 7.0
SWIGLU_ALPHA = 1.702
PLE_DIM = 256
RMS_EPS = 1e-6
GN_EPS = 64e-5
L2_EPS = 1e-12
NEG_BIG = -1e30
LOG2E = 1.4426950408889634

LANES = 128
MXU_N = 256
MOE_ROWS = 512
MOE_TN = 1024
ROW_CHUNKS = D_MODEL // 2 // LANES
VMEM_LIMIT = 52 * 1024 * 1024


def _cparams(n_axes, vmem=VMEM_LIMIT):
    return pltpu.CompilerParams(dimension_semantics=("arbitrary",) * n_axes, vmem_limit_bytes=vmem)


def _head_sum(x):
    r = lax.broadcasted_iota(jnp.int32, (LANES, LANES), 0) // HEAD_DIM
    c = lax.broadcasted_iota(jnp.int32, (LANES, LANES), 1) // HEAD_DIM
    bd = jnp.where(r == c, 1.0, 0.0).astype(BF16)
    hi = x.astype(BF16)
    lo = (x - hi.astype(F32)).astype(BF16)
    parts = []
    for i in range(x.shape[1] // LANES):
        sl = slice(i * LANES, (i + 1) * LANES)
        parts.append(jnp.dot(hi[:, sl], bd, preferred_element_type=F32)
                     + jnp.dot(lo[:, sl], bd, preferred_element_type=F32))
    return parts[0] if len(parts) == 1 else jnp.concatenate(parts, axis=1)


def _log_sigmoid(x):
    return jnp.minimum(x, 0.0) - jnp.log1p(jnp.exp(-jnp.abs(x)))


def _rms(x, g):
    ms = jnp.mean(x * x, axis=-1, keepdims=True)
    return x * lax.rsqrt(ms + RMS_EPS) * g


def _row_tile(n_p, n_s, pref):
    return pref if (n_p % pref == 0 and n_s % pref == 0) else 128


def _two_group_specs(tm, width, np_tiles, n_grid_axes=1):
    if n_grid_axes == 1:
        return (pl.BlockSpec((tm, width), lambda i: (jnp.minimum(i, np_tiles - 1), 0)),
                pl.BlockSpec((tm, width), lambda i: (jnp.maximum(i - np_tiles, 0), 0)))
    return (pl.BlockSpec((tm, width), lambda i, j: (jnp.minimum(i, np_tiles - 1), 0)),
            pl.BlockSpec((tm, width), lambda i, j: (jnp.maximum(i - np_tiles, 0), 0)))


def _inproj_kernel(xp_ref, xs_ref, g_ref, w_ref, o_ref, h_scr, *, np_tiles):
    i = pl.program_id(0)
    first = pl.program_id(1) == 0

    @pl.when(jnp.logical_and(first, i < np_tiles))
    def _():
        h_scr[...] = _rms(xp_ref[...], g_ref[...]).astype(BF16)

    @pl.when(jnp.logical_and(first, i >= np_tiles))
    def _():
        h_scr[...] = _rms(xs_ref[...], g_ref[...]).astype(BF16)

    o_ref[...] = jnp.dot(h_scr[...], w_ref[...], preferred_element_type=F32)


def _in_proj(x_p, x_s, g, w_bf):
    n_p, n_s = x_p.shape[0], x_s.shape[0]
    n = n_p + n_s
    tm = _row_tile(n_p, n_s, 512)
    tn = Z_PAD // 4
    xp_spec, xs_spec = _two_group_specs(tm, D_MODEL, n_p // tm, 2)
    return pl.pallas_call(
        functools.partial(_inproj_kernel, np_tiles=n_p // tm),
        grid=(n // tm, Z_PAD // tn),
        in_specs=[xp_spec, xs_spec,
                  pl.BlockSpec((1, D_MODEL), lambda i, j: (0, 0)),
                  pl.BlockSpec((D_MODEL, tn), lambda i, j: (0, j))],
        out_specs=pl.BlockSpec((tm, tn), lambda i, j: (i, j)),
        out_shape=jax.ShapeDtypeStruct((n, Z_PAD), F32),
        scratch_shapes=[pltpu.VMEM((tm, D_MODEL), BF16)],
        compiler_params=_cparams(2),
        name="in_proj",
    )(x_p, x_s, g.reshape(1, D_MODEL), w_bf)


def _rwkv_pre_kernel(z_ref, prev_ref, mu_ref, vec_ref, wl_ref,
                     r_o, w_o, k_o, kk_o, b_o, v_o, g_o, bn_o, carry, *, n_prompt_tiles):
    i = pl.program_id(0)
    z = z_ref[...]
    tm = z.shape[0]

    @pl.when(i == 0)
    def _():
        carry[...] = prev_ref[0, 0:1, :]

    rolled = pltpu.roll(z, 1, axis=0)
    row = lax.broadcasted_iota(jnp.int32, z.shape, 0)
    shifted = jnp.where(row == 0, carry[...], rolled)
    is_sample = i >= n_prompt_tiles
    for c in range(tm // CHUNK):
        shifted = jnp.where(jnp.logical_and(is_sample, row == c * CHUNK), prev_ref[0, c:c + 1, :], shifted)
    carry[...] = z[tm - 1:tm, :]
    zs = z + mu_ref[...] * (shifted - z)

    r = zs[:, 0:MIX_W]
    k = zs[:, MIX_W:2 * MIX_W]
    v = zs[:, 2 * MIX_W:3 * MIX_W]
    lo = zs[:, 3 * MIX_W:RWKV_PROJ]
    lane = lax.broadcasted_iota(jnp.int32, lo.shape, 1)
    f = jnp.where(lane < 64, jnp.tanh(lo), jnp.where(lane < 128, lo, jax.nn.sigmoid(lo)))
    lora = _dot3(f, wl_ref[...])
    w_pre = vec_ref[0:1, :] + lora[:, 0:MIX_W]
    log_decay = -jnp.exp(_log_sigmoid(w_pre) - 0.5)
    a = jax.nn.sigmoid(vec_ref[1:2, :] + lora[:, MIX_W:2 * MIX_W])
    g = lora[:, 2 * MIX_W:3 * MIX_W]
    kk = k * vec_ref[2:3, :]
    kk = kk * lax.rsqrt(_head_sum(kk * kk) + L2_EPS)
    k_mod = k * (1.0 + (a - 1.0) * vec_ref[3:4, :])
    bonus = _head_sum(r * k_mod * vec_ref[4:5, :]) * v
    kka = kk * a

    g_o[...] = g
    bn_o[...] = bonus
    for p in range(N_PAIRS):
        sl = slice(p * LANES, (p + 1) * LANES)
        r_o[p] = r[:, sl]
        w_o[p] = log_decay[:, sl]
        k_o[p] = k_mod[:, sl]
        kk_o[p] = kk[:, sl]
        b_o[p] = kka[:, sl]
        v_o[p] = v[:, sl]


def _rwkv_pre(z, shift, mu, vecs, wl, n_prompt):
    n = z.shape[0]
    n_s = n - n_prompt
    tm = _row_tile(n_prompt, n_s, 256)
    if tm % CHUNK:
        tm = CHUNK
    cpt = tm // CHUNK
    nt = n // tm
    n_prompt_tiles = n_prompt // tm
    prev = jnp.concatenate([jnp.zeros((1, cpt, RWKV_PROJ), F32),
                            shift.astype(F32).reshape(n_s // tm, cpt, RWKV_PROJ)], axis=0)
    tok = jax.ShapeDtypeStruct((n, MIX_W), F32)
    tspec = pl.BlockSpec((tm, MIX_W), lambda i: (i, 0))
    pm = jax.ShapeDtypeStruct((N_PAIRS, n, LANES), F32)
    pspec = pl.BlockSpec((N_PAIRS, tm, LANES), lambda i: (0, i, 0))
    return pl.pallas_call(
        functools.partial(_rwkv_pre_kernel, n_prompt_tiles=n_prompt_tiles),
        grid=(nt,),
        in_specs=[pl.BlockSpec((tm, Z_HALF), lambda i: (i, 0)),
                  pl.BlockSpec((1, cpt, RWKV_PROJ), lambda i: (jnp.maximum(i - (n_prompt_tiles - 1), 0), 0, 0)),
                  pl.BlockSpec((1, RWKV_PROJ), lambda i: (0, 0)),
                  pl.BlockSpec((8, MIX_W), lambda i: (0, 0)),
                  pl.BlockSpec((256, 3 * MIX_W), lambda i: (0, 0))],
        out_specs=[pspec, pspec, pspec, pspec, pspec, pspec, tspec, tspec],
        out_shape=[pm, pm, pm, pm, pm, pm, tok, tok],
        scratch_shapes=[pltpu.VMEM((1, RWKV_PROJ), F32)],
        compiler_params=_cparams(1),
        name="rwkv_pre",
    )(z, prev, mu, vecs, wl)


def _dot(a, b):
    return jnp.dot(a, b, precision=HI, preferred_element_type=F32)


def _bdot(a, b):
    return jnp.dot(a, b, preferred_element_type=F32)


def _split_bf16(x):
    hi = x.astype(BF16)
    return hi, (x - hi.astype(F32)).astype(BF16)


def _dot3(a, b):
    a_hi, a_lo = _split_bf16(a)
    b_hi, b_lo = _split_bf16(b)
    return _bdot(a_hi, b_hi) + (_bdot(a_hi, b_lo) + _bdot(a_lo, b_hi))


def _pair_rows(x):
    lo_half = lax.broadcasted_iota(jnp.int32, x.shape, 1) < HEAD_DIM
    return jnp.concatenate([jnp.where(lo_half, x, 0.0), jnp.where(lo_half, 0.0, x)], axis=0)


def _scan_chunk(P, r, lw, k, kk, b, v):
    c2 = 2 * CHUNK
    i = lax.broadcasted_iota(jnp.int32, (c2, c2), 0)
    j = lax.broadcasted_iota(jnp.int32, (c2, c2), 1)
    ti = lax.broadcasted_iota(jnp.int32, (CHUNK, CHUNK), 0)
    tj = lax.broadcasted_iota(jnp.int32, (CHUNK, CHUNK), 1)
    cl = _dot((tj <= ti).astype(F32), lw)
    yield
    g_end = cl[CHUNK - 1:CHUNK, :]
    e_neg = jnp.exp(-cl)
    e_end = jnp.exp(g_end - cl)
    kap = _pair_rows(kk * jnp.exp(cl - lw)).astype(BF16)
    rt = _pair_rows(r * jnp.exp(cl)).astype(BF16)
    bt = _pair_rows(b * e_neg).astype(BF16)
    kt = _pair_rows(k * e_neg).astype(BF16)
    kh = _pair_rows(k * e_end)
    bh = _pair_rows(b * e_end)
    vv = _pair_rows(v)
    vv_b = vv.astype(BF16)
    p_b = P.astype(BF16)

    g = lax.dot_general(jnp.concatenate([kap, rt], axis=0), jnp.concatenate([bt, kt], axis=0),
                        (((1,), (1,)), ((), ())), preferred_element_type=F32)
    yield
    strict = j < i
    incl = j <= i
    a_b = jnp.where(strict, g[:c2, :c2], 0.0)
    a_bb = a_b.astype(BF16)
    a_k = jnp.where(strict, g[:c2, c2:], 0.0).astype(BF16)
    r_b = jnp.where(incl, g[c2:, :c2], 0.0).astype(BF16)
    r_k = jnp.where(incl, g[c2:, c2:], 0.0).astype(BF16)

    t_inv = (i == j).astype(F32) - jnp.where(jnp.logical_and((i & 1) == 1, j == i - 1), a_b, 0.0)
    n = 2
    while n < CHUNK:
        m = jnp.logical_and((i >> n.bit_length()) == (j >> n.bit_length()),
                            jnp.logical_and((i & (2 * n - 1)) >= n, (j & (2 * n - 1)) < n))
        t_b = t_inv.astype(BF16)
        ta = _bdot(t_b, a_bb).astype(BF16)
        yield
        t_inv = t_inv - jnp.where(m, _bdot(ta, t_b), 0.0)
        yield
        n *= 2

    w = _bdot(jnp.concatenate([kap, a_k], axis=1), jnp.concatenate([p_b, vv_b], axis=0))
    yield
    u = _bdot(t_inv.astype(BF16), w.astype(BF16))
    yield
    vu_b = jnp.concatenate([vv_b, u.astype(BF16)], axis=0)
    y2 = _bdot(rt, p_b) + _bdot(jnp.concatenate([r_k, -r_b], axis=1), vu_b)
    y = y2[:CHUNK] + y2[CHUNK:]
    yield
    g_col = jnp.broadcast_to(jnp.exp(g_end), (c2, c2)).T
    p_new = g_col * P + _dot3(jnp.concatenate([kh.T, -bh.T], axis=1), jnp.concatenate([vv, u], axis=0))
    return y, p_new


def _run_interleaved(gens):
    results = [None] * len(gens)
    live = list(range(len(gens)))
    while live:
        for idx in list(live):
            try:
                next(gens[idx])
            except StopIteration as stop:
                results[idx] = stop.value
                live.remove(idx)
    return results


def _scan_kernel(r_ref, w_ref, k_ref, kk_ref, b_ref, v_ref, s0_ref, y_ref, sout_ref, s_scr, *, n_prompt_chunks):
    c = pl.program_id(0)

    @pl.when(jnp.logical_or(c == 0, c >= n_prompt_chunks))
    def _():
        s_scr[...] = s0_ref[0]

    outs = _run_interleaved([
        _scan_chunk(s_scr[p], r_ref[p], w_ref[p], k_ref[p], kk_ref[p], b_ref[p], v_ref[p])
        for p in range(N_PAIRS)])
    for p, (y, p_new) in enumerate(outs):
        y_ref[p] = y
        s_scr[p] = p_new
        sout_ref[0, p] = p_new


def _rwkv_scan(r, w, k, kk, b, v, s0, n_prompt):
    n = r.shape[1]
    npc = n_prompt // CHUNK
    n_seq = s0.shape[0]
    pspec = pl.BlockSpec((N_PAIRS, CHUNK, LANES), lambda c: (0, c, 0))
    sspec = pl.BlockSpec((1, N_PAIRS, LANES, LANES), lambda c: (jnp.maximum(c - (npc - 1), 0), 0, 0, 0))
    return pl.pallas_call(
        functools.partial(_scan_kernel, n_prompt_chunks=npc),
        grid=(n // CHUNK,),
        in_specs=[pspec, pspec, pspec, pspec, pspec, pspec, sspec],
        out_specs=[pspec, sspec],
        out_shape=[jax.ShapeDtypeStruct((N_PAIRS, n, LANES), F32),
                   jax.ShapeDtypeStruct((n_seq, N_PAIRS, LANES, LANES), F32)],
        scratch_shapes=[pltpu.VMEM((N_PAIRS, LANES, LANES), F32)],
        compiler_params=_cparams(1),
        name="rwkv_scan",
    )(r, w, k, kk, b, v, s0)


def _fox_pre_kernel(z_ref, vec_ref, bf_ref, q_o, k_o, v_o, knp_o, vnp_o, kns_o, vns_o, lf_o, *, np_tiles):
    z = z_ref[...]
    q = z[:, 0:MIX_W]
    k = z[:, MIX_W:2 * MIX_W]
    v = z[:, 2 * MIX_W:3 * MIX_W]
    fl = z[:, 3 * MIX_W:3 * MIX_W + LANES]
    inv = 1.0 / HEAD_DIM
    qn = q * lax.rsqrt(_head_sum(q * q) * inv + RMS_EPS) * vec_ref[0:1, :]
    kn = k * lax.rsqrt(_head_sum(k * k) * inv + RMS_EPS) * vec_ref[1:2, :]
    qs = (qn * (HEAD_DIM ** -0.5 * LOG2E)).astype(BF16)
    kb = kn.astype(BF16)
    vb = v.astype(BF16)
    for p in range(N_PAIRS):
        sl = slice(p * LANES, (p + 1) * LANES)
        q_o[p] = qs[:, sl]
        k_o[p] = kb[:, sl]
        v_o[p] = vb[:, sl]
    lf_o[...] = _log_sigmoid(fl + bf_ref[...])

    @pl.when(pl.program_id(0) < np_tiles)
    def _():
        knp_o[...] = kn
        vnp_o[...] = v

    @pl.when(pl.program_id(0) >= np_tiles)
    def _():
        kns_o[...] = kn
        vns_o[...] = v


def _fox_pre(z, vecs, bf, n_p):
    n = z.shape[0]
    n_s = n - n_p
    tm = _row_tile(n_p, n_s, 256)
    np_tiles = n_p // tm
    pm = jax.ShapeDtypeStruct((N_PAIRS, n, LANES), BF16)
    pspec = pl.BlockSpec((N_PAIRS, tm, LANES), lambda i: (0, i, 0))
    p_spec, s_spec = _two_group_specs(tm, MIX_W, np_tiles)
    tok_p = jax.ShapeDtypeStruct((n_p, MIX_W), F32)
    tok_s = jax.ShapeDtypeStruct((n_s, MIX_W), F32)
    return pl.pallas_call(
        functools.partial(_fox_pre_kernel, np_tiles=np_tiles),
        grid=(n // tm,),
        in_specs=[pl.BlockSpec((tm, Z_HALF), lambda i: (i, 1)),
                  pl.BlockSpec((8, MIX_W), lambda i: (0, 0)),
                  pl.BlockSpec((1, LANES), lambda i: (0, 0))],
        out_specs=[pspec, pspec, pspec, p_spec, p_spec, s_spec, s_spec,
                   pl.BlockSpec((tm, LANES), lambda i: (i, 0))],
        out_shape=[pm, pm, pm, tok_p, tok_p, tok_s, tok_s, jax.ShapeDtypeStruct((n, LANES), F32)],
        compiler_params=_cparams(1),
        name="fox_pre",
    )(z, vecs, bf)


def _cumsum_kernel(x_ref, o_ref):
    r = lax.broadcasted_iota(jnp.int32, (LANES, LANES), 0)
    c = lax.broadcasted_iota(jnp.int32, (LANES, LANES), 1)
    tri = (r <= c).astype(F32)
    carry = jnp.zeros((N_HEADS, 1), F32)
    for i in range(x_ref.shape[2] // LANES):
        sl = slice(i * LANES, (i + 1) * LANES)
        cs = jnp.dot(x_ref[0, :, sl], tri, precision=HI, preferred_element_type=F32) + carry
        o_ref[0, :, sl] = cs
        carry = cs[:, LANES - 1:LANES]


def _cumsum_lanes(x):
    b, h, t = x.shape
    return pl.pallas_call(
        _cumsum_kernel,
        grid=(b,),
        in_specs=[pl.BlockSpec((1, h, t), lambda i: (i, 0, 0))],
        out_specs=pl.BlockSpec((1, h, t), lambda i: (i, 0, 0)),
        out_shape=jax.ShapeDtypeStruct((b, h, t), F32),
        compiler_params=_cparams(1),
        name="cumsum_logf",
    )(x)


def _rep_lanes(m, tk):
    if tk % LANES == 0:
        return m if tk == LANES else jnp.concatenate([m] * (tk // LANES), axis=1)
    return m[:, :tk]


DEN_LANE = (HEAD_DIM, 0)


def _attn_update(qa, qb, kb, vb, ck, mask, m_ref, acc_ref):
    tk = kb.shape[0]
    lane = lax.broadcasted_iota(jnp.int32, vb.shape, 1)
    own = (lane < HEAD_DIM, lane >= HEAD_DIM)
    ck2 = ck * LOG2E
    for h, qh in enumerate((qa, qb)):
        s = lax.dot_general(qh, kb, (((1,), (1,)), ((), ())), preferred_element_type=F32)
        s = s - ck2[h:h + 1, :]
        if mask is not None:
            s = jnp.where(mask, s, NEG_BIG)
        m_prev = m_ref[h]
        m_next = jnp.maximum(m_prev, jnp.max(s, axis=1, keepdims=True))
        p = jnp.exp2(s - _rep_lanes(m_next, tk))
        alpha = jnp.exp2(m_prev - m_next)
        m_ref[h] = m_next
        v_aug = jnp.where(own[h], vb.astype(F32), jnp.where(lane == DEN_LANE[h], 1.0, 0.0)).astype(BF16)
        acc_ref[h] = acc_ref[h] * alpha + jnp.dot(p.astype(BF16), v_aug, preferred_element_type=F32)


def _attn_finish(acc_ref):
    lo_half = lax.broadcasted_iota(jnp.int32, acc_ref.shape[1:], 1) < HEAD_DIM
    acc_a = acc_ref[0]
    acc_b = acc_ref[1]
    return jnp.where(lo_half, acc_a / acc_a[:, DEN_LANE[0]:DEN_LANE[0] + 1], acc_b / acc_b[:, DEN_LANE[1]:DEN_LANE[1] + 1])


def _split_q(q):
    qf = q.astype(F32)
    lo_half = lax.broadcasted_iota(jnp.int32, qf.shape, 1) < HEAD_DIM
    return jnp.where(lo_half, qf, 0.0).astype(BF16), jnp.where(lo_half, 0.0, qf).astype(BF16)


def _attn_prompt_kernel(qi_ref, ki_ref, q_ref, k_ref, v_ref, c_ref, o_ref,
                        qa_scr, qb_scr, m_scr, acc_scr, *, tq, tk):
    s_id = pl.program_id(1)
    qi = qi_ref[s_id]
    ki = ki_ref[s_id]

    @pl.when(ki == 0)
    def _():
        qa, qb = _split_q(q_ref[0])
        qa_scr[...] = qa
        qb_scr[...] = qb
        m_scr[...] = jnp.full(m_scr.shape, NEG_BIG, F32)
        acc_scr[...] = jnp.zeros(acc_scr.shape, F32)

    crosses_diagonal = ki * tk + (tk - 1) > qi * tq

    @pl.when(crosses_diagonal)
    def _():
        rows = qi * tq + lax.broadcasted_iota(jnp.int32, (tq, tk), 0)
        cols = ki * tk + lax.broadcasted_iota(jnp.int32, (tq, tk), 1)
        _attn_update(qa_scr[...], qb_scr[...], k_ref[0], v_ref[0], c_ref[0], cols <= rows, m_scr, acc_scr)

    @pl.when(jnp.logical_not(crosses_diagonal))
    def _():
        _attn_update(qa_scr[...], qb_scr[...], k_ref[0], v_ref[0], c_ref[0], None, m_scr, acc_scr)

    @pl.when(ki == ((qi + 1) * tq - 1) // tk)
    def _():
        o_ref[0] = _attn_finish(acc_scr).astype(o_ref.dtype)


def _attn_prompt(q, k, v, c, t):
    tq = min(1024, t)
    tk = min(512, t)
    steps = [(qi, ki) for qi in range(t // tq) for ki in range(((qi + 1) * tq - 1) // tk + 1)]
    qi_arr = jnp.asarray(np.array([s[0] for s in steps], np.int32))
    ki_arr = jnp.asarray(np.array([s[1] for s in steps], np.int32))
    grid_spec = pltpu.PrefetchScalarGridSpec(
        num_scalar_prefetch=2,
        grid=(N_PAIRS, len(steps)),
        in_specs=[pl.BlockSpec((1, tq, LANES), lambda p, s, qi, ki: (p, qi[s], 0)),
                  pl.BlockSpec((1, tk, LANES), lambda p, s, qi, ki: (p, ki[s], 0)),
                  pl.BlockSpec((1, tk, LANES), lambda p, s, qi, ki: (p, ki[s], 0)),
                  pl.BlockSpec((1, 2, tk), lambda p, s, qi, ki: (p, 0, ki[s]))],
        out_specs=pl.BlockSpec((1, tq, LANES), lambda p, s, qi, ki: (p, qi[s], 0)),
        scratch_shapes=[pltpu.VMEM((tq, LANES), BF16), pltpu.VMEM((tq, LANES), BF16),
                        pltpu.VMEM((2, tq, LANES), F32), pltpu.VMEM((2, tq, LANES), F32)],
    )
    return pl.pallas_call(
        functools.partial(_attn_prompt_kernel, tq=tq, tk=tk),
        grid_spec=grid_spec,
        out_shape=jax.ShapeDtypeStruct((N_PAIRS, t, LANES), BF16),
        compiler_params=_cparams(2),
        name="fox_attn_prompt",
    )(qi_arr, ki_arr, q, k, v, c)


def _attn_sample_kernel(q_ref, ck_ref, cv_ref, kn_ref, vn_ref, cc_ref, cn_ref, o_ref,
                        m_scr, acc_scr, *, nkb):
    j = pl.program_id(1)
    ts = q_ref.shape[1]

    @pl.when(j == 0)
    def _():
        m_scr[...] = jnp.full(m_scr.shape, NEG_BIG, F32)
        acc_scr[...] = jnp.zeros(acc_scr.shape, F32)

    @pl.when(j < nkb)
    def _():
        for p in range(N_PAIRS):
            sl = slice(p * LANES, (p + 1) * LANES)
            qa, qb = _split_q(q_ref[p])
            _attn_update(qa, qb, ck_ref[0, :, sl].astype(BF16), cv_ref[0, :, sl].astype(BF16),
                         cc_ref[0, 2 * p:2 * p + 2, :], None, m_scr.at[p], acc_scr.at[p])

    @pl.when(j == nkb)
    def _():
        rows = lax.broadcasted_iota(jnp.int32, (ts, ts), 0)
        cols = lax.broadcasted_iota(jnp.int32, (ts, ts), 1)
        for p in range(N_PAIRS):
            qa, qb = _split_q(q_ref[p])
            _attn_update(qa, qb, kn_ref[p], vn_ref[p], cn_ref[0, 2 * p:2 * p + 2, :], cols <= rows,
                         m_scr.at[p], acc_scr.at[p])
            o_ref[p] = _attn_finish(acc_scr.at[p]).astype(o_ref.dtype)


def _attn_sample(q, k, v, cache_k, cache_v, c_cache, c_new, row0):
    b, past, _ = cache_k.shape
    ts = c_new.shape[2]
    tk = min(1024, past)
    nkb = past // tk
    blk0 = row0 // ts
    qspec = pl.BlockSpec((N_PAIRS, ts, LANES), lambda i, j: (0, blk0 + i, 0))
    cspec = pl.BlockSpec((1, tk, MIX_W), lambda i, j: (i, jnp.minimum(j, nkb - 1), 0))
    return pl.pallas_call(
        functools.partial(_attn_sample_kernel, nkb=nkb),
        grid=(b, nkb + 1),
        in_specs=[qspec, cspec, cspec, qspec, qspec,
                  pl.BlockSpec((1, N_HEADS, tk), lambda i, j: (i, 0, jnp.minimum(j, nkb - 1))),
                  pl.BlockSpec((1, N_HEADS, ts), lambda i, j: (i, 0, 0))],
        out_specs=pl.BlockSpec((N_PAIRS, ts, LANES), lambda i, j: (0, i, 0)),
        out_shape=jax.ShapeDtypeStruct((N_PAIRS, b * ts, LANES), BF16),
        scratch_shapes=[pltpu.VMEM((N_PAIRS, 2, ts, LANES), F32), pltpu.VMEM((N_PAIRS, 2, ts, LANES), F32)],
        compiler_params=_cparams(2),
        name="fox_attn_sample",
    )(q, cache_k, cache_v, k, v, c_cache, c_new)


def _mix_out_kernel(xp_ref, xs_ref, ys_ref, bn_ref, g_ref, yfp_ref, yfs_ref, ln_ref, wo_ref, gf_ref, rw_ref, rb_ref,
                    x1_o, hf_o, ti_o, tg_o, cnt_o, cnt_scr, *, np_tiles):
    y = jnp.concatenate([ys_ref[p] for p in range(N_PAIRS)], axis=1)
    inv = 1.0 / HEAD_DIM
    mu = _head_sum(y) * inv
    d = y - mu
    var = _head_sum(d * d) * inv
    yn = d * lax.rsqrt(var + GN_EPS) * ln_ref[0:1, :] + ln_ref[1:2, :]
    yr = ((yn + bn_ref[...]) * g_ref[...]).astype(BF16)
    in_prompt = pl.program_id(0) < np_tiles
    yf = jnp.concatenate([jnp.where(in_prompt, yfp_ref[p].astype(F32), yfs_ref[p].astype(F32)).astype(BF16)
                          for p in range(N_PAIRS)], axis=1)
    mix = jnp.concatenate([yr, yf], axis=1)
    x_res = jnp.where(pl.program_id(0) < np_tiles, xp_ref[...], xs_ref[...])
    x1 = x_res + jnp.dot(mix, wo_ref[...], preferred_element_type=F32)
    x1_o[...] = x1
    hf = _rms(x1, gf_ref[...])
    bits = pltpu.bitcast(hf, jnp.uint32)
    rne = (bits + jnp.uint32(0x7FFF) + ((bits >> 16) & jnp.uint32(1))) >> 16
    words = rne[:, :D_MODEL // 2] | (rne[:, D_MODEL // 2:] << 16)
    tm_rows = hf.shape[0]
    for jc in range(ROW_CHUNKS):
        hf_o[pl.ds(jc, tm_rows, stride=ROW_CHUNKS), :] = words[:, jc * LANES:(jc + 1) * LANES]

    logits = _dot3(hf, rw_ref[...]) + rb_ref[...]
    lane_e = lax.broadcasted_iota(jnp.int32, logits.shape, 1).astype(F32)
    vals = []
    idxs = []
    cur = logits
    for _ in range(TOP_K):
        m = jnp.max(cur, axis=1, keepdims=True)
        am = jnp.min(jnp.where(cur == m, lane_e, float(N_EXPERTS)), axis=1, keepdims=True)
        vals.append(m)
        idxs.append(am)
        cur = jnp.where(lane_e == am, -jnp.inf, cur)
    es = [jnp.exp(vv - vals[0]) for vv in vals]
    tot = es[0] + es[1] + es[2] + es[3]

    @pl.when(pl.program_id(0) == 0)
    def _():
        cnt_scr[...] = jnp.zeros(cnt_scr.shape, F32)

    tm = logits.shape[0]
    sel = [lane_e == idxs[kk] for kk in range(TOP_K)]
    onehot = jnp.where(jnp.logical_or(jnp.logical_or(sel[0], sel[1]), jnp.logical_or(sel[2], sel[3])), 1.0, 0.0)
    rr = lax.broadcasted_iota(jnp.int32, (tm, tm), 0)
    cc = lax.broadcasted_iota(jnp.int32, (tm, tm), 1)
    before = jnp.dot((cc < rr).astype(BF16), onehot.astype(BF16), preferred_element_type=F32) + cnt_scr[...]
    ranks = [jnp.sum(jnp.where(sel[kk], before, 0.0), axis=1, keepdims=True) for kk in range(TOP_K)]
    cnt_scr[...] = cnt_scr[...] + jnp.sum(onehot, axis=0, keepdims=True)
    cnt_o[...] = jnp.broadcast_to(cnt_scr[...], cnt_o.shape).astype(jnp.int32)

    lane = lax.broadcasted_iota(jnp.int32, ti_o.shape, 1)
    ti = jnp.zeros(ti_o.shape, F32)
    tg = jnp.zeros(tg_o.shape, F32)
    for kk in range(TOP_K):
        ti = jnp.where(lane == kk, idxs[kk], ti)
        ti = jnp.where(lane == TOP_K + kk, ranks[kk], ti)
        tg = jnp.where(lane == kk, es[kk] / tot, tg)
    ti_o[...] = ti.astype(jnp.int32)
    tg_o[...] = tg


def _mix_out(x_p, x_s, ys, bonus, g, yf_p, yf_s, ln, wo_bf, gf, rw, rb):
    n_p, n_s = x_p.shape[0], x_s.shape[0]
    n = n_p + n_s
    tm = _row_tile(n_p, n_s, 256)
    np_tiles = n_p // tm
    xp_spec, xs_spec = _two_group_specs(tm, D_MODEL, np_tiles)
    yfp_spec = pl.BlockSpec((N_PAIRS, tm, LANES), lambda i: (0, jnp.minimum(i, np_tiles - 1), 0))
    yfs_spec = pl.BlockSpec((N_PAIRS, tm, LANES), lambda i: (0, jnp.maximum(i - np_tiles, 0), 0))
    row = lambda w: pl.BlockSpec((tm, w), lambda i: (i, 0))
    pspec = pl.BlockSpec((N_PAIRS, tm, LANES), lambda i: (0, i, 0))
    full = lambda a: pl.BlockSpec(a.shape, lambda i: (0,) * a.ndim)
    return pl.pallas_call(
        functools.partial(_mix_out_kernel, np_tiles=n_p // tm),
        grid=(n // tm,),
        in_specs=[xp_spec, xs_spec, pspec, row(MIX_W), row(MIX_W), yfp_spec, yfs_spec, full(ln), full(wo_bf), full(gf),
                  full(rw), full(rb)],
        out_specs=[row(D_MODEL), pl.BlockSpec((tm * ROW_CHUNKS, LANES), lambda i: (i, 0)), row(LANES), row(LANES),
                   pl.BlockSpec((8, N_EXPERTS), lambda i: (0, 0))],
        out_shape=[jax.ShapeDtypeStruct((n, D_MODEL), F32), jax.ShapeDtypeStruct((n * ROW_CHUNKS, LANES), jnp.uint32),
                   jax.ShapeDtypeStruct((n, LANES), jnp.int32), jax.ShapeDtypeStruct((n, LANES), F32),
                   jax.ShapeDtypeStruct((8, N_EXPERTS), jnp.int32)],
        scratch_shapes=[pltpu.VMEM((1, N_EXPERTS), F32)],
        compiler_params=_cparams(1),
        name="mix_out_router",
    )(x_p, x_s, ys, bonus, g, yf_p, yf_s, ln, wo_bf, gf, rw, rb)


def _moe_gather_kernel(tok_ref, hf_hbm, o_ref, buf, sem):
    i = pl.program_id(0)
    nb = pl.num_programs(0)
    rows = o_ref.shape[0]

    def row_copy(blk, slot, r):
        t = tok_ref[blk * rows + r]
        return pltpu.make_async_copy(hf_hbm.at[pl.ds(pl.multiple_of(t * ROW_CHUNKS, ROW_CHUNKS), ROW_CHUNKS)],
                                     buf.at[slot, pl.ds(pl.multiple_of(r * ROW_CHUNKS, ROW_CHUNKS), ROW_CHUNKS)],
                                     sem.at[slot])

    def issue(blk, slot):
        def body(r, carry):
            row_copy(blk, slot, r).start()
            return carry
        lax.fori_loop(0, rows, body, 0, unroll=8)

    @pl.when(i == 0)
    def _():
        issue(0, 0)

    @pl.when(i + 1 < nb)
    def _():
        issue(i + 1, (i + 1) % 2)

    slot = i % 2

    pltpu.make_async_copy(hf_hbm.at[pl.ds(0, rows * ROW_CHUNKS)], buf.at[slot], sem.at[slot]).wait()
    half = D_MODEL // 2
    for jc in range(ROW_CHUNKS):
        words = buf[slot, pl.ds(jc, rows, stride=ROW_CHUNKS), :]
        lo = pltpu.bitcast(words << 16, F32)
        hi = pltpu.bitcast(words & jnp.uint32(0xFFFF0000), F32)
        o_ref[:, jc * LANES:(jc + 1) * LANES] = lo.astype(BF16)
        o_ref[:, half + jc * LANES:half + (jc + 1) * LANES] = hi.astype(BF16)


def _moe_gather(row_tok, hf, n_blocks):
    grid_spec = pltpu.PrefetchScalarGridSpec(
        num_scalar_prefetch=1,
        grid=(n_blocks,),
        in_specs=[pl.BlockSpec(memory_space=pl.ANY)],
        out_specs=pl.BlockSpec((MOE_ROWS, D_MODEL), lambda i, tok: (i, 0)),
        scratch_shapes=[pltpu.VMEM((2, MOE_ROWS * ROW_CHUNKS, LANES), jnp.uint32), pltpu.SemaphoreType.DMA((2,))],
    )
    return pl.pallas_call(
        _moe_gather_kernel,
        grid_spec=grid_spec,
        out_shape=jax.ShapeDtypeStruct((n_blocks * MOE_ROWS, D_MODEL), BF16),
        compiler_params=_cparams(1),
        name="moe_gather",
    )(row_tok, hf)


STEP_RUN, STEP_NEW_WEIGHTS, STEP_ZERO = 0, 1, 2
N_SCHED = 9


def _weight_group_step(s, sf, sg, ng, copies, cast):
    @pl.when(sf[s] == STEP_NEW_WEIGHTS)
    def _():
        g = sg[s]
        slot = g % 2

        @pl.when(g == 0)
        def _():
            for c in copies(g, slot):
                c.start()

        for c in copies(g, slot):
            c.wait()

        @pl.when(g + 1 < ng[0])
        def _():
            for c in copies(g + 1, 1 - slot):
                c.start()

        cast(slot)


def _moe_gu_kernel(se, sw, sb, sj, sf, sg, ge, gw, ng, x_ref, wgu_hbm, bg_ref, bu_ref, o_ref,
                   wg_bf, wu_bf, wg_stage, wu_stage, sem):
    s = pl.program_id(0)

    def copies(g, slot):
        col = pl.multiple_of(gw[g] * MOE_TN, MOE_TN)
        return [pltpu.make_async_copy(wgu_hbm.at[ge[g], :, pl.ds(col, MOE_TN)], wg_stage.at[slot], sem.at[slot]),
                pltpu.make_async_copy(wgu_hbm.at[ge[g], :, pl.ds(D_EXPERT + col, MOE_TN)], wu_stage.at[slot],
                                      sem.at[slot])]

    def cast(slot):
        wg_bf[...] = wg_stage[slot].astype(BF16)
        wu_bf[...] = wu_stage[slot].astype(BF16)

    _weight_group_step(s, sf, sg, ng, copies, cast)

    @pl.when(sf[s] != STEP_ZERO)
    def _():
        x = x_ref[...]
        for c0 in range(0, MOE_TN, MXU_N):
            cols = slice(c0, c0 + MXU_N)
            g = jnp.dot(x, wg_bf[:, cols], preferred_element_type=F32) + bg_ref[0][:, cols]
            u = jnp.dot(x, wu_bf[:, cols], preferred_element_type=F32) + bu_ref[0][:, cols]
            g = jnp.minimum(g, SWIGLU_LIMIT)
            u = jnp.clip(u, -SWIGLU_LIMIT, SWIGLU_LIMIT)
            o_ref[:, cols] = ((u + 1.0) * (g * jax.nn.sigmoid(SWIGLU_ALPHA * g))).astype(BF16)

    @pl.when(sf[s] == STEP_ZERO)
    def _():
        o_ref[...] = jnp.zeros(o_ref.shape, o_ref.dtype)


def _moe_gate_up(sched, xs, w_gu, b_gu, n_blocks):
    nt = D_EXPERT // MOE_TN
    n_steps = nt * n_blocks
    bspec = lambda off: pl.BlockSpec((1, 1, MOE_TN), lambda s, *pf: (pf[0][s], 0, off + pf[1][s]))
    grid_spec = pltpu.PrefetchScalarGridSpec(
        num_scalar_prefetch=N_SCHED,
        grid=(n_steps,),
        in_specs=[pl.BlockSpec((MOE_ROWS, D_MODEL), lambda s, *pf: (pf[2][s], 0)),
                  pl.BlockSpec(memory_space=pl.ANY), bspec(0), bspec(nt)],
        out_specs=pl.BlockSpec((MOE_ROWS, MOE_TN), lambda s, *pf: (pf[2][s], pf[3][s])),
        scratch_shapes=[pltpu.VMEM((D_MODEL, MOE_TN), BF16), pltpu.VMEM((D_MODEL, MOE_TN), BF16),
                        pltpu.VMEM((2, D_MODEL, MOE_TN), F32), pltpu.VMEM((2, D_MODEL, MOE_TN), F32),
                        pltpu.SemaphoreType.DMA((2,))],
    )
    return pl.pallas_call(
        _moe_gu_kernel,
        grid_spec=grid_spec,
        out_shape=jax.ShapeDtypeStruct((n_blocks * MOE_ROWS, D_EXPERT), BF16),
        compiler_params=_cparams(1),
        name="moe_gate_up",
    )(*sched, xs, w_gu, b_gu, b_gu)


def _moe_dn_kernel(se, sw, sb, sj, sf, sg, ge, gw, ng, h_ref, wd_hbm, bd_ref, o_ref, wd_bf, wd_stage, sem):
    s = pl.program_id(0)

    def copies(g, slot):
        col = pl.multiple_of(gw[g] * MOE_TN, MOE_TN)
        return [pltpu.make_async_copy(wd_hbm.at[ge[g], :, pl.ds(col, MOE_TN)], wd_stage.at[slot], sem.at[slot])]

    def cast(slot):
        wd_bf[...] = wd_stage[slot].astype(BF16)

    _weight_group_step(s, sf, sg, ng, copies, cast)

    @pl.when(sf[s] != STEP_ZERO)
    def _():
        h = h_ref[...]
        for c0 in range(0, MOE_TN, MXU_N):
            cols = slice(c0, c0 + MXU_N)
            o_ref[:, cols] = jnp.dot(h, wd_bf[:, cols], preferred_element_type=F32) + bd_ref[0][:, cols]

    @pl.when(sf[s] == STEP_ZERO)
    def _():
        o_ref[...] = jnp.zeros(o_ref.shape, o_ref.dtype)


def _moe_down(sched, hid, w_dn, b_dn, n_blocks):
    nt = D_MODEL // MOE_TN
    n_steps = nt * n_blocks
    grid_spec = pltpu.PrefetchScalarGridSpec(
        num_scalar_prefetch=N_SCHED,
        grid=(n_steps,),
        in_specs=[pl.BlockSpec((MOE_ROWS, D_EXPERT), lambda s, *pf: (pf[2][s], 0)),
                  pl.BlockSpec(memory_space=pl.ANY),
                  pl.BlockSpec((1, 1, MOE_TN), lambda s, *pf: (pf[0][s], 0, pf[1][s]))],
        out_specs=pl.BlockSpec((MOE_ROWS, MOE_TN), lambda s, *pf: (pf[2][s], pf[3][s])),
        scratch_shapes=[pltpu.VMEM((D_EXPERT, MOE_TN), BF16), pltpu.VMEM((2, D_EXPERT, MOE_TN), F32),
                        pltpu.SemaphoreType.DMA((2,))],
    )
    return pl.pallas_call(
        _moe_dn_kernel,
        grid_spec=grid_spec,
        out_shape=jax.ShapeDtypeStruct((n_blocks * MOE_ROWS, D_MODEL), F32),
        compiler_params=_cparams(1),
        name="moe_down",
    )(*sched, hid, w_dn, b_dn)


def _moe_combine_kernel(pos_ref, ys_hbm, x1_ref, tg_ref, o_ref, buf, sem):
    i = pl.program_id(0)
    nb = pl.num_programs(0)
    tm = o_ref.shape[0]

    def row_copy(blk, slot, r, kk):
        src = pos_ref[(blk * tm + r) * TOP_K + kk]
        return pltpu.make_async_copy(ys_hbm.at[pl.ds(src, 1)], buf.at[slot, kk, pl.ds(r, 1)], sem.at[slot])

    def issue(blk, slot):
        def body(r, carry):
            for kk in range(TOP_K):
                row_copy(blk, slot, r, kk).start()
            return carry
        lax.fori_loop(0, tm, body, 0, unroll=4)

    @pl.when(i == 0)
    def _():
        issue(0, 0)

    @pl.when(i + 1 < nb)
    def _():
        issue(i + 1, (i + 1) % 2)

    slot = i % 2

    for kk in range(TOP_K):
        pltpu.make_async_copy(ys_hbm.at[pl.ds(0, tm)], buf.at[slot, kk], sem.at[slot]).wait()

    tg = tg_ref[...]
    acc = x1_ref[...]
    for kk in range(TOP_K):
        acc = acc + tg[:, kk:kk + 1] * buf[slot, kk]
    o_ref[...] = acc


def _moe_combine(pos, ys, x1, tg):
    n = x1.shape[0]
    tm = min(128, n)
    grid_spec = pltpu.PrefetchScalarGridSpec(
        num_scalar_prefetch=1,
        grid=(n // tm,),
        in_specs=[pl.BlockSpec(memory_space=pl.ANY),
                  pl.BlockSpec((tm, D_MODEL), lambda i, pos: (i, 0)),
                  pl.BlockSpec((tm, LANES), lambda i, pos: (i, 0))],
        out_specs=pl.BlockSpec((tm, D_MODEL), lambda i, pos: (i, 0)),
        scratch_shapes=[pltpu.VMEM((2, TOP_K, tm, D_MODEL), F32), pltpu.SemaphoreType.DMA((2,))],
    )
    return pl.pallas_call(
        _moe_combine_kernel,
        grid_spec=grid_spec,
        out_shape=jax.ShapeDtypeStruct((n, D_MODEL), F32),
        compiler_params=_cparams(1),
        name="moe_combine",
    )(pos, ys, x1, tg)


def _moe_schedule(top_idx, rank, counts, n_blocks):
    n = top_idx.shape[0]
    n_rows = n * TOP_K
    nt = D_EXPERT // MOE_TN
    flat_e = top_idx.reshape(n_rows)
    nb_e = (counts + MOE_ROWS - 1) // MOE_ROWS
    blk_end = jnp.cumsum(nb_e)
    blk_start = blk_end - nb_e
    pos = (blk_start[flat_e] * MOE_ROWS + rank.reshape(n_rows)).astype(jnp.int32)
    total = n_blocks * MOE_ROWS
    row_tok = (jnp.arange(total, dtype=jnp.int32) % n).at[pos].set(jnp.arange(n_rows, dtype=jnp.int32) // TOP_K)
    used = blk_end[-1]
    s = jnp.arange(nt * n_blocks, dtype=jnp.int32)
    live = s < nt * used
    s_eff = jnp.maximum(jnp.minimum(s, nt * used - 1), 0)
    e = jnp.minimum(jnp.sum((s_eff[:, None] >= nt * blk_end[None, :]).astype(jnp.int32), axis=1), N_EXPERTS - 1)
    local = s_eff - nt * blk_start[e]
    nbe = jnp.maximum(nb_e[e], 1)
    sw = (local // nbe).astype(jnp.int32)
    tail = s - nt * used
    sb = jnp.where(live, blk_start[e] + local % nbe, used + tail // nt).astype(jnp.int32)
    sj = jnp.where(live, sw, tail % nt).astype(jnp.int32)
    sf = jnp.where(live, jnp.where(local % nbe == 0, STEP_NEW_WEIGHTS, STEP_RUN), STEP_ZERO).astype(jnp.int32)
    active = nb_e > 0
    rank_e = jnp.cumsum(active.astype(jnp.int32)) - 1
    n_active = rank_e[-1] + 1
    expert_of_rank = jnp.zeros((N_EXPERTS,), jnp.int32).at[jnp.where(active, rank_e, N_EXPERTS)].set(
        jnp.arange(N_EXPERTS, dtype=jnp.int32), mode="drop")
    sg = (rank_e[e] * nt + sw).astype(jnp.int32)
    gidx = jnp.arange(N_EXPERTS * nt, dtype=jnp.int32)
    ge = expert_of_rank[jnp.minimum(gidx // nt, n_active - 1)]
    gw = gidx % nt
    ng = (n_active * nt).astype(jnp.int32).reshape(1)
    return pos, row_tok, (e, sw, sb, sj, sf, sg, ge, gw, ng)


def _ple_kernel(x_ref, pp_ref, ps_ref, g_ref, wg_ref, wp_ref, op_ref, os_ref, *, np_tiles):
    i = pl.program_id(0)
    x = x_ref[...]
    h = _rms(x, g_ref[...]).astype(BF16)
    gate = jax.nn.sigmoid(jnp.dot(h, wg_ref[...], preferred_element_type=F32))
    p = jnp.where(i < np_tiles, pp_ref[...], ps_ref[...])
    y = x + gate * jnp.dot(p.astype(BF16), wp_ref[...], preferred_element_type=F32)

    @pl.when(i < np_tiles)
    def _():
        op_ref[...] = y

    @pl.when(i >= np_tiles)
    def _():
        os_ref[...] = y


def _ple(x, p_p, p_s, g, wg_bf, wp_bf):
    n_p, n_s = p_p.shape[0], p_s.shape[0]
    n = n_p + n_s
    tm = _row_tile(n_p, n_s, 256)
    np_tiles = n_p // tm
    full = lambda a: pl.BlockSpec(a.shape, lambda i: (0,) * a.ndim)
    pp_spec, ps_spec = _two_group_specs(tm, PLE_DIM, np_tiles)
    op_spec, os_spec = _two_group_specs(tm, D_MODEL, np_tiles)
    return pl.pallas_call(
        functools.partial(_ple_kernel, np_tiles=np_tiles),
        grid=(n // tm,),
        in_specs=[pl.BlockSpec((tm, D_MODEL), lambda i: (i, 0)), pp_spec, ps_spec,
                  full(g), full(wg_bf), full(wp_bf)],
        out_specs=[op_spec, os_spec],
        out_shape=[jax.ShapeDtypeStruct((n_p, D_MODEL), F32), jax.ShapeDtypeStruct((n_s, D_MODEL), F32)],
        compiler_params=_cparams(1),
        name="ple_gate",
    )(x, p_p, p_s, g, wg_bf, wp_bf)


def _pairs_from_state(s):
    b = s.shape[0]
    st = jnp.swapaxes(s, 2, 3).reshape(b, N_PAIRS, 2, HEAD_DIM, HEAD_DIM)
    z = jnp.zeros((b, N_PAIRS, HEAD_DIM, HEAD_DIM), s.dtype)
    top = jnp.concatenate([st[:, :, 0], z], axis=3)
    bot = jnp.concatenate([z, st[:, :, 1]], axis=3)
    return jnp.concatenate([top, bot], axis=2)


def _state_from_pairs(sp):
    b = sp.shape[0]
    st = jnp.stack([sp[:, :, :HEAD_DIM, :HEAD_DIM], sp[:, :, HEAD_DIM:, HEAD_DIM:]], axis=2)
    return jnp.swapaxes(st.reshape(b, N_HEADS, HEAD_DIM, HEAD_DIM), 2, 3)


def _layer(x_p, x_s, cache_k, cache_v, cache_lf, state, shift, p_p, p_s, lw):
    t_p = x_p.shape[1]
    b_s, t_s, _ = x_s.shape
    n_p = x_p.shape[0] * t_p
    n_s = b_s * t_s
    n = n_p + n_s
    past = cache_k.shape[1]
    xp2 = x_p.reshape(n_p, D_MODEL)
    xs2 = x_s.reshape(n_s, D_MODEL)

    w_in_bf = jnp.pad(lw["w_in"], ((0, 0), (0, Z_PAD - lw["w_in"].shape[1]))).astype(BF16)
    z = _in_proj(xp2, xs2, lw["norm_mix_g"], w_in_bf)

    assert t_s == CHUNK, "each sample stream contributes exactly one scan chunk"
    vecs = jnp.zeros((8, MIX_W), F32)
    vecs = vecs.at[0].set(lw["rwkv_w0"]).at[1].set(lw["rwkv_a0"]).at[2].set(lw["rwkv_kk"])
    vecs = vecs.at[3].set(lw["rwkv_ka"]).at[4].set(lw["rwkv_rk"].reshape(MIX_W))
    wl = jnp.zeros((256, 3 * MIX_W), F32)
    wl = wl.at[0:64, 0:MIX_W].set(lw["rwkv_w2"]).at[64:128, MIX_W:2 * MIX_W].set(lw["rwkv_a2"])
    wl = wl.at[128:256, 2 * MIX_W:].set(lw["rwkv_g2"])
    r, w, k, kk, bb, v_pm, g, bonus = _rwkv_pre(z, shift, lw["rwkv_mu"].reshape(1, RWKV_PROJ), vecs, wl, n_p)
    s0 = jnp.concatenate([jnp.zeros((1, N_PAIRS, LANES, LANES), F32), _pairs_from_state(state.astype(F32))], axis=0)
    y_scan, s_out = _rwkv_scan(r, w, k, kk, bb, v_pm, s0, n_p)
    s_new = _state_from_pairs(s_out)
    shift_new_p = z[n_p - 1:n_p, :RWKV_PROJ].reshape(1, 1, RWKV_PROJ)
    shift_new_s = z[n_p:, :RWKV_PROJ].reshape(b_s, t_s, RWKV_PROJ)[:, -1:, :]

    fvecs = jnp.zeros((8, MIX_W), F32)
    fvecs = fvecs.at[0].set(jnp.tile(lw["fox_q_g"], N_HEADS)).at[1].set(jnp.tile(lw["fox_k_g"], N_HEADS))
    bf = jnp.zeros((1, LANES), F32).at[0, :N_HEADS].set(lw["fox_b_f"])
    q_pm, k_pm, vv_pm, k_new_p, v_new_p, k_new_s, v_new_s, lf = _fox_pre(z, fvecs, bf, n_p)
    lf = lf[:, :N_HEADS]
    lf_p = lf[:n_p].T.reshape(1, N_HEADS, n_p)
    c_p = _cumsum_lanes(lf_p).reshape(N_PAIRS, 2, n_p)
    yf_p = _attn_prompt(q_pm, k_pm, vv_pm, c_p, n_p)
    lf_s = jnp.swapaxes(lf[n_p:].reshape(b_s, t_s, N_HEADS), 1, 2)
    lf_all = jnp.concatenate([jnp.swapaxes(cache_lf.astype(F32), 1, 2), lf_s], axis=2)
    pad = (-lf_all.shape[2]) % LANES
    c_all = _cumsum_lanes(jnp.pad(lf_all, ((0, 0), (0, 0), (0, pad))))
    yf_s = _attn_sample(q_pm, k_pm, vv_pm, cache_k.reshape(b_s, past, MIX_W), cache_v.reshape(b_s, past, MIX_W),
                        c_all[:, :, :past], c_all[:, :, past:past + t_s], n_p)

    ln = jnp.stack([lw["rwkv_ln_g"], lw["rwkv_ln_b"]])
    x1, hf, ti, tg, cnt = _mix_out(xp2, xs2, y_scan, bonus, g, yf_p, yf_s, ln, lw["w_out"].astype(BF16),
                                   lw["norm_ffn_g"].reshape(1, D_MODEL), lw["router_w"],
                                   lw["router_b"].reshape(1, N_EXPERTS))

    n_blocks = n * TOP_K // MOE_ROWS + N_EXPERTS
    pos, row_tok, sched = _moe_schedule(ti[:, :TOP_K], ti[:, TOP_K:2 * TOP_K], cnt[0], n_blocks)
    xs = _moe_gather(row_tok, hf, n_blocks)
    hid = _moe_gate_up(sched, xs, lw["expert_w_gu"], lw["expert_b_gu"].reshape(N_EXPERTS, 1, 2 * D_EXPERT), n_blocks)
    ys = _moe_down(sched, hid, lw["expert_w_down"], lw["expert_b_down"].reshape(N_EXPERTS, 1, D_MODEL), n_blocks)
    x2 = _moe_combine(pos, ys, x1, tg)

    y_p, y_s = _ple(x2, p_p, p_s, lw["ple_norm_g"].reshape(1, D_MODEL), lw["ple_w_gate"].astype(BF16),
                    lw["ple_w_proj"].astype(BF16))

    heads = lambda a, bsz, t: a.reshape(bsz, t, N_HEADS, HEAD_DIM)
    out_p = (y_p.reshape(x_p.shape), heads(k_new_p, 1, n_p), heads(v_new_p, 1, n_p),
             lf[:n_p].reshape(1, n_p, N_HEADS), s_new[:1], shift_new_p)
    out_s = (y_s.reshape(x_s.shape), heads(k_new_s, b_s, t_s), heads(v_new_s, b_s, t_s),
             lf[n_p:].reshape(b_s, t_s, N_HEADS), s_new[1:], shift_new_s)
    return out_p, out_s


def kernel(x_prompt, x_sample, cache_fox_k, cache_fox_v, cache_fox_logf, state_rwkv, state_rwkv_shift, p_prompt, p_sample, norm_mix_g, w_in, rwkv_mu, rwkv_w0, rwkv_w2, rwkv_a0, rwkv_a2, rwkv_g2, rwkv_kk, rwkv_ka, rwkv_rk, rwkv_ln_g, rwkv_ln_b, fox_q_g, fox_k_g, fox_b_f, w_out, norm_ffn_g, router_w, router_b, expert_w_gu, expert_b_gu, expert_w_down, expert_b_down, ple_norm_g, ple_w_gate, ple_w_proj):
    assert x_prompt.shape[0] == 1 and w_in.shape[0] == 1, "one prompt stream, one layer"
    lw = dict(norm_mix_g=norm_mix_g[0], w_in=w_in[0], rwkv_mu=rwkv_mu[0], rwkv_w0=rwkv_w0[0], rwkv_w2=rwkv_w2[0],
              rwkv_a0=rwkv_a0[0], rwkv_a2=rwkv_a2[0], rwkv_g2=rwkv_g2[0], rwkv_kk=rwkv_kk[0], rwkv_ka=rwkv_ka[0],
              rwkv_rk=rwkv_rk[0], rwkv_ln_g=rwkv_ln_g[0], rwkv_ln_b=rwkv_ln_b[0], fox_q_g=fox_q_g[0],
              fox_k_g=fox_k_g[0], fox_b_f=fox_b_f[0], w_out=w_out[0], norm_ffn_g=norm_ffn_g[0],
              router_w=router_w[0], router_b=router_b[0], expert_w_gu=expert_w_gu[0], expert_b_gu=expert_b_gu[0],
              expert_w_down=expert_w_down[0], expert_b_down=expert_b_down[0], ple_norm_g=ple_norm_g[0],
              ple_w_gate=ple_w_gate[0], ple_w_proj=ple_w_proj[0])
    n_p = x_prompt.shape[1]
    (y_p, k_p, v_p, lf_p, s_p, sh_p), (y_s, k_s, v_s, lf_s, s_s, sh_s) = _layer(
        x_prompt, x_sample, cache_fox_k[0], cache_fox_v[0], cache_fox_logf[0], state_rwkv[0],
        state_rwkv_shift[0], p_prompt[0].reshape(n_p, PLE_DIM), p_sample[0].reshape(-1, PLE_DIM), lw)
    add = lambda a: a[None]
    return (y_p, y_s, add(k_p), add(v_p), add(lf_p), add(s_p), add(sh_p),
            add(k_s), add(v_s), add(lf_s), add(s_s), add(sh_s))
```

```python
import functools

import numpy as np
import jax
import jax.numpy as jnp
from jax import lax
from jax.experimental import pallas as pl
from jax.experimental.pallas import tpu as pltpu

F32 = jnp.float32
BF16 = jnp.bfloat16
HI = lax.Precision.HIGHEST

D_MODEL = 2048
HEAD_DIM = 64
N_HEADS = 16
N_PAIRS = N_HEADS // 2
MIX_W = N_HEADS * HEAD_DIM
CHUNK = 64
RWKV_PROJ = 3 * MIX_W + 64 + 64 + 128
FOX_PROJ = 3 * MIX_W + N_HEADS
Z_HALF = RWKV_PROJ
Z_PAD = 2 * Z_HALF
N_EXPERTS = 32
TOP_K = 4
D_EXPERT = 2048
SWIGLU_LIMIT = 7.0
SWIGLU_ALPHA = 1.702
PLE_DIM = 256
RMS_EPS = 1e-6
GN_EPS = 64e-5
L2_EPS = 1e-12
NEG_BIG = -1e30
LOG2E = 1.4426950408889634

LANES = 128
MXU_N = 256
MOE_ROWS = 512
MOE_TN = 1024
ROW_CHUNKS = D_MODEL // 2 // LANES
VMEM_LIMIT = 52 * 1024 * 1024


def _cparams(n_axes, vmem=VMEM_LIMIT):
    return pltpu.CompilerParams(dimension_semantics=("arbitrary",) * n_axes, vmem_limit_bytes=vmem)


def _head_sum(x):
    r = lax.broadcasted_iota(jnp.int32, (LANES, LANES), 0) // HEAD_DIM
    c = lax.broadcasted_iota(jnp.int32, (LANES, LANES), 1) // HEAD_DIM
    bd = jnp.where(r == c, 1.0, 0.0).astype(BF16)
    hi = x.astype(BF16)
    lo = (x - hi.astype(F32)).astype(BF16)
    parts = []
    for i in range(x.shape[1] // LANES):
        sl = slice(i * LANES, (i + 1) * LANES)
        parts.append(jnp.dot(hi[:, sl], bd, preferred_element_type=F32)
                     + jnp.dot(lo[:, sl], bd, preferred_element_type=F32))
    return parts[0] if len(parts) == 1 else jnp.concatenate(parts, axis=1)


def _log_sigmoid(x):
    return jnp.minimum(x, 0.0) - jnp.log1p(jnp.exp(-jnp.abs(x)))


def _rms(x, g):
    ms = jnp.mean(x * x, axis=-1, keepdims=True)
    return x * lax.rsqrt(ms + RMS_EPS) * g


def _row_tile(n_p, n_s, pref):
    return pref if (n_p % pref == 0 and n_s % pref == 0) else 128


def _two_group_specs(tm, width, np_tiles, n_grid_axes=1):
    if n_grid_axes == 1:
        return (pl.BlockSpec((tm, width), lambda i: (jnp.minimum(i, np_tiles - 1), 0)),
                pl.BlockSpec((tm, width), lambda i: (jnp.maximum(i - np_tiles, 0), 0)))
    return (pl.BlockSpec((tm, width), lambda i, j: (jnp.minimum(i, np_tiles - 1), 0)),
            pl.BlockSpec((tm, width), lambda i, j: (jnp.maximum(i - np_tiles, 0), 0)))


def _inproj_kernel(xp_ref, xs_ref, g_ref, w_ref, o_ref, h_scr, *, np_tiles):
    i = pl.program_id(0)
    first = pl.program_id(1) == 0

    @pl.when(jnp.logical_and(first, i < np_tiles))
    def _():
        h_scr[...] = _rms(xp_ref[...], g_ref[...]).astype(BF16)

    @pl.when(jnp.logical_and(first, i >= np_tiles))
    def _():
        h_scr[...] = _rms(xs_ref[...], g_ref[...]).astype(BF16)

    o_ref[...] = jnp.dot(h_scr[...], w_ref[...], preferred_element_type=F32)


def _in_proj(x_p, x_s, g, w_bf):
    n_p, n_s = x_p.shape[0], x_s.shape[0]
    n = n_p + n_s
    tm = _row_tile(n_p, n_s, 512)
    tn = Z_PAD // 4
    xp_spec, xs_spec = _two_group_specs(tm, D_MODEL, n_p // tm, 2)
    return pl.pallas_call(
        functools.partial(_inproj_kernel, np_tiles=n_p // tm),
        grid=(n // tm, Z_PAD // tn),
        in_specs=[xp_spec, xs_spec,
                  pl.BlockSpec((1, D_MODEL), lambda i, j: (0, 0)),
                  pl.BlockSpec((D_MODEL, tn), lambda i, j: (0, j))],
        out_specs=pl.BlockSpec((tm, tn), lambda i, j: (i, j)),
        out_shape=jax.ShapeDtypeStruct((n, Z_PAD), F32),
        scratch_shapes=[pltpu.VMEM((tm, D_MODEL), BF16)],
        compiler_params=_cparams(2),
        name="in_proj",
    )(x_p, x_s, g.reshape(1, D_MODEL), w_bf)


def _rwkv_pre_kernel(z_ref, prev_ref, mu_ref, vec_ref, wl_ref,
                     r_o, w_o, k_o, kk_o, b_o, v_o, g_o, bn_o, carry, *, n_prompt_tiles):
    i = pl.program_id(0)
    z = z_ref[...]
    tm = z.shape[0]

    @pl.when(i == 0)
    def _():
        carry[...] = prev_ref[0, 0:1, :]

    rolled = pltpu.roll(z, 1, axis=0)
    row = lax.broadcasted_iota(jnp.int32, z.shape, 0)
    shifted = jnp.where(row == 0, carry[...], rolled)
    is_sample = i >= n_prompt_tiles
    for c in range(tm // CHUNK):
        shifted = jnp.where(jnp.logical_and(is_sample, row == c * CHUNK), prev_ref[0, c:c + 1, :], shifted)
    carry[...] = z[tm - 1:tm, :]
    zs = z + mu_ref[...] * (shifted - z)

    r = zs[:, 0:MIX_W]
    k = zs[:, MIX_W:2 * MIX_W]
    v = zs[:, 2 * MIX_W:3 * MIX_W]
    lo = zs[:, 3 * MIX_W:RWKV_PROJ]
    lane = lax.broadcasted_iota(jnp.int32, lo.shape, 1)
    f = jnp.where(lane < 64, jnp.tanh(lo), jnp.where(lane < 128, lo, jax.nn.sigmoid(lo)))
    lora = _dot3(f, wl_ref[...])
    w_pre = vec_ref[0:1, :] + lora[:, 0:MIX_W]
    log_decay = -jnp.exp(_log_sigmoid(w_pre) - 0.5)
    a = jax.nn.sigmoid(vec_ref[1:2, :] + lora[:, MIX_W:2 * MIX_W])
    g = lora[:, 2 * MIX_W:3 * MIX_W]
    kk = k * vec_ref[2:3, :]
    kk = kk * lax.rsqrt(_head_sum(kk * kk) + L2_EPS)
    k_mod = k * (1.0 + (a - 1.0) * vec_ref[3:4, :])
    bonus = _head_sum(r * k_mod * vec_ref[4:5, :]) * v
    kka = kk * a

    g_o[...] = g
    bn_o[...] = bonus
    for p in range(N_PAIRS):
        sl = slice(p * LANES, (p + 1) * LANES)
        r_o[p] = r[:, sl]
        w_o[p] = log_decay[:, sl]
        k_o[p] = k_mod[:, sl]
        kk_o[p] = kk[:, sl]
        b_o[p] = kka[:, sl]
        v_o[p] = v[:, sl]


def _rwkv_pre(z, shift, mu, vecs, wl, n_prompt):
    n = z.shape[0]
    n_s = n - n_prompt
    tm = _row_tile(n_prompt, n_s, 256)
    if tm % CHUNK:
        tm = CHUNK
    cpt = tm // CHUNK
    nt = n // tm
    n_prompt_tiles = n_prompt // tm
    prev = jnp.concatenate([jnp.zeros((1, cpt, RWKV_PROJ), F32),
                            shift.astype(F32).reshape(n_s // tm, cpt, RWKV_PROJ)], axis=0)
    tok = jax.ShapeDtypeStruct((n, MIX_W), F32)
    tspec = pl.BlockSpec((tm, MIX_W), lambda i: (i, 0))
    pm = jax.ShapeDtypeStruct((N_PAIRS, n, LANES), F32)
    pspec = pl.BlockSpec((N_PAIRS, tm, LANES), lambda i: (0, i, 0))
    return pl.pallas_call(
        functools.partial(_rwkv_pre_kernel, n_prompt_tiles=n_prompt_tiles),
        grid=(nt,),
        in_specs=[pl.BlockSpec((tm, Z_HALF), lambda i: (i, 0)),
                  pl.BlockSpec((1, cpt, RWKV_PROJ), lambda i: (jnp.maximum(i - (n_prompt_tiles - 1), 0), 0, 0)),
                  pl.BlockSpec((1, RWKV_PROJ), lambda i: (0, 0)),
                  pl.BlockSpec((8, MIX_W), lambda i: (0, 0)),
                  pl.BlockSpec((256, 3 * MIX_W), lambda i: (0, 0))],
        out_specs=[pspec, pspec, pspec, pspec, pspec, pspec, tspec, tspec],
        out_shape=[pm, pm, pm, pm, pm, pm, tok, tok],
        scratch_shapes=[pltpu.VMEM((1, RWKV_PROJ), F32)],
        compiler_params=_cparams(1),
        name="rwkv_pre",
    )(z, prev, mu, vecs, wl)


def _dot(a, b):
    return jnp.dot(a, b, precision=HI, preferred_element_type=F32)


def _bdot(a, b):
    return jnp.dot(a, b, preferred_element_type=F32)


def _split_bf16(x):
    hi = x.astype(BF16)
    return hi, (x - hi.astype(F32)).astype(BF16)


def _dot3(a, b):
    a_hi, a_lo = _split_bf16(a)
    b_hi, b_lo = _split_bf16(b)
    return _bdot(a_hi, b_hi) + (_bdot(a_hi, b_lo) + _bdot(a_lo, b_hi))


def _pair_rows(x):
    lo_half = lax.broadcasted_iota(jnp.int32, x.shape, 1) < HEAD_DIM
    return jnp.concatenate([jnp.where(lo_half, x, 0.0), jnp.where(lo_half, 0.0, x)], axis=0)


def _scan_chunk(P, r, lw, k, kk, b, v):
    c2 = 2 * CHUNK
    i = lax.broadcasted_iota(jnp.int32, (c2, c2), 0)
    j = lax.broadcasted_iota(jnp.int32, (c2, c2), 1)
    ti = lax.broadcasted_iota(jnp.int32, (CHUNK, CHUNK), 0)
    tj = lax.broadcasted_iota(jnp.int32, (CHUNK, CHUNK), 1)
    cl = _dot((tj <= ti).astype(F32), lw)
    yield
    g_end = cl[CHUNK - 1:CHUNK, :]
    e_neg = jnp.exp(-cl)
    e_end = jnp.exp(g_end - cl)
    kap = _pair_rows(kk * jnp.exp(cl - lw)).astype(BF16)
    rt = _pair_rows(r * jnp.exp(cl)).astype(BF16)
    bt = _pair_rows(b * e_neg).astype(BF16)
    kt = _pair_rows(k * e_neg).astype(BF16)
    kh = _pair_rows(k * e_end)
    bh = _pair_rows(b * e_end)
    vv = _pair_rows(v)
    vv_b = vv.astype(BF16)
    p_b = P.astype(BF16)

    g = lax.dot_general(jnp.concatenate([kap, rt], axis=0), jnp.concatenate([bt, kt], axis=0),
                        (((1,), (1,)), ((), ())), preferred_element_type=F32)
    yield
    strict = j < i
    incl = j <= i
    a_b = jnp.where(strict, g[:c2, :c2], 0.0)
    a_bb = a_b.astype(BF16)
    a_k = jnp.where(strict, g[:c2, c2:], 0.0).astype(BF16)
    r_b = jnp.where(incl, g[c2:, :c2], 0.0).astype(BF16)
    r_k = jnp.where(incl, g[c2:, c2:], 0.0).astype(BF16)

    t_inv = (i == j).astype(F32) - jnp.where(jnp.logical_and((i & 1) == 1, j == i - 1), a_b, 0.0)
    n = 2
    while n < CHUNK:
        m = jnp.logical_and((i >> n.bit_length()) == (j >> n.bit_length()),
                            jnp.logical_and((i & (2 * n - 1)) >= n, (j & (2 * n - 1)) < n))
        t_b = t_inv.astype(BF16)
        ta = _bdot(t_b, a_bb).astype(BF16)
        yield
        t_inv = t_inv - jnp.where(m, _bdot(ta, t_b), 0.0)
        yield
        n *= 2

    w = _bdot(jnp.concatenate([kap, a_k], axis=1), jnp.concatenate([p_b, vv_b], axis=0))
    yield
    u = _bdot(t_inv.astype(BF16), w.astype(BF16))
    yield
    vu_b = jnp.concatenate([vv_b, u.astype(BF16)], axis=0)
    y2 = _bdot(rt, p_b) + _bdot(jnp.concatenate([r_k, -r_b], axis=1), vu_b)
    y = y2[:CHUNK] + y2[CHUNK:]
    yield
    g_col = jnp.broadcast_to(jnp.exp(g_end), (c2, c2)).T
    p_new = g_col * P + _dot3(jnp.concatenate([kh.T, -bh.T], axis=1), jnp.concatenate([vv, u], axis=0))
    return y, p_new


def _run_interleaved(gens):
    results = [None] * len(gens)
    live = list(range(len(gens)))
    while live:
        for idx in list(live):
            try:
                next(gens[idx])
            except StopIteration as stop:
                results[idx] = stop.value
                live.remove(idx)
    return results


def _scan_kernel(r_ref, w_ref, k_ref, kk_ref, b_ref, v_ref, s0_ref, y_ref, sout_ref, s_scr, *, n_prompt_chunks):
    c = pl.program_id(0)

    @pl.when(jnp.logical_or(c == 0, c >= n_prompt_chunks))
    def _():
        s_scr[...] = s0_ref[0]

    outs = _run_interleaved([
        _scan_chunk(s_scr[p], r_ref[p], w_ref[p], k_ref[p], kk_ref[p], b_ref[p], v_ref[p])
        for p in range(N_PAIRS)])
    for p, (y, p_new) in enumerate(outs):
        y_ref[p] = y
        s_scr[p] = p_new
        sout_ref[0, p] = p_new


def _rwkv_scan(r, w, k, kk, b, v, s0, n_prompt):
    n = r.shape[1]
    npc = n_prompt // CHUNK
    n_seq = s0.shape[0]
    pspec = pl.BlockSpec((N_PAIRS, CHUNK, LANES), lambda c: (0, c, 0))
    sspec = pl.BlockSpec((1, N_PAIRS, LANES, LANES), lambda c: (jnp.maximum(c - (npc - 1), 0), 0, 0, 0))
    return pl.pallas_call(
        functools.partial(_scan_kernel, n_prompt_chunks=npc),
        grid=(n // CHUNK,),
        in_specs=[pspec, pspec, pspec, pspec, pspec, pspec, sspec],
        out_specs=[pspec, sspec],
        out_shape=[jax.ShapeDtypeStruct((N_PAIRS, n, LANES), F32),
                   jax.ShapeDtypeStruct((n_seq, N_PAIRS, LANES, LANES), F32)],
        scratch_shapes=[pltpu.VMEM((N_PAIRS, LANES, LANES), F32)],
        compiler_params=_cparams(1),
        name="rwkv_scan",
    )(r, w, k, kk, b, v, s0)


def _fox_pre_kernel(z_ref, vec_ref, bf_ref, q_o, k_o, v_o, knp_o, vnp_o, kns_o, vns_o, lf_o, *, np_tiles):
    z = z_ref[...]
    q = z[:, 0:MIX_W]
    k = z[:, MIX_W:2 * MIX_W]
    v = z[:, 2 * MIX_W:3 * MIX_W]
    fl = z[:, 3 * MIX_W:3 * MIX_W + LANES]
    inv = 1.0 / HEAD_DIM
    qn = q * lax.rsqrt(_head_sum(q * q) * inv + RMS_EPS) * vec_ref[0:1, :]
    kn = k * lax.rsqrt(_head_sum(k * k) * inv + RMS_EPS) * vec_ref[1:2, :]
    qs = (qn * (HEAD_DIM ** -0.5 * LOG2E)).astype(BF16)
    kb = kn.astype(BF16)
    vb = v.astype(BF16)
    for p in range(N_PAIRS):
        sl = slice(p * LANES, (p + 1) * LANES)
        q_o[p] = qs[:, sl]
        k_o[p] = kb[:, sl]
        v_o[p] = vb[:, sl]
    lf_o[...] = _log_sigmoid(fl + bf_ref[...])

    @pl.when(pl.program_id(0) < np_tiles)
    def _():
        knp_o[...] = kn
        vnp_o[...] = v

    @pl.when(pl.program_id(0) >= np_tiles)
    def _():
        kns_o[...] = kn
        vns_o[...] = v


def _fox_pre(z, vecs, bf, n_p):
    n = z.shape[0]
    n_s = n - n_p
    tm = _row_tile(n_p, n_s, 256)
    np_tiles = n_p // tm
    pm = jax.ShapeDtypeStruct((N_PAIRS, n, LANES), BF16)
    pspec = pl.BlockSpec((N_PAIRS, tm, LANES), lambda i: (0, i, 0))
    p_spec, s_spec = _two_group_specs(tm, MIX_W, np_tiles)
    tok_p = jax.ShapeDtypeStruct((n_p, MIX_W), F32)
    tok_s = jax.ShapeDtypeStruct((n_s, MIX_W), F32)
    return pl.pallas_call(
        functools.partial(_fox_pre_kernel, np_tiles=np_tiles),
        grid=(n // tm,),
        in_specs=[pl.BlockSpec((tm, Z_HALF), lambda i: (i, 1)),
                  pl.BlockSpec((8, MIX_W), lambda i: (0, 0)),
                  pl.BlockSpec((1, LANES), lambda i: (0, 0))],
        out_specs=[pspec, pspec, pspec, p_spec, p_spec, s_spec, s_spec,
                   pl.BlockSpec((tm, LANES), lambda i: (i, 0))],
        out_shape=[pm, pm, pm, tok_p, tok_p, tok_s, tok_s, jax.ShapeDtypeStruct((n, LANES), F32)],
        compiler_params=_cparams(1),
        name="fox_pre",
    )(z, vecs, bf)


def _cumsum_kernel(x_ref, o_ref):
    r = lax.broadcasted_iota(jnp.int32, (LANES, LANES), 0)
    c = lax.broadcasted_iota(jnp.int32, (LANES, LANES), 1)
    tri = (r <= c).astype(F32)
    carry = jnp.zeros((N_HEADS, 1), F32)
    for i in range(x_ref.shape[2] // LANES):
        sl = slice(i * LANES, (i + 1) * LANES)
        cs = jnp.dot(x_ref[0, :, sl], tri, precision=HI, preferred_element_type=F32) + carry
        o_ref[0, :, sl] = cs
        carry = cs[:, LANES - 1:LANES]


def _cumsum_lanes(x):
    b, h, t = x.shape
    return pl.pallas_call(
        _cumsum_kernel,
        grid=(b,),
        in_specs=[pl.BlockSpec((1, h, t), lambda i: (i, 0, 0))],
        out_specs=pl.BlockSpec((1, h, t), lambda i: (i, 0, 0)),
        out_shape=jax.ShapeDtypeStruct((b, h, t), F32),
        compiler_params=_cparams(1),
        name="cumsum_logf",
    )(x)


def _rep_lanes(m, tk):
    if tk % LANES == 0:
        return m if tk == LANES else jnp.concatenate([m] * (tk // LANES), axis=1)
    return m[:, :tk]


DEN_LANE = (HEAD_DIM, 0)


def _attn_update(qa, qb, kb, vb, ck, mask, m_ref, acc_ref):
    tk = kb.shape[0]
    lane = lax.broadcasted_iota(jnp.int32, vb.shape, 1)
    own = (lane < HEAD_DIM, lane >= HEAD_DIM)
    ck2 = ck * LOG2E
    for h, qh in enumerate((qa, qb)):
        s = lax.dot_general(qh, kb, (((1,), (1,)), ((), ())), preferred_element_type=F32)
        s = s - ck2[h:h + 1, :]
        if mask is not None:
            s = jnp.where(mask, s, NEG_BIG)
        m_prev = m_ref[h]
        m_next = jnp.maximum(m_prev, jnp.max(s, axis=1, keepdims=True))
        p = jnp.exp2(s - _rep_lanes(m_next, tk))
        alpha = jnp.exp2(m_prev - m_next)
        m_ref[h] = m_next
        v_aug = jnp.where(own[h], vb.astype(F32), jnp.where(lane == DEN_LANE[h], 1.0, 0.0)).astype(BF16)
        acc_ref[h] = acc_ref[h] * alpha + jnp.dot(p.astype(BF16), v_aug, preferred_element_type=F32)


def _attn_finish(acc_ref):
    lo_half = lax.broadcasted_iota(jnp.int32, acc_ref.shape[1:], 1) < HEAD_DIM
    acc_a = acc_ref[0]
    acc_b = acc_ref[1]
    return jnp.where(lo_half, acc_a / acc_a[:, DEN_LANE[0]:DEN_LANE[0] + 1], acc_b / acc_b[:, DEN_LANE[1]:DEN_LANE[1] + 1])


def _split_q(q):
    qf = q.astype(F32)
    lo_half = lax.broadcasted_iota(jnp.int32, qf.shape, 1) < HEAD_DIM
    return jnp.where(lo_half, qf, 0.0).astype(BF16), jnp.where(lo_half, 0.0, qf).astype(BF16)


def _attn_prompt_kernel(qi_ref, ki_ref, q_ref, k_ref, v_ref, c_ref, o_ref,
                        qa_scr, qb_scr, m_scr, acc_scr, *, tq, tk):
    s_id = pl.program_id(1)
    qi = qi_ref[s_id]
    ki = ki_ref[s_id]

    @pl.when(ki == 0)
    def _():
        qa, qb = _split_q(q_ref[0])
        qa_scr[...] = qa
        qb_scr[...] = qb
        m_scr[...] = jnp.full(m_scr.shape, NEG_BIG, F32)
        acc_scr[...] = jnp.zeros(acc_scr.shape, F32)

    crosses_diagonal = ki * tk + (tk - 1) > qi * tq

    @pl.when(crosses_diagonal)
    def _():
        rows = qi * tq + lax.broadcasted_iota(jnp.int32, (tq, tk), 0)
        cols = ki * tk + lax.broadcasted_iota(jnp.int32, (tq, tk), 1)
        _attn_update(qa_scr[...], qb_scr[...], k_ref[0], v_ref[0], c_ref[0], cols <= rows, m_scr, acc_scr)

    @pl.when(jnp.logical_not(crosses_diagonal))
    def _():
        _attn_update(qa_scr[...], qb_scr[...], k_ref[0], v_ref[0], c_ref[0], None, m_scr, acc_scr)

    @pl.when(ki == ((qi + 1) * tq - 1) // tk)
    def _():
        o_ref[0] = _attn_finish(acc_scr).astype(o_ref.dtype)


def _attn_prompt(q, k, v, c, t):
    tq = min(1024, t)
    tk = min(512, t)
    steps = [(qi, ki) for qi in range(t // tq) for ki in range(((qi + 1) * tq - 1) // tk + 1)]
    qi_arr = jnp.asarray(np.array([s[0] for s in steps], np.int32))
    ki_arr = jnp.asarray(np.array([s[1] for s in steps], np.int32))
    grid_spec = pltpu.PrefetchScalarGridSpec(
        num_scalar_prefetch=2,
        grid=(N_PAIRS, len(steps)),
        in_specs=[pl.BlockSpec((1, tq, LANES), lambda p, s, qi, ki: (p, qi[s], 0)),
                  pl.BlockSpec((1, tk, LANES), lambda p, s, qi, ki: (p, ki[s], 0)),
                  pl.BlockSpec((1, tk, LANES), lambda p, s, qi, ki: (p, ki[s], 0)),
                  pl.BlockSpec((1, 2, tk), lambda p, s, qi, ki: (p, 0, ki[s]))],
        out_specs=pl.BlockSpec((1, tq, LANES), lambda p, s, qi, ki: (p, qi[s], 0)),
        scratch_shapes=[pltpu.VMEM((tq, LANES), BF16), pltpu.VMEM((tq, LANES), BF16),
                        pltpu.VMEM((2, tq, LANES), F32), pltpu.VMEM((2, tq, LANES), F32)],
    )
    return pl.pallas_call(
        functools.partial(_attn_prompt_kernel, tq=tq, tk=tk),
        grid_spec=grid_spec,
        out_shape=jax.ShapeDtypeStruct((N_PAIRS, t, LANES), BF16),
        compiler_params=_cparams(2),
        name="fox_attn_prompt",
    )(qi_arr, ki_arr, q, k, v, c)


def _attn_sample_kernel(q_ref, ck_ref, cv_ref, kn_ref, vn_ref, cc_ref, cn_ref, o_ref,
                        m_scr, acc_scr, *, nkb):
    j = pl.program_id(1)
    ts = q_ref.shape[1]

    @pl.when(j == 0)
    def _():
        m_scr[...] = jnp.full(m_scr.shape, NEG_BIG, F32)
        acc_scr[...] = jnp.zeros(acc_scr.shape, F32)

    @pl.when(j < nkb)
    def _():
        for p in range(N_PAIRS):
            sl = slice(p * LANES, (p + 1) * LANES)
            qa, qb = _split_q(q_ref[p])
            _attn_update(qa, qb, ck_ref[0, :, sl].astype(BF16), cv_ref[0, :, sl].astype(BF16),
                         cc_ref[0, 2 * p:2 * p + 2, :], None, m_scr.at[p], acc_scr.at[p])

    @pl.when(j == nkb)
    def _():
        rows = lax.broadcasted_iota(jnp.int32, (ts, ts), 0)
        cols = lax.broadcasted_iota(jnp.int32, (ts, ts), 1)
        for p in range(N_PAIRS):
            qa, qb = _split_q(q_ref[p])
            _attn_update(qa, qb, kn_ref[p], vn_ref[p], cn_ref[0, 2 * p:2 * p + 2, :], cols <= rows,
                         m_scr.at[p], acc_scr.at[p])
            o_ref[p] = _attn_finish(acc_scr.at[p]).astype(o_ref.dtype)


def _attn_sample(q, k, v, cache_k, cache_v, c_cache, c_new, row0):
    b, past, _ = cache_k.shape
    ts = c_new.shape[2]
    tk = min(1024, past)
    nkb = past // tk
    blk0 = row0 // ts
    qspec = pl.BlockSpec((N_PAIRS, ts, LANES), lambda i, j: (0, blk0 + i, 0))
    cspec = pl.BlockSpec((1, tk, MIX_W), lambda i, j: (i, jnp.minimum(j, nkb - 1), 0))
    return pl.pallas_call(
        functools.partial(_attn_sample_kernel, nkb=nkb),
        grid=(b, nkb + 1),
        in_specs=[qspec, cspec, cspec, qspec, qspec,
                  pl.BlockSpec((1, N_HEADS, tk), lambda i, j: (i, 0, jnp.minimum(j, nkb - 1))),
                  pl.BlockSpec((1, N_HEADS, ts), lambda i, j: (i, 0, 0))],
        out_specs=pl.BlockSpec((N_PAIRS, ts, LANES), lambda i, j: (0, i, 0)),
        out_shape=jax.ShapeDtypeStruct((N_PAIRS, b * ts, LANES), BF16),
        scratch_shapes=[pltpu.VMEM((N_PAIRS, 2, ts, LANES), F32), pltpu.VMEM((N_PAIRS, 2, ts, LANES), F32)],
        compiler_params=_cparams(2),
        name="fox_attn_sample",
    )(q, cache_k, cache_v, k, v, c_cache, c_new)


def _mix_out_kernel(xp_ref, xs_ref, ys_ref, bn_ref, g_ref, yfp_ref, yfs_ref, ln_ref, wo_ref, gf_ref, rw_ref, rb_ref,
                    x1_o, hf_o, ti_o, tg_o, cnt_o, cnt_scr, *, np_tiles):
    y = jnp.concatenate([ys_ref[p] for p in range(N_PAIRS)], axis=1)
    inv = 1.0 / HEAD_DIM
    mu = _head_sum(y) * inv
    d = y - mu
    var = _head_sum(d * d) * inv
    yn = d * lax.rsqrt(var + GN_EPS) * ln_ref[0:1, :] + ln_ref[1:2, :]
    yr = ((yn + bn_ref[...]) * g_ref[...]).astype(BF16)
    in_prompt = pl.program_id(0) < np_tiles
    yf = jnp.concatenate([jnp.where(in_prompt, yfp_ref[p].astype(F32), yfs_ref[p].astype(F32)).astype(BF16)
                          for p in range(N_PAIRS)], axis=1)
    mix = jnp.concatenate([yr, yf], axis=1)
    x_res = jnp.where(pl.program_id(0) < np_tiles, xp_ref[...], xs_ref[...])
    x1 = x_res + jnp.dot(mix, wo_ref[...], preferred_element_type=F32)
    x1_o[...] = x1
    hf = _rms(x1, gf_ref[...])
    bits = pltpu.bitcast(hf, jnp.uint32)
    rne = (bits + jnp.uint32(0x7FFF) + ((bits >> 16) & jnp.uint32(1))) >> 16
    words = rne[:, :D_MODEL // 2] | (rne[:, D_MODEL // 2:] << 16)
    tm_rows = hf.shape[0]
    for jc in range(ROW_CHUNKS):
        hf_o[pl.ds(jc, tm_rows, stride=ROW_CHUNKS), :] = words[:, jc * LANES:(jc + 1) * LANES]

    logits = _dot3(hf, rw_ref[...]) + rb_ref[...]
    lane_e = lax.broadcasted_iota(jnp.int32, logits.shape, 1).astype(F32)
    vals = []
    idxs = []
    cur = logits
    for _ in range(TOP_K):
        m = jnp.max(cur, axis=1, keepdims=True)
        am = jnp.min(jnp.where(cur == m, lane_e, float(N_EXPERTS)), axis=1, keepdims=True)
        vals.append(m)
        idxs.append(am)
        cur = jnp.where(lane_e == am, -jnp.inf, cur)
    es = [jnp.exp(vv - vals[0]) for vv in vals]
    tot = es[0] + es[1] + es[2] + es[3]

    @pl.when(pl.program_id(0) == 0)
    def _():
        cnt_scr[...] = jnp.zeros(cnt_scr.shape, F32)

    tm = logits.shape[0]
    sel = [lane_e == idxs[kk] for kk in range(TOP_K)]
    onehot = jnp.where(jnp.logical_or(jnp.logical_or(sel[0], sel[1]), jnp.logical_or(sel[2], sel[3])), 1.0, 0.0)
    rr = lax.broadcasted_iota(jnp.int32, (tm, tm), 0)
    cc = lax.broadcasted_iota(jnp.int32, (tm, tm), 1)
    before = jnp.dot((cc < rr).astype(BF16), onehot.astype(BF16), preferred_element_type=F32) + cnt_scr[...]
    ranks = [jnp.sum(jnp.where(sel[kk], before, 0.0), axis=1, keepdims=True) for kk in range(TOP_K)]
    cnt_scr[...] = cnt_scr[...] + jnp.sum(onehot, axis=0, keepdims=True)
    cnt_o[...] = jnp.broadcast_to(cnt_scr[...], cnt_o.shape).astype(jnp.int32)

    lane = lax.broadcasted_iota(jnp.int32, ti_o.shape, 1)
    ti = jnp.zeros(ti_o.shape, F32)
    tg = jnp.zeros(tg_o.shape, F32)
    for kk in range(TOP_K):
        ti = jnp.where(lane == kk, idxs[kk], ti)
        ti = jnp.where(lane == TOP_K + kk, ranks[kk], ti)
        tg = jnp.where(lane == kk, es[kk] / tot, tg)
    ti_o[...] = ti.astype(jnp.int32)
    tg_o[...] = tg


def _mix_out(x_p, x_s, ys, bonus, g, yf_p, yf_s, ln, wo_bf, gf, rw, rb):
    n_p, n_s = x_p.shape[0], x_s.shape[0]
    n = n_p + n_s
    tm = _row_tile(n_p, n_s, 256)
    np_tiles = n_p // tm
    xp_spec, xs_spec = _two_group_specs(tm, D_MODEL, np_tiles)
    yfp_spec = pl.BlockSpec((N_PAIRS, tm, LANES), lambda i: (0, jnp.minimum(i, np_tiles - 1), 0))
    yfs_spec = pl.BlockSpec((N_PAIRS, tm, LANES), lambda i: (0, jnp.maximum(i - np_tiles, 0), 0))
    row = lambda w: pl.BlockSpec((tm, w), lambda i: (i, 0))
    pspec = pl.BlockSpec((N_PAIRS, tm, LANES), lambda i: (0, i, 0))
    full = lambda a: pl.BlockSpec(a.shape, lambda i: (0,) * a.ndim)
    return pl.pallas_call(
        functools.partial(_mix_out_kernel, np_tiles=n_p // tm),
        grid=(n // tm,),
        in_specs=[xp_spec, xs_spec, pspec, row(MIX_W), row(MIX_W), yfp_spec, yfs_spec, full(ln), full(wo_bf), full(gf),
                  full(rw), full(rb)],
        out_specs=[row(D_MODEL), pl.BlockSpec((tm * ROW_CHUNKS, LANES), lambda i: (i, 0)), row(LANES), row(LANES),
                   pl.BlockSpec((8, N_EXPERTS), lambda i: (0, 0))],
        out_shape=[jax.ShapeDtypeStruct((n, D_MODEL), F32), jax.ShapeDtypeStruct((n * ROW_CHUNKS, LANES), jnp.uint32),
                   jax.ShapeDtypeStruct((n, LANES), jnp.int32), jax.ShapeDtypeStruct((n, LANES), F32),
                   jax.ShapeDtypeStruct((8, N_EXPERTS), jnp.int32)],
        scratch_shapes=[pltpu.VMEM((1, N_EXPERTS), F32)],
        compiler_params=_cparams(1),
        name="mix_out_router",
    )(x_p, x_s, ys, bonus, g, yf_p, yf_s, ln, wo_bf, gf, rw, rb)


DISPATCH_TOKENS = 256


def _moe_dispatch_kernel(pos_ref, pad_base_ref, pad_cnt_ref, x_ref, xs_hbm, sem):
    i = pl.program_id(0)
    tile_rows = x_ref.shape[0]

    def row_copy(src_tok, dst_row, slot):
        return pltpu.make_async_copy(
            x_ref.at[pl.ds(pl.multiple_of(src_tok * ROW_CHUNKS, ROW_CHUNKS), ROW_CHUNKS)],
            xs_hbm.at[pl.ds(pl.multiple_of(dst_row * ROW_CHUNKS, ROW_CHUNKS), ROW_CHUNKS)], sem.at[slot])

    @pl.when(i == 0)
    def _():
        pieces = [1 << b for b in reversed(range((MOE_ROWS - 1).bit_length()))]
        assert max(pieces) <= DISPATCH_TOKENS

        def pad_expert(e, carry):
            cnt = pad_cnt_ref[e]
            for starting in (True, False):
                off = pad_base_ref[e]
                for piece in pieces:
                    cp = pltpu.make_async_copy(
                        x_ref.at[pl.ds(0, piece * ROW_CHUNKS)],
                        xs_hbm.at[pl.ds(pl.multiple_of(off * ROW_CHUNKS, ROW_CHUNKS), piece * ROW_CHUNKS)], sem.at[1])

                    @pl.when((cnt & piece) != 0)
                    def _():
                        if starting:
                            cp.start()
                        else:
                            cp.wait()
                    off = off + (cnt & piece)
            return carry

        lax.fori_loop(0, N_EXPERTS, pad_expert, 0)

    def issue(t, carry):
        for kk in range(TOP_K):
            row_copy(t, pos_ref[(i * DISPATCH_TOKENS + t) * TOP_K + kk], 0).start()
        return carry
    lax.fori_loop(0, DISPATCH_TOKENS, issue, 0, unroll=4)

    for _ in range(TOP_K):
        pltpu.make_async_copy(x_ref, xs_hbm.at[pl.ds(0, tile_rows)], sem.at[0]).wait()


def _moe_dispatch(pos, pad_base, pad_cnt, hf, n_blocks):
    n_tok = pos.shape[0] // TOP_K
    assert n_tok % DISPATCH_TOKENS == 0
    tile_rows = DISPATCH_TOKENS * ROW_CHUNKS
    grid_spec = pltpu.PrefetchScalarGridSpec(
        num_scalar_prefetch=3,
        grid=(n_tok // DISPATCH_TOKENS,),
        in_specs=[pl.BlockSpec((tile_rows, LANES), lambda i, *pf: (i, 0))],
        out_specs=pl.BlockSpec(memory_space=pl.ANY),
        scratch_shapes=[pltpu.SemaphoreType.DMA((2,))],
    )
    return pl.pallas_call(
        _moe_dispatch_kernel,
        grid_spec=grid_spec,
        out_shape=jax.ShapeDtypeStruct((n_blocks * MOE_ROWS * ROW_CHUNKS, LANES), jnp.uint32),
        compiler_params=_cparams(1),
        name="moe_dispatch",
    )(pos, pad_base, pad_cnt, hf)


def _unpack_rows(x_ref):
    rows = x_ref.shape[0] // ROW_CHUNKS
    lo = []
    hi = []
    for jc in range(ROW_CHUNKS):
        words = x_ref[pl.ds(jc, rows, stride=ROW_CHUNKS), :]
        lo.append(pltpu.bitcast(words << 16, F32).astype(BF16))
        hi.append(pltpu.bitcast(words & jnp.uint32(0xFFFF0000), F32).astype(BF16))
    return jnp.concatenate(lo + hi, axis=1)


STEP_RUN, STEP_NEW_WEIGHTS, STEP_ZERO = 0, 1, 2
N_SCHED = 10


def _weight_group_step(s, sf, sg, ng, copies, cast):
    @pl.when(sf[s] == STEP_NEW_WEIGHTS)
    def _():
        g = sg[s]
        slot = g % 2

        @pl.when(g == 0)
        def _():
            for c in copies(g, slot):
                c.start()

        for c in copies(g, slot):
            c.wait()

        @pl.when(g + 1 < ng[0])
        def _():
            for c in copies(g + 1, 1 - slot):
                c.start()

        cast(slot)


def _moe_gu_kernel(se, sw, sb, sj, sf, sg, ge, gw, ng, sx, x_ref, wgu_hbm, bg_ref, bu_ref, o_ref,
                   wg_bf, wu_bf, wg_stage, wu_stage, sem):
    s = pl.program_id(0)

    def copies(g, slot):
        col = pl.multiple_of(gw[g] * MOE_TN, MOE_TN)
        return [pltpu.make_async_copy(wgu_hbm.at[ge[g], :, pl.ds(col, MOE_TN)], wg_stage.at[slot], sem.at[slot]),
                pltpu.make_async_copy(wgu_hbm.at[ge[g], :, pl.ds(D_EXPERT + col, MOE_TN)], wu_stage.at[slot],
                                      sem.at[slot])]

    def cast(slot):
        wg_bf[...] = wg_stage[slot].astype(BF16)
        wu_bf[...] = wu_stage[slot].astype(BF16)

    _weight_group_step(s, sf, sg, ng, copies, cast)

    @pl.when(sf[s] != STEP_ZERO)
    def _():
        x = _unpack_rows(x_ref)
        for c0 in range(0, MOE_TN, MXU_N):
            cols = slice(c0, c0 + MXU_N)
            g = jnp.dot(x, wg_bf[:, cols], preferred_element_type=F32) + bg_ref[0][:, cols]
            u = jnp.dot(x, wu_bf[:, cols], preferred_element_type=F32) + bu_ref[0][:, cols]
            g = jnp.minimum(g, SWIGLU_LIMIT)
            u = jnp.clip(u, -SWIGLU_LIMIT, SWIGLU_LIMIT)
            o_ref[:, cols] = ((u + 1.0) * (g * jax.nn.sigmoid(SWIGLU_ALPHA * g))).astype(BF16)

    @pl.when(sf[s] == STEP_ZERO)
    def _():
        o_ref[...] = jnp.zeros(o_ref.shape, o_ref.dtype)


def _moe_gate_up(sched, xs, w_gu, b_gu, n_blocks):
    nt = D_EXPERT // MOE_TN
    n_steps = nt * n_blocks
    bspec = lambda off: pl.BlockSpec((1, 1, MOE_TN), lambda s, *pf: (pf[0][s], 0, off + pf[1][s]))
    grid_spec = pltpu.PrefetchScalarGridSpec(
        num_scalar_prefetch=N_SCHED,
        grid=(n_steps,),
        in_specs=[pl.BlockSpec((MOE_ROWS * ROW_CHUNKS, LANES), lambda s, *pf: (pf[9][s], 0)),
                  pl.BlockSpec(memory_space=pl.ANY), bspec(0), bspec(nt)],
        out_specs=pl.BlockSpec((MOE_ROWS, MOE_TN), lambda s, *pf: (pf[2][s], pf[3][s])),
        scratch_shapes=[pltpu.VMEM((D_MODEL, MOE_TN), BF16), pltpu.VMEM((D_MODEL, MOE_TN), BF16),
                        pltpu.VMEM((2, D_MODEL, MOE_TN), F32), pltpu.VMEM((2, D_MODEL, MOE_TN), F32),
                        pltpu.SemaphoreType.DMA((2,))],
    )
    return pl.pallas_call(
        _moe_gu_kernel,
        grid_spec=grid_spec,
        out_shape=jax.ShapeDtypeStruct((n_blocks * MOE_ROWS, D_EXPERT), BF16),
        compiler_params=_cparams(1),
        name="moe_gate_up",
    )(*sched, xs, w_gu, b_gu, b_gu)


def _moe_dn_kernel(se, sw, sb, sj, sf, sg, ge, gw, ng, sx, h_ref, wd_hbm, bd_ref, o_ref, wd_bf, wd_stage, sem):
    s = pl.program_id(0)

    def copies(g, slot):
        col = pl.multiple_of(gw[g] * MOE_TN, MOE_TN)
        return [pltpu.make_async_copy(wd_hbm.at[ge[g], :, pl.ds(col, MOE_TN)], wd_stage.at[slot], sem.at[slot])]

    def cast(slot):
        wd_bf[...] = wd_stage[slot].astype(BF16)

    _weight_group_step(s, sf, sg, ng, copies, cast)

    @pl.when(sf[s] != STEP_ZERO)
    def _():
        h = h_ref[...]
        for c0 in range(0, MOE_TN, MXU_N):
            cols = slice(c0, c0 + MXU_N)
            o_ref[:, cols] = jnp.dot(h, wd_bf[:, cols], preferred_element_type=F32) + bd_ref[0][:, cols]

    @pl.when(sf[s] == STEP_ZERO)
    def _():
        o_ref[...] = jnp.zeros(o_ref.shape, o_ref.dtype)


def _moe_down(sched, hid, w_dn, b_dn, n_blocks):
    nt = D_MODEL // MOE_TN
    n_steps = nt * n_blocks
    grid_spec = pltpu.PrefetchScalarGridSpec(
        num_scalar_prefetch=N_SCHED,
        grid=(n_steps,),
        in_specs=[pl.BlockSpec((MOE_ROWS, D_EXPERT), lambda s, *pf: (pf[2][s], 0)),
                  pl.BlockSpec(memory_space=pl.ANY),
                  pl.BlockSpec((1, 1, MOE_TN), lambda s, *pf: (pf[0][s], 0, pf[1][s]))],
        out_specs=pl.BlockSpec((MOE_ROWS, MOE_TN), lambda s, *pf: (pf[2][s], pf[3][s])),
        scratch_shapes=[pltpu.VMEM((D_EXPERT, MOE_TN), BF16), pltpu.VMEM((2, D_EXPERT, MOE_TN), F32),
                        pltpu.SemaphoreType.DMA((2,))],
    )
    return pl.pallas_call(
        _moe_dn_kernel,
        grid_spec=grid_spec,
        out_shape=jax.ShapeDtypeStruct((n_blocks * MOE_ROWS, D_MODEL), F32),
        compiler_params=_cparams(1),
        name="moe_down",
    )(*sched, hid, w_dn, b_dn)


def _moe_combine_kernel(pos_ref, ys_hbm, x1_ref, tg_ref, o_ref, buf, sem):
    i = pl.program_id(0)
    nb = pl.num_programs(0)
    tm = o_ref.shape[0]

    def row_copy(blk, slot, r, kk):
        src = pos_ref[(blk * tm + r) * TOP_K + kk]
        return pltpu.make_async_copy(ys_hbm.at[pl.ds(src, 1)], buf.at[slot, kk, pl.ds(r, 1)], sem.at[slot])

    def issue(blk, slot):
        def body(r, carry):
            for kk in range(TOP_K):
                row_copy(blk, slot, r, kk).start()
            return carry
        lax.fori_loop(0, tm, body, 0, unroll=4)

    @pl.when(i == 0)
    def _():
        issue(0, 0)

    @pl.when(i + 1 < nb)
    def _():
        issue(i + 1, (i + 1) % 2)

    slot = i % 2

    for kk in range(TOP_K):
        pltpu.make_async_copy(ys_hbm.at[pl.ds(0, tm)], buf.at[slot, kk], sem.at[slot]).wait()

    tg = tg_ref[...]
    acc = x1_ref[...]
    for kk in range(TOP_K):
        acc = acc + tg[:, kk:kk + 1] * buf[slot, kk]
    o_ref[...] = acc


def _moe_combine(pos, ys, x1, tg):
    n = x1.shape[0]
    tm = min(128, n)
    grid_spec = pltpu.PrefetchScalarGridSpec(
        num_scalar_prefetch=1,
        grid=(n // tm,),
        in_specs=[pl.BlockSpec(memory_space=pl.ANY),
                  pl.BlockSpec((tm, D_MODEL), lambda i, pos: (i, 0)),
                  pl.BlockSpec((tm, LANES), lambda i, pos: (i, 0))],
        out_specs=pl.BlockSpec((tm, D_MODEL), lambda i, pos: (i, 0)),
        scratch_shapes=[pltpu.VMEM((2, TOP_K, tm, D_MODEL), F32), pltpu.SemaphoreType.DMA((2,))],
    )
    return pl.pallas_call(
        _moe_combine_kernel,
        grid_spec=grid_spec,
        out_shape=jax.ShapeDtypeStruct((n, D_MODEL), F32),
        compiler_params=_cparams(1),
        name="moe_combine",
    )(pos, ys, x1, tg)


def _moe_schedule(top_idx, rank, counts, n_blocks):
    n = top_idx.shape[0]
    n_rows = n * TOP_K
    nt = D_EXPERT // MOE_TN
    flat_e = top_idx.reshape(n_rows)
    nb_e = (counts + MOE_ROWS - 1) // MOE_ROWS
    blk_end = jnp.cumsum(nb_e)
    blk_start = blk_end - nb_e
    pos = (blk_start[flat_e] * MOE_ROWS + rank.reshape(n_rows)).astype(jnp.int32)
    pad_base = (blk_start * MOE_ROWS + counts).astype(jnp.int32)
    pad_cnt = (nb_e * MOE_ROWS - counts).astype(jnp.int32)
    used = blk_end[-1]
    s = jnp.arange(nt * n_blocks, dtype=jnp.int32)
    live = s < nt * used
    s_eff = jnp.maximum(jnp.minimum(s, nt * used - 1), 0)
    e = jnp.minimum(jnp.sum((s_eff[:, None] >= nt * blk_end[None, :]).astype(jnp.int32), axis=1), N_EXPERTS - 1)
    local = s_eff - nt * blk_start[e]
    nbe = jnp.maximum(nb_e[e], 1)
    sw = (local // nbe).astype(jnp.int32)
    tail = s - nt * used
    sb = jnp.where(live, blk_start[e] + local % nbe, used + tail // nt).astype(jnp.int32)
    sj = jnp.where(live, sw, tail % nt).astype(jnp.int32)
    sf = jnp.where(live, jnp.where(local % nbe == 0, STEP_NEW_WEIGHTS, STEP_RUN), STEP_ZERO).astype(jnp.int32)
    active = nb_e > 0
    rank_e = jnp.cumsum(active.astype(jnp.int32)) - 1
    n_active = rank_e[-1] + 1
    expert_of_rank = jnp.zeros((N_EXPERTS,), jnp.int32).at[jnp.where(active, rank_e, N_EXPERTS)].set(
        jnp.arange(N_EXPERTS, dtype=jnp.int32), mode="drop")
    sg = (rank_e[e] * nt + sw).astype(jnp.int32)
    gidx = jnp.arange(N_EXPERTS * nt, dtype=jnp.int32)
    ge = expert_of_rank[jnp.minimum(gidx // nt, n_active - 1)]
    gw = gidx % nt
    ng = (n_active * nt).astype(jnp.int32).reshape(1)
    sx = jnp.where(live, sb, 0).astype(jnp.int32)
    return pos, pad_base, pad_cnt, (e, sw, sb, sj, sf, sg, ge, gw, ng, sx)


def _ple_kernel(x_ref, pp_ref, ps_ref, g_ref, wg_ref, wp_ref, op_ref, os_ref, *, np_tiles):
    i = pl.program_id(0)
    x = x_ref[...]
    h = _rms(x, g_ref[...]).astype(BF16)
    gate = jax.nn.sigmoid(jnp.dot(h, wg_ref[...], preferred_element_type=F32))
    p = jnp.where(i < np_tiles, pp_ref[...], ps_ref[...])
    y = x + gate * jnp.dot(p.astype(BF16), wp_ref[...], preferred_element_type=F32)

    @pl.when(i < np_tiles)
    def _():
        op_ref[...] = y

    @pl.when(i >= np_tiles)
    def _():
        os_ref[...] = y


def _ple(x, p_p, p_s, g, wg_bf, wp_bf):
    n_p, n_s = p_p.shape[0], p_s.shape[0]
    n = n_p + n_s
    tm = _row_tile(n_p, n_s, 256)
    np_tiles = n_p // tm
    full = lambda a: pl.BlockSpec(a.shape, lambda i: (0,) * a.ndim)
    pp_spec, ps_spec = _two_group_specs(tm, PLE_DIM, np_tiles)
    op_spec, os_spec = _two_group_specs(tm, D_MODEL, np_tiles)
    return pl.pallas_call(
        functools.partial(_ple_kernel, np_tiles=np_tiles),
        grid=(n // tm,),
        in_specs=[pl.BlockSpec((tm, D_MODEL), lambda i: (i, 0)), pp_spec, ps_spec,
                  full(g), full(wg_bf), full(wp_bf)],
        out_specs=[op_spec, os_spec],
        out_shape=[jax.ShapeDtypeStruct((n_p, D_MODEL), F32), jax.ShapeDtypeStruct((n_s, D_MODEL), F32)],
        compiler_params=_cparams(1),
        name="ple_gate",
    )(x, p_p, p_s, g, wg_bf, wp_bf)


def _pairs_from_state(s):
    b = s.shape[0]
    st = jnp.swapaxes(s, 2, 3).reshape(b, N_PAIRS, 2, HEAD_DIM, HEAD_DIM)
    z = jnp.zeros((b, N_PAIRS, HEAD_DIM, HEAD_DIM), s.dtype)
    top = jnp.concatenate([st[:, :, 0], z], axis=3)
    bot = jnp.concatenate([z, st[:, :, 1]], axis=3)
    return jnp.concatenate([top, bot], axis=2)


def _state_from_pairs(sp):
    b = sp.shape[0]
    st = jnp.stack([sp[:, :, :HEAD_DIM, :HEAD_DIM], sp[:, :, HEAD_DIM:, HEAD_DIM:]], axis=2)
    return jnp.swapaxes(st.reshape(b, N_HEADS, HEAD_DIM, HEAD_DIM), 2, 3)


def _layer(x_p, x_s, cache_k, cache_v, cache_lf, state, shift, p_p, p_s, lw):
    t_p = x_p.shape[1]
    b_s, t_s, _ = x_s.shape
    n_p = x_p.shape[0] * t_p
    n_s = b_s * t_s
    n = n_p + n_s
    past = cache_k.shape[1]
    xp2 = x_p.reshape(n_p, D_MODEL)
    xs2 = x_s.reshape(n_s, D_MODEL)

    w_in_bf = jnp.pad(lw["w_in"], ((0, 0), (0, Z_PAD - lw["w_in"].shape[1]))).astype(BF16)
    z = _in_proj(xp2, xs2, lw["norm_mix_g"], w_in_bf)

    assert t_s == CHUNK, "each sample stream contributes exactly one scan chunk"
    vecs = jnp.zeros((8, MIX_W), F32)
    vecs = vecs.at[0].set(lw["rwkv_w0"]).at[1].set(lw["rwkv_a0"]).at[2].set(lw["rwkv_kk"])
    vecs = vecs.at[3].set(lw["rwkv_ka"]).at[4].set(lw["rwkv_rk"].reshape(MIX_W))
    wl = jnp.zeros((256, 3 * MIX_W), F32)
    wl = wl.at[0:64, 0:MIX_W].set(lw["rwkv_w2"]).at[64:128, MIX_W:2 * MIX_W].set(lw["rwkv_a2"])
    wl = wl.at[128:256, 2 * MIX_W:].set(lw["rwkv_g2"])
    r, w, k, kk, bb, v_pm, g, bonus = _rwkv_pre(z, shift, lw["rwkv_mu"].reshape(1, RWKV_PROJ), vecs, wl, n_p)
    s0 = jnp.concatenate([jnp.zeros((1, N_PAIRS, LANES, LANES), F32), _pairs_from_state(state.astype(F32))], axis=0)
    y_scan, s_out = _rwkv_scan(r, w, k, kk, bb, v_pm, s0, n_p)
    s_new = _state_from_pairs(s_out)
    shift_new_p = z[n_p - 1:n_p, :RWKV_PROJ].reshape(1, 1, RWKV_PROJ)
    shift_new_s = z[n_p:, :RWKV_PROJ].reshape(b_s, t_s, RWKV_PROJ)[:, -1:, :]

    fvecs = jnp.zeros((8, MIX_W), F32)
    fvecs = fvecs.at[0].set(jnp.tile(lw["fox_q_g"], N_HEADS)).at[1].set(jnp.tile(lw["fox_k_g"], N_HEADS))
    bf = jnp.zeros((1, LANES), F32).at[0, :N_HEADS].set(lw["fox_b_f"])
    q_pm, k_pm, vv_pm, k_new_p, v_new_p, k_new_s, v_new_s, lf = _fox_pre(z, fvecs, bf, n_p)
    lf = lf[:, :N_HEADS]
    lf_p = lf[:n_p].T.reshape(1, N_HEADS, n_p)
    c_p = _cumsum_lanes(lf_p).reshape(N_PAIRS, 2, n_p)
    yf_p = _attn_prompt(q_pm, k_pm, vv_pm, c_p, n_p)
    lf_s = jnp.swapaxes(lf[n_p:].reshape(b_s, t_s, N_HEADS), 1, 2)
    lf_all = jnp.concatenate([jnp.swapaxes(cache_lf.astype(F32), 1, 2), lf_s], axis=2)
    pad = (-lf_all.shape[2]) % LANES
    c_all = _cumsum_lanes(jnp.pad(lf_all, ((0, 0), (0, 0), (0, pad))))
    yf_s = _attn_sample(q_pm, k_pm, vv_pm, cache_k.reshape(b_s, past, MIX_W), cache_v.reshape(b_s, past, MIX_W),
                        c_all[:, :, :past], c_all[:, :, past:past + t_s], n_p)

    ln = jnp.stack([lw["rwkv_ln_g"], lw["rwkv_ln_b"]])
    x1, hf, ti, tg, cnt = _mix_out(xp2, xs2, y_scan, bonus, g, yf_p, yf_s, ln, lw["w_out"].astype(BF16),
                                   lw["norm_ffn_g"].reshape(1, D_MODEL), lw["router_w"],
                                   lw["router_b"].reshape(1, N_EXPERTS))

    n_blocks = n * TOP_K // MOE_ROWS + N_EXPERTS
    pos, pad_base, pad_cnt, sched = _moe_schedule(ti[:, :TOP_K], ti[:, TOP_K:2 * TOP_K], cnt[0], n_blocks)
    xs = _moe_dispatch(pos, pad_base, pad_cnt, hf, n_blocks)
    hid = _moe_gate_up(sched, xs, lw["expert_w_gu"], lw["expert_b_gu"].reshape(N_EXPERTS, 1, 2 * D_EXPERT), n_blocks)
    ys = _moe_down(sched, hid, lw["expert_w_down"], lw["expert_b_down"].reshape(N_EXPERTS, 1, D_MODEL), n_blocks)
    x2 = _moe_combine(pos, ys, x1, tg)

    y_p, y_s = _ple(x2, p_p, p_s, lw["ple_norm_g"].reshape(1, D_MODEL), lw["ple_w_gate"].astype(BF16),
                    lw["ple_w_proj"].astype(BF16))

    heads = lambda a, bsz, t: a.reshape(bsz, t, N_HEADS, HEAD_DIM)
    out_p = (y_p.reshape(x_p.shape), heads(k_new_p, 1, n_p), heads(v_new_p, 1, n_p),
             lf[:n_p].reshape(1, n_p, N_HEADS), s_new[:1], shift_new_p)
    out_s = (y_s.reshape(x_s.shape), heads(k_new_s, b_s, t_s), heads(v_new_s, b_s, t_s),
             lf[n_p:].reshape(b_s, t_s, N_HEADS), s_new[1:], shift_new_s)
    return out_p, out_s


def kernel(x_prompt, x_sample, cache_fox_k, cache_fox_v, cache_fox_logf, state_rwkv, state_rwkv_shift, p_prompt, p_sample, norm_mix_g, w_in, rwkv_mu, rwkv_w0, rwkv_w2, rwkv_a0, rwkv_a2, rwkv_g2, rwkv_kk, rwkv_ka, rwkv_rk, rwkv_ln_g, rwkv_ln_b, fox_q_g, fox_k_g, fox_b_f, w_out, norm_ffn_g, router_w, router_b, expert_w_gu, expert_b_gu, expert_w_down, expert_b_down, ple_norm_g, ple_w_gate, ple_w_proj):
    assert x_prompt.shape[0] == 1 and w_in.shape[0] == 1, "one prompt stream, one layer"
    lw = dict(norm_mix_g=norm_mix_g[0], w_in=w_in[0], rwkv_mu=rwkv_mu[0], rwkv_w0=rwkv_w0[0], rwkv_w2=rwkv_w2[0],
              rwkv_a0=rwkv_a0[0], rwkv_a2=rwkv_a2[0], rwkv_g2=rwkv_g2[0], rwkv_kk=rwkv_kk[0], rwkv_ka=rwkv_ka[0],
              rwkv_rk=rwkv_rk[0], rwkv_ln_g=rwkv_ln_g[0], rwkv_ln_b=rwkv_ln_b[0], fox_q_g=fox_q_g[0],
              fox_k_g=fox_k_g[0], fox_b_f=fox_b_f[0], w_out=w_out[0], norm_ffn_g=norm_ffn_g[0],
              router_w=router_w[0], router_b=router_b[0], expert_w_gu=expert_w_gu[0], expert_b_gu=expert_b_gu[0],
              expert_w_down=expert_w_down[0], expert_b_down=expert_b_down[0], ple_norm_g=ple_norm_g[0],
              ple_w_gate=ple_w_gate[0], ple_w_proj=ple_w_proj[0])
    n_p = x_prompt.shape[1]
    (y_p, k_p, v_p, lf_p, s_p, sh_p), (y_s, k_s, v_s, lf_s, s_s, sh_s) = _layer(
        x_prompt, x_sample, cache_fox_k[0], cache_fox_v[0], cache_fox_logf[0], state_rwkv[0],
        state_rwkv_shift[0], p_prompt[0].reshape(n_p, PLE_DIM), p_sample[0].reshape(-1, PLE_DIM), lw)
    add = lambda a: a[None]
    return (y_p, y_s, add(k_p), add(v_p), add(lf_p), add(s_p), add(sh_p),
            add(k_s), add(v_s), add(lf_s), add(s_s), add(sh_s))
```

```python
import functools

import numpy as np
import jax
import jax.numpy as jnp
from jax import lax
from jax.experimental import pallas as pl
from jax.experimental.pallas import tpu as pltpu

F32 = jnp.float32
BF16 = jnp.bfloat16
HI = lax.Precision.HIGHEST

D_MODEL = 2048
HEAD_DIM = 64
N_HEADS = 16
N_PAIRS = N_HEADS // 2
MIX_W = N_HEADS * HEAD_DIM
CHUNK = 64
RWKV_PROJ = 3 * MIX_W + 64 + 64 + 128
FOX_PROJ = 3 * MIX_W + N_HEADS
Z_HALF = RWKV_PROJ
Z_PAD = 2 * Z_HALF
N_EXPERTS = 32
TOP_K = 4
D_EXPERT = 2048
SWIGLU_LIMIT = 7.0
SWIGLU_ALPHA = 1.702
PLE_DIM = 256
RMS_EPS = 1e-6
GN_EPS = 64e-5
L2_EPS = 1e-12
NEG_BIG = -1e30
LOG2E = 1.4426950408889634

LANES = 128
MXU_N = 256
MOE_ROWS = 512
MOE_TN = 1024
ROW_CHUNKS = D_MODEL // 2 // LANES
VMEM_LIMIT = 52 * 1024 * 1024


def _cparams(n_axes, vmem=VMEM_LIMIT):
    return pltpu.CompilerParams(dimension_semantics=("arbitrary",) * n_axes, vmem_limit_bytes=vmem)


def _head_sum(x):
    r = lax.broadcasted_iota(jnp.int32, (LANES, LANES), 0) // HEAD_DIM
    c = lax.broadcasted_iota(jnp.int32, (LANES, LANES), 1) // HEAD_DIM
    bd = jnp.where(r == c, 1.0, 0.0).astype(BF16)
    hi = x.astype(BF16)
    lo = (x - hi.astype(F32)).astype(BF16)
    parts = []
    for i in range(x.shape[1] // LANES):
        sl = slice(i * LANES, (i + 1) * LANES)
        parts.append(jnp.dot(hi[:, sl], bd, preferred_element_type=F32)
                     + jnp.dot(lo[:, sl], bd, preferred_element_type=F32))
    return parts[0] if len(parts) == 1 else jnp.concatenate(parts, axis=1)


def _log_sigmoid(x):
    return jnp.minimum(x, 0.0) - jnp.log1p(jnp.exp(-jnp.abs(x)))


def _rms(x, g):
    ms = jnp.mean(x * x, axis=-1, keepdims=True)
    return x * lax.rsqrt(ms + RMS_EPS) * g


def _row_tile(n_p, n_s, pref):
    return pref if (n_p % pref == 0 and n_s % pref == 0) else 128


def _two_group_specs(tm, width, np_tiles, n_grid_axes=1):
    if n_grid_axes == 1:
        return (pl.BlockSpec((tm, width), lambda i: (jnp.minimum(i, np_tiles - 1), 0)),
                pl.BlockSpec((tm, width), lambda i: (jnp.maximum(i - np_tiles, 0), 0)))
    return (pl.BlockSpec((tm, width), lambda i, j: (jnp.minimum(i, np_tiles - 1), 0)),
            pl.BlockSpec((tm, width), lambda i, j: (jnp.maximum(i - np_tiles, 0), 0)))


def _inproj_kernel(xp_ref, xs_ref, g_ref, w_ref, o_ref, h_scr, *, np_tiles):
    i = pl.program_id(0)
    first = pl.program_id(1) == 0

    @pl.when(jnp.logical_and(first, i < np_tiles))
    def _():
        h_scr[...] = _rms(xp_ref[...], g_ref[...]).astype(BF16)

    @pl.when(jnp.logical_and(first, i >= np_tiles))
    def _():
        h_scr[...] = _rms(xs_ref[...], g_ref[...]).astype(BF16)

    o_ref[...] = jnp.dot(h_scr[...], w_ref[...], preferred_element_type=F32)


def _in_proj(x_p, x_s, g, w_bf):
    n_p, n_s = x_p.shape[0], x_s.shape[0]
    n = n_p + n_s
    tm = _row_tile(n_p, n_s, 512)
    tn = Z_PAD // 4
    xp_spec, xs_spec = _two_group_specs(tm, D_MODEL, n_p // tm, 2)
    return pl.pallas_call(
        functools.partial(_inproj_kernel, np_tiles=n_p // tm),
        grid=(n // tm, Z_PAD // tn),
        in_specs=[xp_spec, xs_spec,
                  pl.BlockSpec((1, D_MODEL), lambda i, j: (0, 0)),
                  pl.BlockSpec((D_MODEL, tn), lambda i, j: (0, j))],
        out_specs=pl.BlockSpec((tm, tn), lambda i, j: (i, j)),
        out_shape=jax.ShapeDtypeStruct((n, Z_PAD), F32),
        scratch_shapes=[pltpu.VMEM((tm, D_MODEL), BF16)],
        compiler_params=_cparams(2),
        name="in_proj",
    )(x_p, x_s, g.reshape(1, D_MODEL), w_bf)


def _rwkv_pre_kernel(z_ref, prev_ref, mu_ref, vec_ref, wl_ref,
                     r_o, w_o, k_o, kk_o, b_o, v_o, g_o, bn_o, carry, *, n_prompt_tiles):
    i = pl.program_id(0)
    z = z_ref[...]
    tm = z.shape[0]

    @pl.when(i == 0)
    def _():
        carry[...] = prev_ref[0, 0:1, :]

    rolled = pltpu.roll(z, 1, axis=0)
    row = lax.broadcasted_iota(jnp.int32, z.shape, 0)
    shifted = jnp.where(row == 0, carry[...], rolled)
    is_sample = i >= n_prompt_tiles
    for c in range(tm // CHUNK):
        shifted = jnp.where(jnp.logical_and(is_sample, row == c * CHUNK), prev_ref[0, c:c + 1, :], shifted)
    carry[...] = z[tm - 1:tm, :]
    zs = z + mu_ref[...] * (shifted - z)

    r = zs[:, 0:MIX_W]
    k = zs[:, MIX_W:2 * MIX_W]
    v = zs[:, 2 * MIX_W:3 * MIX_W]
    lo = zs[:, 3 * MIX_W:RWKV_PROJ]
    lane = lax.broadcasted_iota(jnp.int32, lo.shape, 1)
    f = jnp.where(lane < 64, jnp.tanh(lo), jnp.where(lane < 128, lo, jax.nn.sigmoid(lo)))
    lora = _dot3(f, wl_ref[...])
    w_pre = vec_ref[0:1, :] + lora[:, 0:MIX_W]
    log_decay = -jnp.exp(_log_sigmoid(w_pre) - 0.5)
    a = jax.nn.sigmoid(vec_ref[1:2, :] + lora[:, MIX_W:2 * MIX_W])
    g = lora[:, 2 * MIX_W:3 * MIX_W]
    kk = k * vec_ref[2:3, :]
    kk = kk * lax.rsqrt(_head_sum(kk * kk) + L2_EPS)
    k_mod = k * (1.0 + (a - 1.0) * vec_ref[3:4, :])
    bonus = _head_sum(r * k_mod * vec_ref[4:5, :]) * v
    kka = kk * a

    g_o[...] = g
    bn_o[...] = bonus
    for p in range(N_PAIRS):
        sl = slice(p * LANES, (p + 1) * LANES)
        r_o[p] = r[:, sl]
        w_o[p] = log_decay[:, sl]
        k_o[p] = k_mod[:, sl]
        kk_o[p] = kk[:, sl]
        b_o[p] = kka[:, sl]
        v_o[p] = v[:, sl]


def _rwkv_pre(z, shift, mu, vecs, wl, n_prompt):
    n = z.shape[0]
    n_s = n - n_prompt
    tm = _row_tile(n_prompt, n_s, 256)
    if tm % CHUNK:
        tm = CHUNK
    cpt = tm // CHUNK
    nt = n // tm
    n_prompt_tiles = n_prompt // tm
    prev = jnp.concatenate([jnp.zeros((1, cpt, RWKV_PROJ), F32),
                            shift.astype(F32).reshape(n_s // tm, cpt, RWKV_PROJ)], axis=0)
    tok = jax.ShapeDtypeStruct((n, MIX_W), F32)
    tspec = pl.BlockSpec((tm, MIX_W), lambda i: (i, 0))
    pm = jax.ShapeDtypeStruct((N_PAIRS, n, LANES), F32)
    pspec = pl.BlockSpec((N_PAIRS, tm, LANES), lambda i: (0, i, 0))
    return pl.pallas_call(
        functools.partial(_rwkv_pre_kernel, n_prompt_tiles=n_prompt_tiles),
        grid=(nt,),
        in_specs=[pl.BlockSpec((tm, Z_HALF), lambda i: (i, 0)),
                  pl.BlockSpec((1, cpt, RWKV_PROJ), lambda i: (jnp.maximum(i - (n_prompt_tiles - 1), 0), 0, 0)),
                  pl.BlockSpec((1, RWKV_PROJ), lambda i: (0, 0)),
                  pl.BlockSpec((8, MIX_W), lambda i: (0, 0)),
                  pl.BlockSpec((256, 3 * MIX_W), lambda i: (0, 0))],
        out_specs=[pspec, pspec, pspec, pspec, pspec, pspec, tspec, tspec],
        out_shape=[pm, pm, pm, pm, pm, pm, tok, tok],
        scratch_shapes=[pltpu.VMEM((1, RWKV_PROJ), F32)],
        compiler_params=_cparams(1),
        name="rwkv_pre",
    )(z, prev, mu, vecs, wl)


def _dot(a, b):
    return jnp.dot(a, b, precision=HI, preferred_element_type=F32)


def _bdot(a, b):
    return jnp.dot(a, b, preferred_element_type=F32)


def _split_bf16(x):
    hi = x.astype(BF16)
    return hi, (x - hi.astype(F32)).astype(BF16)


def _dot3(a, b):
    a_hi, a_lo = _split_bf16(a)
    b_hi, b_lo = _split_bf16(b)
    return _bdot(a_hi, b_hi) + (_bdot(a_hi, b_lo) + _bdot(a_lo, b_hi))


def _pair_rows(x):
    lo_half = lax.broadcasted_iota(jnp.int32, x.shape, 1) < HEAD_DIM
    return jnp.concatenate([jnp.where(lo_half, x, 0.0), jnp.where(lo_half, 0.0, x)], axis=0)


def _scan_chunk(P, r, lw, k, kk, b, v):
    c2 = 2 * CHUNK
    i = lax.broadcasted_iota(jnp.int32, (c2, c2), 0)
    j = lax.broadcasted_iota(jnp.int32, (c2, c2), 1)
    ti = lax.broadcasted_iota(jnp.int32, (CHUNK, CHUNK), 0)
    tj = lax.broadcasted_iota(jnp.int32, (CHUNK, CHUNK), 1)
    cl = _dot((tj <= ti).astype(F32), lw)
    yield
    g_end = cl[CHUNK - 1:CHUNK, :]
    e_neg = jnp.exp(-cl)
    e_end = jnp.exp(g_end - cl)
    kap = _pair_rows(kk * jnp.exp(cl - lw)).astype(BF16)
    rt = _pair_rows(r * jnp.exp(cl)).astype(BF16)
    bt = _pair_rows(b * e_neg).astype(BF16)
    kt = _pair_rows(k * e_neg).astype(BF16)
    kh = _pair_rows(k * e_end)
    bh = _pair_rows(b * e_end)
    vv = _pair_rows(v)
    vv_b = vv.astype(BF16)
    p_b = P.astype(BF16)

    g = lax.dot_general(jnp.concatenate([kap, rt], axis=0), jnp.concatenate([bt, kt], axis=0),
                        (((1,), (1,)), ((), ())), preferred_element_type=F32)
    yield
    strict = j < i
    incl = j <= i
    a_b = jnp.where(strict, g[:c2, :c2], 0.0)
    a_bb = a_b.astype(BF16)
    a_k = jnp.where(strict, g[:c2, c2:], 0.0).astype(BF16)
    r_b = jnp.where(incl, g[c2:, :c2], 0.0).astype(BF16)
    r_k = jnp.where(incl, g[c2:, c2:], 0.0).astype(BF16)

    t_inv = (i == j).astype(F32) - jnp.where(jnp.logical_and((i & 1) == 1, j == i - 1), a_b, 0.0)
    n = 2
    while n < CHUNK:
        m = jnp.logical_and((i >> n.bit_length()) == (j >> n.bit_length()),
                            jnp.logical_and((i & (2 * n - 1)) >= n, (j & (2 * n - 1)) < n))
        t_b = t_inv.astype(BF16)
        ta = _bdot(t_b, a_bb).astype(BF16)
        yield
        t_inv = t_inv - jnp.where(m, _bdot(ta, t_b), 0.0)
        yield
        n *= 2

    w = _bdot(jnp.concatenate([kap, a_k], axis=1), jnp.concatenate([p_b, vv_b], axis=0))
    yield
    u = _bdot(t_inv.astype(BF16), w.astype(BF16))
    yield
    vu_b = jnp.concatenate([vv_b, u.astype(BF16)], axis=0)
    y2 = _bdot(rt, p_b) + _bdot(jnp.concatenate([r_k, -r_b], axis=1), vu_b)
    y = y2[:CHUNK] + y2[CHUNK:]
    yield
    g_col = jnp.broadcast_to(jnp.exp(g_end), (c2, c2)).T
    p_new = g_col * P + _dot3(jnp.concatenate([kh.T, -bh.T], axis=1), jnp.concatenate([vv, u], axis=0))
    return y, p_new


def _run_interleaved(gens):
    results = [None] * len(gens)
    live = list(range(len(gens)))
    while live:
        for idx in list(live):
            try:
                next(gens[idx])
            except StopIteration as stop:
                results[idx] = stop.value
                live.remove(idx)
    return results


def _scan_kernel(r_ref, w_ref, k_ref, kk_ref, b_ref, v_ref, s0_ref, y_ref, sout_ref, s_scr, *, n_prompt_chunks):
    c = pl.program_id(0)

    @pl.when(jnp.logical_or(c == 0, c >= n_prompt_chunks))
    def _():
        s_scr[...] = s0_ref[0]

    outs = _run_interleaved([
        _scan_chunk(s_scr[p], r_ref[p], w_ref[p], k_ref[p], kk_ref[p], b_ref[p], v_ref[p])
        for p in range(N_PAIRS)])
    for p, (y, p_new) in enumerate(outs):
        y_ref[p] = y
        s_scr[p] = p_new
        sout_ref[0, p] = p_new


def _rwkv_scan(r, w, k, kk, b, v, s0, n_prompt):
    n = r.shape[1]
    npc = n_prompt // CHUNK
    n_seq = s0.shape[0]
    pspec = pl.BlockSpec((N_PAIRS, CHUNK, LANES), lambda c: (0, c, 0))
    sspec = pl.BlockSpec((1, N_PAIRS, LANES, LANES), lambda c: (jnp.maximum(c - (npc - 1), 0), 0, 0, 0))
    return pl.pallas_call(
        functools.partial(_scan_kernel, n_prompt_chunks=npc),
        grid=(n // CHUNK,),
        in_specs=[pspec, pspec, pspec, pspec, pspec, pspec, sspec],
        out_specs=[pspec, sspec],
        out_shape=[jax.ShapeDtypeStruct((N_PAIRS, n, LANES), F32),
                   jax.ShapeDtypeStruct((n_seq, N_PAIRS, LANES, LANES), F32)],
        scratch_shapes=[pltpu.VMEM((N_PAIRS, LANES, LANES), F32)],
        compiler_params=_cparams(1),
        name="rwkv_scan",
    )(r, w, k, kk, b, v, s0)


def _fox_pre_kernel(z_ref, vec_ref, bf_ref, q_o, k_o, v_o, knp_o, vnp_o, kns_o, vns_o, lf_o, *, np_tiles):
    z = z_ref[...]
    q = z[:, 0:MIX_W]
    k = z[:, MIX_W:2 * MIX_W]
    v = z[:, 2 * MIX_W:3 * MIX_W]
    fl = z[:, 3 * MIX_W:3 * MIX_W + LANES]
    inv = 1.0 / HEAD_DIM
    qn = q * lax.rsqrt(_head_sum(q * q) * inv + RMS_EPS) * vec_ref[0:1, :]
    kn = k * lax.rsqrt(_head_sum(k * k) * inv + RMS_EPS) * vec_ref[1:2, :]
    qs = (qn * (HEAD_DIM ** -0.5 * LOG2E)).astype(BF16)
    kb = kn.astype(BF16)
    vb = v.astype(BF16)
    for p in range(N_PAIRS):
        sl = slice(p * LANES, (p + 1) * LANES)
        q_o[p] = qs[:, sl]
        k_o[p] = kb[:, sl]
        v_o[p] = vb[:, sl]
    lf_o[...] = _log_sigmoid(fl + bf_ref[...])

    @pl.when(pl.program_id(0) < np_tiles)
    def _():
        knp_o[...] = kn
        vnp_o[...] = v

    @pl.when(pl.program_id(0) >= np_tiles)
    def _():
        kns_o[...] = kn
        vns_o[...] = v


def _fox_pre(z, vecs, bf, n_p):
    n = z.shape[0]
    n_s = n - n_p
    tm = _row_tile(n_p, n_s, 256)
    np_tiles = n_p // tm
    pm = jax.ShapeDtypeStruct((N_PAIRS, n, LANES), BF16)
    pspec = pl.BlockSpec((N_PAIRS, tm, LANES), lambda i: (0, i, 0))
    p_spec, s_spec = _two_group_specs(tm, MIX_W, np_tiles)
    tok_p = jax.ShapeDtypeStruct((n_p, MIX_W), F32)
    tok_s = jax.ShapeDtypeStruct((n_s, MIX_W), F32)
    return pl.pallas_call(
        functools.partial(_fox_pre_kernel, np_tiles=np_tiles),
        grid=(n // tm,),
        in_specs=[pl.BlockSpec((tm, Z_HALF), lambda i: (i, 1)),
                  pl.BlockSpec((8, MIX_W), lambda i: (0, 0)),
                  pl.BlockSpec((1, LANES), lambda i: (0, 0))],
        out_specs=[pspec, pspec, pspec, p_spec, p_spec, s_spec, s_spec,
                   pl.BlockSpec((tm, LANES), lambda i: (i, 0))],
        out_shape=[pm, pm, pm, tok_p, tok_p, tok_s, tok_s, jax.ShapeDtypeStruct((n, LANES), F32)],
        compiler_params=_cparams(1),
        name="fox_pre",
    )(z, vecs, bf)


def _cumsum_kernel(x_ref, o_ref):
    r = lax.broadcasted_iota(jnp.int32, (LANES, LANES), 0)
    c = lax.broadcasted_iota(jnp.int32, (LANES, LANES), 1)
    tri = (r <= c).astype(F32)
    carry = jnp.zeros((N_HEADS, 1), F32)
    for i in range(x_ref.shape[2] // LANES):
        sl = slice(i * LANES, (i + 1) * LANES)
        cs = jnp.dot(x_ref[0, :, sl], tri, precision=HI, preferred_element_type=F32) + carry
        o_ref[0, :, sl] = cs
        carry = cs[:, LANES - 1:LANES]


def _cumsum_lanes(x):
    b, h, t = x.shape
    return pl.pallas_call(
        _cumsum_kernel,
        grid=(b,),
        in_specs=[pl.BlockSpec((1, h, t), lambda i: (i, 0, 0))],
        out_specs=pl.BlockSpec((1, h, t), lambda i: (i, 0, 0)),
        out_shape=jax.ShapeDtypeStruct((b, h, t), F32),
        compiler_params=_cparams(1),
        name="cumsum_logf",
    )(x)


def _rep_lanes(m, tk):
    if tk % LANES == 0:
        return m if tk == LANES else jnp.concatenate([m] * (tk // LANES), axis=1)
    return m[:, :tk]


DEN_LANE = (HEAD_DIM, 0)


def _attn_update(qa, qb, kb, vb, ck, mask, m_ref, acc_ref):
    tk = kb.shape[0]
    lane = lax.broadcasted_iota(jnp.int32, vb.shape, 1)
    own = (lane < HEAD_DIM, lane >= HEAD_DIM)
    ck2 = ck * LOG2E
    for h, qh in enumerate((qa, qb)):
        s = lax.dot_general(qh, kb, (((1,), (1,)), ((), ())), preferred_element_type=F32)
        s = s - ck2[h:h + 1, :]
        if mask is not None:
            s = jnp.where(mask, s, NEG_BIG)
        m_prev = m_ref[h]
        m_next = jnp.maximum(m_prev, jnp.max(s, axis=1, keepdims=True))
        p = jnp.exp2(s - _rep_lanes(m_next, tk))
        alpha = jnp.exp2(m_prev - m_next)
        m_ref[h] = m_next
        v_aug = jnp.where(own[h], vb.astype(F32), jnp.where(lane == DEN_LANE[h], 1.0, 0.0)).astype(BF16)
        acc_ref[h] = acc_ref[h] * alpha + jnp.dot(p.astype(BF16), v_aug, preferred_element_type=F32)


def _attn_finish(acc_ref):
    lo_half = lax.broadcasted_iota(jnp.int32, acc_ref.shape[1:], 1) < HEAD_DIM
    acc_a = acc_ref[0]
    acc_b = acc_ref[1]
    return jnp.where(lo_half, acc_a / acc_a[:, DEN_LANE[0]:DEN_LANE[0] + 1], acc_b / acc_b[:, DEN_LANE[1]:DEN_LANE[1] + 1])


def _split_q(q):
    qf = q.astype(F32)
    lo_half = lax.broadcasted_iota(jnp.int32, qf.shape, 1) < HEAD_DIM
    return jnp.where(lo_half, qf, 0.0).astype(BF16), jnp.where(lo_half, 0.0, qf).astype(BF16)


def _attn_prompt_kernel(qi_ref, ki_ref, q_ref, k_ref, v_ref, c_ref, o_ref,
                        qa_scr, qb_scr, m_scr, acc_scr, *, tq, tk):
    s_id = pl.program_id(1)
    qi = qi_ref[s_id]
    ki = ki_ref[s_id]

    @pl.when(ki == 0)
    def _():
        qa, qb = _split_q(q_ref[0])
        qa_scr[...] = qa
        qb_scr[...] = qb
        m_scr[...] = jnp.full(m_scr.shape, NEG_BIG, F32)
        acc_scr[...] = jnp.zeros(acc_scr.shape, F32)

    crosses_diagonal = ki * tk + (tk - 1) > qi * tq

    @pl.when(crosses_diagonal)
    def _():
        rows = qi * tq + lax.broadcasted_iota(jnp.int32, (tq, tk), 0)
        cols = ki * tk + lax.broadcasted_iota(jnp.int32, (tq, tk), 1)
        _attn_update(qa_scr[...], qb_scr[...], k_ref[0], v_ref[0], c_ref[0], cols <= rows, m_scr, acc_scr)

    @pl.when(jnp.logical_not(crosses_diagonal))
    def _():
        _attn_update(qa_scr[...], qb_scr[...], k_ref[0], v_ref[0], c_ref[0], None, m_scr, acc_scr)

    @pl.when(ki == ((qi + 1) * tq - 1) // tk)
    def _():
        o_ref[0] = _attn_finish(acc_scr).astype(o_ref.dtype)


def _attn_prompt(q, k, v, c, t):
    tq = min(1024, t)
    tk = min(512, t)
    steps = [(qi, ki) for qi in range(t // tq) for ki in range(((qi + 1) * tq - 1) // tk + 1)]
    qi_arr = jnp.asarray(np.array([s[0] for s in steps], np.int32))
    ki_arr = jnp.asarray(np.array([s[1] for s in steps], np.int32))
    grid_spec = pltpu.PrefetchScalarGridSpec(
        num_scalar_prefetch=2,
        grid=(N_PAIRS, len(steps)),
        in_specs=[pl.BlockSpec((1, tq, LANES), lambda p, s, qi, ki: (p, qi[s], 0)),
                  pl.BlockSpec((1, tk, LANES), lambda p, s, qi, ki: (p, ki[s], 0)),
                  pl.BlockSpec((1, tk, LANES), lambda p, s, qi, ki: (p, ki[s], 0)),
                  pl.BlockSpec((1, 2, tk), lambda p, s, qi, ki: (p, 0, ki[s]))],
        out_specs=pl.BlockSpec((1, tq, LANES), lambda p, s, qi, ki: (p, qi[s], 0)),
        scratch_shapes=[pltpu.VMEM((tq, LANES), BF16), pltpu.VMEM((tq, LANES), BF16),
                        pltpu.VMEM((2, tq, LANES), F32), pltpu.VMEM((2, tq, LANES), F32)],
    )
    return pl.pallas_call(
        functools.partial(_attn_prompt_kernel, tq=tq, tk=tk),
        grid_spec=grid_spec,
        out_shape=jax.ShapeDtypeStruct((N_PAIRS, t, LANES), BF16),
        compiler_params=_cparams(2),
        name="fox_attn_prompt",
    )(qi_arr, ki_arr, q, k, v, c)


def _attn_sample_kernel(q_ref, ck_ref, cv_ref, kn_ref, vn_ref, cc_ref, cn_ref, o_ref,
                        m_scr, acc_scr, *, nkb):
    j = pl.program_id(1)
    ts = q_ref.shape[1]

    @pl.when(j == 0)
    def _():
        m_scr[...] = jnp.full(m_scr.shape, NEG_BIG, F32)
        acc_scr[...] = jnp.zeros(acc_scr.shape, F32)

    @pl.when(j < nkb)
    def _():
        for p in range(N_PAIRS):
            sl = slice(p * LANES, (p + 1) * LANES)
            qa, qb = _split_q(q_ref[p])
            _attn_update(qa, qb, ck_ref[0, :, sl].astype(BF16), cv_ref[0, :, sl].astype(BF16),
                         cc_ref[0, 2 * p:2 * p + 2, :], None, m_scr.at[p], acc_scr.at[p])

    @pl.when(j == nkb)
    def _():
        rows = lax.broadcasted_iota(jnp.int32, (ts, ts), 0)
        cols = lax.broadcasted_iota(jnp.int32, (ts, ts), 1)
        for p in range(N_PAIRS):
            qa, qb = _split_q(q_ref[p])
            _attn_update(qa, qb, kn_ref[p], vn_ref[p], cn_ref[0, 2 * p:2 * p + 2, :], cols <= rows,
                         m_scr.at[p], acc_scr.at[p])
            o_ref[p] = _attn_finish(acc_scr.at[p]).astype(o_ref.dtype)


def _attn_sample(q, k, v, cache_k, cache_v, c_cache, c_new, row0):
    b, past, _ = cache_k.shape
    ts = c_new.shape[2]
    tk = min(1024, past)
    nkb = past // tk
    blk0 = row0 // ts
    qspec = pl.BlockSpec((N_PAIRS, ts, LANES), lambda i, j: (0, blk0 + i, 0))
    cspec = pl.BlockSpec((1, tk, MIX_W), lambda i, j: (i, jnp.minimum(j, nkb - 1), 0))
    return pl.pallas_call(
        functools.partial(_attn_sample_kernel, nkb=nkb),
        grid=(b, nkb + 1),
        in_specs=[qspec, cspec, cspec, qspec, qspec,
                  pl.BlockSpec((1, N_HEADS, tk), lambda i, j: (i, 0, jnp.minimum(j, nkb - 1))),
                  pl.BlockSpec((1, N_HEADS, ts), lambda i, j: (i, 0, 0))],
        out_specs=pl.BlockSpec((N_PAIRS, ts, LANES), lambda i, j: (0, i, 0)),
        out_shape=jax.ShapeDtypeStruct((N_PAIRS, b * ts, LANES), BF16),
        scratch_shapes=[pltpu.VMEM((N_PAIRS, 2, ts, LANES), F32), pltpu.VMEM((N_PAIRS, 2, ts, LANES), F32)],
        compiler_params=_cparams(2),
        name="fox_attn_sample",
    )(q, cache_k, cache_v, k, v, c_cache, c_new)


def _mix_out_kernel(xp_ref, xs_ref, ys_ref, bn_ref, g_ref, yfp_ref, yfs_ref, ln_ref, wo_ref, gf_ref, rw_ref, rb_ref,
                    x1_o, hf_o, ti_o, tg_o, cnt_o, cnt_scr, *, np_tiles):
    y = jnp.concatenate([ys_ref[p] for p in range(N_PAIRS)], axis=1)
    inv = 1.0 / HEAD_DIM
    mu = _head_sum(y) * inv
    d = y - mu
    var = _head_sum(d * d) * inv
    yn = d * lax.rsqrt(var + GN_EPS) * ln_ref[0:1, :] + ln_ref[1:2, :]
    yr = ((yn + bn_ref[...]) * g_ref[...]).astype(BF16)
    in_prompt = pl.program_id(0) < np_tiles
    yf = jnp.concatenate([jnp.where(in_prompt, yfp_ref[p].astype(F32), yfs_ref[p].astype(F32)).astype(BF16)
                          for p in range(N_PAIRS)], axis=1)
    mix = jnp.concatenate([yr, yf], axis=1)
    x_res = jnp.where(pl.program_id(0) < np_tiles, xp_ref[...], xs_ref[...])
    x1 = x_res + jnp.dot(mix, wo_ref[...], preferred_element_type=F32)
    x1_o[...] = x1
    hf = _rms(x1, gf_ref[...])
    bits = pltpu.bitcast(hf, jnp.uint32)
    rne = (bits + jnp.uint32(0x7FFF) + ((bits >> 16) & jnp.uint32(1))) >> 16
    words = rne[:, :D_MODEL // 2] | (rne[:, D_MODEL // 2:] << 16)
    tm_rows = hf.shape[0]
    for jc in range(ROW_CHUNKS):
        hf_o[pl.ds(jc, tm_rows, stride=ROW_CHUNKS), :] = words[:, jc * LANES:(jc + 1) * LANES]

    logits = _dot3(hf, rw_ref[...]) + rb_ref[...]
    lane_e = lax.broadcasted_iota(jnp.int32, logits.shape, 1).astype(F32)
    vals = []
    idxs = []
    cur = logits
    for _ in range(TOP_K):
        m = jnp.max(cur, axis=1, keepdims=True)
        am = jnp.min(jnp.where(cur == m, lane_e, float(N_EXPERTS)), axis=1, keepdims=True)
        vals.append(m)
        idxs.append(am)
        cur = jnp.where(lane_e == am, -jnp.inf, cur)
    es = [jnp.exp(vv - vals[0]) for vv in vals]
    tot = es[0] + es[1] + es[2] + es[3]

    @pl.when(pl.program_id(0) == 0)
    def _():
        cnt_scr[...] = jnp.zeros(cnt_scr.shape, F32)

    tm = logits.shape[0]
    sel = [lane_e == idxs[kk] for kk in range(TOP_K)]
    onehot = jnp.where(jnp.logical_or(jnp.logical_or(sel[0], sel[1]), jnp.logical_or(sel[2], sel[3])), 1.0, 0.0)
    rr = lax.broadcasted_iota(jnp.int32, (tm, tm), 0)
    cc = lax.broadcasted_iota(jnp.int32, (tm, tm), 1)
    before = jnp.dot((cc < rr).astype(BF16), onehot.astype(BF16), preferred_element_type=F32) + cnt_scr[...]
    ranks = [jnp.sum(jnp.where(sel[kk], before, 0.0), axis=1, keepdims=True) for kk in range(TOP_K)]
    cnt_scr[...] = cnt_scr[...] + jnp.sum(onehot, axis=0, keepdims=True)
    cnt_o[...] = jnp.broadcast_to(cnt_scr[...], cnt_o.shape).astype(jnp.int32)

    lane = lax.broadcasted_iota(jnp.int32, ti_o.shape, 1)
    ti = jnp.zeros(ti_o.shape, F32)
    tg = jnp.zeros(tg_o.shape, F32)
    for kk in range(TOP_K):
        ti = jnp.where(lane == kk, idxs[kk], ti)
        ti = jnp.where(lane == TOP_K + kk, ranks[kk], ti)
        tg = jnp.where(lane == kk, es[kk] / tot, tg)
    ti_o[...] = ti.astype(jnp.int32)
    tg_o[...] = tg


def _mix_out(x_p, x_s, ys, bonus, g, yf_p, yf_s, ln, wo_bf, gf, rw, rb):
    n_p, n_s = x_p.shape[0], x_s.shape[0]
    n = n_p + n_s
    tm = _row_tile(n_p, n_s, 256)
    np_tiles = n_p // tm
    xp_spec, xs_spec = _two_group_specs(tm, D_MODEL, np_tiles)
    yfp_spec = pl.BlockSpec((N_PAIRS, tm, LANES), lambda i: (0, jnp.minimum(i, np_tiles - 1), 0))
    yfs_spec = pl.BlockSpec((N_PAIRS, tm, LANES), lambda i: (0, jnp.maximum(i - np_tiles, 0), 0))
    row = lambda w: pl.BlockSpec((tm, w), lambda i: (i, 0))
    pspec = pl.BlockSpec((N_PAIRS, tm, LANES), lambda i: (0, i, 0))
    full = lambda a: pl.BlockSpec(a.shape, lambda i: (0,) * a.ndim)
    return pl.pallas_call(
        functools.partial(_mix_out_kernel, np_tiles=n_p // tm),
        grid=(n // tm,),
        in_specs=[xp_spec, xs_spec, pspec, row(MIX_W), row(MIX_W), yfp_spec, yfs_spec, full(ln), full(wo_bf), full(gf),
                  full(rw), full(rb)],
        out_specs=[row(D_MODEL), pl.BlockSpec((tm * ROW_CHUNKS, LANES), lambda i: (i, 0)), row(LANES), row(LANES),
                   pl.BlockSpec((8, N_EXPERTS), lambda i: (0, 0))],
        out_shape=[jax.ShapeDtypeStruct((n, D_MODEL), F32), jax.ShapeDtypeStruct((n * ROW_CHUNKS, LANES), jnp.uint32),
                   jax.ShapeDtypeStruct((n, LANES), jnp.int32), jax.ShapeDtypeStruct((n, LANES), F32),
                   jax.ShapeDtypeStruct((8, N_EXPERTS), jnp.int32)],
        scratch_shapes=[pltpu.VMEM((1, N_EXPERTS), F32)],
        compiler_params=_cparams(1),
        name="mix_out_router",
    )(x_p, x_s, ys, bonus, g, yf_p, yf_s, ln, wo_bf, gf, rw, rb)


DISPATCH_TOKENS = 256


def _moe_dispatch_kernel(pos_ref, pad_base_ref, pad_cnt_ref, x_ref, xs_hbm, sem):
    i = pl.program_id(0)
    tile_rows = x_ref.shape[0]

    def row_copy(src_tok, dst_row, slot):
        return pltpu.make_async_copy(
            x_ref.at[pl.ds(pl.multiple_of(src_tok * ROW_CHUNKS, ROW_CHUNKS), ROW_CHUNKS)],
            xs_hbm.at[pl.ds(pl.multiple_of(dst_row * ROW_CHUNKS, ROW_CHUNKS), ROW_CHUNKS)], sem.at[slot])

    @pl.when(i == 0)
    def _():
        pieces = [1 << b for b in reversed(range((MOE_ROWS - 1).bit_length()))]
        assert max(pieces) <= DISPATCH_TOKENS

        def pad_expert(e, carry):
            cnt = pad_cnt_ref[e]
            for starting in (True, False):
                off = pad_base_ref[e]
                for piece in pieces:
                    cp = pltpu.make_async_copy(
                        x_ref.at[pl.ds(0, piece * ROW_CHUNKS)],
                        xs_hbm.at[pl.ds(pl.multiple_of(off * ROW_CHUNKS, ROW_CHUNKS), piece * ROW_CHUNKS)], sem.at[1])

                    @pl.when((cnt & piece) != 0)
                    def _():
                        if starting:
                            cp.start()
                        else:
                            cp.wait()
                    off = off + (cnt & piece)
            return carry

        lax.fori_loop(0, N_EXPERTS, pad_expert, 0)

        tail_row0 = pad_base_ref[N_EXPERTS]
        for starting in (True, False):
            def tail_block(b, carry):
                for part in range(MOE_ROWS // DISPATCH_TOKENS):
                    row = tail_row0 + b * MOE_ROWS + part * DISPATCH_TOKENS
                    cp = pltpu.make_async_copy(
                        x_ref, xs_hbm.at[pl.ds(pl.multiple_of(row * ROW_CHUNKS, ROW_CHUNKS), tile_rows)], sem.at[1])
                    if starting:
                        cp.start()
                    else:
                        cp.wait()
                return carry
            lax.fori_loop(0, pad_cnt_ref[N_EXPERTS], tail_block, 0)

    def issue(t, carry):
        for kk in range(TOP_K):
            row_copy(t, pos_ref[(i * DISPATCH_TOKENS + t) * TOP_K + kk], 0).start()
        return carry
    lax.fori_loop(0, DISPATCH_TOKENS, issue, 0, unroll=4)

    for _ in range(TOP_K):
        pltpu.make_async_copy(x_ref, xs_hbm.at[pl.ds(0, tile_rows)], sem.at[0]).wait()


def _moe_dispatch(pos, pad_base, pad_cnt, hf, n_blocks):
    n_tok = pos.shape[0] // TOP_K
    assert n_tok % DISPATCH_TOKENS == 0
    tile_rows = DISPATCH_TOKENS * ROW_CHUNKS
    grid_spec = pltpu.PrefetchScalarGridSpec(
        num_scalar_prefetch=3,
        grid=(n_tok // DISPATCH_TOKENS,),
        in_specs=[pl.BlockSpec((tile_rows, LANES), lambda i, *pf: (i, 0))],
        out_specs=pl.BlockSpec(memory_space=pl.ANY),
        scratch_shapes=[pltpu.SemaphoreType.DMA((2,))],
    )
    return pl.pallas_call(
        _moe_dispatch_kernel,
        grid_spec=grid_spec,
        out_shape=jax.ShapeDtypeStruct((n_blocks * MOE_ROWS * ROW_CHUNKS, LANES), jnp.uint32),
        compiler_params=_cparams(1),
        name="moe_dispatch",
    )(pos, pad_base, pad_cnt, hf)


def _unpack_rows(x_ref):
    rows = x_ref.shape[0] // ROW_CHUNKS
    lo = []
    hi = []
    for jc in range(ROW_CHUNKS):
        words = x_ref[pl.ds(jc, rows, stride=ROW_CHUNKS), :]
        lo.append(pltpu.bitcast(words << 16, F32).astype(BF16))
        hi.append(pltpu.bitcast(words & jnp.uint32(0xFFFF0000), F32).astype(BF16))
    return jnp.concatenate(lo + hi, axis=1)


STEP_RUN, STEP_NEW_WEIGHTS, STEP_ZERO = 0, 1, 2
N_SCHED = 10


def _weight_group_step(s, sf, sg, ng, copies, cast):
    @pl.when(sf[s] == STEP_NEW_WEIGHTS)
    def _():
        g = sg[s]
        slot = g % 2

        @pl.when(g == 0)
        def _():
            for c in copies(g, slot):
                c.start()

        for c in copies(g, slot):
            c.wait()

        @pl.when(g + 1 < ng[0])
        def _():
            for c in copies(g + 1, 1 - slot):
                c.start()

        cast(slot)


def _moe_gu_kernel(se, sw, sb, sj, sf, sg, ge, gw, ng, sx, x_ref, wgu_hbm, bg_ref, bu_ref, o_ref,
                   wg_bf, wu_bf, wg_stage, wu_stage, sem):
    s = pl.program_id(0)

    def copies(g, slot):
        col = pl.multiple_of(gw[g] * MOE_TN, MOE_TN)
        return [pltpu.make_async_copy(wgu_hbm.at[ge[g], :, pl.ds(col, MOE_TN)], wg_stage.at[slot], sem.at[slot]),
                pltpu.make_async_copy(wgu_hbm.at[ge[g], :, pl.ds(D_EXPERT + col, MOE_TN)], wu_stage.at[slot],
                                      sem.at[slot])]

    def cast(slot):
        wg_bf[...] = wg_stage[slot].astype(BF16)
        wu_bf[...] = wu_stage[slot].astype(BF16)

    _weight_group_step(s, sf, sg, ng, copies, cast)

    @pl.when(sf[s] != STEP_ZERO)
    def _():
        x = _unpack_rows(x_ref)
        for c0 in range(0, MOE_TN, MXU_N):
            cols = slice(c0, c0 + MXU_N)
            g = jnp.dot(x, wg_bf[:, cols], preferred_element_type=F32) + bg_ref[0][:, cols]
            u = jnp.dot(x, wu_bf[:, cols], preferred_element_type=F32) + bu_ref[0][:, cols]
            g = jnp.minimum(g, SWIGLU_LIMIT)
            u = jnp.clip(u, -SWIGLU_LIMIT, SWIGLU_LIMIT)
            o_ref[:, cols] = ((u + 1.0) * (g * jax.nn.sigmoid(SWIGLU_ALPHA * g))).astype(BF16)

    @pl.when(sf[s] == STEP_ZERO)
    def _():
        o_ref[...] = jnp.zeros(o_ref.shape, o_ref.dtype)


def _moe_gate_up(sched, xs, w_gu, b_gu, n_blocks):
    nt = D_EXPERT // MOE_TN
    n_steps = nt * n_blocks
    bspec = lambda off: pl.BlockSpec((1, 1, MOE_TN), lambda s, *pf: (pf[0][s], 0, off + pf[1][s]))
    grid_spec = pltpu.PrefetchScalarGridSpec(
        num_scalar_prefetch=N_SCHED,
        grid=(n_steps,),
        in_specs=[pl.BlockSpec((MOE_ROWS * ROW_CHUNKS, LANES), lambda s, *pf: (pf[9][s], 0)),
                  pl.BlockSpec(memory_space=pl.ANY), bspec(0), bspec(nt)],
        out_specs=pl.BlockSpec((MOE_ROWS, MOE_TN), lambda s, *pf: (pf[2][s], pf[3][s])),
        scratch_shapes=[pltpu.VMEM((D_MODEL, MOE_TN), BF16), pltpu.VMEM((D_MODEL, MOE_TN), BF16),
                        pltpu.VMEM((2, D_MODEL, MOE_TN), F32), pltpu.VMEM((2, D_MODEL, MOE_TN), F32),
                        pltpu.SemaphoreType.DMA((2,))],
    )
    return pl.pallas_call(
        _moe_gu_kernel,
        grid_spec=grid_spec,
        out_shape=jax.ShapeDtypeStruct((n_blocks * MOE_ROWS, D_EXPERT), BF16),
        compiler_params=_cparams(1),
        name="moe_gate_up",
    )(*sched, xs, w_gu, b_gu, b_gu)


def _moe_dn_kernel(se, sw, sb, sj, sf, sg, ge, gw, ng, sx, h_ref, wd_hbm, bd_ref, o_ref, wd_bf, wd_stage, sem):
    s = pl.program_id(0)

    def copies(g, slot):
        col = pl.multiple_of(gw[g] * MOE_TN, MOE_TN)
        return [pltpu.make_async_copy(wd_hbm.at[ge[g], :, pl.ds(col, MOE_TN)], wd_stage.at[slot], sem.at[slot])]

    def cast(slot):
        wd_bf[...] = wd_stage[slot].astype(BF16)

    _weight_group_step(s, sf, sg, ng, copies, cast)

    @pl.when(sf[s] != STEP_ZERO)
    def _():
        h = h_ref[...]
        for c0 in range(0, MOE_TN, MXU_N):
            cols = slice(c0, c0 + MXU_N)
            o_ref[:, cols] = jnp.dot(h, wd_bf[:, cols], preferred_element_type=F32) + bd_ref[0][:, cols]

    @pl.when(sf[s] == STEP_ZERO)
    def _():
        o_ref[...] = jnp.zeros(o_ref.shape, o_ref.dtype)


def _moe_down(sched, hid, w_dn, b_dn, n_blocks):
    nt = D_MODEL // MOE_TN
    n_steps = nt * n_blocks
    grid_spec = pltpu.PrefetchScalarGridSpec(
        num_scalar_prefetch=N_SCHED,
        grid=(n_steps,),
        in_specs=[pl.BlockSpec((MOE_ROWS, D_EXPERT), lambda s, *pf: (pf[2][s], 0)),
                  pl.BlockSpec(memory_space=pl.ANY),
                  pl.BlockSpec((1, 1, MOE_TN), lambda s, *pf: (pf[0][s], 0, pf[1][s]))],
        out_specs=pl.BlockSpec((MOE_ROWS, MOE_TN), lambda s, *pf: (pf[2][s], pf[3][s])),
        scratch_shapes=[pltpu.VMEM((D_EXPERT, MOE_TN), BF16), pltpu.VMEM((2, D_EXPERT, MOE_TN), F32),
                        pltpu.SemaphoreType.DMA((2,))],
    )
    return pl.pallas_call(
        _moe_dn_kernel,
        grid_spec=grid_spec,
        out_shape=jax.ShapeDtypeStruct((n_blocks * MOE_ROWS, D_MODEL), F32),
        compiler_params=_cparams(1),
        name="moe_down",
    )(*sched, hid, w_dn, b_dn)


def _moe_combine_kernel(pos_ref, ys_hbm, x1_ref, tg_ref, o_ref, buf, sem):
    i = pl.program_id(0)
    nb = pl.num_programs(0)
    tm = o_ref.shape[0]

    def row_copy(blk, slot, r, kk):
        src = pos_ref[(blk * tm + r) * TOP_K + kk]
        return pltpu.make_async_copy(ys_hbm.at[pl.ds(src, 1)], buf.at[slot, kk, pl.ds(r, 1)], sem.at[slot])

    def issue(blk, slot):
        def body(r, carry):
            for kk in range(TOP_K):
                row_copy(blk, slot, r, kk).start()
            return carry
        lax.fori_loop(0, tm, body, 0, unroll=4)

    @pl.when(i == 0)
    def _():
        issue(0, 0)

    @pl.when(i + 1 < nb)
    def _():
        issue(i + 1, (i + 1) % 2)

    slot = i % 2

    for kk in range(TOP_K):
        pltpu.make_async_copy(ys_hbm.at[pl.ds(0, tm)], buf.at[slot, kk], sem.at[slot]).wait()

    tg = tg_ref[...]
    acc = x1_ref[...]
    for kk in range(TOP_K):
        acc = acc + tg[:, kk:kk + 1] * buf[slot, kk]
    o_ref[...] = acc


def _moe_combine(pos, ys, x1, tg):
    n = x1.shape[0]
    tm = min(128, n)
    grid_spec = pltpu.PrefetchScalarGridSpec(
        num_scalar_prefetch=1,
        grid=(n // tm,),
        in_specs=[pl.BlockSpec(memory_space=pl.ANY),
                  pl.BlockSpec((tm, D_MODEL), lambda i, pos: (i, 0)),
                  pl.BlockSpec((tm, LANES), lambda i, pos: (i, 0))],
        out_specs=pl.BlockSpec((tm, D_MODEL), lambda i, pos: (i, 0)),
        scratch_shapes=[pltpu.VMEM((2, TOP_K, tm, D_MODEL), F32), pltpu.SemaphoreType.DMA((2,))],
    )
    return pl.pallas_call(
        _moe_combine_kernel,
        grid_spec=grid_spec,
        out_shape=jax.ShapeDtypeStruct((n, D_MODEL), F32),
        compiler_params=_cparams(1),
        name="moe_combine",
    )(pos, ys, x1, tg)


def _moe_schedule(top_idx, rank, counts, n_blocks):
    n = top_idx.shape[0]
    n_rows = n * TOP_K
    nt = D_EXPERT // MOE_TN
    flat_e = top_idx.reshape(n_rows)
    nb_e = (counts + MOE_ROWS - 1) // MOE_ROWS
    blk_end = jnp.cumsum(nb_e)
    blk_start = blk_end - nb_e
    pos = (blk_start[flat_e] * MOE_ROWS + rank.reshape(n_rows)).astype(jnp.int32)
    used_blocks = blk_end[-1]
    pad_base = jnp.concatenate([blk_start * MOE_ROWS + counts, (used_blocks * MOE_ROWS).reshape(1)]).astype(jnp.int32)
    pad_cnt = jnp.concatenate([nb_e * MOE_ROWS - counts, (n_blocks - used_blocks).reshape(1)]).astype(jnp.int32)
    used = blk_end[-1]
    s = jnp.arange(nt * n_blocks, dtype=jnp.int32)
    live = s < nt * used
    s_eff = jnp.maximum(jnp.minimum(s, nt * used - 1), 0)
    e = jnp.minimum(jnp.sum((s_eff[:, None] >= nt * blk_end[None, :]).astype(jnp.int32), axis=1), N_EXPERTS - 1)
    local = s_eff - nt * blk_start[e]
    nbe = jnp.maximum(nb_e[e], 1)
    sw = (local // nbe).astype(jnp.int32)
    tail = s - nt * used
    sb = jnp.where(live, blk_start[e] + local % nbe, used + tail // nt).astype(jnp.int32)
    sj = jnp.where(live, sw, tail % nt).astype(jnp.int32)
    sf = jnp.where(live, jnp.where(local % nbe == 0, STEP_NEW_WEIGHTS, STEP_RUN), STEP_ZERO).astype(jnp.int32)
    active = nb_e > 0
    rank_e = jnp.cumsum(active.astype(jnp.int32)) - 1
    n_active = rank_e[-1] + 1
    expert_of_rank = jnp.zeros((N_EXPERTS,), jnp.int32).at[jnp.where(active, rank_e, N_EXPERTS)].set(
        jnp.arange(N_EXPERTS, dtype=jnp.int32), mode="drop")
    sg = (rank_e[e] * nt + sw).astype(jnp.int32)
    gidx = jnp.arange(N_EXPERTS * nt, dtype=jnp.int32)
    ge = expert_of_rank[jnp.minimum(gidx // nt, n_active - 1)]
    gw = gidx % nt
    ng = (n_active * nt).astype(jnp.int32).reshape(1)
    sx = jnp.where(live, sb, 0).astype(jnp.int32)
    return pos, pad_base, pad_cnt, (e, sw, sb, sj, sf, sg, ge, gw, ng, sx)


def _ple_kernel(x_ref, pp_ref, ps_ref, g_ref, wg_ref, wp_ref, op_ref, os_ref, *, np_tiles):
    i = pl.program_id(0)
    x = x_ref[...]
    h = _rms(x, g_ref[...]).astype(BF16)
    gate = jax.nn.sigmoid(jnp.dot(h, wg_ref[...], preferred_element_type=F32))
    p = jnp.where(i < np_tiles, pp_ref[...], ps_ref[...])
    y = x + gate * jnp.dot(p.astype(BF16), wp_ref[...], preferred_element_type=F32)

    @pl.when(i < np_tiles)
    def _():
        op_ref[...] = y

    @pl.when(i >= np_tiles)
    def _():
        os_ref[...] = y


def _ple(x, p_p, p_s, g, wg_bf, wp_bf):
    n_p, n_s = p_p.shape[0], p_s.shape[0]
    n = n_p + n_s
    tm = _row_tile(n_p, n_s, 256)
    np_tiles = n_p // tm
    full = lambda a: pl.BlockSpec(a.shape, lambda i: (0,) * a.ndim)
    pp_spec, ps_spec = _two_group_specs(tm, PLE_DIM, np_tiles)
    op_spec, os_spec = _two_group_specs(tm, D_MODEL, np_tiles)
    return pl.pallas_call(
        functools.partial(_ple_kernel, np_tiles=np_tiles),
        grid=(n // tm,),
        in_specs=[pl.BlockSpec((tm, D_MODEL), lambda i: (i, 0)), pp_spec, ps_spec,
                  full(g), full(wg_bf), full(wp_bf)],
        out_specs=[op_spec, os_spec],
        out_shape=[jax.ShapeDtypeStruct((n_p, D_MODEL), F32), jax.ShapeDtypeStruct((n_s, D_MODEL), F32)],
        compiler_params=_cparams(1),
        name="ple_gate",
    )(x, p_p, p_s, g, wg_bf, wp_bf)


def _pairs_from_state(s):
    b = s.shape[0]
    st = jnp.swapaxes(s, 2, 3).reshape(b, N_PAIRS, 2, HEAD_DIM, HEAD_DIM)
    z = jnp.zeros((b, N_PAIRS, HEAD_DIM, HEAD_DIM), s.dtype)
    top = jnp.concatenate([st[:, :, 0], z], axis=3)
    bot = jnp.concatenate([z, st[:, :, 1]], axis=3)
    return jnp.concatenate([top, bot], axis=2)


def _state_from_pairs(sp):
    b = sp.shape[0]
    st = jnp.stack([sp[:, :, :HEAD_DIM, :HEAD_DIM], sp[:, :, HEAD_DIM:, HEAD_DIM:]], axis=2)
    return jnp.swapaxes(st.reshape(b, N_HEADS, HEAD_DIM, HEAD_DIM), 2, 3)


def _layer(x_p, x_s, cache_k, cache_v, cache_lf, state, shift, p_p, p_s, lw):
    t_p = x_p.shape[1]
    b_s, t_s, _ = x_s.shape
    n_p = x_p.shape[0] * t_p
    n_s = b_s * t_s
    n = n_p + n_s
    past = cache_k.shape[1]
    xp2 = x_p.reshape(n_p, D_MODEL)
    xs2 = x_s.reshape(n_s, D_MODEL)

    w_in_bf = jnp.pad(lw["w_in"], ((0, 0), (0, Z_PAD - lw["w_in"].shape[1]))).astype(BF16)
    z = _in_proj(xp2, xs2, lw["norm_mix_g"], w_in_bf)

    assert t_s == CHUNK, "each sample stream contributes exactly one scan chunk"
    vecs = jnp.zeros((8, MIX_W), F32)
    vecs = vecs.at[0].set(lw["rwkv_w0"]).at[1].set(lw["rwkv_a0"]).at[2].set(lw["rwkv_kk"])
    vecs = vecs.at[3].set(lw["rwkv_ka"]).at[4].set(lw["rwkv_rk"].reshape(MIX_W))
    wl = jnp.zeros((256, 3 * MIX_W), F32)
    wl = wl.at[0:64, 0:MIX_W].set(lw["rwkv_w2"]).at[64:128, MIX_W:2 * MIX_W].set(lw["rwkv_a2"])
    wl = wl.at[128:256, 2 * MIX_W:].set(lw["rwkv_g2"])
    r, w, k, kk, bb, v_pm, g, bonus = _rwkv_pre(z, shift, lw["rwkv_mu"].reshape(1, RWKV_PROJ), vecs, wl, n_p)
    s0 = jnp.concatenate([jnp.zeros((1, N_PAIRS, LANES, LANES), F32), _pairs_from_state(state.astype(F32))], axis=0)
    y_scan, s_out = _rwkv_scan(r, w, k, kk, bb, v_pm, s0, n_p)
    s_new = _state_from_pairs(s_out)
    shift_new_p = z[n_p - 1:n_p, :RWKV_PROJ].reshape(1, 1, RWKV_PROJ)
    shift_new_s = z[n_p:, :RWKV_PROJ].reshape(b_s, t_s, RWKV_PROJ)[:, -1:, :]

    fvecs = jnp.zeros((8, MIX_W), F32)
    fvecs = fvecs.at[0].set(jnp.tile(lw["fox_q_g"], N_HEADS)).at[1].set(jnp.tile(lw["fox_k_g"], N_HEADS))
    bf = jnp.zeros((1, LANES), F32).at[0, :N_HEADS].set(lw["fox_b_f"])
    q_pm, k_pm, vv_pm, k_new_p, v_new_p, k_new_s, v_new_s, lf = _fox_pre(z, fvecs, bf, n_p)
    lf = lf[:, :N_HEADS]
    lf_p = lf[:n_p].T.reshape(1, N_HEADS, n_p)
    c_p = _cumsum_lanes(lf_p).reshape(N_PAIRS, 2, n_p)
    yf_p = _attn_prompt(q_pm, k_pm, vv_pm, c_p, n_p)
    lf_s = jnp.swapaxes(lf[n_p:].reshape(b_s, t_s, N_HEADS), 1, 2)
    lf_all = jnp.concatenate([jnp.swapaxes(cache_lf.astype(F32), 1, 2), lf_s], axis=2)
    pad = (-lf_all.shape[2]) % LANES
    c_all = _cumsum_lanes(jnp.pad(lf_all, ((0, 0), (0, 0), (0, pad))))
    yf_s = _attn_sample(q_pm, k_pm, vv_pm, cache_k.reshape(b_s, past, MIX_W), cache_v.reshape(b_s, past, MIX_W),
                        c_all[:, :, :past], c_all[:, :, past:past + t_s], n_p)

    ln = jnp.stack([lw["rwkv_ln_g"], lw["rwkv_ln_b"]])
    x1, hf, ti, tg, cnt = _mix_out(xp2, xs2, y_scan, bonus, g, yf_p, yf_s, ln, lw["w_out"].astype(BF16),
                                   lw["norm_ffn_g"].reshape(1, D_MODEL), lw["router_w"],
                                   lw["router_b"].reshape(1, N_EXPERTS))

    n_blocks = n * TOP_K // MOE_ROWS + N_EXPERTS
    pos, pad_base, pad_cnt, sched = _moe_schedule(ti[:, :TOP_K], ti[:, TOP_K:2 * TOP_K], cnt[0], n_blocks)
    xs = _moe_dispatch(pos, pad_base, pad_cnt, hf, n_blocks)
    hid = _moe_gate_up(sched, xs, lw["expert_w_gu"], lw["expert_b_gu"].reshape(N_EXPERTS, 1, 2 * D_EXPERT), n_blocks)
    ys = _moe_down(sched, hid, lw["expert_w_down"], lw["expert_b_down"].reshape(N_EXPERTS, 1, D_MODEL), n_blocks)
    x2 = _moe_combine(pos, ys, x1, tg)

    y_p, y_s = _ple(x2, p_p, p_s, lw["ple_norm_g"].reshape(1, D_MODEL), lw["ple_w_gate"].astype(BF16),
                    lw["ple_w_proj"].astype(BF16))

    heads = lambda a, bsz, t: a.reshape(bsz, t, N_HEADS, HEAD_DIM)
    out_p = (y_p.reshape(x_p.shape), heads(k_new_p, 1, n_p), heads(v_new_p, 1, n_p),
             lf[:n_p].reshape(1, n_p, N_HEADS), s_new[:1], shift_new_p)
    out_s = (y_s.reshape(x_s.shape), heads(k_new_s, b_s, t_s), heads(v_new_s, b_s, t_s),
             lf[n_p:].reshape(b_s, t_s, N_HEADS), s_new[1:], shift_new_s)
    return out_p, out_s


def kernel(x_prompt, x_sample, cache_fox_k, cache_fox_v, cache_fox_logf, state_rwkv, state_rwkv_shift, p_prompt, p_sample, norm_mix_g, w_in, rwkv_mu, rwkv_w0, rwkv_w2, rwkv_a0, rwkv_a2, rwkv_g2, rwkv_kk, rwkv_ka, rwkv_rk, rwkv_ln_g, rwkv_ln_b, fox_q_g, fox_k_g, fox_b_f, w_out, norm_ffn_g, router_w, router_b, expert_w_gu, expert_b_gu, expert_w_down, expert_b_down, ple_norm_g, ple_w_gate, ple_w_proj):
    assert x_prompt.shape[0] == 1 and w_in.shape[0] == 1, "one prompt stream, one layer"
    lw = dict(norm_mix_g=norm_mix_g[0], w_in=w_in[0], rwkv_mu=rwkv_mu[0], rwkv_w0=rwkv_w0[0], rwkv_w2=rwkv_w2[0],
              rwkv_a0=rwkv_a0[0], rwkv_a2=rwkv_a2[0], rwkv_g2=rwkv_g2[0], rwkv_kk=rwkv_kk[0], rwkv_ka=rwkv_ka[0],
              rwkv_rk=rwkv_rk[0], rwkv_ln_g=rwkv_ln_g[0], rwkv_ln_b=rwkv_ln_b[0], fox_q_g=fox_q_g[0],
              fox_k_g=fox_k_g[0], fox_b_f=fox_b_f[0], w_out=w_out[0], norm_ffn_g=norm_ffn_g[0],
              router_w=router_w[0], router_b=router_b[0], expert_w_gu=expert_w_gu[0], expert_b_gu=expert_b_gu[0],
              expert_w_down=expert_w_down[0], expert_b_down=expert_b_down[0], ple_norm_g=ple_norm_g[0],
              ple_w_gate=ple_w_gate[0], ple_w_proj=ple_w_proj[0])
    n_p = x_prompt.shape[1]
    (y_p, k_p, v_p, lf_p, s_p, sh_p), (y_s, k_s, v_s, lf_s, s_s, sh_s) = _layer(
        x_prompt, x_sample, cache_fox_k[0], cache_fox_v[0], cache_fox_logf[0], state_rwkv[0],
        state_rwkv_shift[0], p_prompt[0].reshape(n_p, PLE_DIM), p_sample[0].reshape(-1, PLE_DIM), lw)
    add = lambda a: a[None]
    return (y_p, y_s, add(k_p), add(v_p), add(lf_p), add(s_p), add(sh_p),
            add(k_s), add(v_s), add(lf_s), add(s_s), add(sh_s))
```

```python
import functools

import numpy as np
import jax
import jax.numpy as jnp
from jax import lax
from jax.experimental import pallas as pl
from jax.experimental.pallas import tpu as pltpu

F32 = jnp.float32
BF16 = jnp.bfloat16
HI = lax.Precision.HIGHEST

D_MODEL = 2048
HEAD_DIM = 64
N_HEADS = 16
N_PAIRS = N_HEADS // 2
MIX_W = N_HEADS * HEAD_DIM
CHUNK = 64
RWKV_PROJ = 3 * MIX_W + 64 + 64 + 128
FOX_PROJ = 3 * MIX_W + N_HEADS
Z_HALF = RWKV_PROJ
Z_PAD = 2 * Z_HALF
N_EXPERTS = 32
TOP_K = 4
D_EXPERT = 2048
SWIGLU_LIMIT = 7.0
SWIGLU_ALPHA = 1.702
PLE_DIM = 256
RMS_EPS = 1e-6
GN_EPS = 64e-5
L2_EPS = 1e-12
NEG_BIG = -1e30
LOG2E = 1.4426950408889634

LANES = 128
MXU_N = 256
MOE_ROWS = 512
MOE_TN = 1024
ROW_CHUNKS = D_MODEL // 2 // LANES
VMEM_LIMIT = 52 * 1024 * 1024


def _cparams(n_axes, vmem=VMEM_LIMIT):
    return pltpu.CompilerParams(dimension_semantics=("arbitrary",) * n_axes, vmem_limit_bytes=vmem)


def _head_sum(x):
    r = lax.broadcasted_iota(jnp.int32, (LANES, LANES), 0) // HEAD_DIM
    c = lax.broadcasted_iota(jnp.int32, (LANES, LANES), 1) // HEAD_DIM
    bd = jnp.where(r == c, 1.0, 0.0).astype(BF16)
    hi = x.astype(BF16)
    lo = (x - hi.astype(F32)).astype(BF16)
    parts = []
    for i in range(x.shape[1] // LANES):
        sl = slice(i * LANES, (i + 1) * LANES)
        parts.append(jnp.dot(hi[:, sl], bd, preferred_element_type=F32)
                     + jnp.dot(lo[:, sl], bd, preferred_element_type=F32))
    return parts[0] if len(parts) == 1 else jnp.concatenate(parts, axis=1)


def _log_sigmoid(x):
    return jnp.minimum(x, 0.0) - jnp.log1p(jnp.exp(-jnp.abs(x)))


def _rms(x, g):
    ms = jnp.mean(x * x, axis=-1, keepdims=True)
    return x * lax.rsqrt(ms + RMS_EPS) * g


def _row_tile(n_p, n_s, pref):
    return pref if (n_p % pref == 0 and n_s % pref == 0) else 128


def _two_group_specs(tm, width, np_tiles, n_grid_axes=1):
    if n_grid_axes == 1:
        return (pl.BlockSpec((tm, width), lambda i: (jnp.minimum(i, np_tiles - 1), 0)),
                pl.BlockSpec((tm, width), lambda i: (jnp.maximum(i - np_tiles, 0), 0)))
    return (pl.BlockSpec((tm, width), lambda i, j: (jnp.minimum(i, np_tiles - 1), 0)),
            pl.BlockSpec((tm, width), lambda i, j: (jnp.maximum(i - np_tiles, 0), 0)))


def _inproj_kernel(xp_ref, xs_ref, g_ref, w_ref, o_ref, h_scr, *, np_tiles):
    i = pl.program_id(0)
    first = pl.program_id(1) == 0

    @pl.when(jnp.logical_and(first, i < np_tiles))
    def _():
        h_scr[...] = _rms(xp_ref[...], g_ref[...]).astype(BF16)

    @pl.when(jnp.logical_and(first, i >= np_tiles))
    def _():
        h_scr[...] = _rms(xs_ref[...], g_ref[...]).astype(BF16)

    o_ref[...] = jnp.dot(h_scr[...], w_ref[...], preferred_element_type=F32)


def _in_proj(x_p, x_s, g, w_bf):
    n_p, n_s = x_p.shape[0], x_s.shape[0]
    n = n_p + n_s
    tm = _row_tile(n_p, n_s, 512)
    tn = Z_PAD // 4
    xp_spec, xs_spec = _two_group_specs(tm, D_MODEL, n_p // tm, 2)
    return pl.pallas_call(
        functools.partial(_inproj_kernel, np_tiles=n_p // tm),
        grid=(n // tm, Z_PAD // tn),
        in_specs=[xp_spec, xs_spec,
                  pl.BlockSpec((1, D_MODEL), lambda i, j: (0, 0)),
                  pl.BlockSpec((D_MODEL, tn), lambda i, j: (0, j))],
        out_specs=pl.BlockSpec((tm, tn), lambda i, j: (i, j)),
        out_shape=jax.ShapeDtypeStruct((n, Z_PAD), F32),
        scratch_shapes=[pltpu.VMEM((tm, D_MODEL), BF16)],
        compiler_params=_cparams(2),
        name="in_proj",
    )(x_p, x_s, g.reshape(1, D_MODEL), w_bf)


def _rwkv_pre_kernel(z_ref, prev_ref, mu_ref, vec_ref, wl_ref,
                     r_o, w_o, k_o, kk_o, b_o, v_o, g_o, bn_o, carry, *, n_prompt_tiles):
    i = pl.program_id(0)
    z = z_ref[...]
    tm = z.shape[0]

    @pl.when(i == 0)
    def _():
        carry[...] = prev_ref[0, 0:1, :]

    rolled = pltpu.roll(z, 1, axis=0)
    row = lax.broadcasted_iota(jnp.int32, z.shape, 0)
    shifted = jnp.where(row == 0, carry[...], rolled)
    is_sample = i >= n_prompt_tiles
    for c in range(tm // CHUNK):
        shifted = jnp.where(jnp.logical_and(is_sample, row == c * CHUNK), prev_ref[0, c:c + 1, :], shifted)
    carry[...] = z[tm - 1:tm, :]
    zs = z + mu_ref[...] * (shifted - z)

    r = zs[:, 0:MIX_W]
    k = zs[:, MIX_W:2 * MIX_W]
    v = zs[:, 2 * MIX_W:3 * MIX_W]
    lo = zs[:, 3 * MIX_W:RWKV_PROJ]
    lane = lax.broadcasted_iota(jnp.int32, lo.shape, 1)
    f = jnp.where(lane < 64, jnp.tanh(lo), jnp.where(lane < 128, lo, jax.nn.sigmoid(lo)))
    lora = _dot3(f, wl_ref[...])
    w_pre = vec_ref[0:1, :] + lora[:, 0:MIX_W]
    log_decay = -jnp.exp(_log_sigmoid(w_pre) - 0.5)
    a = jax.nn.sigmoid(vec_ref[1:2, :] + lora[:, MIX_W:2 * MIX_W])
    g = lora[:, 2 * MIX_W:3 * MIX_W]
    kk = k * vec_ref[2:3, :]
    kk = kk * lax.rsqrt(_head_sum(kk * kk) + L2_EPS)
    k_mod = k * (1.0 + (a - 1.0) * vec_ref[3:4, :])
    bonus = _head_sum(r * k_mod * vec_ref[4:5, :]) * v
    kka = kk * a

    g_o[...] = g
    bn_o[...] = bonus
    for p in range(N_PAIRS):
        sl = slice(p * LANES, (p + 1) * LANES)
        r_o[p] = r[:, sl]
        w_o[p] = log_decay[:, sl]
        k_o[p] = k_mod[:, sl]
        kk_o[p] = kk[:, sl]
        b_o[p] = kka[:, sl]
        v_o[p] = v[:, sl]


def _rwkv_pre(z, shift, mu, vecs, wl, n_prompt):
    n = z.shape[0]
    n_s = n - n_prompt
    tm = _row_tile(n_prompt, n_s, 256)
    if tm % CHUNK:
        tm = CHUNK
    cpt = tm // CHUNK
    nt = n // tm
    n_prompt_tiles = n_prompt // tm
    prev = jnp.concatenate([jnp.zeros((1, cpt, RWKV_PROJ), F32),
                            shift.astype(F32).reshape(n_s // tm, cpt, RWKV_PROJ)], axis=0)
    tok = jax.ShapeDtypeStruct((n, MIX_W), F32)
    tspec = pl.BlockSpec((tm, MIX_W), lambda i: (i, 0))
    pm = jax.ShapeDtypeStruct((N_PAIRS, n, LANES), F32)
    pspec = pl.BlockSpec((N_PAIRS, tm, LANES), lambda i: (0, i, 0))
    return pl.pallas_call(
        functools.partial(_rwkv_pre_kernel, n_prompt_tiles=n_prompt_tiles),
        grid=(nt,),
        in_specs=[pl.BlockSpec((tm, Z_HALF), lambda i: (i, 0)),
                  pl.BlockSpec((1, cpt, RWKV_PROJ), lambda i: (jnp.maximum(i - (n_prompt_tiles - 1), 0), 0, 0)),
                  pl.BlockSpec((1, RWKV_PROJ), lambda i: (0, 0)),
                  pl.BlockSpec((8, MIX_W), lambda i: (0, 0)),
                  pl.BlockSpec((256, 3 * MIX_W), lambda i: (0, 0))],
        out_specs=[pspec, pspec, pspec, pspec, pspec, pspec, tspec, tspec],
        out_shape=[pm, pm, pm, pm, pm, pm, tok, tok],
        scratch_shapes=[pltpu.VMEM((1, RWKV_PROJ), F32)],
        compiler_params=_cparams(1),
        name="rwkv_pre",
    )(z, prev, mu, vecs, wl)


def _dot(a, b):
    return jnp.dot(a, b, precision=HI, preferred_element_type=F32)


def _bdot(a, b):
    return jnp.dot(a, b, preferred_element_type=F32)


def _split_bf16(x):
    hi = x.astype(BF16)
    return hi, (x - hi.astype(F32)).astype(BF16)


def _dot3(a, b):
    a_hi, a_lo = _split_bf16(a)
    b_hi, b_lo = _split_bf16(b)
    return _bdot(a_hi, b_hi) + (_bdot(a_hi, b_lo) + _bdot(a_lo, b_hi))


def _pair_rows(x):
    lo_half = lax.broadcasted_iota(jnp.int32, x.shape, 1) < HEAD_DIM
    return jnp.concatenate([jnp.where(lo_half, x, 0.0), jnp.where(lo_half, 0.0, x)], axis=0)


def _scan_chunk(P, r, lw, k, kk, b, v):
    c2 = 2 * CHUNK
    i = lax.broadcasted_iota(jnp.int32, (c2, c2), 0)
    j = lax.broadcasted_iota(jnp.int32, (c2, c2), 1)
    ti = lax.broadcasted_iota(jnp.int32, (CHUNK, CHUNK), 0)
    tj = lax.broadcasted_iota(jnp.int32, (CHUNK, CHUNK), 1)
    cl = _dot((tj <= ti).astype(F32), lw)
    yield
    g_end = cl[CHUNK - 1:CHUNK, :]
    e_neg = jnp.exp(-cl)
    e_end = jnp.exp(g_end - cl)
    kap = _pair_rows(kk * jnp.exp(cl - lw)).astype(BF16)
    rt = _pair_rows(r * jnp.exp(cl)).astype(BF16)
    bt = _pair_rows(b * e_neg).astype(BF16)
    kt = _pair_rows(k * e_neg).astype(BF16)
    kh = _pair_rows(k * e_end)
    bh = _pair_rows(b * e_end)
    vv = _pair_rows(v)
    vv_b = vv.astype(BF16)
    p_b = P.astype(BF16)

    g = lax.dot_general(jnp.concatenate([kap, rt], axis=0), jnp.concatenate([bt, kt], axis=0),
                        (((1,), (1,)), ((), ())), preferred_element_type=F32)
    yield
    strict = j < i
    incl = j <= i
    a_b = jnp.where(strict, g[:c2, :c2], 0.0)
    a_bb = a_b.astype(BF16)
    a_k = jnp.where(strict, g[:c2, c2:], 0.0).astype(BF16)
    r_b = jnp.where(incl, g[c2:, :c2], 0.0).astype(BF16)
    r_k = jnp.where(incl, g[c2:, c2:], 0.0).astype(BF16)

    t_inv = (i == j).astype(F32) - jnp.where(jnp.logical_and((i & 1) == 1, j == i - 1), a_b, 0.0)
    n = 2
    while n < CHUNK:
        m = jnp.logical_and((i >> n.bit_length()) == (j >> n.bit_length()),
                            jnp.logical_and((i & (2 * n - 1)) >= n, (j & (2 * n - 1)) < n))
        t_b = t_inv.astype(BF16)
        ta = _bdot(t_b, a_bb).astype(BF16)
        yield
        t_inv = t_inv - jnp.where(m, _bdot(ta, t_b), 0.0)
        yield
        n *= 2

    w = _bdot(jnp.concatenate([kap, a_k], axis=1), jnp.concatenate([p_b, vv_b], axis=0))
    yield
    u = _bdot(t_inv.astype(BF16), w.astype(BF16))
    yield
    vu_b = jnp.concatenate([vv_b, u.astype(BF16)], axis=0)
    y2 = _bdot(rt, p_b) + _bdot(jnp.concatenate([r_k, -r_b], axis=1), vu_b)
    y = y2[:CHUNK] + y2[CHUNK:]
    yield
    g_col = jnp.broadcast_to(jnp.exp(g_end), (c2, c2)).T
    p_new = g_col * P + _dot3(jnp.concatenate([kh.T, -bh.T], axis=1), jnp.concatenate([vv, u], axis=0))
    return y, p_new


def _run_interleaved(gens):
    results = [None] * len(gens)
    live = list(range(len(gens)))
    while live:
        for idx in list(live):
            try:
                next(gens[idx])
            except StopIteration as stop:
                results[idx] = stop.value
                live.remove(idx)
    return results


def _scan_kernel(r_ref, w_ref, k_ref, kk_ref, b_ref, v_ref, s0_ref, y_ref, sout_ref, s_scr, *, n_prompt_chunks):
    c = pl.program_id(0)

    @pl.when(jnp.logical_or(c == 0, c >= n_prompt_chunks))
    def _():
        s_scr[...] = s0_ref[0]

    outs = _run_interleaved([
        _scan_chunk(s_scr[p], r_ref[p], w_ref[p], k_ref[p], kk_ref[p], b_ref[p], v_ref[p])
        for p in range(N_PAIRS)])
    for p, (y, p_new) in enumerate(outs):
        y_ref[p] = y
        s_scr[p] = p_new
        sout_ref[0, p] = p_new


def _rwkv_scan(r, w, k, kk, b, v, s0, n_prompt):
    n = r.shape[1]
    npc = n_prompt // CHUNK
    n_seq = s0.shape[0]
    pspec = pl.BlockSpec((N_PAIRS, CHUNK, LANES), lambda c: (0, c, 0))
    sspec = pl.BlockSpec((1, N_PAIRS, LANES, LANES), lambda c: (jnp.maximum(c - (npc - 1), 0), 0, 0, 0))
    return pl.pallas_call(
        functools.partial(_scan_kernel, n_prompt_chunks=npc),
        grid=(n // CHUNK,),
        in_specs=[pspec, pspec, pspec, pspec, pspec, pspec, sspec],
        out_specs=[pspec, sspec],
        out_shape=[jax.ShapeDtypeStruct((N_PAIRS, n, LANES), F32),
                   jax.ShapeDtypeStruct((n_seq, N_PAIRS, LANES, LANES), F32)],
        scratch_shapes=[pltpu.VMEM((N_PAIRS, LANES, LANES), F32)],
        compiler_params=_cparams(1),
        name="rwkv_scan",
    )(r, w, k, kk, b, v, s0)


def _fox_pre_kernel(z_ref, vec_ref, bf_ref, q_o, k_o, v_o, knp_o, vnp_o, kns_o, vns_o, lf_o, *, np_tiles):
    z = z_ref[...]
    q = z[:, 0:MIX_W]
    k = z[:, MIX_W:2 * MIX_W]
    v = z[:, 2 * MIX_W:3 * MIX_W]
    fl = z[:, 3 * MIX_W:3 * MIX_W + LANES]
    inv = 1.0 / HEAD_DIM
    qn = q * lax.rsqrt(_head_sum(q * q) * inv + RMS_EPS) * vec_ref[0:1, :]
    kn = k * lax.rsqrt(_head_sum(k * k) * inv + RMS_EPS) * vec_ref[1:2, :]
    qs = (qn * (HEAD_DIM ** -0.5 * LOG2E)).astype(BF16)
    kb = kn.astype(BF16)
    vb = v.astype(BF16)
    for p in range(N_PAIRS):
        sl = slice(p * LANES, (p + 1) * LANES)
        q_o[p] = qs[:, sl]
        k_o[p] = kb[:, sl]
        v_o[p] = vb[:, sl]
    lf_o[...] = _log_sigmoid(fl + bf_ref[...])

    @pl.when(pl.program_id(0) < np_tiles)
    def _():
        knp_o[...] = kn
        vnp_o[...] = v

    @pl.when(pl.program_id(0) >= np_tiles)
    def _():
        kns_o[...] = kn
        vns_o[...] = v


def _fox_pre(z, vecs, bf, n_p):
    n = z.shape[0]
    n_s = n - n_p
    tm = _row_tile(n_p, n_s, 256)
    np_tiles = n_p // tm
    pm = jax.ShapeDtypeStruct((N_PAIRS, n, LANES), BF16)
    pspec = pl.BlockSpec((N_PAIRS, tm, LANES), lambda i: (0, i, 0))
    p_spec, s_spec = _two_group_specs(tm, MIX_W, np_tiles)
    tok_p = jax.ShapeDtypeStruct((n_p, MIX_W), F32)
    tok_s = jax.ShapeDtypeStruct((n_s, MIX_W), F32)
    return pl.pallas_call(
        functools.partial(_fox_pre_kernel, np_tiles=np_tiles),
        grid=(n // tm,),
        in_specs=[pl.BlockSpec((tm, Z_HALF), lambda i: (i, 1)),
                  pl.BlockSpec((8, MIX_W), lambda i: (0, 0)),
                  pl.BlockSpec((1, LANES), lambda i: (0, 0))],
        out_specs=[pspec, pspec, pspec, p_spec, p_spec, s_spec, s_spec,
                   pl.BlockSpec((tm, LANES), lambda i: (i, 0))],
        out_shape=[pm, pm, pm, tok_p, tok_p, tok_s, tok_s, jax.ShapeDtypeStruct((n, LANES), F32)],
        compiler_params=_cparams(1),
        name="fox_pre",
    )(z, vecs, bf)


def _cumsum_kernel(x_ref, o_ref):
    r = lax.broadcasted_iota(jnp.int32, (LANES, LANES), 0)
    c = lax.broadcasted_iota(jnp.int32, (LANES, LANES), 1)
    tri = (r <= c).astype(F32)
    carry = jnp.zeros((N_HEADS, 1), F32)
    for i in range(x_ref.shape[2] // LANES):
        sl = slice(i * LANES, (i + 1) * LANES)
        cs = jnp.dot(x_ref[0, :, sl], tri, precision=HI, preferred_element_type=F32) + carry
        o_ref[0, :, sl] = cs
        carry = cs[:, LANES - 1:LANES]


def _cumsum_lanes(x):
    b, h, t = x.shape
    return pl.pallas_call(
        _cumsum_kernel,
        grid=(b,),
        in_specs=[pl.BlockSpec((1, h, t), lambda i: (i, 0, 0))],
        out_specs=pl.BlockSpec((1, h, t), lambda i: (i, 0, 0)),
        out_shape=jax.ShapeDtypeStruct((b, h, t), F32),
        compiler_params=_cparams(1),
        name="cumsum_logf",
    )(x)


def _rep_lanes(m, tk):
    if tk % LANES == 0:
        return m if tk == LANES else jnp.concatenate([m] * (tk // LANES), axis=1)
    return m[:, :tk]


DEN_LANE = (HEAD_DIM, 0)


def _attn_update(qa, qb, kb, vb, ck, mask, m_ref, acc_ref):
    tk = kb.shape[0]
    lane = lax.broadcasted_iota(jnp.int32, vb.shape, 1)
    own = (lane < HEAD_DIM, lane >= HEAD_DIM)
    ck2 = ck * LOG2E
    for h, qh in enumerate((qa, qb)):
        s = lax.dot_general(qh, kb, (((1,), (1,)), ((), ())), preferred_element_type=F32)
        s = s - ck2[h:h + 1, :]
        if mask is not None:
            s = jnp.where(mask, s, NEG_BIG)
        m_prev = m_ref[h]
        m_next = jnp.maximum(m_prev, jnp.max(s, axis=1, keepdims=True))
        p = jnp.exp2(s - _rep_lanes(m_next, tk))
        alpha = jnp.exp2(m_prev - m_next)
        m_ref[h] = m_next
        v_aug = jnp.where(own[h], vb.astype(F32), jnp.where(lane == DEN_LANE[h], 1.0, 0.0)).astype(BF16)
        acc_ref[h] = acc_ref[h] * alpha + jnp.dot(p.astype(BF16), v_aug, preferred_element_type=F32)


def _attn_finish(acc_ref):
    lo_half = lax.broadcasted_iota(jnp.int32, acc_ref.shape[1:], 1) < HEAD_DIM
    acc_a = acc_ref[0]
    acc_b = acc_ref[1]
    return jnp.where(lo_half, acc_a / acc_a[:, DEN_LANE[0]:DEN_LANE[0] + 1], acc_b / acc_b[:, DEN_LANE[1]:DEN_LANE[1] + 1])


def _split_q(q):
    qf = q.astype(F32)
    lo_half = lax.broadcasted_iota(jnp.int32, qf.shape, 1) < HEAD_DIM
    return jnp.where(lo_half, qf, 0.0).astype(BF16), jnp.where(lo_half, 0.0, qf).astype(BF16)


def _attn_prompt_kernel(qi_ref, ki_ref, q_ref, k_ref, v_ref, c_ref, o_ref,
                        qa_scr, qb_scr, m_scr, acc_scr, *, tq, tk):
    s_id = pl.program_id(1)
    qi = qi_ref[s_id]
    ki = ki_ref[s_id]

    @pl.when(ki == 0)
    def _():
        qa, qb = _split_q(q_ref[0])
        qa_scr[...] = qa
        qb_scr[...] = qb
        m_scr[...] = jnp.full(m_scr.shape, NEG_BIG, F32)
        acc_scr[...] = jnp.zeros(acc_scr.shape, F32)

    crosses_diagonal = ki * tk + (tk - 1) > qi * tq

    @pl.when(crosses_diagonal)
    def _():
        rows = qi * tq + lax.broadcasted_iota(jnp.int32, (tq, tk), 0)
        cols = ki * tk + lax.broadcasted_iota(jnp.int32, (tq, tk), 1)
        _attn_update(qa_scr[...], qb_scr[...], k_ref[0], v_ref[0], c_ref[0], cols <= rows, m_scr, acc_scr)

    @pl.when(jnp.logical_not(crosses_diagonal))
    def _():
        _attn_update(qa_scr[...], qb_scr[...], k_ref[0], v_ref[0], c_ref[0], None, m_scr, acc_scr)

    @pl.when(ki == ((qi + 1) * tq - 1) // tk)
    def _():
        o_ref[0] = _attn_finish(acc_scr).astype(o_ref.dtype)


def _attn_prompt(q, k, v, c, t):
    tq = min(1024, t)
    tk = min(1024, t)
    steps = [(qi, ki) for qi in range(t // tq) for ki in range(((qi + 1) * tq - 1) // tk + 1)]
    qi_arr = jnp.asarray(np.array([s[0] for s in steps], np.int32))
    ki_arr = jnp.asarray(np.array([s[1] for s in steps], np.int32))
    grid_spec = pltpu.PrefetchScalarGridSpec(
        num_scalar_prefetch=2,
        grid=(N_PAIRS, len(steps)),
        in_specs=[pl.BlockSpec((1, tq, LANES), lambda p, s, qi, ki: (p, qi[s], 0)),
                  pl.BlockSpec((1, tk, LANES), lambda p, s, qi, ki: (p, ki[s], 0)),
                  pl.BlockSpec((1, tk, LANES), lambda p, s, qi, ki: (p, ki[s], 0)),
                  pl.BlockSpec((1, 2, tk), lambda p, s, qi, ki: (p, 0, ki[s]))],
        out_specs=pl.BlockSpec((1, tq, LANES), lambda p, s, qi, ki: (p, qi[s], 0)),
        scratch_shapes=[pltpu.VMEM((tq, LANES), BF16), pltpu.VMEM((tq, LANES), BF16),
                        pltpu.VMEM((2, tq, LANES), F32), pltpu.VMEM((2, tq, LANES), F32)],
    )
    return pl.pallas_call(
        functools.partial(_attn_prompt_kernel, tq=tq, tk=tk),
        grid_spec=grid_spec,
        out_shape=jax.ShapeDtypeStruct((N_PAIRS, t, LANES), BF16),
        compiler_params=_cparams(2),
        name="fox_attn_prompt",
    )(qi_arr, ki_arr, q, k, v, c)


def _attn_sample_kernel(q_ref, ck_ref, cv_ref, kn_ref, vn_ref, cc_ref, cn_ref, o_ref,
                        m_scr, acc_scr, *, nkb):
    j = pl.program_id(1)
    ts = q_ref.shape[1]

    @pl.when(j == 0)
    def _():
        m_scr[...] = jnp.full(m_scr.shape, NEG_BIG, F32)
        acc_scr[...] = jnp.zeros(acc_scr.shape, F32)

    @pl.when(j < nkb)
    def _():
        for p in range(N_PAIRS):
            sl = slice(p * LANES, (p + 1) * LANES)
            qa, qb = _split_q(q_ref[p])
            _attn_update(qa, qb, ck_ref[0, :, sl].astype(BF16), cv_ref[0, :, sl].astype(BF16),
                         cc_ref[0, 2 * p:2 * p + 2, :], None, m_scr.at[p], acc_scr.at[p])

    @pl.when(j == nkb)
    def _():
        rows = lax.broadcasted_iota(jnp.int32, (ts, ts), 0)
        cols = lax.broadcasted_iota(jnp.int32, (ts, ts), 1)
        for p in range(N_PAIRS):
            qa, qb = _split_q(q_ref[p])
            _attn_update(qa, qb, kn_ref[p], vn_ref[p], cn_ref[0, 2 * p:2 * p + 2, :], cols <= rows,
                         m_scr.at[p], acc_scr.at[p])
            o_ref[p] = _attn_finish(acc_scr.at[p]).astype(o_ref.dtype)


def _attn_sample(q, k, v, cache_k, cache_v, c_cache, c_new, row0):
    b, past, _ = cache_k.shape
    ts = c_new.shape[2]
    tk = min(1024, past)
    nkb = past // tk
    blk0 = row0 // ts
    qspec = pl.BlockSpec((N_PAIRS, ts, LANES), lambda i, j: (0, blk0 + i, 0))
    cspec = pl.BlockSpec((1, tk, MIX_W), lambda i, j: (i, jnp.minimum(j, nkb - 1), 0))
    return pl.pallas_call(
        functools.partial(_attn_sample_kernel, nkb=nkb),
        grid=(b, nkb + 1),
        in_specs=[qspec, cspec, cspec, qspec, qspec,
                  pl.BlockSpec((1, N_HEADS, tk), lambda i, j: (i, 0, jnp.minimum(j, nkb - 1))),
                  pl.BlockSpec((1, N_HEADS, ts), lambda i, j: (i, 0, 0))],
        out_specs=pl.BlockSpec((N_PAIRS, ts, LANES), lambda i, j: (0, i, 0)),
        out_shape=jax.ShapeDtypeStruct((N_PAIRS, b * ts, LANES), BF16),
        scratch_shapes=[pltpu.VMEM((N_PAIRS, 2, ts, LANES), F32), pltpu.VMEM((N_PAIRS, 2, ts, LANES), F32)],
        compiler_params=_cparams(2),
        name="fox_attn_sample",
    )(q, cache_k, cache_v, k, v, c_cache, c_new)


def _mix_out_kernel(xp_ref, xs_ref, ys_ref, bn_ref, g_ref, yfp_ref, yfs_ref, ln_ref, wo_ref, gf_ref, rw_ref, rb_ref,
                    x1_o, hf_o, ti_o, tg_o, cnt_o, cnt_scr, *, np_tiles):
    y = jnp.concatenate([ys_ref[p] for p in range(N_PAIRS)], axis=1)
    inv = 1.0 / HEAD_DIM
    mu = _head_sum(y) * inv
    d = y - mu
    var = _head_sum(d * d) * inv
    yn = d * lax.rsqrt(var + GN_EPS) * ln_ref[0:1, :] + ln_ref[1:2, :]
    yr = ((yn + bn_ref[...]) * g_ref[...]).astype(BF16)
    in_prompt = pl.program_id(0) < np_tiles
    yf = jnp.concatenate([jnp.where(in_prompt, yfp_ref[p].astype(F32), yfs_ref[p].astype(F32)).astype(BF16)
                          for p in range(N_PAIRS)], axis=1)
    mix = jnp.concatenate([yr, yf], axis=1)
    x_res = jnp.where(pl.program_id(0) < np_tiles, xp_ref[...], xs_ref[...])
    x1 = x_res + jnp.dot(mix, wo_ref[...], preferred_element_type=F32)
    x1_o[...] = x1
    hf = _rms(x1, gf_ref[...])
    bits = pltpu.bitcast(hf, jnp.uint32)
    rne = (bits + jnp.uint32(0x7FFF) + ((bits >> 16) & jnp.uint32(1))) >> 16
    words = rne[:, :D_MODEL // 2] | (rne[:, D_MODEL // 2:] << 16)
    tm_rows = hf.shape[0]
    for jc in range(ROW_CHUNKS):
        hf_o[pl.ds(jc, tm_rows, stride=ROW_CHUNKS), :] = words[:, jc * LANES:(jc + 1) * LANES]

    logits = _dot3(hf, rw_ref[...]) + rb_ref[...]
    lane_e = lax.broadcasted_iota(jnp.int32, logits.shape, 1).astype(F32)
    vals = []
    idxs = []
    cur = logits
    for _ in range(TOP_K):
        m = jnp.max(cur, axis=1, keepdims=True)
        am = jnp.min(jnp.where(cur == m, lane_e, float(N_EXPERTS)), axis=1, keepdims=True)
        vals.append(m)
        idxs.append(am)
        cur = jnp.where(lane_e == am, -jnp.inf, cur)
    es = [jnp.exp(vv - vals[0]) for vv in vals]
    tot = es[0] + es[1] + es[2] + es[3]

    @pl.when(pl.program_id(0) == 0)
    def _():
        cnt_scr[...] = jnp.zeros(cnt_scr.shape, F32)

    tm = logits.shape[0]
    sel = [lane_e == idxs[kk] for kk in range(TOP_K)]
    onehot = jnp.where(jnp.logical_or(jnp.logical_or(sel[0], sel[1]), jnp.logical_or(sel[2], sel[3])), 1.0, 0.0)
    rr = lax.broadcasted_iota(jnp.int32, (tm, tm), 0)
    cc = lax.broadcasted_iota(jnp.int32, (tm, tm), 1)
    before = jnp.dot((cc < rr).astype(BF16), onehot.astype(BF16), preferred_element_type=F32) + cnt_scr[...]
    ranks = [jnp.sum(jnp.where(sel[kk], before, 0.0), axis=1, keepdims=True) for kk in range(TOP_K)]
    cnt_scr[...] = cnt_scr[...] + jnp.sum(onehot, axis=0, keepdims=True)
    cnt_o[...] = jnp.broadcast_to(cnt_scr[...], cnt_o.shape).astype(jnp.int32)

    lane = lax.broadcasted_iota(jnp.int32, ti_o.shape, 1)
    ti = jnp.zeros(ti_o.shape, F32)
    tg = jnp.zeros(tg_o.shape, F32)
    for kk in range(TOP_K):
        ti = jnp.where(lane == kk, idxs[kk], ti)
        ti = jnp.where(lane == TOP_K + kk, ranks[kk], ti)
        tg = jnp.where(lane == kk, es[kk] / tot, tg)
    ti_o[...] = ti.astype(jnp.int32)
    tg_o[...] = tg


def _mix_out(x_p, x_s, ys, bonus, g, yf_p, yf_s, ln, wo_bf, gf, rw, rb):
    n_p, n_s = x_p.shape[0], x_s.shape[0]
    n = n_p + n_s
    tm = _row_tile(n_p, n_s, 256)
    np_tiles = n_p // tm
    xp_spec, xs_spec = _two_group_specs(tm, D_MODEL, np_tiles)
    yfp_spec = pl.BlockSpec((N_PAIRS, tm, LANES), lambda i: (0, jnp.minimum(i, np_tiles - 1), 0))
    yfs_spec = pl.BlockSpec((N_PAIRS, tm, LANES), lambda i: (0, jnp.maximum(i - np_tiles, 0), 0))
    row = lambda w: pl.BlockSpec((tm, w), lambda i: (i, 0))
    pspec = pl.BlockSpec((N_PAIRS, tm, LANES), lambda i: (0, i, 0))
    full = lambda a: pl.BlockSpec(a.shape, lambda i: (0,) * a.ndim)
    return pl.pallas_call(
        functools.partial(_mix_out_kernel, np_tiles=n_p // tm),
        grid=(n // tm,),
        in_specs=[xp_spec, xs_spec, pspec, row(MIX_W), row(MIX_W), yfp_spec, yfs_spec, full(ln), full(wo_bf), full(gf),
                  full(rw), full(rb)],
        out_specs=[row(D_MODEL), pl.BlockSpec((tm * ROW_CHUNKS, LANES), lambda i: (i, 0)), row(LANES), row(LANES),
                   pl.BlockSpec((8, N_EXPERTS), lambda i: (0, 0))],
        out_shape=[jax.ShapeDtypeStruct((n, D_MODEL), F32), jax.ShapeDtypeStruct((n * ROW_CHUNKS, LANES), jnp.uint32),
                   jax.ShapeDtypeStruct((n, LANES), jnp.int32), jax.ShapeDtypeStruct((n, LANES), F32),
                   jax.ShapeDtypeStruct((8, N_EXPERTS), jnp.int32)],
        scratch_shapes=[pltpu.VMEM((1, N_EXPERTS), F32)],
        compiler_params=_cparams(1),
        name="mix_out_router",
    )(x_p, x_s, ys, bonus, g, yf_p, yf_s, ln, wo_bf, gf, rw, rb)


DISPATCH_TOKENS = 256


def _moe_dispatch_kernel(pos_ref, pad_base_ref, pad_cnt_ref, x_ref, xs_hbm, sem):
    i = pl.program_id(0)
    tile_rows = x_ref.shape[0]

    def row_copy(src_tok, dst_row, slot):
        return pltpu.make_async_copy(
            x_ref.at[pl.ds(pl.multiple_of(src_tok * ROW_CHUNKS, ROW_CHUNKS), ROW_CHUNKS)],
            xs_hbm.at[pl.ds(pl.multiple_of(dst_row * ROW_CHUNKS, ROW_CHUNKS), ROW_CHUNKS)], sem.at[slot])

    @pl.when(i == 0)
    def _():
        pieces = [1 << b for b in reversed(range((MOE_ROWS - 1).bit_length()))]
        assert max(pieces) <= DISPATCH_TOKENS

        def pad_expert(e, carry):
            cnt = pad_cnt_ref[e]
            for starting in (True, False):
                off = pad_base_ref[e]
                for piece in pieces:
                    cp = pltpu.make_async_copy(
                        x_ref.at[pl.ds(0, piece * ROW_CHUNKS)],
                        xs_hbm.at[pl.ds(pl.multiple_of(off * ROW_CHUNKS, ROW_CHUNKS), piece * ROW_CHUNKS)], sem.at[1])

                    @pl.when((cnt & piece) != 0)
                    def _():
                        if starting:
                            cp.start()
                        else:
                            cp.wait()
                    off = off + (cnt & piece)
            return carry

        lax.fori_loop(0, N_EXPERTS, pad_expert, 0)

        tail_row0 = pad_base_ref[N_EXPERTS]
        for starting in (True, False):
            def tail_block(b, carry):
                for part in range(MOE_ROWS // DISPATCH_TOKENS):
                    row = tail_row0 + b * MOE_ROWS + part * DISPATCH_TOKENS
                    cp = pltpu.make_async_copy(
                        x_ref, xs_hbm.at[pl.ds(pl.multiple_of(row * ROW_CHUNKS, ROW_CHUNKS), tile_rows)], sem.at[1])
                    if starting:
                        cp.start()
                    else:
                        cp.wait()
                return carry
            lax.fori_loop(0, pad_cnt_ref[N_EXPERTS], tail_block, 0)

    def issue(t, carry):
        for kk in range(TOP_K):
            row_copy(t, pos_ref[(i * DISPATCH_TOKENS + t) * TOP_K + kk], 0).start()
        return carry
    lax.fori_loop(0, DISPATCH_TOKENS, issue, 0, unroll=4)

    for _ in range(TOP_K):
        pltpu.make_async_copy(x_ref, xs_hbm.at[pl.ds(0, tile_rows)], sem.at[0]).wait()


def _moe_dispatch(pos, pad_base, pad_cnt, hf, n_blocks):
    n_tok = pos.shape[0] // TOP_K
    assert n_tok % DISPATCH_TOKENS == 0
    tile_rows = DISPATCH_TOKENS * ROW_CHUNKS
    grid_spec = pltpu.PrefetchScalarGridSpec(
        num_scalar_prefetch=3,
        grid=(n_tok // DISPATCH_TOKENS,),
        in_specs=[pl.BlockSpec((tile_rows, LANES), lambda i, *pf: (i, 0))],
        out_specs=pl.BlockSpec(memory_space=pl.ANY),
        scratch_shapes=[pltpu.SemaphoreType.DMA((2,))],
    )
    return pl.pallas_call(
        _moe_dispatch_kernel,
        grid_spec=grid_spec,
        out_shape=jax.ShapeDtypeStruct((n_blocks * MOE_ROWS * ROW_CHUNKS, LANES), jnp.uint32),
        compiler_params=_cparams(1),
        name="moe_dispatch",
    )(pos, pad_base, pad_cnt, hf)


def _unpack_rows(x_ref):
    rows = x_ref.shape[0] // ROW_CHUNKS
    lo = []
    hi = []
    for jc in range(ROW_CHUNKS):
        words = x_ref[pl.ds(jc, rows, stride=ROW_CHUNKS), :]
        lo.append(pltpu.bitcast(words << 16, F32).astype(BF16))
        hi.append(pltpu.bitcast(words & jnp.uint32(0xFFFF0000), F32).astype(BF16))
    return jnp.concatenate(lo + hi, axis=1)


STEP_RUN, STEP_NEW_WEIGHTS, STEP_ZERO = 0, 1, 2
N_SCHED = 10


def _weight_group_step(s, sf, sg, ng, copies, cast):
    @pl.when(sf[s] == STEP_NEW_WEIGHTS)
    def _():
        g = sg[s]
        slot = g % 2

        @pl.when(g == 0)
        def _():
            for c in copies(g, slot):
                c.start()

        for c in copies(g, slot):
            c.wait()

        @pl.when(g + 1 < ng[0])
        def _():
            for c in copies(g + 1, 1 - slot):
                c.start()

        cast(slot)


def _moe_gu_kernel(se, sw, sb, sj, sf, sg, ge, gw, ng, sx, x_ref, wgu_hbm, bg_ref, bu_ref, o_ref,
                   wg_bf, wu_bf, wg_stage, wu_stage, sem):
    s = pl.program_id(0)

    def copies(g, slot):
        col = pl.multiple_of(gw[g] * MOE_TN, MOE_TN)
        return [pltpu.make_async_copy(wgu_hbm.at[ge[g], :, pl.ds(col, MOE_TN)], wg_stage.at[slot], sem.at[slot]),
                pltpu.make_async_copy(wgu_hbm.at[ge[g], :, pl.ds(D_EXPERT + col, MOE_TN)], wu_stage.at[slot],
                                      sem.at[slot])]

    def cast(slot):
        wg_bf[...] = wg_stage[slot].astype(BF16)
        wu_bf[...] = wu_stage[slot].astype(BF16)

    _weight_group_step(s, sf, sg, ng, copies, cast)

    @pl.when(sf[s] != STEP_ZERO)
    def _():
        x = _unpack_rows(x_ref)
        for c0 in range(0, MOE_TN, MXU_N):
            cols = slice(c0, c0 + MXU_N)
            g = jnp.dot(x, wg_bf[:, cols], preferred_element_type=F32) + bg_ref[0][:, cols]
            u = jnp.dot(x, wu_bf[:, cols], preferred_element_type=F32) + bu_ref[0][:, cols]
            g = jnp.minimum(g, SWIGLU_LIMIT)
            u = jnp.clip(u, -SWIGLU_LIMIT, SWIGLU_LIMIT)
            o_ref[:, cols] = ((u + 1.0) * (g * jax.nn.sigmoid(SWIGLU_ALPHA * g))).astype(BF16)

    @pl.when(sf[s] == STEP_ZERO)
    def _():
        o_ref[...] = jnp.zeros(o_ref.shape, o_ref.dtype)


def _moe_gate_up(sched, xs, w_gu, b_gu, n_blocks):
    nt = D_EXPERT // MOE_TN
    n_steps = nt * n_blocks
    bspec = lambda off: pl.BlockSpec((1, 1, MOE_TN), lambda s, *pf: (pf[0][s], 0, off + pf[1][s]))
    grid_spec = pltpu.PrefetchScalarGridSpec(
        num_scalar_prefetch=N_SCHED,
        grid=(n_steps,),
        in_specs=[pl.BlockSpec((MOE_ROWS * ROW_CHUNKS, LANES), lambda s, *pf: (pf[9][s], 0)),
                  pl.BlockSpec(memory_space=pl.ANY), bspec(0), bspec(nt)],
        out_specs=pl.BlockSpec((MOE_ROWS, MOE_TN), lambda s, *pf: (pf[2][s], pf[3][s])),
        scratch_shapes=[pltpu.VMEM((D_MODEL, MOE_TN), BF16), pltpu.VMEM((D_MODEL, MOE_TN), BF16),
                        pltpu.VMEM((2, D_MODEL, MOE_TN), F32), pltpu.VMEM((2, D_MODEL, MOE_TN), F32),
                        pltpu.SemaphoreType.DMA((2,))],
    )
    return pl.pallas_call(
        _moe_gu_kernel,
        grid_spec=grid_spec,
        out_shape=jax.ShapeDtypeStruct((n_blocks * MOE_ROWS, D_EXPERT), BF16),
        compiler_params=_cparams(1),
        name="moe_gate_up",
    )(*sched, xs, w_gu, b_gu, b_gu)


def _moe_dn_kernel(se, sw, sb, sj, sf, sg, ge, gw, ng, sx, h_ref, wd_hbm, bd_ref, o_ref, wd_bf, wd_stage, sem):
    s = pl.program_id(0)

    def copies(g, slot):
        col = pl.multiple_of(gw[g] * MOE_TN, MOE_TN)
        return [pltpu.make_async_copy(wd_hbm.at[ge[g], :, pl.ds(col, MOE_TN)], wd_stage.at[slot], sem.at[slot])]

    def cast(slot):
        wd_bf[...] = wd_stage[slot].astype(BF16)

    _weight_group_step(s, sf, sg, ng, copies, cast)

    @pl.when(sf[s] != STEP_ZERO)
    def _():
        h = h_ref[...]
        for c0 in range(0, MOE_TN, MXU_N):
            cols = slice(c0, c0 + MXU_N)
            o_ref[:, cols] = jnp.dot(h, wd_bf[:, cols], preferred_element_type=F32) + bd_ref[0][:, cols]

    @pl.when(sf[s] == STEP_ZERO)
    def _():
        o_ref[...] = jnp.zeros(o_ref.shape, o_ref.dtype)


def _moe_down(sched, hid, w_dn, b_dn, n_blocks):
    nt = D_MODEL // MOE_TN
    n_steps = nt * n_blocks
    grid_spec = pltpu.PrefetchScalarGridSpec(
        num_scalar_prefetch=N_SCHED,
        grid=(n_steps,),
        in_specs=[pl.BlockSpec((MOE_ROWS, D_EXPERT), lambda s, *pf: (pf[2][s], 0)),
                  pl.BlockSpec(memory_space=pl.ANY),
                  pl.BlockSpec((1, 1, MOE_TN), lambda s, *pf: (pf[0][s], 0, pf[1][s]))],
        out_specs=pl.BlockSpec((MOE_ROWS, MOE_TN), lambda s, *pf: (pf[2][s], pf[3][s])),
        scratch_shapes=[pltpu.VMEM((D_EXPERT, MOE_TN), BF16), pltpu.VMEM((2, D_EXPERT, MOE_TN), F32),
                        pltpu.SemaphoreType.DMA((2,))],
    )
    return pl.pallas_call(
        _moe_dn_kernel,
        grid_spec=grid_spec,
        out_shape=jax.ShapeDtypeStruct((n_blocks * MOE_ROWS, D_MODEL), F32),
        compiler_params=_cparams(1),
        name="moe_down",
    )(*sched, hid, w_dn, b_dn)


def _moe_combine_kernel(pos_ref, ys_hbm, x1_ref, tg_ref, o_ref, buf, sem):
    i = pl.program_id(0)
    nb = pl.num_programs(0)
    tm = o_ref.shape[0]

    def row_copy(blk, slot, r, kk):
        src = pos_ref[(blk * tm + r) * TOP_K + kk]
        return pltpu.make_async_copy(ys_hbm.at[pl.ds(src, 1)], buf.at[slot, kk, pl.ds(r, 1)], sem.at[slot])

    def issue(blk, slot):
        def body(r, carry):
            for kk in range(TOP_K):
                row_copy(blk, slot, r, kk).start()
            return carry
        lax.fori_loop(0, tm, body, 0, unroll=4)

    @pl.when(i == 0)
    def _():
        issue(0, 0)

    @pl.when(i + 1 < nb)
    def _():
        issue(i + 1, (i + 1) % 2)

    slot = i % 2

    for kk in range(TOP_K):
        pltpu.make_async_copy(ys_hbm.at[pl.ds(0, tm)], buf.at[slot, kk], sem.at[slot]).wait()

    tg = tg_ref[...]
    acc = x1_ref[...]
    for kk in range(TOP_K):
        acc = acc + tg[:, kk:kk + 1] * buf[slot, kk]
    o_ref[...] = acc


def _moe_combine(pos, ys, x1, tg):
    n = x1.shape[0]
    tm = min(128, n)
    grid_spec = pltpu.PrefetchScalarGridSpec(
        num_scalar_prefetch=1,
        grid=(n // tm,),
        in_specs=[pl.BlockSpec(memory_space=pl.ANY),
                  pl.BlockSpec((tm, D_MODEL), lambda i, pos: (i, 0)),
                  pl.BlockSpec((tm, LANES), lambda i, pos: (i, 0))],
        out_specs=pl.BlockSpec((tm, D_MODEL), lambda i, pos: (i, 0)),
        scratch_shapes=[pltpu.VMEM((2, TOP_K, tm, D_MODEL), F32), pltpu.SemaphoreType.DMA((2,))],
    )
    return pl.pallas_call(
        _moe_combine_kernel,
        grid_spec=grid_spec,
        out_shape=jax.ShapeDtypeStruct((n, D_MODEL), F32),
        compiler_params=_cparams(1),
        name="moe_combine",
    )(pos, ys, x1, tg)


def _moe_schedule(top_idx, rank, counts, n_blocks):
    n = top_idx.shape[0]
    n_rows = n * TOP_K
    nt = D_EXPERT // MOE_TN
    flat_e = top_idx.reshape(n_rows)
    nb_e = (counts + MOE_ROWS - 1) // MOE_ROWS
    blk_end = jnp.cumsum(nb_e)
    blk_start = blk_end - nb_e
    pos = (blk_start[flat_e] * MOE_ROWS + rank.reshape(n_rows)).astype(jnp.int32)
    used_blocks = blk_end[-1]
    pad_base = jnp.concatenate([blk_start * MOE_ROWS + counts, (used_blocks * MOE_ROWS).reshape(1)]).astype(jnp.int32)
    pad_cnt = jnp.concatenate([nb_e * MOE_ROWS - counts, (n_blocks - used_blocks).reshape(1)]).astype(jnp.int32)
    used = blk_end[-1]
    s = jnp.arange(nt * n_blocks, dtype=jnp.int32)
    live = s < nt * used
    s_eff = jnp.maximum(jnp.minimum(s, nt * used - 1), 0)
    e = jnp.minimum(jnp.sum((s_eff[:, None] >= nt * blk_end[None, :]).astype(jnp.int32), axis=1), N_EXPERTS - 1)
    local = s_eff - nt * blk_start[e]
    nbe = jnp.maximum(nb_e[e], 1)
    sw = (local // nbe).astype(jnp.int32)
    tail = s - nt * used
    sb = jnp.where(live, blk_start[e] + local % nbe, used + tail // nt).astype(jnp.int32)
    sj = jnp.where(live, sw, tail % nt).astype(jnp.int32)
    sf = jnp.where(live, jnp.where(local % nbe == 0, STEP_NEW_WEIGHTS, STEP_RUN), STEP_ZERO).astype(jnp.int32)
    active = nb_e > 0
    rank_e = jnp.cumsum(active.astype(jnp.int32)) - 1
    n_active = rank_e[-1] + 1
    expert_of_rank = jnp.zeros((N_EXPERTS,), jnp.int32).at[jnp.where(active, rank_e, N_EXPERTS)].set(
        jnp.arange(N_EXPERTS, dtype=jnp.int32), mode="drop")
    sg = (rank_e[e] * nt + sw).astype(jnp.int32)
    gidx = jnp.arange(N_EXPERTS * nt, dtype=jnp.int32)
    ge = expert_of_rank[jnp.minimum(gidx // nt, n_active - 1)]
    gw = gidx % nt
    ng = (n_active * nt).astype(jnp.int32).reshape(1)
    sx = jnp.where(live, sb, 0).astype(jnp.int32)
    return pos, pad_base, pad_cnt, (e, sw, sb, sj, sf, sg, ge, gw, ng, sx)


def _ple_kernel(x_ref, pp_ref, ps_ref, g_ref, wg_ref, wp_ref, op_ref, os_ref, *, np_tiles):
    i = pl.program_id(0)
    x = x_ref[...]
    h = _rms(x, g_ref[...]).astype(BF16)
    gate = jax.nn.sigmoid(jnp.dot(h, wg_ref[...], preferred_element_type=F32))
    p = jnp.where(i < np_tiles, pp_ref[...], ps_ref[...])
    y = x + gate * jnp.dot(p.astype(BF16), wp_ref[...], preferred_element_type=F32)

    @pl.when(i < np_tiles)
    def _():
        op_ref[...] = y

    @pl.when(i >= np_tiles)
    def _():
        os_ref[...] = y


def _ple(x, p_p, p_s, g, wg_bf, wp_bf):
    n_p, n_s = p_p.shape[0], p_s.shape[0]
    n = n_p + n_s
    tm = _row_tile(n_p, n_s, 256)
    np_tiles = n_p // tm
    full = lambda a: pl.BlockSpec(a.shape, lambda i: (0,) * a.ndim)
    pp_spec, ps_spec = _two_group_specs(tm, PLE_DIM, np_tiles)
    op_spec, os_spec = _two_group_specs(tm, D_MODEL, np_tiles)
    return pl.pallas_call(
        functools.partial(_ple_kernel, np_tiles=np_tiles),
        grid=(n // tm,),
        in_specs=[pl.BlockSpec((tm, D_MODEL), lambda i: (i, 0)), pp_spec, ps_spec,
                  full(g), full(wg_bf), full(wp_bf)],
        out_specs=[op_spec, os_spec],
        out_shape=[jax.ShapeDtypeStruct((n_p, D_MODEL), F32), jax.ShapeDtypeStruct((n_s, D_MODEL), F32)],
        compiler_params=_cparams(1),
        name="ple_gate",
    )(x, p_p, p_s, g, wg_bf, wp_bf)


def _pairs_from_state(s):
    b = s.shape[0]
    st = jnp.swapaxes(s, 2, 3).reshape(b, N_PAIRS, 2, HEAD_DIM, HEAD_DIM)
    z = jnp.zeros((b, N_PAIRS, HEAD_DIM, HEAD_DIM), s.dtype)
    top = jnp.concatenate([st[:, :, 0], z], axis=3)
    bot = jnp.concatenate([z, st[:, :, 1]], axis=3)
    return jnp.concatenate([top, bot], axis=2)


def _state_from_pairs(sp):
    b = sp.shape[0]
    st = jnp.stack([sp[:, :, :HEAD_DIM, :HEAD_DIM], sp[:, :, HEAD_DIM:, HEAD_DIM:]], axis=2)
    return jnp.swapaxes(st.reshape(b, N_HEADS, HEAD_DIM, HEAD_DIM), 2, 3)


def _layer(x_p, x_s, cache_k, cache_v, cache_lf, state, shift, p_p, p_s, lw):
    t_p = x_p.shape[1]
    b_s, t_s, _ = x_s.shape
    n_p = x_p.shape[0] * t_p
    n_s = b_s * t_s
    n = n_p + n_s
    past = cache_k.shape[1]
    xp2 = x_p.reshape(n_p, D_MODEL)
    xs2 = x_s.reshape(n_s, D_MODEL)

    w_in_bf = jnp.pad(lw["w_in"], ((0, 0), (0, Z_PAD - lw["w_in"].shape[1]))).astype(BF16)
    z = _in_proj(xp2, xs2, lw["norm_mix_g"], w_in_bf)

    assert t_s == CHUNK, "each sample stream contributes exactly one scan chunk"
    vecs = jnp.zeros((8, MIX_W), F32)
    vecs = vecs.at[0].set(lw["rwkv_w0"]).at[1].set(lw["rwkv_a0"]).at[2].set(lw["rwkv_kk"])
    vecs = vecs.at[3].set(lw["rwkv_ka"]).at[4].set(lw["rwkv_rk"].reshape(MIX_W))
    wl = jnp.zeros((256, 3 * MIX_W), F32)
    wl = wl.at[0:64, 0:MIX_W].set(lw["rwkv_w2"]).at[64:128, MIX_W:2 * MIX_W].set(lw["rwkv_a2"])
    wl = wl.at[128:256, 2 * MIX_W:].set(lw["rwkv_g2"])
    r, w, k, kk, bb, v_pm, g, bonus = _rwkv_pre(z, shift, lw["rwkv_mu"].reshape(1, RWKV_PROJ), vecs, wl, n_p)
    s0 = jnp.concatenate([jnp.zeros((1, N_PAIRS, LANES, LANES), F32), _pairs_from_state(state.astype(F32))], axis=0)
    y_scan, s_out = _rwkv_scan(r, w, k, kk, bb, v_pm, s0, n_p)
    s_new = _state_from_pairs(s_out)
    shift_new_p = z[n_p - 1:n_p, :RWKV_PROJ].reshape(1, 1, RWKV_PROJ)
    shift_new_s = z[n_p:, :RWKV_PROJ].reshape(b_s, t_s, RWKV_PROJ)[:, -1:, :]

    fvecs = jnp.zeros((8, MIX_W), F32)
    fvecs = fvecs.at[0].set(jnp.tile(lw["fox_q_g"], N_HEADS)).at[1].set(jnp.tile(lw["fox_k_g"], N_HEADS))
    bf = jnp.zeros((1, LANES), F32).at[0, :N_HEADS].set(lw["fox_b_f"])
    q_pm, k_pm, vv_pm, k_new_p, v_new_p, k_new_s, v_new_s, lf = _fox_pre(z, fvecs, bf, n_p)
    lf = lf[:, :N_HEADS]
    lf_p = lf[:n_p].T.reshape(1, N_HEADS, n_p)
    c_p = _cumsum_lanes(lf_p).reshape(N_PAIRS, 2, n_p)
    yf_p = _attn_prompt(q_pm, k_pm, vv_pm, c_p, n_p)
    lf_s = jnp.swapaxes(lf[n_p:].reshape(b_s, t_s, N_HEADS), 1, 2)
    lf_all = jnp.concatenate([jnp.swapaxes(cache_lf.astype(F32), 1, 2), lf_s], axis=2)
    pad = (-lf_all.shape[2]) % LANES
    c_all = _cumsum_lanes(jnp.pad(lf_all, ((0, 0), (0, 0), (0, pad))))
    yf_s = _attn_sample(q_pm, k_pm, vv_pm, cache_k.reshape(b_s, past, MIX_W), cache_v.reshape(b_s, past, MIX_W),
                        c_all[:, :, :past], c_all[:, :, past:past + t_s], n_p)

    ln = jnp.stack([lw["rwkv_ln_g"], lw["rwkv_ln_b"]])
    x1, hf, ti, tg, cnt = _mix_out(xp2, xs2, y_scan, bonus, g, yf_p, yf_s, ln, lw["w_out"].astype(BF16),
                                   lw["norm_ffn_g"].reshape(1, D_MODEL), lw["router_w"],
                                   lw["router_b"].reshape(1, N_EXPERTS))

    n_blocks = n * TOP_K // MOE_ROWS + N_EXPERTS
    pos, pad_base, pad_cnt, sched = _moe_schedule(ti[:, :TOP_K], ti[:, TOP_K:2 * TOP_K], cnt[0], n_blocks)
    xs = _moe_dispatch(pos, pad_base, pad_cnt, hf, n_blocks)
    hid = _moe_gate_up(sched, xs, lw["expert_w_gu"], lw["expert_b_gu"].reshape(N_EXPERTS, 1, 2 * D_EXPERT), n_blocks)
    ys = _moe_down(sched, hid, lw["expert_w_down"], lw["expert_b_down"].reshape(N_EXPERTS, 1, D_MODEL), n_blocks)
    x2 = _moe_combine(pos, ys, x1, tg)

    y_p, y_s = _ple(x2, p_p, p_s, lw["ple_norm_g"].reshape(1, D_MODEL), lw["ple_w_gate"].astype(BF16),
                    lw["ple_w_proj"].astype(BF16))

    heads = lambda a, bsz, t: a.reshape(bsz, t, N_HEADS, HEAD_DIM)
    out_p = (y_p.reshape(x_p.shape), heads(k_new_p, 1, n_p), heads(v_new_p, 1, n_p),
             lf[:n_p].reshape(1, n_p, N_HEADS), s_new[:1], shift_new_p)
    out_s = (y_s.reshape(x_s.shape), heads(k_new_s, b_s, t_s), heads(v_new_s, b_s, t_s),
             lf[n_p:].reshape(b_s, t_s, N_HEADS), s_new[1:], shift_new_s)
    return out_p, out_s


def kernel(x_prompt, x_sample, cache_fox_k, cache_fox_v, cache_fox_logf, state_rwkv, state_rwkv_shift, p_prompt, p_sample, norm_mix_g, w_in, rwkv_mu, rwkv_w0, rwkv_w2, rwkv_a0, rwkv_a2, rwkv_g2, rwkv_kk, rwkv_ka, rwkv_rk, rwkv_ln_g, rwkv_ln_b, fox_q_g, fox_k_g, fox_b_f, w_out, norm_ffn_g, router_w, router_b, expert_w_gu, expert_b_gu, expert_w_down, expert_b_down, ple_norm_g, ple_w_gate, ple_w_proj):
    assert x_prompt.shape[0] == 1 and w_in.shape[0] == 1, "one prompt stream, one layer"
    lw = dict(norm_mix_g=norm_mix_g[0], w_in=w_in[0], rwkv_mu=rwkv_mu[0], rwkv_w0=rwkv_w0[0], rwkv_w2=rwkv_w2[0],
              rwkv_a0=rwkv_a0[0], rwkv_a2=rwkv_a2[0], rwkv_g2=rwkv_g2[0], rwkv_kk=rwkv_kk[0], rwkv_ka=rwkv_ka[0],
              rwkv_rk=rwkv_rk[0], rwkv_ln_g=rwkv_ln_g[0], rwkv_ln_b=rwkv_ln_b[0], fox_q_g=fox_q_g[0],
              fox_k_g=fox_k_g[0], fox_b_f=fox_b_f[0], w_out=w_out[0], norm_ffn_g=norm_ffn_g[0],
              router_w=router_w[0], router_b=router_b[0], expert_w_gu=expert_w_gu[0], expert_b_gu=expert_b_gu[0],
              expert_w_down=expert_w_down[0], expert_b_down=expert_b_down[0], ple_norm_g=ple_norm_g[0],
              ple_w_gate=ple_w_gate[0], ple_w_proj=ple_w_proj[0])
    n_p = x_prompt.shape[1]
    (y_p, k_p, v_p, lf_p, s_p, sh_p), (y_s, k_s, v_s, lf_s, s_s, sh_s) = _layer(
        x_prompt, x_sample, cache_fox_k[0], cache_fox_v[0], cache_fox_logf[0], state_rwkv[0],
        state_rwkv_shift[0], p_prompt[0].reshape(n_p, PLE_DIM), p_sample[0].reshape(-1, PLE_DIM), lw)
    add = lambda a: a[None]
    return (y_p, y_s, add(k_p), add(v_p), add(lf_p), add(s_p), add(sh_p),
            add(k_s), add(v_s), add(lf_s), add(s_s), add(sh_s))
```

```python
import functools

import numpy as np
import jax
import jax.numpy as jnp
from jax import lax
from jax.experimental import pallas as pl
from jax.experimental.pallas import tpu as pltpu

F32 = jnp.float32
BF16 = jnp.bfloat16
HI = lax.Precision.HIGHEST

D_MODEL = 2048
HEAD_DIM = 64
N_HEADS = 16
N_PAIRS = N_HEADS // 2
MIX_W = N_HEADS * HEAD_DIM
CHUNK = 64
RWKV_PROJ = 3 * MIX_W + 64 + 64 + 128
FOX_PROJ = 3 * MIX_W + N_HEADS
Z_HALF = RWKV_PROJ
Z_PAD = 2 * Z_HALF
N_EXPERTS = 32
TOP_K = 4
D_EXPERT = 2048
SWIGLU_LIMIT = 7.0
SWIGLU_ALPHA = 1.702
PLE_DIM = 256
RMS_EPS = 1e-6
GN_EPS = 64e-5
L2_EPS = 1e-12
NEG_BIG = -1e30
LOG2E = 1.4426950408889634

LANES = 128
MXU_N = 256
MOE_ROWS = 512
MOE_TN = 1024
ROW_CHUNKS = D_MODEL // 2 // LANES
VMEM_LIMIT = 52 * 1024 * 1024


def _cparams(n_axes, vmem=VMEM_LIMIT):
    return pltpu.CompilerParams(dimension_semantics=("arbitrary",) * n_axes, vmem_limit_bytes=vmem)


def _head_sum(x):
    r = lax.broadcasted_iota(jnp.int32, (LANES, LANES), 0) // HEAD_DIM
    c = lax.broadcasted_iota(jnp.int32, (LANES, LANES), 1) // HEAD_DIM
    bd = jnp.where(r == c, 1.0, 0.0).astype(BF16)
    hi = x.astype(BF16)
    lo = (x - hi.astype(F32)).astype(BF16)
    parts = []
    for i in range(x.shape[1] // LANES):
        sl = slice(i * LANES, (i + 1) * LANES)
        parts.append(jnp.dot(hi[:, sl], bd, preferred_element_type=F32)
                     + jnp.dot(lo[:, sl], bd, preferred_element_type=F32))
    return parts[0] if len(parts) == 1 else jnp.concatenate(parts, axis=1)


def _log_sigmoid(x):
    return jnp.minimum(x, 0.0) - jnp.log1p(jnp.exp(-jnp.abs(x)))


def _rms(x, g):
    ms = jnp.mean(x * x, axis=-1, keepdims=True)
    return x * lax.rsqrt(ms + RMS_EPS) * g


def _row_tile(n_p, n_s, pref):
    return pref if (n_p % pref == 0 and n_s % pref == 0) else 128


def _two_group_specs(tm, width, np_tiles, n_grid_axes=1):
    if n_grid_axes == 1:
        return (pl.BlockSpec((tm, width), lambda i: (jnp.minimum(i, np_tiles - 1), 0)),
                pl.BlockSpec((tm, width), lambda i: (jnp.maximum(i - np_tiles, 0), 0)))
    return (pl.BlockSpec((tm, width), lambda i, j: (jnp.minimum(i, np_tiles - 1), 0)),
            pl.BlockSpec((tm, width), lambda i, j: (jnp.maximum(i - np_tiles, 0), 0)))


def _inproj_kernel(xp_ref, xs_ref, g_ref, w_ref, o_ref, h_scr, *, np_tiles):
    i = pl.program_id(0)
    first = pl.program_id(1) == 0

    @pl.when(jnp.logical_and(first, i < np_tiles))
    def _():
        h_scr[...] = _rms(xp_ref[...], g_ref[...]).astype(BF16)

    @pl.when(jnp.logical_and(first, i >= np_tiles))
    def _():
        h_scr[...] = _rms(xs_ref[...], g_ref[...]).astype(BF16)

    o_ref[...] = jnp.dot(h_scr[...], w_ref[...], preferred_element_type=F32)


def _in_proj(x_p, x_s, g, w_bf):
    n_p, n_s = x_p.shape[0], x_s.shape[0]
    n = n_p + n_s
    tm = _row_tile(n_p, n_s, 512)
    tn = Z_PAD // 4
    xp_spec, xs_spec = _two_group_specs(tm, D_MODEL, n_p // tm, 2)
    return pl.pallas_call(
        functools.partial(_inproj_kernel, np_tiles=n_p // tm),
        grid=(n // tm, Z_PAD // tn),
        in_specs=[xp_spec, xs_spec,
                  pl.BlockSpec((1, D_MODEL), lambda i, j: (0, 0)),
                  pl.BlockSpec((D_MODEL, tn), lambda i, j: (0, j))],
        out_specs=pl.BlockSpec((tm, tn), lambda i, j: (i, j)),
        out_shape=jax.ShapeDtypeStruct((n, Z_PAD), F32),
        scratch_shapes=[pltpu.VMEM((tm, D_MODEL), BF16)],
        compiler_params=_cparams(2),
        name="in_proj",
    )(x_p, x_s, g.reshape(1, D_MODEL), w_bf)


def _rwkv_pre_kernel(z_ref, prev_ref, mu_ref, vec_ref, wl_ref,
                     r_o, w_o, k_o, kk_o, b_o, v_o, g_o, bn_o, carry, *, n_prompt_tiles):
    i = pl.program_id(0)
    z = z_ref[...]
    tm = z.shape[0]

    @pl.when(i == 0)
    def _():
        carry[...] = prev_ref[0, 0:1, :]

    rolled = pltpu.roll(z, 1, axis=0)
    row = lax.broadcasted_iota(jnp.int32, z.shape, 0)
    shifted = jnp.where(row == 0, carry[...], rolled)
    is_sample = i >= n_prompt_tiles
    for c in range(tm // CHUNK):
        shifted = jnp.where(jnp.logical_and(is_sample, row == c * CHUNK), prev_ref[0, c:c + 1, :], shifted)
    carry[...] = z[tm - 1:tm, :]
    zs = z + mu_ref[...] * (shifted - z)

    r = zs[:, 0:MIX_W]
    k = zs[:, MIX_W:2 * MIX_W]
    v = zs[:, 2 * MIX_W:3 * MIX_W]
    lo = zs[:, 3 * MIX_W:RWKV_PROJ]
    lane = lax.broadcasted_iota(jnp.int32, lo.shape, 1)
    f = jnp.where(lane < 64, jnp.tanh(lo), jnp.where(lane < 128, lo, jax.nn.sigmoid(lo)))
    lora = _dot3(f, wl_ref[...])
    w_pre = vec_ref[0:1, :] + lora[:, 0:MIX_W]
    log_decay = -jnp.exp(_log_sigmoid(w_pre) - 0.5)
    a = jax.nn.sigmoid(vec_ref[1:2, :] + lora[:, MIX_W:2 * MIX_W])
    g = lora[:, 2 * MIX_W:3 * MIX_W]
    kk = k * vec_ref[2:3, :]
    kk = kk * lax.rsqrt(_head_sum(kk * kk) + L2_EPS)
    k_mod = k * (1.0 + (a - 1.0) * vec_ref[3:4, :])
    bonus = _head_sum(r * k_mod * vec_ref[4:5, :]) * v
    kka = kk * a

    g_o[...] = g
    bn_o[...] = bonus
    for p in range(N_PAIRS):
        sl = slice(p * LANES, (p + 1) * LANES)
        r_o[p] = r[:, sl]
        w_o[p] = log_decay[:, sl]
        k_o[p] = k_mod[:, sl]
        kk_o[p] = kk[:, sl]
        b_o[p] = kka[:, sl]
        v_o[p] = v[:, sl]


def _rwkv_pre(z, shift, mu, vecs, wl, n_prompt):
    n = z.shape[0]
    n_s = n - n_prompt
    tm = _row_tile(n_prompt, n_s, 256)
    if tm % CHUNK:
        tm = CHUNK
    cpt = tm // CHUNK
    nt = n // tm
    n_prompt_tiles = n_prompt // tm
    prev = jnp.concatenate([jnp.zeros((1, cpt, RWKV_PROJ), F32),
                            shift.astype(F32).reshape(n_s // tm, cpt, RWKV_PROJ)], axis=0)
    tok = jax.ShapeDtypeStruct((n, MIX_W), F32)
    tspec = pl.BlockSpec((tm, MIX_W), lambda i: (i, 0))
    pm = jax.ShapeDtypeStruct((N_PAIRS, n, LANES), F32)
    pspec = pl.BlockSpec((N_PAIRS, tm, LANES), lambda i: (0, i, 0))
    return pl.pallas_call(
        functools.partial(_rwkv_pre_kernel, n_prompt_tiles=n_prompt_tiles),
        grid=(nt,),
        in_specs=[pl.BlockSpec((tm, Z_HALF), lambda i: (i, 0)),
                  pl.BlockSpec((1, cpt, RWKV_PROJ), lambda i: (jnp.maximum(i - (n_prompt_tiles - 1), 0), 0, 0)),
                  pl.BlockSpec((1, RWKV_PROJ), lambda i: (0, 0)),
                  pl.BlockSpec((8, MIX_W), lambda i: (0, 0)),
                  pl.BlockSpec((256, 3 * MIX_W), lambda i: (0, 0))],
        out_specs=[pspec, pspec, pspec, pspec, pspec, pspec, tspec, tspec],
        out_shape=[pm, pm, pm, pm, pm, pm, tok, tok],
        scratch_shapes=[pltpu.VMEM((1, RWKV_PROJ), F32)],
        compiler_params=_cparams(1),
        name="rwkv_pre",
    )(z, prev, mu, vecs, wl)


def _dot(a, b):
    return jnp.dot(a, b, precision=HI, preferred_element_type=F32)


def _bdot(a, b):
    return jnp.dot(a, b, preferred_element_type=F32)


def _split_bf16(x):
    hi = x.astype(BF16)
    return hi, (x - hi.astype(F32)).astype(BF16)


def _dot3(a, b):
    a_hi, a_lo = _split_bf16(a)
    b_hi, b_lo = _split_bf16(b)
    return _bdot(a_hi, b_hi) + (_bdot(a_hi, b_lo) + _bdot(a_lo, b_hi))


def _pair_rows(x):
    lo_half = lax.broadcasted_iota(jnp.int32, x.shape, 1) < HEAD_DIM
    return jnp.concatenate([jnp.where(lo_half, x, 0.0), jnp.where(lo_half, 0.0, x)], axis=0)


def _scan_chunk(P, r, lw, k, kk, b, v):
    c2 = 2 * CHUNK
    i = lax.broadcasted_iota(jnp.int32, (c2, c2), 0)
    j = lax.broadcasted_iota(jnp.int32, (c2, c2), 1)
    ti = lax.broadcasted_iota(jnp.int32, (CHUNK, CHUNK), 0)
    tj = lax.broadcasted_iota(jnp.int32, (CHUNK, CHUNK), 1)
    cl = _dot((tj <= ti).astype(F32), lw)
    yield
    g_end = cl[CHUNK - 1:CHUNK, :]
    e_neg = jnp.exp(-cl)
    e_end = jnp.exp(g_end - cl)
    kap = _pair_rows(kk * jnp.exp(cl - lw)).astype(BF16)
    rt = _pair_rows(r * jnp.exp(cl)).astype(BF16)
    bt = _pair_rows(b * e_neg).astype(BF16)
    kt = _pair_rows(k * e_neg).astype(BF16)
    kh = _pair_rows(k * e_end)
    bh = _pair_rows(b * e_end)
    vv = _pair_rows(v)
    vv_b = vv.astype(BF16)
    p_b = P.astype(BF16)

    g = lax.dot_general(jnp.concatenate([kap, rt], axis=0), jnp.concatenate([bt, kt], axis=0),
                        (((1,), (1,)), ((), ())), preferred_element_type=F32)
    yield
    strict = j < i
    incl = j <= i
    a_b = jnp.where(strict, g[:c2, :c2], 0.0)
    a_bb = a_b.astype(BF16)
    a_k = jnp.where(strict, g[:c2, c2:], 0.0).astype(BF16)
    r_b = jnp.where(incl, g[c2:, :c2], 0.0).astype(BF16)
    r_k = jnp.where(incl, g[c2:, c2:], 0.0).astype(BF16)

    t_inv = (i == j).astype(F32) - jnp.where(jnp.logical_and((i & 1) == 1, j == i - 1), a_b, 0.0)
    n = 2
    while n < CHUNK:
        m = jnp.logical_and((i >> n.bit_length()) == (j >> n.bit_length()),
                            jnp.logical_and((i & (2 * n - 1)) >= n, (j & (2 * n - 1)) < n))
        t_b = t_inv.astype(BF16)
        ta = _bdot(t_b, a_bb).astype(BF16)
        yield
        t_inv = t_inv - jnp.where(m, _bdot(ta, t_b), 0.0)
        yield
        n *= 2

    w = _bdot(jnp.concatenate([kap, a_k], axis=1), jnp.concatenate([p_b, vv_b], axis=0))
    yield
    u = _bdot(t_inv.astype(BF16), w.astype(BF16))
    yield
    vu_b = jnp.concatenate([vv_b, u.astype(BF16)], axis=0)
    y2 = _bdot(rt, p_b) + _bdot(jnp.concatenate([r_k, -r_b], axis=1), vu_b)
    y = y2[:CHUNK] + y2[CHUNK:]
    yield
    g_col = jnp.broadcast_to(jnp.exp(g_end), (c2, c2)).T
    p_new = g_col * P + _dot3(jnp.concatenate([kh.T, -bh.T], axis=1), jnp.concatenate([vv, u], axis=0))
    return y, p_new


def _run_interleaved(gens):
    results = [None] * len(gens)
    live = list(range(len(gens)))
    while live:
        for idx in list(live):
            try:
                next(gens[idx])
            except StopIteration as stop:
                results[idx] = stop.value
                live.remove(idx)
    return results


def _scan_kernel(r_ref, w_ref, k_ref, kk_ref, b_ref, v_ref, s0_ref, y_ref, sout_ref, s_scr, *, n_prompt_chunks):
    c = pl.program_id(0)

    @pl.when(jnp.logical_or(c == 0, c >= n_prompt_chunks))
    def _():
        s_scr[...] = s0_ref[0]

    outs = _run_interleaved([
        _scan_chunk(s_scr[p], r_ref[p], w_ref[p], k_ref[p], kk_ref[p], b_ref[p], v_ref[p])
        for p in range(N_PAIRS)])
    for p, (y, p_new) in enumerate(outs):
        y_ref[p] = y
        s_scr[p] = p_new
        sout_ref[0, p] = p_new


def _rwkv_scan(r, w, k, kk, b, v, s0, n_prompt):
    n = r.shape[1]
    npc = n_prompt // CHUNK
    n_seq = s0.shape[0]
    pspec = pl.BlockSpec((N_PAIRS, CHUNK, LANES), lambda c: (0, c, 0))
    sspec = pl.BlockSpec((1, N_PAIRS, LANES, LANES), lambda c: (jnp.maximum(c - (npc - 1), 0), 0, 0, 0))
    return pl.pallas_call(
        functools.partial(_scan_kernel, n_prompt_chunks=npc),
        grid=(n // CHUNK,),
        in_specs=[pspec, pspec, pspec, pspec, pspec, pspec, sspec],
        out_specs=[pspec, sspec],
        out_shape=[jax.ShapeDtypeStruct((N_PAIRS, n, LANES), F32),
                   jax.ShapeDtypeStruct((n_seq, N_PAIRS, LANES, LANES), F32)],
        scratch_shapes=[pltpu.VMEM((N_PAIRS, LANES, LANES), F32)],
        compiler_params=_cparams(1),
        name="rwkv_scan",
    )(r, w, k, kk, b, v, s0)


def _fox_pre_kernel(z_ref, vec_ref, bf_ref, q_o, k_o, v_o, knp_o, vnp_o, kns_o, vns_o, lf_o, *, np_tiles):
    z = z_ref[...]
    q = z[:, 0:MIX_W]
    k = z[:, MIX_W:2 * MIX_W]
    v = z[:, 2 * MIX_W:3 * MIX_W]
    fl = z[:, 3 * MIX_W:3 * MIX_W + LANES]
    inv = 1.0 / HEAD_DIM
    qn = q * lax.rsqrt(_head_sum(q * q) * inv + RMS_EPS) * vec_ref[0:1, :]
    kn = k * lax.rsqrt(_head_sum(k * k) * inv + RMS_EPS) * vec_ref[1:2, :]
    qs = (qn * (HEAD_DIM ** -0.5 * LOG2E)).astype(BF16)
    kb = kn.astype(BF16)
    vb = v.astype(BF16)
    for p in range(N_PAIRS):
        sl = slice(p * LANES, (p + 1) * LANES)
        q_o[p] = qs[:, sl]
        k_o[p] = kb[:, sl]
        v_o[p] = vb[:, sl]
    lf_o[...] = _log_sigmoid(fl + bf_ref[...])

    @pl.when(pl.program_id(0) < np_tiles)
    def _():
        knp_o[...] = kn
        vnp_o[...] = v

    @pl.when(pl.program_id(0) >= np_tiles)
    def _():
        kns_o[...] = kn
        vns_o[...] = v


def _fox_pre(z, vecs, bf, n_p):
    n = z.shape[0]
    n_s = n - n_p
    tm = _row_tile(n_p, n_s, 256)
    np_tiles = n_p // tm
    pm = jax.ShapeDtypeStruct((N_PAIRS, n, LANES), BF16)
    pspec = pl.BlockSpec((N_PAIRS, tm, LANES), lambda i: (0, i, 0))
    p_spec, s_spec = _two_group_specs(tm, MIX_W, np_tiles)
    tok_p = jax.ShapeDtypeStruct((n_p, MIX_W), F32)
    tok_s = jax.ShapeDtypeStruct((n_s, MIX_W), F32)
    return pl.pallas_call(
        functools.partial(_fox_pre_kernel, np_tiles=np_tiles),
        grid=(n // tm,),
        in_specs=[pl.BlockSpec((tm, Z_HALF), lambda i: (i, 1)),
                  pl.BlockSpec((8, MIX_W), lambda i: (0, 0)),
                  pl.BlockSpec((1, LANES), lambda i: (0, 0))],
        out_specs=[pspec, pspec, pspec, p_spec, p_spec, s_spec, s_spec,
                   pl.BlockSpec((tm, LANES), lambda i: (i, 0))],
        out_shape=[pm, pm, pm, tok_p, tok_p, tok_s, tok_s, jax.ShapeDtypeStruct((n, LANES), F32)],
        compiler_params=_cparams(1),
        name="fox_pre",
    )(z, vecs, bf)


def _cumsum_kernel(x_ref, o_ref):
    r = lax.broadcasted_iota(jnp.int32, (LANES, LANES), 0)
    c = lax.broadcasted_iota(jnp.int32, (LANES, LANES), 1)
    tri = (r <= c).astype(F32)
    carry = jnp.zeros((N_HEADS, 1), F32)
    for i in range(x_ref.shape[2] // LANES):
        sl = slice(i * LANES, (i + 1) * LANES)
        cs = jnp.dot(x_ref[0, :, sl], tri, precision=HI, preferred_element_type=F32) + carry
        o_ref[0, :, sl] = cs
        carry = cs[:, LANES - 1:LANES]


def _cumsum_lanes(x):
    b, h, t = x.shape
    return pl.pallas_call(
        _cumsum_kernel,
        grid=(b,),
        in_specs=[pl.BlockSpec((1, h, t), lambda i: (i, 0, 0))],
        out_specs=pl.BlockSpec((1, h, t), lambda i: (i, 0, 0)),
        out_shape=jax.ShapeDtypeStruct((b, h, t), F32),
        compiler_params=_cparams(1),
        name="cumsum_logf",
    )(x)


def _rep_lanes(m, tk):
    if tk % LANES == 0:
        return m if tk == LANES else jnp.concatenate([m] * (tk // LANES), axis=1)
    return m[:, :tk]


DEN_LANE = (HEAD_DIM, 0)


def _attn_update(qa, qb, kb, vb, ck, mask, m_ref, acc_ref):
    tk = kb.shape[0]
    lane = lax.broadcasted_iota(jnp.int32, vb.shape, 1)
    own = (lane < HEAD_DIM, lane >= HEAD_DIM)
    ck2 = ck * LOG2E
    for h, qh in enumerate((qa, qb)):
        s = lax.dot_general(qh, kb, (((1,), (1,)), ((), ())), preferred_element_type=F32)
        s = s - ck2[h:h + 1, :]
        if mask is not None:
            s = jnp.where(mask, s, NEG_BIG)
        m_prev = m_ref[h]
        m_next = jnp.maximum(m_prev, jnp.max(s, axis=1, keepdims=True))
        p = jnp.exp2(s - _rep_lanes(m_next, tk))
        alpha = jnp.exp2(m_prev - m_next)
        m_ref[h] = m_next
        v_aug = jnp.where(own[h], vb.astype(F32), jnp.where(lane == DEN_LANE[h], 1.0, 0.0)).astype(BF16)
        acc_ref[h] = acc_ref[h] * alpha + jnp.dot(p.astype(BF16), v_aug, preferred_element_type=F32)


def _attn_finish(acc_ref):
    lo_half = lax.broadcasted_iota(jnp.int32, acc_ref.shape[1:], 1) < HEAD_DIM
    acc_a = acc_ref[0]
    acc_b = acc_ref[1]
    return jnp.where(lo_half, acc_a / acc_a[:, DEN_LANE[0]:DEN_LANE[0] + 1], acc_b / acc_b[:, DEN_LANE[1]:DEN_LANE[1] + 1])


def _split_q(q):
    qf = q.astype(F32)
    lo_half = lax.broadcasted_iota(jnp.int32, qf.shape, 1) < HEAD_DIM
    return jnp.where(lo_half, qf, 0.0).astype(BF16), jnp.where(lo_half, 0.0, qf).astype(BF16)


def _attn_prompt_kernel(qi_ref, ki_ref, q_ref, k_ref, v_ref, c_ref, o_ref,
                        qa_scr, qb_scr, m_scr, acc_scr, *, tq, tk):
    s_id = pl.program_id(1)
    qi = qi_ref[s_id]
    ki = ki_ref[s_id]

    @pl.when(ki == 0)
    def _():
        qa, qb = _split_q(q_ref[0])
        qa_scr[...] = qa
        qb_scr[...] = qb
        m_scr[...] = jnp.full(m_scr.shape, NEG_BIG, F32)
        acc_scr[...] = jnp.zeros(acc_scr.shape, F32)

    crosses_diagonal = ki * tk + (tk - 1) > qi * tq

    @pl.when(crosses_diagonal)
    def _():
        rows = qi * tq + lax.broadcasted_iota(jnp.int32, (tq, tk), 0)
        cols = ki * tk + lax.broadcasted_iota(jnp.int32, (tq, tk), 1)
        _attn_update(qa_scr[...], qb_scr[...], k_ref[0], v_ref[0], c_ref[0], cols <= rows, m_scr, acc_scr)

    @pl.when(jnp.logical_not(crosses_diagonal))
    def _():
        _attn_update(qa_scr[...], qb_scr[...], k_ref[0], v_ref[0], c_ref[0], None, m_scr, acc_scr)

    @pl.when(ki == ((qi + 1) * tq - 1) // tk)
    def _():
        o_ref[0] = _attn_finish(acc_scr).astype(o_ref.dtype)


def _attn_prompt(q, k, v, c, t):
    tq = min(1024, t)
    tk = min(1024, t)
    steps = [(qi, ki) for qi in range(t // tq) for ki in range(((qi + 1) * tq - 1) // tk + 1)]
    qi_arr = jnp.asarray(np.array([s[0] for s in steps], np.int32))
    ki_arr = jnp.asarray(np.array([s[1] for s in steps], np.int32))
    grid_spec = pltpu.PrefetchScalarGridSpec(
        num_scalar_prefetch=2,
        grid=(N_PAIRS, len(steps)),
        in_specs=[pl.BlockSpec((1, tq, LANES), lambda p, s, qi, ki: (p, qi[s], 0)),
                  pl.BlockSpec((1, tk, LANES), lambda p, s, qi, ki: (p, ki[s], 0)),
                  pl.BlockSpec((1, tk, LANES), lambda p, s, qi, ki: (p, ki[s], 0)),
                  pl.BlockSpec((1, 2, tk), lambda p, s, qi, ki: (p, 0, ki[s]))],
        out_specs=pl.BlockSpec((1, tq, LANES), lambda p, s, qi, ki: (p, qi[s], 0)),
        scratch_shapes=[pltpu.VMEM((tq, LANES), BF16), pltpu.VMEM((tq, LANES), BF16),
                        pltpu.VMEM((2, tq, LANES), F32), pltpu.VMEM((2, tq, LANES), F32)],
    )
    return pl.pallas_call(
        functools.partial(_attn_prompt_kernel, tq=tq, tk=tk),
        grid_spec=grid_spec,
        out_shape=jax.ShapeDtypeStruct((N_PAIRS, t, LANES), BF16),
        compiler_params=_cparams(2),
        name="fox_attn_prompt",
    )(qi_arr, ki_arr, q, k, v, c)


def _attn_sample_kernel(q_ref, ck_ref, cv_ref, kn_ref, vn_ref, cc_ref, cn_ref, o_ref,
                        m_scr, acc_scr, *, nkb):
    j = pl.program_id(1)
    ts = q_ref.shape[1]

    @pl.when(j == 0)
    def _():
        m_scr[...] = jnp.full(m_scr.shape, NEG_BIG, F32)
        acc_scr[...] = jnp.zeros(acc_scr.shape, F32)

    @pl.when(j < nkb)
    def _():
        for p in range(N_PAIRS):
            sl = slice(p * LANES, (p + 1) * LANES)
            qa, qb = _split_q(q_ref[p])
            _attn_update(qa, qb, ck_ref[0, :, sl].astype(BF16), cv_ref[0, :, sl].astype(BF16),
                         cc_ref[0, 2 * p:2 * p + 2, :], None, m_scr.at[p], acc_scr.at[p])

    @pl.when(j == nkb)
    def _():
        rows = lax.broadcasted_iota(jnp.int32, (ts, ts), 0)
        cols = lax.broadcasted_iota(jnp.int32, (ts, ts), 1)
        for p in range(N_PAIRS):
            qa, qb = _split_q(q_ref[p])
            _attn_update(qa, qb, kn_ref[p], vn_ref[p], cn_ref[0, 2 * p:2 * p + 2, :], cols <= rows,
                         m_scr.at[p], acc_scr.at[p])
            o_ref[p] = _attn_finish(acc_scr.at[p]).astype(o_ref.dtype)


def _attn_sample(q, k, v, cache_k, cache_v, c_cache, c_new, row0):
    b, past, _ = cache_k.shape
    ts = c_new.shape[2]
    tk = min(1024, past)
    nkb = past // tk
    blk0 = row0 // ts
    qspec = pl.BlockSpec((N_PAIRS, ts, LANES), lambda i, j: (0, blk0 + i, 0))
    cspec = pl.BlockSpec((1, tk, MIX_W), lambda i, j: (i, jnp.minimum(j, nkb - 1), 0))
    return pl.pallas_call(
        functools.partial(_attn_sample_kernel, nkb=nkb),
        grid=(b, nkb + 1),
        in_specs=[qspec, cspec, cspec, qspec, qspec,
                  pl.BlockSpec((1, N_HEADS, tk), lambda i, j: (i, 0, jnp.minimum(j, nkb - 1))),
                  pl.BlockSpec((1, N_HEADS, ts), lambda i, j: (i, 0, 0))],
        out_specs=pl.BlockSpec((N_PAIRS, ts, LANES), lambda i, j: (0, i, 0)),
        out_shape=jax.ShapeDtypeStruct((N_PAIRS, b * ts, LANES), BF16),
        scratch_shapes=[pltpu.VMEM((N_PAIRS, 2, ts, LANES), F32), pltpu.VMEM((N_PAIRS, 2, ts, LANES), F32)],
        compiler_params=_cparams(2),
        name="fox_attn_sample",
    )(q, cache_k, cache_v, k, v, c_cache, c_new)


def _mix_out_kernel(xp_ref, xs_ref, ys_ref, bn_ref, g_ref, yfp_ref, yfs_ref, ln_ref, wo_ref, gf_ref, rw_ref, rb_ref,
                    x1_o, hf_o, ti_o, tg_o, cnt_o, cnt_scr, *, np_tiles):
    y = jnp.concatenate([ys_ref[p] for p in range(N_PAIRS)], axis=1)
    inv = 1.0 / HEAD_DIM
    mu = _head_sum(y) * inv
    d = y - mu
    var = _head_sum(d * d) * inv
    yn = d * lax.rsqrt(var + GN_EPS) * ln_ref[0:1, :] + ln_ref[1:2, :]
    yr = ((yn + bn_ref[...]) * g_ref[...]).astype(BF16)
    in_prompt = pl.program_id(0) < np_tiles
    yf = jnp.concatenate([jnp.where(in_prompt, yfp_ref[p].astype(F32), yfs_ref[p].astype(F32)).astype(BF16)
                          for p in range(N_PAIRS)], axis=1)
    mix = jnp.concatenate([yr, yf], axis=1)
    x_res = jnp.where(pl.program_id(0) < np_tiles, xp_ref[...], xs_ref[...])
    x1 = x_res + jnp.dot(mix, wo_ref[...], preferred_element_type=F32)
    x1_o[...] = x1
    hf = _rms(x1, gf_ref[...])
    bits = pltpu.bitcast(hf, jnp.uint32)
    rne = (bits + jnp.uint32(0x7FFF) + ((bits >> 16) & jnp.uint32(1))) >> 16
    words = rne[:, :D_MODEL // 2] | (rne[:, D_MODEL // 2:] << 16)
    tm_rows = hf.shape[0]
    for jc in range(ROW_CHUNKS):
        hf_o[pl.ds(jc, tm_rows, stride=ROW_CHUNKS), :] = words[:, jc * LANES:(jc + 1) * LANES]

    logits = _dot3(hf, rw_ref[...]) + rb_ref[...]
    lane_e = lax.broadcasted_iota(jnp.int32, logits.shape, 1).astype(F32)
    vals = []
    idxs = []
    cur = logits
    for _ in range(TOP_K):
        m = jnp.max(cur, axis=1, keepdims=True)
        am = jnp.min(jnp.where(cur == m, lane_e, float(N_EXPERTS)), axis=1, keepdims=True)
        vals.append(m)
        idxs.append(am)
        cur = jnp.where(lane_e == am, -jnp.inf, cur)
    es = [jnp.exp(vv - vals[0]) for vv in vals]
    tot = es[0] + es[1] + es[2] + es[3]

    @pl.when(pl.program_id(0) == 0)
    def _():
        cnt_scr[...] = jnp.zeros(cnt_scr.shape, F32)

    tm = logits.shape[0]
    sel = [lane_e == idxs[kk] for kk in range(TOP_K)]
    onehot = jnp.where(jnp.logical_or(jnp.logical_or(sel[0], sel[1]), jnp.logical_or(sel[2], sel[3])), 1.0, 0.0)
    rr = lax.broadcasted_iota(jnp.int32, (tm, tm), 0)
    cc = lax.broadcasted_iota(jnp.int32, (tm, tm), 1)
    before = jnp.dot((cc < rr).astype(BF16), onehot.astype(BF16), preferred_element_type=F32) + cnt_scr[...]
    ranks = [jnp.sum(jnp.where(sel[kk], before, 0.0), axis=1, keepdims=True) for kk in range(TOP_K)]
    cnt_scr[...] = cnt_scr[...] + jnp.sum(onehot, axis=0, keepdims=True)
    cnt_o[...] = jnp.broadcast_to(cnt_scr[...], cnt_o.shape).astype(jnp.int32)

    lane = lax.broadcasted_iota(jnp.int32, ti_o.shape, 1)
    ti = jnp.zeros(ti_o.shape, F32)
    tg = jnp.zeros(tg_o.shape, F32)
    for kk in range(TOP_K):
        ti = jnp.where(lane == kk, idxs[kk], ti)
        ti = jnp.where(lane == TOP_K + kk, ranks[kk], ti)
        tg = jnp.where(lane == kk, es[kk] / tot, tg)
    ti_o[...] = ti.astype(jnp.int32)
    tg_o[...] = tg


def _mix_out(x_p, x_s, ys, bonus, g, yf_p, yf_s, ln, wo_bf, gf, rw, rb):
    n_p, n_s = x_p.shape[0], x_s.shape[0]
    n = n_p + n_s
    tm = _row_tile(n_p, n_s, 256)
    np_tiles = n_p // tm
    xp_spec, xs_spec = _two_group_specs(tm, D_MODEL, np_tiles)
    yfp_spec = pl.BlockSpec((N_PAIRS, tm, LANES), lambda i: (0, jnp.minimum(i, np_tiles - 1), 0))
    yfs_spec = pl.BlockSpec((N_PAIRS, tm, LANES), lambda i: (0, jnp.maximum(i - np_tiles, 0), 0))
    row = lambda w: pl.BlockSpec((tm, w), lambda i: (i, 0))
    pspec = pl.BlockSpec((N_PAIRS, tm, LANES), lambda i: (0, i, 0))
    full = lambda a: pl.BlockSpec(a.shape, lambda i: (0,) * a.ndim)
    return pl.pallas_call(
        functools.partial(_mix_out_kernel, np_tiles=n_p // tm),
        grid=(n // tm,),
        in_specs=[xp_spec, xs_spec, pspec, row(MIX_W), row(MIX_W), yfp_spec, yfs_spec, full(ln), full(wo_bf), full(gf),
                  full(rw), full(rb)],
        out_specs=[row(D_MODEL), pl.BlockSpec((tm * ROW_CHUNKS, LANES), lambda i: (i, 0)), row(LANES), row(LANES),
                   pl.BlockSpec((8, N_EXPERTS), lambda i: (0, 0))],
        out_shape=[jax.ShapeDtypeStruct((n, D_MODEL), F32), jax.ShapeDtypeStruct((n * ROW_CHUNKS, LANES), jnp.uint32),
                   jax.ShapeDtypeStruct((n, LANES), jnp.int32), jax.ShapeDtypeStruct((n, LANES), F32),
                   jax.ShapeDtypeStruct((8, N_EXPERTS), jnp.int32)],
        scratch_shapes=[pltpu.VMEM((1, N_EXPERTS), F32)],
        compiler_params=_cparams(1),
        name="mix_out_router",
    )(x_p, x_s, ys, bonus, g, yf_p, yf_s, ln, wo_bf, gf, rw, rb)


DISPATCH_TOKENS = 256


def _moe_dispatch_kernel(pos_ref, pad_base_ref, pad_cnt_ref, x_ref, xs_hbm, sem):
    i = pl.program_id(0)
    tile_rows = x_ref.shape[0]

    def row_copy(src_tok, dst_row, slot):
        return pltpu.make_async_copy(
            x_ref.at[pl.ds(pl.multiple_of(src_tok * ROW_CHUNKS, ROW_CHUNKS), ROW_CHUNKS)],
            xs_hbm.at[pl.ds(pl.multiple_of(dst_row * ROW_CHUNKS, ROW_CHUNKS), ROW_CHUNKS)], sem.at[slot])

    @pl.when(i == 0)
    def _():
        pieces = [1 << b for b in reversed(range((MOE_ROWS - 1).bit_length()))]
        assert max(pieces) <= DISPATCH_TOKENS

        def pad_expert(e, carry):
            cnt = pad_cnt_ref[e]
            for starting in (True, False):
                off = pad_base_ref[e]
                for piece in pieces:
                    cp = pltpu.make_async_copy(
                        x_ref.at[pl.ds(0, piece * ROW_CHUNKS)],
                        xs_hbm.at[pl.ds(pl.multiple_of(off * ROW_CHUNKS, ROW_CHUNKS), piece * ROW_CHUNKS)], sem.at[1])

                    @pl.when((cnt & piece) != 0)
                    def _():
                        if starting:
                            cp.start()
                        else:
                            cp.wait()
                    off = off + (cnt & piece)
            return carry

        lax.fori_loop(0, N_EXPERTS, pad_expert, 0)

        tail_row0 = pad_base_ref[N_EXPERTS]
        for starting in (True, False):
            def tail_block(b, carry):
                for part in range(MOE_ROWS // DISPATCH_TOKENS):
                    row = tail_row0 + b * MOE_ROWS + part * DISPATCH_TOKENS
                    cp = pltpu.make_async_copy(
                        x_ref, xs_hbm.at[pl.ds(pl.multiple_of(row * ROW_CHUNKS, ROW_CHUNKS), tile_rows)], sem.at[1])
                    if starting:
                        cp.start()
                    else:
                        cp.wait()
                return carry
            lax.fori_loop(0, pad_cnt_ref[N_EXPERTS], tail_block, 0)

    def issue(t, carry):
        for kk in range(TOP_K):
            row_copy(t, pos_ref[(i * DISPATCH_TOKENS + t) * TOP_K + kk], 0).start(priority=kk % 2)
        return carry
    lax.fori_loop(0, DISPATCH_TOKENS, issue, 0, unroll=4)

    for _ in range(TOP_K):
        pltpu.make_async_copy(x_ref, xs_hbm.at[pl.ds(0, tile_rows)], sem.at[0]).wait()


def _moe_dispatch(pos, pad_base, pad_cnt, hf, n_blocks):
    n_tok = pos.shape[0] // TOP_K
    assert n_tok % DISPATCH_TOKENS == 0
    tile_rows = DISPATCH_TOKENS * ROW_CHUNKS
    grid_spec = pltpu.PrefetchScalarGridSpec(
        num_scalar_prefetch=3,
        grid=(n_tok // DISPATCH_TOKENS,),
        in_specs=[pl.BlockSpec((tile_rows, LANES), lambda i, *pf: (i, 0))],
        out_specs=pl.BlockSpec(memory_space=pl.ANY),
        scratch_shapes=[pltpu.SemaphoreType.DMA((2,))],
    )
    return pl.pallas_call(
        _moe_dispatch_kernel,
        grid_spec=grid_spec,
        out_shape=jax.ShapeDtypeStruct((n_blocks * MOE_ROWS * ROW_CHUNKS, LANES), jnp.uint32),
        compiler_params=_cparams(1),
        name="moe_dispatch",
    )(pos, pad_base, pad_cnt, hf)


def _unpack_rows(x_ref):
    rows = x_ref.shape[0] // ROW_CHUNKS
    lo = []
    hi = []
    for jc in range(ROW_CHUNKS):
        words = x_ref[pl.ds(jc, rows, stride=ROW_CHUNKS), :]
        lo.append(pltpu.bitcast(words << 16, F32).astype(BF16))
        hi.append(pltpu.bitcast(words & jnp.uint32(0xFFFF0000), F32).astype(BF16))
    return jnp.concatenate(lo + hi, axis=1)


STEP_RUN, STEP_NEW_WEIGHTS, STEP_ZERO = 0, 1, 2
N_SCHED = 10


def _weight_group_step(s, sf, sg, ng, copies, cast):
    @pl.when(sf[s] == STEP_NEW_WEIGHTS)
    def _():
        g = sg[s]
        slot = g % 2

        @pl.when(g == 0)
        def _():
            for c in copies(g, slot):
                c.start()

        for c in copies(g, slot):
            c.wait()

        @pl.when(g + 1 < ng[0])
        def _():
            for c in copies(g + 1, 1 - slot):
                c.start()

        cast(slot)


def _moe_gu_kernel(se, sw, sb, sj, sf, sg, ge, gw, ng, sx, x_ref, wgu_hbm, bg_ref, bu_ref, o_ref,
                   wg_bf, wu_bf, wg_stage, wu_stage, sem):
    s = pl.program_id(0)

    def copies(g, slot):
        col = pl.multiple_of(gw[g] * MOE_TN, MOE_TN)
        return [pltpu.make_async_copy(wgu_hbm.at[ge[g], :, pl.ds(col, MOE_TN)], wg_stage.at[slot], sem.at[slot]),
                pltpu.make_async_copy(wgu_hbm.at[ge[g], :, pl.ds(D_EXPERT + col, MOE_TN)], wu_stage.at[slot],
                                      sem.at[slot])]

    def cast(slot):
        wg_bf[...] = wg_stage[slot].astype(BF16)
        wu_bf[...] = wu_stage[slot].astype(BF16)

    _weight_group_step(s, sf, sg, ng, copies, cast)

    @pl.when(sf[s] != STEP_ZERO)
    def _():
        x = _unpack_rows(x_ref)
        for c0 in range(0, MOE_TN, MXU_N):
            cols = slice(c0, c0 + MXU_N)
            g = jnp.dot(x, wg_bf[:, cols], preferred_element_type=F32) + bg_ref[0][:, cols]
            u = jnp.dot(x, wu_bf[:, cols], preferred_element_type=F32) + bu_ref[0][:, cols]
            g = jnp.minimum(g, SWIGLU_LIMIT)
            u = jnp.clip(u, -SWIGLU_LIMIT, SWIGLU_LIMIT)
            o_ref[:, cols] = ((u + 1.0) * (g * jax.nn.sigmoid(SWIGLU_ALPHA * g))).astype(BF16)

    @pl.when(sf[s] == STEP_ZERO)
    def _():
        o_ref[...] = jnp.zeros(o_ref.shape, o_ref.dtype)


def _moe_gate_up(sched, xs, w_gu, b_gu, n_blocks):
    nt = D_EXPERT // MOE_TN
    n_steps = nt * n_blocks
    bspec = lambda off: pl.BlockSpec((1, 1, MOE_TN), lambda s, *pf: (pf[0][s], 0, off + pf[1][s]))
    grid_spec = pltpu.PrefetchScalarGridSpec(
        num_scalar_prefetch=N_SCHED,
        grid=(n_steps,),
        in_specs=[pl.BlockSpec((MOE_ROWS * ROW_CHUNKS, LANES), lambda s, *pf: (pf[9][s], 0)),
                  pl.BlockSpec(memory_space=pl.ANY), bspec(0), bspec(nt)],
        out_specs=pl.BlockSpec((MOE_ROWS, MOE_TN), lambda s, *pf: (pf[2][s], pf[3][s])),
        scratch_shapes=[pltpu.VMEM((D_MODEL, MOE_TN), BF16), pltpu.VMEM((D_MODEL, MOE_TN), BF16),
                        pltpu.VMEM((2, D_MODEL, MOE_TN), F32), pltpu.VMEM((2, D_MODEL, MOE_TN), F32),
                        pltpu.SemaphoreType.DMA((2,))],
    )
    return pl.pallas_call(
        _moe_gu_kernel,
        grid_spec=grid_spec,
        out_shape=jax.ShapeDtypeStruct((n_blocks * MOE_ROWS, D_EXPERT), BF16),
        compiler_params=_cparams(1),
        name="moe_gate_up",
    )(*sched, xs, w_gu, b_gu, b_gu)


def _moe_dn_kernel(se, sw, sb, sj, sf, sg, ge, gw, ng, sx, h_ref, wd_hbm, bd_ref, o_ref, wd_bf, wd_stage, sem):
    s = pl.program_id(0)

    def copies(g, slot):
        col = pl.multiple_of(gw[g] * MOE_TN, MOE_TN)
        return [pltpu.make_async_copy(wd_hbm.at[ge[g], :, pl.ds(col, MOE_TN)], wd_stage.at[slot], sem.at[slot])]

    def cast(slot):
        wd_bf[...] = wd_stage[slot].astype(BF16)

    _weight_group_step(s, sf, sg, ng, copies, cast)

    @pl.when(sf[s] != STEP_ZERO)
    def _():
        h = h_ref[...]
        for c0 in range(0, MOE_TN, MXU_N):
            cols = slice(c0, c0 + MXU_N)
            o_ref[:, cols] = jnp.dot(h, wd_bf[:, cols], preferred_element_type=F32) + bd_ref[0][:, cols]

    @pl.when(sf[s] == STEP_ZERO)
    def _():
        o_ref[...] = jnp.zeros(o_ref.shape, o_ref.dtype)


def _moe_down(sched, hid, w_dn, b_dn, n_blocks):
    nt = D_MODEL // MOE_TN
    n_steps = nt * n_blocks
    grid_spec = pltpu.PrefetchScalarGridSpec(
        num_scalar_prefetch=N_SCHED,
        grid=(n_steps,),
        in_specs=[pl.BlockSpec((MOE_ROWS, D_EXPERT), lambda s, *pf: (pf[2][s], 0)),
                  pl.BlockSpec(memory_space=pl.ANY),
                  pl.BlockSpec((1, 1, MOE_TN), lambda s, *pf: (pf[0][s], 0, pf[1][s]))],
        out_specs=pl.BlockSpec((MOE_ROWS, MOE_TN), lambda s, *pf: (pf[2][s], pf[3][s])),
        scratch_shapes=[pltpu.VMEM((D_EXPERT, MOE_TN), BF16), pltpu.VMEM((2, D_EXPERT, MOE_TN), F32),
                        pltpu.SemaphoreType.DMA((2,))],
    )
    return pl.pallas_call(
        _moe_dn_kernel,
        grid_spec=grid_spec,
        out_shape=jax.ShapeDtypeStruct((n_blocks * MOE_ROWS, D_MODEL), F32),
        compiler_params=_cparams(1),
        name="moe_down",
    )(*sched, hid, w_dn, b_dn)


def _moe_combine_kernel(pos_ref, ys_hbm, x1_ref, tg_ref, o_ref, buf, sem):
    i = pl.program_id(0)
    nb = pl.num_programs(0)
    tm = o_ref.shape[0]

    def row_copy(blk, slot, r, kk):
        src = pos_ref[(blk * tm + r) * TOP_K + kk]
        return pltpu.make_async_copy(ys_hbm.at[pl.ds(src, 1)], buf.at[slot, kk, pl.ds(r, 1)], sem.at[slot])

    def issue(blk, slot):
        def body(r, carry):
            for kk in range(TOP_K):
                row_copy(blk, slot, r, kk).start(priority=kk % 2)
            return carry
        lax.fori_loop(0, tm, body, 0, unroll=4)

    @pl.when(i == 0)
    def _():
        issue(0, 0)

    @pl.when(i + 1 < nb)
    def _():
        issue(i + 1, (i + 1) % 2)

    slot = i % 2

    for kk in range(TOP_K):
        pltpu.make_async_copy(ys_hbm.at[pl.ds(0, tm)], buf.at[slot, kk], sem.at[slot]).wait()

    tg = tg_ref[...]
    acc = x1_ref[...]
    for kk in range(TOP_K):
        acc = acc + tg[:, kk:kk + 1] * buf[slot, kk]
    o_ref[...] = acc


def _moe_combine(pos, ys, x1, tg):
    n = x1.shape[0]
    tm = min(128, n)
    grid_spec = pltpu.PrefetchScalarGridSpec(
        num_scalar_prefetch=1,
        grid=(n // tm,),
        in_specs=[pl.BlockSpec(memory_space=pl.ANY),
                  pl.BlockSpec((tm, D_MODEL), lambda i, pos: (i, 0)),
                  pl.BlockSpec((tm, LANES), lambda i, pos: (i, 0))],
        out_specs=pl.BlockSpec((tm, D_MODEL), lambda i, pos: (i, 0)),
        scratch_shapes=[pltpu.VMEM((2, TOP_K, tm, D_MODEL), F32), pltpu.SemaphoreType.DMA((2,))],
    )
    return pl.pallas_call(
        _moe_combine_kernel,
        grid_spec=grid_spec,
        out_shape=jax.ShapeDtypeStruct((n, D_MODEL), F32),
        compiler_params=_cparams(1),
        name="moe_combine",
    )(pos, ys, x1, tg)


def _moe_schedule(top_idx, rank, counts, n_blocks):
    n = top_idx.shape[0]
    n_rows = n * TOP_K
    nt = D_EXPERT // MOE_TN
    flat_e = top_idx.reshape(n_rows)
    nb_e = (counts + MOE_ROWS - 1) // MOE_ROWS
    blk_end = jnp.cumsum(nb_e)
    blk_start = blk_end - nb_e
    pos = (blk_start[flat_e] * MOE_ROWS + rank.reshape(n_rows)).astype(jnp.int32)
    used_blocks = blk_end[-1]
    pad_base = jnp.concatenate([blk_start * MOE_ROWS + counts, (used_blocks * MOE_ROWS).reshape(1)]).astype(jnp.int32)
    pad_cnt = jnp.concatenate([nb_e * MOE_ROWS - counts, (n_blocks - used_blocks).reshape(1)]).astype(jnp.int32)
    used = blk_end[-1]
    s = jnp.arange(nt * n_blocks, dtype=jnp.int32)
    live = s < nt * used
    s_eff = jnp.maximum(jnp.minimum(s, nt * used - 1), 0)
    e = jnp.minimum(jnp.sum((s_eff[:, None] >= nt * blk_end[None, :]).astype(jnp.int32), axis=1), N_EXPERTS - 1)
    local = s_eff - nt * blk_start[e]
    nbe = jnp.maximum(nb_e[e], 1)
    sw = (local // nbe).astype(jnp.int32)
    tail = s - nt * used
    sb = jnp.where(live, blk_start[e] + local % nbe, used + tail // nt).astype(jnp.int32)
    sj = jnp.where(live, sw, tail % nt).astype(jnp.int32)
    sf = jnp.where(live, jnp.where(local % nbe == 0, STEP_NEW_WEIGHTS, STEP_RUN), STEP_ZERO).astype(jnp.int32)
    active = nb_e > 0
    rank_e = jnp.cumsum(active.astype(jnp.int32)) - 1
    n_active = rank_e[-1] + 1
    expert_of_rank = jnp.zeros((N_EXPERTS,), jnp.int32).at[jnp.where(active, rank_e, N_EXPERTS)].set(
        jnp.arange(N_EXPERTS, dtype=jnp.int32), mode="drop")
    sg = (rank_e[e] * nt + sw).astype(jnp.int32)
    gidx = jnp.arange(N_EXPERTS * nt, dtype=jnp.int32)
    ge = expert_of_rank[jnp.minimum(gidx // nt, n_active - 1)]
    gw = gidx % nt
    ng = (n_active * nt).astype(jnp.int32).reshape(1)
    sx = jnp.where(live, sb, 0).astype(jnp.int32)
    return pos, pad_base, pad_cnt, (e, sw, sb, sj, sf, sg, ge, gw, ng, sx)


def _ple_kernel(x_ref, pp_ref, ps_ref, g_ref, wg_ref, wp_ref, op_ref, os_ref, *, np_tiles):
    i = pl.program_id(0)
    x = x_ref[...]
    h = _rms(x, g_ref[...]).astype(BF16)
    gate = jax.nn.sigmoid(jnp.dot(h, wg_ref[...], preferred_element_type=F32))
    p = jnp.where(i < np_tiles, pp_ref[...], ps_ref[...])
    y = x + gate * jnp.dot(p.astype(BF16), wp_ref[...], preferred_element_type=F32)

    @pl.when(i < np_tiles)
    def _():
        op_ref[...] = y

    @pl.when(i >= np_tiles)
    def _():
        os_ref[...] = y


def _ple(x, p_p, p_s, g, wg_bf, wp_bf):
    n_p, n_s = p_p.shape[0], p_s.shape[0]
    n = n_p + n_s
    tm = _row_tile(n_p, n_s, 256)
    np_tiles = n_p // tm
    full = lambda a: pl.BlockSpec(a.shape, lambda i: (0,) * a.ndim)
    pp_spec, ps_spec = _two_group_specs(tm, PLE_DIM, np_tiles)
    op_spec, os_spec = _two_group_specs(tm, D_MODEL, np_tiles)
    return pl.pallas_call(
        functools.partial(_ple_kernel, np_tiles=np_tiles),
        grid=(n // tm,),
        in_specs=[pl.BlockSpec((tm, D_MODEL), lambda i: (i, 0)), pp_spec, ps_spec,
                  full(g), full(wg_bf), full(wp_bf)],
        out_specs=[op_spec, os_spec],
        out_shape=[jax.ShapeDtypeStruct((n_p, D_MODEL), F32), jax.ShapeDtypeStruct((n_s, D_MODEL), F32)],
        compiler_params=_cparams(1),
        name="ple_gate",
    )(x, p_p, p_s, g, wg_bf, wp_bf)


def _pairs_from_state(s):
    b = s.shape[0]
    st = jnp.swapaxes(s, 2, 3).reshape(b, N_PAIRS, 2, HEAD_DIM, HEAD_DIM)
    z = jnp.zeros((b, N_PAIRS, HEAD_DIM, HEAD_DIM), s.dtype)
    top = jnp.concatenate([st[:, :, 0], z], axis=3)
    bot = jnp.concatenate([z, st[:, :, 1]], axis=3)
    return jnp.concatenate([top, bot], axis=2)


def _state_from_pairs(sp):
    b = sp.shape[0]
    st = jnp.stack([sp[:, :, :HEAD_DIM, :HEAD_DIM], sp[:, :, HEAD_DIM:, HEAD_DIM:]], axis=2)
    return jnp.swapaxes(st.reshape(b, N_HEADS, HEAD_DIM, HEAD_DIM), 2, 3)


def _layer(x_p, x_s, cache_k, cache_v, cache_lf, state, shift, p_p, p_s, lw):
    t_p = x_p.shape[1]
    b_s, t_s, _ = x_s.shape
    n_p = x_p.shape[0] * t_p
    n_s = b_s * t_s
    n = n_p + n_s
    past = cache_k.shape[1]
    xp2 = x_p.reshape(n_p, D_MODEL)
    xs2 = x_s.reshape(n_s, D_MODEL)

    w_in_bf = jnp.pad(lw["w_in"], ((0, 0), (0, Z_PAD - lw["w_in"].shape[1]))).astype(BF16)
    z = _in_proj(xp2, xs2, lw["norm_mix_g"], w_in_bf)

    assert t_s == CHUNK, "each sample stream contributes exactly one scan chunk"
    vecs = jnp.zeros((8, MIX_W), F32)
    vecs = vecs.at[0].set(lw["rwkv_w0"]).at[1].set(lw["rwkv_a0"]).at[2].set(lw["rwkv_kk"])
    vecs = vecs.at[3].set(lw["rwkv_ka"]).at[4].set(lw["rwkv_rk"].reshape(MIX_W))
    wl = jnp.zeros((256, 3 * MIX_W), F32)
    wl = wl.at[0:64, 0:MIX_W].set(lw["rwkv_w2"]).at[64:128, MIX_W:2 * MIX_W].set(lw["rwkv_a2"])
    wl = wl.at[128:256, 2 * MIX_W:].set(lw["rwkv_g2"])
    r, w, k, kk, bb, v_pm, g, bonus = _rwkv_pre(z, shift, lw["rwkv_mu"].reshape(1, RWKV_PROJ), vecs, wl, n_p)
    s0 = jnp.concatenate([jnp.zeros((1, N_PAIRS, LANES, LANES), F32), _pairs_from_state(state.astype(F32))], axis=0)
    y_scan, s_out = _rwkv_scan(r, w, k, kk, bb, v_pm, s0, n_p)
    s_new = _state_from_pairs(s_out)
    shift_new_p = z[n_p - 1:n_p, :RWKV_PROJ].reshape(1, 1, RWKV_PROJ)
    shift_new_s = z[n_p:, :RWKV_PROJ].reshape(b_s, t_s, RWKV_PROJ)[:, -1:, :]

    fvecs = jnp.zeros((8, MIX_W), F32)
    fvecs = fvecs.at[0].set(jnp.tile(lw["fox_q_g"], N_HEADS)).at[1].set(jnp.tile(lw["fox_k_g"], N_HEADS))
    bf = jnp.zeros((1, LANES), F32).at[0, :N_HEADS].set(lw["fox_b_f"])
    q_pm, k_pm, vv_pm, k_new_p, v_new_p, k_new_s, v_new_s, lf = _fox_pre(z, fvecs, bf, n_p)
    lf = lf[:, :N_HEADS]
    lf_p = lf[:n_p].T.reshape(1, N_HEADS, n_p)
    c_p = _cumsum_lanes(lf_p).reshape(N_PAIRS, 2, n_p)
    yf_p = _attn_prompt(q_pm, k_pm, vv_pm, c_p, n_p)
    lf_s = jnp.swapaxes(lf[n_p:].reshape(b_s, t_s, N_HEADS), 1, 2)
    lf_all = jnp.concatenate([jnp.swapaxes(cache_lf.astype(F32), 1, 2), lf_s], axis=2)
    pad = (-lf_all.shape[2]) % LANES
    c_all = _cumsum_lanes(jnp.pad(lf_all, ((0, 0), (0, 0), (0, pad))))
    yf_s = _attn_sample(q_pm, k_pm, vv_pm, cache_k.reshape(b_s, past, MIX_W), cache_v.reshape(b_s, past, MIX_W),
                        c_all[:, :, :past], c_all[:, :, past:past + t_s], n_p)

    ln = jnp.stack([lw["rwkv_ln_g"], lw["rwkv_ln_b"]])
    x1, hf, ti, tg, cnt = _mix_out(xp2, xs2, y_scan, bonus, g, yf_p, yf_s, ln, lw["w_out"].astype(BF16),
                                   lw["norm_ffn_g"].reshape(1, D_MODEL), lw["router_w"],
                                   lw["router_b"].reshape(1, N_EXPERTS))

    n_blocks = n * TOP_K // MOE_ROWS + N_EXPERTS
    pos, pad_base, pad_cnt, sched = _moe_schedule(ti[:, :TOP_K], ti[:, TOP_K:2 * TOP_K], cnt[0], n_blocks)
    xs = _moe_dispatch(pos, pad_base, pad_cnt, hf, n_blocks)
    hid = _moe_gate_up(sched, xs, lw["expert_w_gu"], lw["expert_b_gu"].reshape(N_EXPERTS, 1, 2 * D_EXPERT), n_blocks)
    ys = _moe_down(sched, hid, lw["expert_w_down"], lw["expert_b_down"].reshape(N_EXPERTS, 1, D_MODEL), n_blocks)
    x2 = _moe_combine(pos, ys, x1, tg)

    y_p, y_s = _ple(x2, p_p, p_s, lw["ple_norm_g"].reshape(1, D_MODEL), lw["ple_w_gate"].astype(BF16),
                    lw["ple_w_proj"].astype(BF16))

    heads = lambda a, bsz, t: a.reshape(bsz, t, N_HEADS, HEAD_DIM)
    out_p = (y_p.reshape(x_p.shape), heads(k_new_p, 1, n_p), heads(v_new_p, 1, n_p),
             lf[:n_p].reshape(1, n_p, N_HEADS), s_new[:1], shift_new_p)
    out_s = (y_s.reshape(x_s.shape), heads(k_new_s, b_s, t_s), heads(v_new_s, b_s, t_s),
             lf[n_p:].reshape(b_s, t_s, N_HEADS), s_new[1:], shift_new_s)
    return out_p, out_s


def kernel(x_prompt, x_sample, cache_fox_k, cache_fox_v, cache_fox_logf, state_rwkv, state_rwkv_shift, p_prompt, p_sample, norm_mix_g, w_in, rwkv_mu, rwkv_w0, rwkv_w2, rwkv_a0, rwkv_a2, rwkv_g2, rwkv_kk, rwkv_ka, rwkv_rk, rwkv_ln_g, rwkv_ln_b, fox_q_g, fox_k_g, fox_b_f, w_out, norm_ffn_g, router_w, router_b, expert_w_gu, expert_b_gu, expert_w_down, expert_b_down, ple_norm_g, ple_w_gate, ple_w_proj):
    assert x_prompt.shape[0] == 1 and w_in.shape[0] == 1, "one prompt stream, one layer"
    lw = dict(norm_mix_g=norm_mix_g[0], w_in=w_in[0], rwkv_mu=rwkv_mu[0], rwkv_w0=rwkv_w0[0], rwkv_w2=rwkv_w2[0],
              rwkv_a0=rwkv_a0[0], rwkv_a2=rwkv_a2[0], rwkv_g2=rwkv_g2[0], rwkv_kk=rwkv_kk[0], rwkv_ka=rwkv_ka[0],
              rwkv_rk=rwkv_rk[0], rwkv_ln_g=rwkv_ln_g[0], rwkv_ln_b=rwkv_ln_b[0], fox_q_g=fox_q_g[0],
              fox_k_g=fox_k_g[0], fox_b_f=fox_b_f[0], w_out=w_out[0], norm_ffn_g=norm_ffn_g[0],
              router_w=router_w[0], router_b=router_b[0], expert_w_gu=expert_w_gu[0], expert_b_gu=expert_b_gu[0],
              expert_w_down=expert_w_down[0], expert_b_down=expert_b_down[0], ple_norm_g=ple_norm_g[0],
              ple_w_gate=ple_w_gate[0], ple_w_proj=ple_w_proj[0])
    n_p = x_prompt.shape[1]
    (y_p, k_p, v_p, lf_p, s_p, sh_p), (y_s, k_s, v_s, lf_s, s_s, sh_s) = _layer(
        x_prompt, x_sample, cache_fox_k[0], cache_fox_v[0], cache_fox_logf[0], state_rwkv[0],
        state_rwkv_shift[0], p_prompt[0].reshape(n_p, PLE_DIM), p_sample[0].reshape(-1, PLE_DIM), lw)
    add = lambda a: a[None]
    return (y_p, y_s, add(k_p), add(v_p), add(lf_p), add(s_p), add(sh_p),
            add(k_s), add(v_s), add(lf_s), add(s_s), add(sh_s))
```
